```python
import jax, jax.numpy as jnp
from jax import lax
import numpy as np

D_MODEL = 2048
BATCH = 2
SEQ = 4096
DEPTH = 1

CHUNK = 64
D_MIX = D_MODEL
D_CONV = D_MIX // 2
CONV_GROUPS = 16
D_RWKV = D_MIX - D_CONV
RWKV_HEAD = 64
N_RWKV_HEADS = D_RWKV // RWKV_HEAD
CONV_WIDTH = 31
D_DECAY_LORA = 64
D_AAA_LORA = 64
D_GATE_LORA = 160
N_GROUPS = 4
EXPERTS_PER_GROUP = 8
N_EXPERTS = N_GROUPS * EXPERTS_PER_GROUP
TOP_K = 2
D_EXPERT = 512
MOE_BLOCK = 128
RMS_EPS = 1e-6
LN_EPS = 1e-5
GN_EPS = 64e-5
D_IN = 2 * D_CONV + 3 * D_RWKV + D_DECAY_LORA + D_AAA_LORA + D_GATE_LORA
D_SHIFT = D_IN - 2 * D_CONV

kernel_name = 'hybrid_conformer_rwkv7_hmoe_block'


def rms_norm(x, g):
    xf = x.astype(jnp.float32)
    y = xf * lax.rsqrt(jnp.mean(xf * xf, axis=-1, keepdims=True) + RMS_EPS)
    return (y * g.astype(jnp.float32)).astype(x.dtype)


def layer_norm(u, g, b):
    uf = u.astype(jnp.float32)
    mu = jnp.mean(uf, axis=-1, keepdims=True)
    var = jnp.mean(jnp.square(uf - mu), axis=-1, keepdims=True)
    y = (uf - mu) * lax.rsqrt(var + LN_EPS) * g.astype(jnp.float32) + b.astype(jnp.float32)
    return y.astype(u.dtype)


def prev_frame(u):
    return jnp.pad(u, ((0, 0), (1, 0), (0, 0)))[:, :-1]


def conformer_conv(val, gate, dw, bias, ln_g, ln_b):
    u = val * jax.nn.sigmoid(gate)
    u = lax.conv_general_dilated(
        u, dw[:, None, :].astype(u.dtype), window_strides=(1,),
        padding=[(CONV_WIDTH - 1, 0)], dimension_numbers=('NWC', 'WIO', 'NWC'),
        feature_group_count=D_CONV) + bias
    u = layer_norm(u, ln_g, ln_b)
    return jax.nn.silu(u)


def rwkv7_scan(r, decay, k, v, a, b):
    B, T, H, N = r.shape
    def step(S, inp):
        rt, wt, kt, vt, at, bt = inp
        sa = jnp.einsum('bhvk,bhk->bhv', S, at)
        S = S * wt[:, :, None, :] + sa[..., None] * bt[:, :, None, :] + vt[..., None] * kt[:, :, None, :]
        return S, jnp.einsum('bhvk,bhk->bhv', S, rt)
    S0 = jnp.zeros((B, H, N, N), jnp.float32)
    xs = tuple(jnp.moveaxis(t, 1, 0) for t in (r, decay, k, v, a, b))
    _, y = lax.scan(step, S0, xs)
    return jnp.moveaxis(y, 0, 1)


def rwkv7_time_mix(proj, mu, w0, w_up, a0, a_up, g_up, k_k, k_a, r_k, gn_g, gn_b):
    B, T, _ = proj.shape
    f32 = jnp.float32
    proj = proj + (prev_frame(proj) - proj) * mu
    o1, o2, o3 = D_RWKV, 2 * D_RWKV, 3 * D_RWKV
    r, k, v, wd, ad, gd = jnp.split(
        proj, [o1, o2, o3, o3 + D_DECAY_LORA, o3 + D_DECAY_LORA + D_AAA_LORA], axis=-1)
    w = -jax.nn.softplus(-(w0 + jnp.tanh(wd) @ w_up)) - 0.5
    decay = jnp.exp(-jnp.exp(w.astype(f32)))
    a = jax.nn.sigmoid(a0 + ad @ a_up).astype(f32)
    g = jax.nn.sigmoid(gd) @ g_up
    heads = lambda t: t.astype(f32).reshape(B, T, N_RWKV_HEADS, RWKV_HEAD)
    kk = heads(k * k_k)
    kk = kk / jnp.maximum(jnp.sqrt(jnp.sum(kk * kk, axis=-1, keepdims=True)), 1e-12)
    k = k.astype(f32) * (1.0 + (a - 1.0) * k_a.astype(f32))
    rh, kh, vh, ah, wh = heads(r), heads(k), heads(v), heads(a), heads(decay)
    y = rwkv7_scan(rh, wh, kh, vh, -kk, kk * ah)
    mu_y = jnp.mean(y, axis=-1, keepdims=True)
    var_y = jnp.mean(jnp.square(y - mu_y), axis=-1, keepdims=True)
    y = ((y - mu_y) * lax.rsqrt(var_y + GN_EPS)).reshape(B, T, D_RWKV)
    y = y * gn_g.astype(f32) + gn_b.astype(f32)
    bonus = jnp.sum(rh * kh * r_k.astype(f32), axis=-1, keepdims=True) * vh
    y = y + bonus.reshape(B, T, D_RWKV)
    return (y * g.astype(f32)).astype(proj.dtype)


def hierarchical_moe(h, rg_w, rg_b, re_w, re_b, w_gate, w_up, w_down):
    B, T, D = h.shape
    xt = h.reshape(-1, D)
    NT = xt.shape[0]
    xf = xt.astype(jnp.float32)
    g_logits = xf @ rg_w.astype(jnp.float32) + rg_b.astype(jnp.float32)
    g_prob = jax.nn.softmax(g_logits, axis=-1)
    grp = jnp.argmax(g_logits, axis=-1)
    p_grp = jnp.take_along_axis(g_prob, grp[:, None], axis=-1)
    e_logits = (xf @ re_w.astype(jnp.float32) + re_b.astype(jnp.float32))
    e_logits = e_logits.reshape(NT, N_GROUPS, EXPERTS_PER_GROUP)
    e_logits = jnp.take_along_axis(e_logits, grp[:, None, None], axis=1)[:, 0]
    top_v, top_i = lax.top_k(e_logits, TOP_K)
    gates = p_grp * jax.nn.softmax(top_v, axis=-1)
    expert_id = (grp[:, None] * EXPERTS_PER_GROUP + top_i).astype(jnp.int32)

    A = NT * TOP_K
    flat_e = expert_id.reshape(-1)
    flat_tok = jnp.repeat(jnp.arange(NT, dtype=jnp.int32), TOP_K)
    flat_gate = gates.reshape(-1)
    order = jnp.argsort(flat_e)
    se, stok, sgate = flat_e[order], flat_tok[order], flat_gate[order]
    counts = jnp.zeros((N_EXPERTS,), jnp.int32).at[flat_e].add(1)
    starts = jnp.cumsum(counts) - counts
    padded = (counts + MOE_BLOCK - 1) // MOE_BLOCK * MOE_BLOCK
    pend = jnp.cumsum(padded)
    pstarts = pend - padded
    dest = pstarts[se] + (jnp.arange(A, dtype=jnp.int32) - starts[se])
    n_blocks = (A + N_EXPERTS * (MOE_BLOCK - 1) + MOE_BLOCK - 1) // MOE_BLOCK
    P = n_blocks * MOE_BLOCK
    slot_tok = jnp.full((P,), NT, jnp.int32).at[dest].set(stok)
    slot_gate = jnp.zeros((P,), jnp.float32).at[dest].set(sgate)
    block_e = jnp.minimum(
        jnp.searchsorted(pend, jnp.arange(n_blocks, dtype=jnp.int32) * MOE_BLOCK, side='right'),
        N_EXPERTS - 1)
    x_pad = jnp.concatenate([xt, jnp.zeros((1, D), xt.dtype)], axis=0)
    xb = x_pad[slot_tok].reshape(n_blocks, MOE_BLOCK, D)

    def expert_block(args):
        xblk, e = args
        hid = jax.nn.silu(xblk @ w_gate[e]) * (xblk @ w_up[e])
        return hid @ w_down[e]

    yb = lax.map(expert_block, (xb, block_e))
    y = yb.reshape(P, D) * slot_gate[:, None].astype(yb.dtype)
    out = jnp.zeros((NT + 1, D), yb.dtype).at[slot_tok].add(y)[:NT]
    return out.reshape(B, T, D).astype(h.dtype)


def setup_inputs(seed: int = 0) -> dict:
    key = jax.random.key(seed)
    ks = jax.random.split(key, 28)
    f32 = jnp.float32
    L = DEPTH
    def nrm(k, shape, scale):
        return jax.random.normal(k, shape, f32) * scale
    return {
        'x': nrm(ks[0], (BATCH, SEQ, D_MODEL), 1.0),
        'norm_mix': 1.0 + nrm(ks[1], (L, D_MODEL), 0.02),
        'w_in': nrm(ks[2], (L, D_MODEL, D_IN), D_MODEL ** -0.5),
        'conv_dw': nrm(ks[3], (L, CONV_WIDTH, D_CONV), CONV_WIDTH ** -0.5),
        'conv_b': nrm(ks[4], (L, D_CONV), 0.02),
        'conv_ln_g': 1.0 + nrm(ks[5], (L, D_CONV), 0.02),
        'conv_ln_b': nrm(ks[6], (L, D_CONV), 0.02),
        'shift_mu': jax.random.uniform(ks[7], (L, D_SHIFT), f32),
        'w0': -1.0 + nrm(ks[8], (L, D_RWKV), 0.5),
        'w_lora_up': nrm(ks[9], (L, D_DECAY_LORA, D_RWKV), 0.1 * D_DECAY_LORA ** -0.5),
        'a0': nrm(ks[10], (L, D_RWKV), 0.5),
        'a_lora_up': nrm(ks[11], (L, D_AAA_LORA, D_RWKV), 0.1 * D_AAA_LORA ** -0.5),
        'g_lora_up': nrm(ks[12], (L, D_GATE_LORA, D_RWKV), D_GATE_LORA ** -0.5),
        'k_k': 0.85 + nrm(ks[13], (L, D_RWKV), 0.05),
        'k_a': 1.0 + nrm(ks[14], (L, D_RWKV), 0.05),
        'r_k': nrm(ks[15], (L, N_RWKV_HEADS, RWKV_HEAD), 0.1),
        'gn_g': 1.0 + nrm(ks[16], (L, D_RWKV), 0.02),
        'gn_b': nrm(ks[17], (L, D_RWKV), 0.02),
        'w_out': nrm(ks[18], (L, D_MIX, D_MODEL), D_MIX ** -0.5),
        'norm_ffn': 1.0 + nrm(ks[19], (L, D_MODEL), 0.02),
        'router_group_w': nrm(ks[20], (L, D_MODEL, N_GROUPS), D_MODEL ** -0.5),
        'router_group_b': nrm(ks[21], (L, N_GROUPS), 0.01),
        'router_expert_w': nrm(ks[22], (L, D_MODEL, N_EXPERTS), D_MODEL ** -0.5),
        'router_expert_b': nrm(ks[23], (L, N_EXPERTS), 0.01),
        'expert_w_gate': nrm(ks[24], (L, N_EXPERTS, D_MODEL, D_EXPERT), D_MODEL ** -0.5),
        'expert_w_up': nrm(ks[25], (L, N_EXPERTS, D_MODEL, D_EXPERT), D_MODEL ** -0.5),
        'expert_w_down': nrm(ks[26], (L, N_EXPERTS, D_EXPERT, D_MODEL), D_EXPERT ** -0.5),
        'norm_final': 1.0 + nrm(ks[27], (D_MODEL,), 0.02),
    }


def reference(x, norm_mix, w_in, conv_dw, conv_b, conv_ln_g, conv_ln_b, shift_mu, w0,
              w_lora_up, a0, a_lora_up, g_lora_up, k_k, k_a, r_k, gn_g, gn_b, w_out,
              norm_ffn, router_group_w, router_group_b, router_expert_w, router_expert_b,
              expert_w_gate, expert_w_up, expert_w_down, norm_final):
    h = x
    for l in range(DEPTH):
        u = rms_norm(h, norm_mix[l])
        proj = u @ w_in[l]
        conv_val = proj[..., :D_CONV]
        conv_gate = proj[..., D_CONV:2 * D_CONV]
        rwkv_cols = proj[..., 2 * D_CONV:]
        y_conv = conformer_conv(conv_val, conv_gate, conv_dw[l], conv_b[l], conv_ln_g[l], conv_ln_b[l])
        y_rwkv = rwkv7_time_mix(rwkv_cols, shift_mu[l], w0[l], w_lora_up[l], a0[l], a_lora_up[l],
                                g_lora_up[l], k_k[l], k_a[l], r_k[l], gn_g[l], gn_b[l])
        mixed = jnp.concatenate([y_conv, y_rwkv], axis=-1)
        h = h + mixed @ w_out[l]
        v = rms_norm(h, norm_ffn[l])
        h = h + hierarchical_moe(v, router_group_w[l], router_group_b[l], router_expert_w[l],
                                 router_expert_b[l], expert_w_gate[l], expert_w_up[l], expert_w_down[l])
    return rms_norm(h, norm_final)
```

```python
import functools

import jax
import jax.numpy as jnp
from jax import lax
from jax.experimental import pallas as pl
from jax.experimental.pallas import tpu as pltpu

F32 = jnp.float32
BF16 = jnp.bfloat16

RWKV_HEAD = 64
CONV_WIDTH = 31
N_GROUPS = 4
EXPERTS_PER_GROUP = 8
N_EXPERTS = N_GROUPS * EXPERTS_PER_GROUP
TOP_K = 2
RMS_EPS = 1e-6
LN_EPS = 1e-5
GN_EPS = 64e-5
LANES = 128
VMEM_LIMIT = 56 * 1024 * 1024

_NT = (((1,), (1,)), ((), ()))
_TN = (((0,), (0,)), ((), ()))


def _dot(a, b, dims=None):
    a = a.astype(BF16)
    b = b.astype(BF16)
    if dims is None:
        return jnp.dot(a, b, preferred_element_type=F32)
    return lax.dot_general(a, b, dims, preferred_element_type=F32)


def _dot_split(x, m):
    hi = x.astype(BF16)
    lo = (x - hi.astype(F32)).astype(BF16)
    return (jnp.dot(hi, m, preferred_element_type=F32)
            + jnp.dot(lo, m, preferred_element_type=F32))


def _rwkv_kernel(r_ref, k_ref, v_ref, lora_ref, mu_main_ref, mu_lora_ref, w0_ref, wup_ref, a0_ref, aup_ref,
                 gup_ref, kk_ref, ka_ref, rk_ref, gng_ref, gnb_ref, y_ref,
                 state_ref, prev_main_ref, prev_lora_ref, *, n_heads, d_decay, d_aaa):
    C = r_ref.shape[0]
    N = RWKV_HEAD
    c_idx = pl.program_id(1)

    @pl.when(c_idx == 0)
    def _():
        state_ref[...] = jnp.zeros_like(state_ref)
        prev_main_ref[...] = jnp.zeros_like(prev_main_ref)
        prev_lora_ref[...] = jnp.zeros_like(prev_lora_ref)

    row = lax.broadcasted_iota(jnp.int32, (C, 1), 0)

    def shifted(x, prev_row, mu):
        xp = jnp.where(row == 0, prev_row, pltpu.roll(x, 1, 0))
        return x + (xp - x) * mu

    d_r = r_ref.shape[1]
    rr = r_ref[...]
    kx = k_ref[...]
    vx = v_ref[...]
    lx = lora_ref[...]
    r = shifted(rr, prev_main_ref[0:1, :], mu_main_ref[0:1, :])
    k = shifted(kx, prev_main_ref[1:2, :], mu_main_ref[1:2, :])
    v = shifted(vx, prev_main_ref[2:3, :], mu_main_ref[2:3, :])
    lo = shifted(lx, prev_lora_ref[...], mu_lora_ref[...])
    prev_main_ref[0:1, :] = rr[C - 1:C, :]
    prev_main_ref[1:2, :] = kx[C - 1:C, :]
    prev_main_ref[2:3, :] = vx[C - 1:C, :]
    prev_lora_ref[...] = lx[C - 1:C, :]

    wd = lo[:, :d_decay]
    ad = lo[:, d_decay:d_decay + d_aaa]
    gd = lo[:, d_decay + d_aaa:]

    z = w0_ref[...] + _dot(jnp.tanh(wd), wup_ref[...])
    w = -(jnp.maximum(-z, 0.0) + jnp.log(1.0 + jnp.exp(-jnp.abs(z)))) - 0.5
    logd = -jnp.exp(w)
    a = jax.nn.sigmoid(a0_ref[...] + _dot(ad, aup_ref[...]))
    g = _dot(jax.nn.sigmoid(gd), gup_ref[...])

    li = lax.broadcasted_iota(jnp.int32, (LANES, LANES), 0) // N
    lj = lax.broadcasted_iota(jnp.int32, (LANES, LANES), 1) // N
    head_ones = (li == lj).astype(BF16)

    def head_sum(x):
        return jnp.concatenate(
            [_dot_split(x[:, c * LANES:(c + 1) * LANES], head_ones) for c in range(d_r // LANES)], axis=1)

    kkr = k * kk_ref[...]
    kk = kkr / jnp.maximum(jnp.sqrt(head_sum(kkr * kkr)), 1e-12)
    k2 = k * (1.0 + (a - 1.0) * ka_ref[...])
    a_s = -kk
    b_s = kk * a

    ti = lax.broadcasted_iota(jnp.int32, (C, C), 0)
    tj = lax.broadcasted_iota(jnp.int32, (C, C), 1)
    tri = (tj <= ti).astype(BF16)
    hi = logd.astype(BF16)
    rem = logd - hi.astype(F32)
    mid = rem.astype(BF16)
    low = (rem - mid.astype(F32)).astype(BF16)
    cum = (jnp.dot(tri, hi, preferred_element_type=F32) + jnp.dot(tri, mid, preferred_element_type=F32)
           + jnp.dot(tri, low, preferred_element_type=F32))
    p_incl = jnp.exp(cum)
    p_excl = jnp.exp(cum - logd)
    p_inv = jnp.exp(-cum)
    p_last = p_incl[C - 1:C, :]

    at = (a_s * p_excl).astype(BF16)
    rt = (r * p_incl).astype(BF16)
    bt = b_s * p_inv
    kt = k2 * p_inv
    bh = (bt * p_last).astype(BF16)
    kh = (kt * p_last).astype(BF16)
    bt = bt.astype(BF16)
    kt = kt.astype(BF16)
    vb = v.astype(BF16)

    strict = tj < ti
    incl = tj <= ti
    mask2 = jnp.concatenate([strict, incl], axis=0)
    eye = (ti == tj).astype(F32)
    level_masks = []
    s = 1
    while s < C:
        level_masks.append(((ti // (2 * s)) == (tj // (2 * s))) & (((ti // s) % 2) == 1) & (((tj // s) % 2) == 0))
        s *= 2

    heads = range(n_heads)
    sls = [slice(h * N, (h + 1) * N) for h in heads]
    ar = [jnp.concatenate([at[:, sl], rt[:, sl]], axis=0) for sl in sls]
    g_b = [jnp.where(mask2, _dot(ar[h], bt[:, sls[h]], _NT), 0.0) for h in heads]
    g_k = [jnp.where(mask2, _dot(ar[h], kt[:, sls[h]], _NT), 0.0) for h in heads]
    s0 = [state_ref[h] for h in heads]
    ars = [_dot(ar[h], s0[h], _NT) for h in heads]
    gv = [_dot(g_k[h], vb[:, sls[h]]) for h in heads]
    tinv = [eye + jnp.where(level_masks[0], g_b[h][:C], 0.0) for h in heads]
    for lm in level_masks[1:]:
        xs = [_dot(jnp.where(lm, g_b[h][:C], 0.0), tinv[h]) for h in heads]
        tinv = [tinv[h] + _dot(tinv[h], xs[h]) for h in heads]
    u = [_dot(tinv[h], ars[h][:C] + gv[h][:C]) for h in heads]
    ys = [ars[h][C:] + gv[h][C:] + _dot(g_b[h][C:], u[h]) for h in heads]
    for h in heads:
        uv = jnp.concatenate([u[h].astype(BF16), vb[:, sls[h]]], axis=0)
        bk = jnp.concatenate([bh[:, sls[h]], kh[:, sls[h]]], axis=0)
        state_ref[h] = s0[h] * p_last[:, sls[h]] + _dot(uv, bk, _TN)
    y = jnp.concatenate(ys, axis=1)

    inv_n = 1.0 / N
    mu_y = head_sum(y) * inv_n
    yc = y - mu_y
    var_y = head_sum(yc * yc) * inv_n
    yn = yc * lax.rsqrt(var_y + GN_EPS) * gng_ref[...] + gnb_ref[...]
    bonus = head_sum(r * k2 * rk_ref[...]) * v
    y_ref[...] = ((yn + bonus) * g).astype(y_ref.dtype)


def rwkv_mixer(proj_main, proj_lora, col0, shift_mu, w0, w_up, a0, a_up, g_up, k_k, k_a, r_k, gn_g, gn_b,
               *, batch, chunk=64):
    nt = proj_main.shape[0]
    seq = nt // batch
    d_r = w0.shape[-1]
    n_heads = d_r // RWKV_HEAD
    d_lora = proj_lora.shape[1]
    d_decay, d_aaa = w_up.shape[0], a_up.shape[0]
    n_chunks = seq // chunk
    assert seq % chunk == 0 and col0 % d_r == 0
    cb = col0 // d_r
    row2 = lambda t: t.reshape(1, -1).astype(F32)
    mu_main = shift_mu[:3 * d_r].reshape(3, d_r)
    mu_lora = shift_mu[3 * d_r:].reshape(1, d_lora)

    def main_spec(j):
        return pl.BlockSpec((chunk, d_r), lambda b, c, j=j: (b * n_chunks + c, cb + j))

    full = lambda shape: pl.BlockSpec(shape, lambda b, c: (0,) * len(shape))
    kern = functools.partial(_rwkv_kernel, n_heads=n_heads, d_decay=d_decay, d_aaa=d_aaa)
    return pl.pallas_call(
        kern,
        grid=(batch, n_chunks),
        in_specs=[main_spec(0), main_spec(1), main_spec(2),
                  pl.BlockSpec((chunk, d_lora), lambda b, c: (b * n_chunks + c, 0)),
                  full((3, d_r)), full((1, d_lora)), full((1, d_r)), full((d_decay, d_r)), full((1, d_r)),
                  full((d_aaa, d_r)), full((g_up.shape[0], d_r)), full((1, d_r)), full((1, d_r)), full((1, d_r)),
                  full((1, d_r)), full((1, d_r))],
        out_specs=pl.BlockSpec((chunk, d_r), lambda b, c: (b * n_chunks + c, 0)),
        out_shape=jax.ShapeDtypeStruct((nt, d_r), F32),
        scratch_shapes=[pltpu.VMEM((n_heads, RWKV_HEAD, RWKV_HEAD), F32),
                        pltpu.VMEM((3, d_r), F32),
                        pltpu.VMEM((1, d_lora), F32)],
        compiler_params=pltpu.CompilerParams(dimension_semantics=("arbitrary", "arbitrary"),
                                             vmem_limit_bytes=VMEM_LIMIT),
        name="rwkv_mixer",
    )(proj_main, proj_main, proj_main, proj_lora, mu_main, mu_lora, row2(w0), w_up.astype(BF16), row2(a0),
      a_up.astype(BF16), g_up.astype(BF16), row2(k_k), row2(k_a), row2(r_k), row2(gn_g), row2(gn_b))


def _rms(x, g):
    return x * lax.rsqrt(jnp.mean(x * x, axis=-1, keepdims=True) + RMS_EPS) * g


def _inproj_kernel(x_ref, g_ref, w_ref, wl_ref, o_ref, ol_ref, u_ref):
    @pl.when(pl.program_id(1) == 0)
    def _():
        u = _rms(x_ref[...], g_ref[...]).astype(BF16)
        u_ref[...] = u
        ol_ref[...] = jnp.dot(u, wl_ref[...], preferred_element_type=F32)

    o_ref[...] = jnp.dot(u_ref[...], w_ref[...], preferred_element_type=F32)


def in_proj(x2, norm_g, w_in, d_main, *, tm=1024, tn=1024):
    nt, d = x2.shape
    d_lora = w_in.shape[1] - d_main
    assert nt % tm == 0 and d_main % tn == 0
    w_lora = w_in[:, d_main:]
    return pl.pallas_call(
        _inproj_kernel,
        grid=(nt // tm, d_main // tn),
        in_specs=[pl.BlockSpec((tm, d), lambda i, j: (i, 0)),
                  pl.BlockSpec((1, d), lambda i, j: (0, 0)),
                  pl.BlockSpec((d, tn), lambda i, j: (0, j)),
                  pl.BlockSpec((d, d_lora), lambda i, j: (0, 0))],
        out_specs=[pl.BlockSpec((tm, tn), lambda i, j: (i, j)),
                   pl.BlockSpec((tm, d_lora), lambda i, j: (i, 0))],
        out_shape=[jax.ShapeDtypeStruct((nt, d_main), F32), jax.ShapeDtypeStruct((nt, d_lora), F32)],
        scratch_shapes=[pltpu.VMEM((tm, d), BF16)],
        compiler_params=pltpu.CompilerParams(dimension_semantics=("arbitrary", "arbitrary"),
                                             vmem_limit_bytes=VMEM_LIMIT),
        name="in_proj",
    )(x2, norm_g.reshape(1, d).astype(F32), w_in, w_lora)


CONV_HALO = 32
CONV_ROWS = 32


def _conv_kernel(val_ref, gate_ref, dw_ref, b_ref, g_ref, beta_ref, o_ref, ubuf_ref):
    tt = val_ref.shape[0]
    t_idx = pl.program_id(1)

    @pl.when(t_idx == 0)
    def _():
        ubuf_ref[0:CONV_HALO, :] = jnp.zeros((CONV_HALO, ubuf_ref.shape[1]), F32)

    @pl.when(t_idx > 0)
    def _():
        ubuf_ref[0:CONV_HALO, :] = ubuf_ref[tt:tt + CONV_HALO, :]

    ubuf_ref[CONV_HALO:CONV_HALO + tt, :] = val_ref[...] * jax.nn.sigmoid(gate_ref[...])
    dw = dw_ref[...]
    lead = CONV_HALO - (CONV_WIDTH - 1)
    for c in range(tt // CONV_ROWS):
        r0 = c * CONV_ROWS
        acc = jnp.zeros((CONV_ROWS, ubuf_ref.shape[1]), F32)
        for j in range(CONV_WIDTH):
            acc = acc + dw[j:j + 1, :] * ubuf_ref[r0 + lead + j:r0 + lead + j + CONV_ROWS, :]
        acc = acc + b_ref[...]
        mu = jnp.mean(acc, axis=-1, keepdims=True)
        cen = acc - mu
        var = jnp.mean(cen * cen, axis=-1, keepdims=True)
        yv = cen * lax.rsqrt(var + LN_EPS) * g_ref[...] + beta_ref[...]
        o_ref[r0:r0 + CONV_ROWS, :] = yv * jax.nn.sigmoid(yv)


def conv_mixer(proj_main, dw, bias, ln_g, ln_b, *, batch, tt=256):
    nt = proj_main.shape[0]
    seq = nt // batch
    d_c = dw.shape[1]
    n_t = seq // tt
    assert seq % tt == 0 and tt % CONV_ROWS == 0
    row = lambda t: t.reshape(1, d_c).astype(F32)
    full = lambda shape: pl.BlockSpec(shape, lambda b, t: (0, 0))
    return pl.pallas_call(
        _conv_kernel,
        grid=(batch, n_t),
        in_specs=[pl.BlockSpec((tt, d_c), lambda b, t: (b * n_t + t, 0)),
                  pl.BlockSpec((tt, d_c), lambda b, t: (b * n_t + t, 1)),
                  full((CONV_WIDTH, d_c)), full((1, d_c)), full((1, d_c)), full((1, d_c))],
        out_specs=pl.BlockSpec((tt, d_c), lambda b, t: (b * n_t + t, 0)),
        out_shape=jax.ShapeDtypeStruct((nt, d_c), F32),
        scratch_shapes=[pltpu.VMEM((tt + CONV_HALO, d_c), F32)],
        compiler_params=pltpu.CompilerParams(dimension_semantics=("arbitrary", "arbitrary"),
                                             vmem_limit_bytes=VMEM_LIMIT),
        name="conv_mixer",
    )(proj_main, proj_main, dw.astype(F32), row(bias), row(ln_g), row(ln_b))


def _out_router_kernel(x_ref, yc_ref, yr_ref, wc_ref, wr_ref, nf_ref, rw_hi_ref, rw_lo_ref, rb_ref,
                       h_ref, v_ref, eid_ref, gate_ref):
    h = (x_ref[...] + jnp.dot(yc_ref[...].astype(BF16), wc_ref[...], preferred_element_type=F32)
         + jnp.dot(yr_ref[...].astype(BF16), wr_ref[...], preferred_element_type=F32))
    h_ref[...] = h
    v = _rms(h, nf_ref[...])
    v_ref[...] = v
    v_hi = v.astype(BF16)
    v_lo = (v - v_hi.astype(F32)).astype(BF16)
    logits = (jnp.dot(v_hi, rw_hi_ref[...], preferred_element_type=F32)
              + jnp.dot(v_lo, rw_hi_ref[...], preferred_element_type=F32)
              + jnp.dot(v_hi, rw_lo_ref[...], preferred_element_type=F32)) + rb_ref[...]
    lane = lax.broadcasted_iota(jnp.int32, logits.shape, 1)
    neg = jnp.float32(-jnp.inf)
    big = jnp.int32(LANES)

    def first_max(vals):
        m = jnp.max(vals, axis=-1, keepdims=True)
        return m, jnp.min(jnp.where(vals == m, lane, big), axis=-1, keepdims=True)

    gl = jnp.where(lane < N_GROUPS, logits, neg)
    gmax, grp = first_max(gl)
    p_grp = 1.0 / jnp.sum(jnp.exp(gl - gmax), axis=-1, keepdims=True)
    lo = N_GROUPS + grp * EXPERTS_PER_GROUP
    el = jnp.where((lane >= lo) & (lane < lo + EXPERTS_PER_GROUP), logits, neg)
    m1, i1 = first_max(el)
    m2, i2 = first_max(jnp.where(lane == i1, neg, el))
    e2 = jnp.exp(m2 - m1)
    g1 = p_grp / (1.0 + e2)
    g2 = p_grp * e2 / (1.0 + e2)
    eid_ref[...] = jnp.where(lane == 0, i1 - N_GROUPS, jnp.where(lane == 1, i2 - N_GROUPS, 0))
    gate_ref[...] = jnp.where(lane == 0, g1, jnp.where(lane == 1, g2, 0.0))


def out_router(x2, y_conv, y_rwkv, w_out, norm_ffn, rg_w, rg_b, re_w, re_b, *, tm=256):
    nt, d = x2.shape
    d_c, d_r = y_conv.shape[1], y_rwkv.shape[1]
    assert d_c == d_r and nt % tm == 0
    n_r = N_GROUPS + N_EXPERTS
    rw = jnp.zeros((d, LANES), F32).at[:, :n_r].set(jnp.concatenate([rg_w, re_w], axis=1).astype(F32))
    rw_hi = rw.astype(BF16)
    rw_lo = (rw - rw_hi.astype(F32)).astype(BF16)
    rb = jnp.zeros((1, LANES), F32).at[0, :n_r].set(jnp.concatenate([rg_b, re_b]).astype(F32))
    full = lambda shape: pl.BlockSpec(shape, lambda i: (0, 0))
    rows = lambda w: pl.BlockSpec((tm, w), lambda i: (i, 0))
    return pl.pallas_call(
        _out_router_kernel,
        grid=(nt // tm,),
        in_specs=[rows(d), rows(d_c), rows(d_r),
                  pl.BlockSpec((d_c, d), lambda i: (0, 0)), pl.BlockSpec((d_r, d), lambda i: (1, 0)),
                  full((1, d)), full((d, LANES)), full((d, LANES)), full((1, LANES))],
        out_specs=[rows(d), rows(d), rows(LANES), rows(LANES)],
        out_shape=[jax.ShapeDtypeStruct((nt, d), F32), jax.ShapeDtypeStruct((nt, d), F32),
                   jax.ShapeDtypeStruct((nt, LANES), jnp.int32), jax.ShapeDtypeStruct((nt, LANES), F32)],
        compiler_params=pltpu.CompilerParams(dimension_semantics=("arbitrary",), vmem_limit_bytes=VMEM_LIMIT),
        name="out_router",
    )(x2, y_conv, y_rwkv, w_out, w_out, norm_ffn.reshape(1, d).astype(F32), rw_hi, rw_lo, rb)


MOE_ROWS = 256


def slot_plan(expert_id, n_rows_pad):
    flat_e = expert_id.reshape(-1)
    onehot = (flat_e[:, None] == jnp.arange(N_EXPERTS, dtype=jnp.int32)[None, :]).astype(jnp.int32)
    csum = jnp.cumsum(onehot, axis=0)
    rank = jnp.sum(csum * onehot, axis=1) - 1
    counts = csum[-1]
    padded = (counts + MOE_ROWS - 1) // MOE_ROWS * MOE_ROWS
    pend = jnp.cumsum(padded)
    pstarts = pend - padded
    dest = (pstarts[flat_e] + rank).astype(jnp.int32).reshape(expert_id.shape)
    n_blocks = n_rows_pad // MOE_ROWS
    block_e = jnp.minimum(
        jnp.searchsorted(pend, jnp.arange(n_blocks, dtype=jnp.int32) * MOE_ROWS, side='right'),
        N_EXPERTS - 1).astype(jnp.int32)
    n_used = (pend[-1:] // MOE_ROWS).astype(jnp.int32)
    return dest, block_e, n_used


def _dispatch_kernel(dest_ref, v_ref, xs_in_ref, xs_ref, sem):
    del xs_in_ref
    tm = v_ref.shape[0]

    def copy(i, k):
        d = dest_ref[0, 0, i * TOP_K + k]
        return pltpu.make_async_copy(v_ref.at[pl.ds(i, 1), :], xs_ref.at[pl.ds(d, 1), :], sem)

    def start(i, carry):
        for k in range(TOP_K):
            copy(i, k).start()
        return carry

    def wait(i, carry):
        for k in range(TOP_K):
            copy(i, k).wait()
        return carry

    lax.fori_loop(0, tm, start, 0)
    lax.fori_loop(0, tm, wait, 0)


def dispatch(v, dest, n_rows_pad, *, tm=256):
    nt, d = v.shape
    assert nt % tm == 0
    dest3 = dest.reshape(nt // tm, 1, tm * TOP_K)
    xs0 = jnp.zeros((n_rows_pad, d), v.dtype)
    return pl.pallas_call(
        _dispatch_kernel,
        grid=(nt // tm,),
        in_specs=[pl.BlockSpec((1, 1, tm * TOP_K), lambda i: (i, 0, 0), memory_space=pltpu.SMEM),
                  pl.BlockSpec((tm, d), lambda i: (i, 0)),
                  pl.BlockSpec(memory_space=pl.ANY)],
        out_specs=pl.BlockSpec(memory_space=pl.ANY),
        out_shape=jax.ShapeDtypeStruct((n_rows_pad, d), v.dtype),
        scratch_shapes=[pltpu.SemaphoreType.DMA(())],
        input_output_aliases={2: 0},
        compiler_params=pltpu.CompilerParams(dimension_semantics=("arbitrary",), vmem_limit_bytes=VMEM_LIMIT),
        name="moe_dispatch",
    )(dest3, v, xs0)


def _experts_kernel(be_ref, nu_ref, xs_ref, wg_ref, wu_ref, wd_ref, ys_ref, wg_s, wu_s, wd_s):
    j = pl.program_id(0)
    prev = be_ref[jnp.maximum(j - 1, 0)]

    @pl.when((j == 0) | (be_ref[j] != prev))
    def _():
        wg_s[...] = wg_ref[...].astype(BF16)
        wu_s[...] = wu_ref[...].astype(BF16)
        wd_s[...] = wd_ref[...].astype(BF16)

    @pl.when(j < nu_ref[0])
    def _():
        xb = xs_ref[...].astype(BF16)
        gt = jnp.dot(xb, wg_s[...], preferred_element_type=F32)
        up = jnp.dot(xb, wu_s[...], preferred_element_type=F32)
        hid = (gt * jax.nn.sigmoid(gt) * up).astype(BF16)
        ys_ref[...] = jnp.dot(hid, wd_s[...], preferred_element_type=F32)

    @pl.when(j >= nu_ref[0])
    def _():
        ys_ref[...] = jnp.zeros_like(ys_ref)


def experts(xs, block_e, n_used, w_gate, w_up, w_down):
    p, d = xs.shape
    d_e = w_gate.shape[2]
    n_blocks = p // MOE_ROWS
    grid_spec = pltpu.PrefetchScalarGridSpec(
        num_scalar_prefetch=2,
        grid=(n_blocks,),
        in_specs=[pl.BlockSpec((MOE_ROWS, d), lambda j, be, nu: (j, 0)),
                  pl.BlockSpec((None, d, d_e), lambda j, be, nu: (be[j], 0, 0)),
                  pl.BlockSpec((None, d, d_e), lambda j, be, nu: (be[j], 0, 0)),
                  pl.BlockSpec((None, d_e, d), lambda j, be, nu: (be[j], 0, 0))],
        out_specs=pl.BlockSpec((MOE_ROWS, d), lambda j, be, nu: (j, 0)),
        scratch_shapes=[pltpu.VMEM((d, d_e), BF16), pltpu.VMEM((d, d_e), BF16), pltpu.VMEM((d_e, d), BF16)],
    )
    return pl.pallas_call(
        _experts_kernel,
        grid_spec=grid_spec,
        out_shape=jax.ShapeDtypeStruct((p, d), F32),
        compiler_params=pltpu.CompilerParams(dimension_semantics=("arbitrary",), vmem_limit_bytes=VMEM_LIMIT),
        name="moe_experts",
    )(block_e, n_used, xs, w_gate, w_up, w_down)


def _combine_kernel(dest_ref, h_ref, gate_ref, nfin_ref, ys_ref, o_ref, ybuf, sem):
    tm = h_ref.shape[0]

    def copy(i, k):
        d = dest_ref[0, 0, i * TOP_K + k]
        return pltpu.make_async_copy(ys_ref.at[pl.ds(d, 1), :], ybuf.at[k, pl.ds(i, 1), :], sem)

    def start(i, carry):
        for k in range(TOP_K):
            copy(i, k).start()
        return carry

    def wait(i, carry):
        for k in range(TOP_K):
            copy(i, k).wait()
        return carry

    lax.fori_loop(0, tm, start, 0)
    lax.fori_loop(0, tm, wait, 0)
    gates = gate_ref[...]
    moe = gates[:, 0:1] * ybuf[0] + gates[:, 1:2] * ybuf[1]
    o_ref[...] = _rms(h_ref[...] + moe, nfin_ref[...])


def combine(h, gates, dest, ys, norm_final, *, tm=256):
    nt, d = h.shape
    assert nt % tm == 0
    dest3 = dest.reshape(nt // tm, 1, tm * TOP_K)
    return pl.pallas_call(
        _combine_kernel,
        grid=(nt // tm,),
        in_specs=[pl.BlockSpec((1, 1, tm * TOP_K), lambda i: (i, 0, 0), memory_space=pltpu.SMEM),
                  pl.BlockSpec((tm, d), lambda i: (i, 0)),
                  pl.BlockSpec((tm, LANES), lambda i: (i, 0)),
                  pl.BlockSpec((1, d), lambda i: (0, 0)),
                  pl.BlockSpec(memory_space=pl.ANY)],
        out_specs=pl.BlockSpec((tm, d), lambda i: (i, 0)),
        out_shape=jax.ShapeDtypeStruct((nt, d), F32),
        scratch_shapes=[pltpu.VMEM((TOP_K, tm, d), F32), pltpu.SemaphoreType.DMA(())],
        compiler_params=pltpu.CompilerParams(dimension_semantics=("arbitrary",), vmem_limit_bytes=VMEM_LIMIT),
        name="moe_combine",
    )(dest3, h, gates, norm_final.reshape(1, d).astype(F32), ys)


def kernel(x, norm_mix, w_in, conv_dw, conv_b, conv_ln_g, conv_ln_b, shift_mu, w0, w_lora_up, a0, a_lora_up, g_lora_up, k_k, k_a, r_k, gn_g, gn_b, w_out, norm_ffn, router_group_w, router_group_b, router_expert_w, router_expert_b, expert_w_gate, expert_w_up, expert_w_down, norm_final):
    B, T, D = x.shape
    depth = w_in.shape[0]
    d_c = conv_dw.shape[2]
    d_r = w0.shape[1]
    d_main = 2 * d_c + 3 * d_r
    nt = B * T
    n_rows_pad = -(-(nt * TOP_K + N_EXPERTS * (MOE_ROWS - 1)) // MOE_ROWS) * MOE_ROWS
    h = x.reshape(nt, D)
    for l in range(depth):
        proj_main, proj_lora = in_proj(h, norm_mix[l], w_in[l].astype(BF16), d_main)
        y_conv = conv_mixer(proj_main, conv_dw[l], conv_b[l], conv_ln_g[l], conv_ln_b[l], batch=B)
        y_rwkv = rwkv_mixer(proj_main, proj_lora, 2 * d_c, shift_mu[l], w0[l], w_lora_up[l], a0[l], a_lora_up[l],
                            g_lora_up[l], k_k[l], k_a[l], r_k[l].reshape(-1), gn_g[l], gn_b[l], batch=B)
        h, v, eid, gates = out_router(h, y_conv, y_rwkv, w_out[l].astype(BF16), norm_ffn[l], router_group_w[l],
                                      router_group_b[l], router_expert_w[l], router_expert_b[l])
        dest, block_e, n_used = slot_plan(eid[:, :TOP_K], n_rows_pad)
        xs = dispatch(v, dest, n_rows_pad)
        ys = experts(xs, block_e, n_used, expert_w_gate[l], expert_w_up[l], expert_w_down[l])
        assert depth == 1
        h = combine(h, gates, dest, ys, norm_final)
    return h.reshape(B, T, D)
```

```python
import functools

import jax
import jax.numpy as jnp
from jax import lax
from jax.experimental import pallas as pl
from jax.experimental.pallas import tpu as pltpu

F32 = jnp.float32
BF16 = jnp.bfloat16

RWKV_HEAD = 64
CONV_WIDTH = 31
N_GROUPS = 4
EXPERTS_PER_GROUP = 8
N_EXPERTS = N_GROUPS * EXPERTS_PER_GROUP
TOP_K = 2
RMS_EPS = 1e-6
LN_EPS = 1e-5
GN_EPS = 64e-5
LANES = 128
SUBLANES = 8
VMEM_LIMIT = 56 * 1024 * 1024

_NT = (((1,), (1,)), ((), ()))
_TN = (((0,), (0,)), ((), ()))


def _dot(a, b, dims=None):
    a = a.astype(BF16)
    b = b.astype(BF16)
    if dims is None:
        return jnp.dot(a, b, preferred_element_type=F32)
    return lax.dot_general(a, b, dims, preferred_element_type=F32)


def _dot_split(x, m):
    hi = x.astype(BF16)
    lo = (x - hi.astype(F32)).astype(BF16)
    return (jnp.dot(hi, m, preferred_element_type=F32)
            + jnp.dot(lo, m, preferred_element_type=F32))


def _pack_bf16_pairs(x):
    w = x.shape[1] // 2
    lo = lax.bitcast_convert_type(x[:, :w].astype(BF16).astype(F32), jnp.uint32)
    hi = lax.bitcast_convert_type(x[:, w:].astype(BF16).astype(F32), jnp.uint32)
    return (lo >> 16) | (hi & jnp.uint32(0xFFFF0000))


def _unpack_bf16_pairs(p):
    lo = lax.bitcast_convert_type(p << 16, F32)
    hi = lax.bitcast_convert_type(p & jnp.uint32(0xFFFF0000), F32)
    return jnp.concatenate([lo, hi], axis=1)


def _rwkv_kernel(r_ref, k_ref, v_ref, lora_ref, mu_main_ref, mu_lora_ref, w0_ref, wup_ref, a0_ref, aup_ref,
                 gup_ref, kk_ref, ka_ref, rk_ref, gng_ref, gnb_ref, y_ref,
                 state_ref, prev_main_ref, prev_lora_ref, *, n_heads, d_decay, d_aaa):
    nb, C, d_r = r_ref.shape
    N = RWKV_HEAD
    R = nb * C

    @pl.when(pl.program_id(0) == 0)
    def _():
        state_ref[...] = jnp.zeros_like(state_ref)
        prev_main_ref[...] = jnp.zeros_like(prev_main_ref)
        prev_lora_ref[...] = jnp.zeros_like(prev_lora_ref)

    row = lax.broadcasted_iota(jnp.int32, (R, 1), 0)

    def shifted(x_ref, prev_ref, j, mu):
        x = x_ref[...].reshape(R, x_ref.shape[2])
        xp = pltpu.roll(x, 1, 0)
        for b in range(nb):
            xp = jnp.where(row == b * C, prev_ref[b, j:j + 1, :], xp)
            prev_ref[b, j:j + 1, :] = x[(b + 1) * C - 1:(b + 1) * C, :]
        return x + (xp - x) * mu

    r = shifted(r_ref, prev_main_ref, 0, mu_main_ref[0:1, :])
    k = shifted(k_ref, prev_main_ref, 1, mu_main_ref[1:2, :])
    v = shifted(v_ref, prev_main_ref, 2, mu_main_ref[2:3, :])
    lo = shifted(lora_ref, prev_lora_ref, 0, mu_lora_ref[...])

    wd = lo[:, :d_decay]
    ad = lo[:, d_decay:d_decay + d_aaa]
    gd = lo[:, d_decay + d_aaa:]

    z = w0_ref[...] + _dot(jnp.tanh(wd), wup_ref[...])
    w = -(jnp.maximum(-z, 0.0) + jnp.log(1.0 + jnp.exp(-jnp.abs(z)))) - 0.5
    logd = -jnp.exp(w)
    a = jax.nn.sigmoid(a0_ref[...] + _dot(ad, aup_ref[...]))
    g = _dot(jax.nn.sigmoid(gd), gup_ref[...])

    li = lax.broadcasted_iota(jnp.int32, (LANES, LANES), 0) // N
    lj = lax.broadcasted_iota(jnp.int32, (LANES, LANES), 1) // N
    head_ones = (li == lj).astype(BF16)

    def head_sum(x):
        return jnp.concatenate(
            [_dot_split(x[:, c * LANES:(c + 1) * LANES], head_ones) for c in range(d_r // LANES)], axis=1)

    kkr = k * kk_ref[...]
    kk = kkr / jnp.maximum(jnp.sqrt(head_sum(kkr * kkr)), 1e-12)
    k2 = k * (1.0 + (a - 1.0) * ka_ref[...])
    a_s = -kk
    b_s = kk * a

    ri = lax.broadcasted_iota(jnp.int32, (R, R), 0)
    rj = lax.broadcasted_iota(jnp.int32, (R, R), 1)
    tri = ((rj <= ri) & ((ri // C) == (rj // C))).astype(BF16)
    hi = logd.astype(BF16)
    rem = logd - hi.astype(F32)
    mid = rem.astype(BF16)
    low = (rem - mid.astype(F32)).astype(BF16)
    cum = (jnp.dot(tri, hi, preferred_element_type=F32) + jnp.dot(tri, mid, preferred_element_type=F32)
           + jnp.dot(tri, low, preferred_element_type=F32))
    p_incl = jnp.exp(cum)
    p_excl = jnp.exp(cum - logd)
    p_inv = jnp.exp(-cum)
    p_last = [p_incl[(b + 1) * C - 1:(b + 1) * C, :] for b in range(nb)]
    p_last_rows = jnp.concatenate([jnp.broadcast_to(p, (C, d_r)) for p in p_last], axis=0)

    at = (a_s * p_excl).astype(BF16)
    rt = (r * p_incl).astype(BF16)
    bt = b_s * p_inv
    kt = k2 * p_inv
    bh = (bt * p_last_rows).astype(BF16)
    kh = (kt * p_last_rows).astype(BF16)
    bt = bt.astype(BF16)
    kt = kt.astype(BF16)
    vb = v.astype(BF16)

    ti = lax.broadcasted_iota(jnp.int32, (C, C), 0)
    tj = lax.broadcasted_iota(jnp.int32, (C, C), 1)
    strict = tj < ti
    incl = tj <= ti
    mask2 = jnp.concatenate([strict, incl], axis=0)
    eye = (ti == tj).astype(F32)
    level_masks = []
    s = 1
    while s < C:
        level_masks.append(((ti // (2 * s)) == (tj // (2 * s))) & (((ti // s) % 2) == 1) & (((tj // s) % 2) == 0))
        s *= 2

    units = [(b, h) for b in range(nb) for h in range(n_heads)]
    idx = range(len(units))
    rs = [slice(b * C, (b + 1) * C) for b, _ in units]
    ls = [slice(h * N, (h + 1) * N) for _, h in units]
    ar = [jnp.concatenate([at[rs[i], ls[i]], rt[rs[i], ls[i]]], axis=0) for i in idx]
    g_b = [jnp.where(mask2, _dot(ar[i], bt[rs[i], ls[i]], _NT), 0.0) for i in idx]
    g_k = [jnp.where(mask2, _dot(ar[i], kt[rs[i], ls[i]], _NT), 0.0) for i in idx]
    s0 = [state_ref[i] for i in idx]
    ars = [_dot(ar[i], s0[i], _NT) for i in idx]
    gv = [_dot(g_k[i], vb[rs[i], ls[i]]) for i in idx]
    tinv = [eye + jnp.where(level_masks[0], g_b[i][:C], 0.0) for i in idx]
    for lm in level_masks[1:]:
        xs = [_dot(jnp.where(lm, g_b[i][:C], 0.0), tinv[i]) for i in idx]
        tinv = [tinv[i] + _dot(tinv[i], xs[i]) for i in idx]
    u = [_dot(tinv[i], ars[i][:C] + gv[i][:C]) for i in idx]
    ys = [ars[i][C:] + gv[i][C:] + _dot(g_b[i][C:], u[i]) for i in idx]
    for i in idx:
        uv = jnp.concatenate([u[i].astype(BF16), vb[rs[i], ls[i]]], axis=0)
        bk = jnp.concatenate([bh[rs[i], ls[i]], kh[rs[i], ls[i]]], axis=0)
        state_ref[i] = s0[i] * p_last[units[i][0]][:, ls[i]] + _dot(uv, bk, _TN)
    y = jnp.concatenate([jnp.concatenate(ys[b * n_heads:(b + 1) * n_heads], axis=1) for b in range(nb)],
                        axis=0)

    inv_n = 1.0 / N
    mu_y = head_sum(y) * inv_n
    yc = y - mu_y
    var_y = head_sum(yc * yc) * inv_n
    yn = yc * lax.rsqrt(var_y + GN_EPS) * gng_ref[...] + gnb_ref[...]
    bonus = head_sum(r * k2 * rk_ref[...]) * v
    y_ref[...] = ((yn + bonus) * g).reshape(nb, C, d_r).astype(y_ref.dtype)


def rwkv_mixer(proj_main, proj_lora, col0, shift_mu, w0, w_up, a0, a_up, g_up, k_k, k_a, r_k, gn_g, gn_b,
               *, batch, chunk=64):
    nt = proj_main.shape[0]
    seq = nt // batch
    d_r = w0.shape[-1]
    n_heads = d_r // RWKV_HEAD
    d_lora = proj_lora.shape[1]
    d_decay, d_aaa = w_up.shape[0], a_up.shape[0]
    n_chunks = seq // chunk
    assert seq % chunk == 0 and col0 % d_r == 0
    cb = col0 // d_r
    row2 = lambda t: t.reshape(1, -1).astype(F32)
    mu_main = shift_mu[:3 * d_r].reshape(3, d_r)
    mu_lora = shift_mu[3 * d_r:].reshape(1, d_lora)
    pm3 = proj_main.reshape(batch, seq, proj_main.shape[1])
    pl3 = proj_lora.reshape(batch, seq, d_lora)

    def main_spec(j):
        return pl.BlockSpec((batch, chunk, d_r), lambda c, j=j: (0, c, cb + j))

    full = lambda shape: pl.BlockSpec(shape, lambda c: (0,) * len(shape))
    kern = functools.partial(_rwkv_kernel, n_heads=n_heads, d_decay=d_decay, d_aaa=d_aaa)
    y = pl.pallas_call(
        kern,
        grid=(n_chunks,),
        in_specs=[main_spec(0), main_spec(1), main_spec(2),
                  pl.BlockSpec((batch, chunk, d_lora), lambda c: (0, c, 0)),
                  full((3, d_r)), full((1, d_lora)), full((1, d_r)), full((d_decay, d_r)), full((1, d_r)),
                  full((d_aaa, d_r)), full((g_up.shape[0], d_r)), full((1, d_r)), full((1, d_r)), full((1, d_r)),
                  full((1, d_r)), full((1, d_r))],
        out_specs=pl.BlockSpec((batch, chunk, d_r), lambda c: (0, c, 0)),
        out_shape=jax.ShapeDtypeStruct((batch, seq, d_r), F32),
        scratch_shapes=[pltpu.VMEM((batch * n_heads, RWKV_HEAD, RWKV_HEAD), F32),
                        pltpu.VMEM((batch, 3, d_r), F32),
                        pltpu.VMEM((batch, 1, d_lora), F32)],
        compiler_params=pltpu.CompilerParams(dimension_semantics=("arbitrary",), vmem_limit_bytes=VMEM_LIMIT),
        name="rwkv_mixer",
    )(pm3, pm3, pm3, pl3, mu_main, mu_lora, row2(w0), w_up.astype(BF16), row2(a0),
      a_up.astype(BF16), g_up.astype(BF16), row2(k_k), row2(k_a), row2(r_k), row2(gn_g), row2(gn_b))
    return y.reshape(nt, d_r)


def _rms(x, g):
    return x * lax.rsqrt(jnp.mean(x * x, axis=-1, keepdims=True) + RMS_EPS) * g


def _inproj_kernel(x_ref, g_ref, w_ref, wl_ref, o_ref, ol_ref, u_ref):
    @pl.when(pl.program_id(1) == 0)
    def _():
        u = _rms(x_ref[...], g_ref[...]).astype(BF16)
        u_ref[...] = u
        ol_ref[...] = jnp.dot(u, wl_ref[...], preferred_element_type=F32)

    o_ref[...] = jnp.dot(u_ref[...], w_ref[...], preferred_element_type=F32)


def in_proj(x2, norm_g, w_in, d_main, *, tm=1024, tn=1024):
    nt, d = x2.shape
    d_lora = w_in.shape[1] - d_main
    assert nt % tm == 0 and d_main % tn == 0
    w_lora = w_in[:, d_main:]
    return pl.pallas_call(
        _inproj_kernel,
        grid=(nt // tm, d_main // tn),
        in_specs=[pl.BlockSpec((tm, d), lambda i, j: (i, 0)),
                  pl.BlockSpec((1, d), lambda i, j: (0, 0)),
                  pl.BlockSpec((d, tn), lambda i, j: (0, j)),
                  pl.BlockSpec((d, d_lora), lambda i, j: (0, 0))],
        out_specs=[pl.BlockSpec((tm, tn), lambda i, j: (i, j)),
                   pl.BlockSpec((tm, d_lora), lambda i, j: (i, 0))],
        out_shape=[jax.ShapeDtypeStruct((nt, d_main), F32), jax.ShapeDtypeStruct((nt, d_lora), F32)],
        scratch_shapes=[pltpu.VMEM((tm, d), BF16)],
        compiler_params=pltpu.CompilerParams(dimension_semantics=("arbitrary", "arbitrary"),
                                             vmem_limit_bytes=VMEM_LIMIT),
        name="in_proj",
    )(x2, norm_g.reshape(1, d).astype(F32), w_in, w_lora)


CONV_HALO = 32
CONV_ROWS = 32


def _conv_kernel(val_ref, gate_ref, dw_ref, b_ref, g_ref, beta_ref, o_ref, ubuf_ref):
    tt = val_ref.shape[0]
    d_c = val_ref.shape[1]
    t_idx = pl.program_id(1)

    @pl.when(t_idx == 0)
    def _():
        ubuf_ref[0, 0:CONV_HALO, :] = jnp.zeros((CONV_HALO, d_c), F32)

    @pl.when(t_idx > 0)
    def _():
        ubuf_ref[0, 0:CONV_HALO, :] = ubuf_ref[0, tt:tt + CONV_HALO, :]

    ubuf_ref[0, CONV_HALO:CONV_HALO + tt, :] = val_ref[...] * jax.nn.sigmoid(gate_ref[...])
    n_sh = tt + CONV_HALO - SUBLANES
    for s in range(1, SUBLANES):
        for r in range(0, n_sh, CONV_ROWS):
            n = min(CONV_ROWS, n_sh - r)
            ubuf_ref[s, r:r + n, :] = ubuf_ref[0, r + s:r + s + n, :]
    dw = dw_ref[...]
    lead = CONV_HALO - (CONV_WIDTH - 1)
    for c in range(tt // CONV_ROWS):
        r0 = c * CONV_ROWS
        acc = jnp.zeros((CONV_ROWS, d_c), F32)
        for j in range(CONV_WIDTH):
            q, s = divmod(lead + j, SUBLANES)
            r = r0 + q * SUBLANES
            acc = acc + dw[j:j + 1, :] * ubuf_ref[s, r:r + CONV_ROWS, :]
        acc = acc + b_ref[...]
        mu = jnp.mean(acc, axis=-1, keepdims=True)
        cen = acc - mu
        var = jnp.mean(cen * cen, axis=-1, keepdims=True)
        yv = cen * lax.rsqrt(var + LN_EPS) * g_ref[...] + beta_ref[...]
        o_ref[r0:r0 + CONV_ROWS, :] = yv * jax.nn.sigmoid(yv)


def conv_mixer(proj_main, dw, bias, ln_g, ln_b, *, batch, tt=256):
    nt = proj_main.shape[0]
    seq = nt // batch
    d_c = dw.shape[1]
    n_t = seq // tt
    assert seq % tt == 0 and tt % CONV_ROWS == 0
    row = lambda t: t.reshape(1, d_c).astype(F32)
    full = lambda shape: pl.BlockSpec(shape, lambda b, t: (0, 0))
    return pl.pallas_call(
        _conv_kernel,
        grid=(batch, n_t),
        in_specs=[pl.BlockSpec((tt, d_c), lambda b, t: (b * n_t + t, 0)),
                  pl.BlockSpec((tt, d_c), lambda b, t: (b * n_t + t, 1)),
                  full((CONV_WIDTH, d_c)), full((1, d_c)), full((1, d_c)), full((1, d_c))],
        out_specs=pl.BlockSpec((tt, d_c), lambda b, t: (b * n_t + t, 0)),
        out_shape=jax.ShapeDtypeStruct((nt, d_c), F32),
        scratch_shapes=[pltpu.VMEM((SUBLANES, tt + CONV_HALO, d_c), F32)],
        compiler_params=pltpu.CompilerParams(dimension_semantics=("arbitrary", "arbitrary"),
                                             vmem_limit_bytes=VMEM_LIMIT),
        name="conv_mixer",
    )(proj_main, proj_main, dw.astype(F32), row(bias), row(ln_g), row(ln_b))


def _out_router_kernel(x_ref, yc_ref, yr_ref, wc_ref, wr_ref, nf_ref, rw_hi_ref, rw_lo_ref, rb_ref,
                       h_ref, v_ref, eid_ref, gate_ref):
    h = (x_ref[...] + jnp.dot(yc_ref[...].astype(BF16), wc_ref[...], preferred_element_type=F32)
         + jnp.dot(yr_ref[...].astype(BF16), wr_ref[...], preferred_element_type=F32))
    h_ref[...] = h
    v = _rms(h, nf_ref[...])
    v_ref[...] = _pack_bf16_pairs(v)
    v_hi = v.astype(BF16)
    v_lo = (v - v_hi.astype(F32)).astype(BF16)
    logits = (jnp.dot(v_hi, rw_hi_ref[...], preferred_element_type=F32)
              + jnp.dot(v_lo, rw_hi_ref[...], preferred_element_type=F32)
              + jnp.dot(v_hi, rw_lo_ref[...], preferred_element_type=F32)) + rb_ref[...]
    lane = lax.broadcasted_iota(jnp.int32, logits.shape, 1)
    neg = jnp.float32(-jnp.inf)
    big = jnp.int32(LANES)

    def first_max(vals):
        m = jnp.max(vals, axis=-1, keepdims=True)
        return m, jnp.min(jnp.where(vals == m, lane, big), axis=-1, keepdims=True)

    gl = jnp.where(lane < N_GROUPS, logits, neg)
    gmax, grp = first_max(gl)
    p_grp = 1.0 / jnp.sum(jnp.exp(gl - gmax), axis=-1, keepdims=True)
    lo = N_GROUPS + grp * EXPERTS_PER_GROUP
    el = jnp.where((lane >= lo) & (lane < lo + EXPERTS_PER_GROUP), logits, neg)
    m1, i1 = first_max(el)
    m2, i2 = first_max(jnp.where(lane == i1, neg, el))
    e2 = jnp.exp(m2 - m1)
    g1 = p_grp / (1.0 + e2)
    g2 = p_grp * e2 / (1.0 + e2)
    eid_ref[...] = jnp.where(lane == 0, i1 - N_GROUPS, jnp.where(lane == 1, i2 - N_GROUPS, 0))
    gate_ref[...] = jnp.where(lane == 0, g1, jnp.where(lane == 1, g2, 0.0))


def out_router(x2, y_conv, y_rwkv, w_out, norm_ffn, rg_w, rg_b, re_w, re_b, *, tm=256):
    nt, d = x2.shape
    d_c, d_r = y_conv.shape[1], y_rwkv.shape[1]
    assert d_c == d_r and nt % tm == 0
    n_r = N_GROUPS + N_EXPERTS
    rw = jnp.zeros((d, LANES), F32).at[:, :n_r].set(jnp.concatenate([rg_w, re_w], axis=1).astype(F32))
    rw_hi = rw.astype(BF16)
    rw_lo = (rw - rw_hi.astype(F32)).astype(BF16)
    rb = jnp.zeros((1, LANES), F32).at[0, :n_r].set(jnp.concatenate([rg_b, re_b]).astype(F32))
    full = lambda shape: pl.BlockSpec(shape, lambda i: (0, 0))
    rows = lambda w: pl.BlockSpec((tm, w), lambda i: (i, 0))
    return pl.pallas_call(
        _out_router_kernel,
        grid=(nt // tm,),
        in_specs=[rows(d), rows(d_c), rows(d_r),
                  pl.BlockSpec((d_c, d), lambda i: (0, 0)), pl.BlockSpec((d_r, d), lambda i: (1, 0)),
                  full((1, d)), full((d, LANES)), full((d, LANES)), full((1, LANES))],
        out_specs=[rows(d), rows(d // 2), rows(LANES), rows(LANES)],
        out_shape=[jax.ShapeDtypeStruct((nt, d), F32), jax.ShapeDtypeStruct((nt, d // 2), jnp.uint32),
                   jax.ShapeDtypeStruct((nt, LANES), jnp.int32), jax.ShapeDtypeStruct((nt, LANES), F32)],
        compiler_params=pltpu.CompilerParams(dimension_semantics=("arbitrary",), vmem_limit_bytes=VMEM_LIMIT),
        name="out_router",
    )(x2, y_conv, y_rwkv, w_out, w_out, norm_ffn.reshape(1, d).astype(F32), rw_hi, rw_lo, rb)


MOE_ROWS = 256


def slot_plan(expert_id, n_rows_pad):
    flat_e = expert_id.reshape(-1)
    onehot = (flat_e[:, None] == jnp.arange(N_EXPERTS, dtype=jnp.int32)[None, :]).astype(jnp.int32)
    csum = jnp.cumsum(onehot, axis=0)
    rank = jnp.sum(csum * onehot, axis=1) - 1
    counts = csum[-1]
    padded = (counts + MOE_ROWS - 1) // MOE_ROWS * MOE_ROWS
    pend = jnp.cumsum(padded)
    pstarts = pend - padded
    dest = (pstarts[flat_e] + rank).astype(jnp.int32).reshape(expert_id.shape)
    n_blocks = n_rows_pad // MOE_ROWS
    block_e = jnp.minimum(
        jnp.searchsorted(pend, jnp.arange(n_blocks, dtype=jnp.int32) * MOE_ROWS, side='right'),
        N_EXPERTS - 1).astype(jnp.int32)
    n_used = (pend[-1:] // MOE_ROWS).astype(jnp.int32)
    return dest, block_e, n_used


def _dispatch_kernel(dest_ref, v_ref, xs_in_ref, xs_ref, sem):
    del xs_in_ref
    tm = v_ref.shape[0]

    def copy(i, k):
        d = dest_ref[0, 0, i * TOP_K + k]
        return pltpu.make_async_copy(v_ref.at[pl.ds(i, 1), :], xs_ref.at[pl.ds(d, 1), :], sem)

    def start(i, carry):
        for k in range(TOP_K):
            copy(i, k).start(priority=k)
        return carry

    def wait(i, carry):
        for k in range(TOP_K):
            copy(i, k).wait()
        return carry

    lax.fori_loop(0, tm, start, 0, unroll=8)
    lax.fori_loop(0, tm, wait, 0, unroll=8)


def dispatch(v, dest, n_rows_pad, *, tm=256):
    nt, d = v.shape
    assert nt % tm == 0
    dest3 = dest.reshape(nt // tm, 1, tm * TOP_K)
    xs0 = jnp.zeros((n_rows_pad, d), v.dtype)
    return pl.pallas_call(
        _dispatch_kernel,
        grid=(nt // tm,),
        in_specs=[pl.BlockSpec((1, 1, tm * TOP_K), lambda i: (i, 0, 0), memory_space=pltpu.SMEM),
                  pl.BlockSpec((tm, d), lambda i: (i, 0)),
                  pl.BlockSpec(memory_space=pl.ANY)],
        out_specs=pl.BlockSpec(memory_space=pl.ANY),
        out_shape=jax.ShapeDtypeStruct((n_rows_pad, d), v.dtype),
        scratch_shapes=[pltpu.SemaphoreType.DMA(())],
        input_output_aliases={2: 0},
        compiler_params=pltpu.CompilerParams(dimension_semantics=("arbitrary",), vmem_limit_bytes=VMEM_LIMIT),
        name="moe_dispatch",
    )(dest3, v, xs0)


def _experts_kernel(be_ref, nu_ref, xs_ref, wg_ref, wu_ref, wd_ref, ys_ref, wg_s, wu_s, wd_s):
    j = pl.program_id(0)
    prev = be_ref[jnp.maximum(j - 1, 0)]

    @pl.when((j == 0) | (be_ref[j] != prev))
    def _():
        wg_s[...] = wg_ref[...].astype(BF16)
        wu_s[...] = wu_ref[...].astype(BF16)
        wd_s[...] = wd_ref[...].astype(BF16)

    @pl.when(j < nu_ref[0])
    def _():
        xb = _unpack_bf16_pairs(xs_ref[...]).astype(BF16)
        gt = jnp.dot(xb, wg_s[...], preferred_element_type=F32)
        up = jnp.dot(xb, wu_s[...], preferred_element_type=F32)
        hid = (gt * jax.nn.sigmoid(gt) * up).astype(BF16)
        ys_ref[...] = _pack_bf16_pairs(jnp.dot(hid, wd_s[...], preferred_element_type=F32))

    @pl.when(j >= nu_ref[0])
    def _():
        ys_ref[...] = jnp.zeros_like(ys_ref)


def experts(xs, block_e, n_used, w_gate, w_up, w_down, n_rows_pad):
    dh = xs.shape[1]
    d, d_e = w_gate.shape[1], w_gate.shape[2]
    assert d == 2 * dh
    n_blocks = n_rows_pad // MOE_ROWS
    grid_spec = pltpu.PrefetchScalarGridSpec(
        num_scalar_prefetch=2,
        grid=(n_blocks,),
        in_specs=[pl.BlockSpec((MOE_ROWS, dh), lambda j, be, nu: (jnp.minimum(j, nu[0] - 1), 0)),
                  pl.BlockSpec((None, d, d_e), lambda j, be, nu: (be[j], 0, 0)),
                  pl.BlockSpec((None, d, d_e), lambda j, be, nu: (be[j], 0, 0)),
                  pl.BlockSpec((None, d_e, d), lambda j, be, nu: (be[j], 0, 0))],
        out_specs=pl.BlockSpec((MOE_ROWS, dh), lambda j, be, nu: (j, 0)),
        scratch_shapes=[pltpu.VMEM((d, d_e), BF16), pltpu.VMEM((d, d_e), BF16), pltpu.VMEM((d_e, d), BF16)],
    )
    return pl.pallas_call(
        _experts_kernel,
        grid_spec=grid_spec,
        out_shape=jax.ShapeDtypeStruct((n_rows_pad, dh), jnp.uint32),
        compiler_params=pltpu.CompilerParams(dimension_semantics=("arbitrary",), vmem_limit_bytes=VMEM_LIMIT),
        name="moe_experts",
    )(block_e, n_used, xs, w_gate, w_up, w_down)


def _combine_kernel(dest_ref, h_ref, gate_ref, nfin_ref, ys_ref, o_ref, ybuf, sem):
    tm = h_ref.shape[0]

    def copy(i, k):
        d = dest_ref[0, 0, i * TOP_K + k]
        return pltpu.make_async_copy(ys_ref.at[pl.ds(d, 1), :], ybuf.at[k, pl.ds(i, 1), :], sem)

    def start(i, carry):
        for k in range(TOP_K):
            copy(i, k).start(priority=k)
        return carry

    def wait(i, carry):
        for k in range(TOP_K):
            copy(i, k).wait()
        return carry

    lax.fori_loop(0, tm, start, 0, unroll=8)
    lax.fori_loop(0, tm, wait, 0, unroll=8)
    gates = gate_ref[...]
    moe = gates[:, 0:1] * _unpack_bf16_pairs(ybuf[0]) + gates[:, 1:2] * _unpack_bf16_pairs(ybuf[1])
    o_ref[...] = _rms(h_ref[...] + moe, nfin_ref[...])


def combine(h, gates, dest, ys, norm_final, *, tm=256):
    nt, d = h.shape
    assert nt % tm == 0 and ys.shape[1] * 2 == d
    dest3 = dest.reshape(nt // tm, 1, tm * TOP_K)
    return pl.pallas_call(
        _combine_kernel,
        grid=(nt // tm,),
        in_specs=[pl.BlockSpec((1, 1, tm * TOP_K), lambda i: (i, 0, 0), memory_space=pltpu.SMEM),
                  pl.BlockSpec((tm, d), lambda i: (i, 0)),
                  pl.BlockSpec((tm, LANES), lambda i: (i, 0)),
                  pl.BlockSpec((1, d), lambda i: (0, 0)),
                  pl.BlockSpec(memory_space=pl.ANY)],
        out_specs=pl.BlockSpec((tm, d), lambda i: (i, 0)),
        out_shape=jax.ShapeDtypeStruct((nt, d), F32),
        scratch_shapes=[pltpu.VMEM((TOP_K, tm, d // 2), jnp.uint32), pltpu.SemaphoreType.DMA(())],
        compiler_params=pltpu.CompilerParams(dimension_semantics=("arbitrary",), vmem_limit_bytes=VMEM_LIMIT),
        name="moe_combine",
    )(dest3, h, gates, norm_final.reshape(1, d).astype(F32), ys)


def kernel(x, norm_mix, w_in, conv_dw, conv_b, conv_ln_g, conv_ln_b, shift_mu, w0, w_lora_up, a0, a_lora_up, g_lora_up, k_k, k_a, r_k, gn_g, gn_b, w_out, norm_ffn, router_group_w, router_group_b, router_expert_w, router_expert_b, expert_w_gate, expert_w_up, expert_w_down, norm_final):
    B, T, D = x.shape
    depth = w_in.shape[0]
    d_c = conv_dw.shape[2]
    d_r = w0.shape[1]
    d_main = 2 * d_c + 3 * d_r
    nt = B * T
    n_rows_pad = -(-(nt * TOP_K + N_EXPERTS * (MOE_ROWS - 1)) // MOE_ROWS) * MOE_ROWS
    h = x.reshape(nt, D)
    for l in range(depth):
        proj_main, proj_lora = in_proj(h, norm_mix[l], w_in[l].astype(BF16), d_main)
        y_conv = conv_mixer(proj_main, conv_dw[l], conv_b[l], conv_ln_g[l], conv_ln_b[l], batch=B)
        y_rwkv = rwkv_mixer(proj_main, proj_lora, 2 * d_c, shift_mu[l], w0[l], w_lora_up[l], a0[l], a_lora_up[l],
                            g_lora_up[l], k_k[l], k_a[l], r_k[l].reshape(-1), gn_g[l], gn_b[l], batch=B)
        h, v, eid, gates = out_router(h, y_conv, y_rwkv, w_out[l].astype(BF16), norm_ffn[l], router_group_w[l],
                                      router_group_b[l], router_expert_w[l], router_expert_b[l])
        dest, block_e, n_used = slot_plan(eid[:, :TOP_K], n_rows_pad)
        xs = dispatch(v, dest, n_rows_pad)
        ys = experts(xs, block_e, n_used, expert_w_gate[l], expert_w_up[l], expert_w_down[l], n_rows_pad)
        assert depth == 1
        h = combine(h, gates, dest, ys, norm_final)
    return h.reshape(B, T, D)
```

```python
import functools

import jax
import jax.numpy as jnp
from jax import lax
from jax.experimental import pallas as pl
from jax.experimental.pallas import tpu as pltpu

F32 = jnp.float32
BF16 = jnp.bfloat16

RWKV_HEAD = 64
CONV_WIDTH = 31
N_GROUPS = 4
EXPERTS_PER_GROUP = 8
N_EXPERTS = N_GROUPS * EXPERTS_PER_GROUP
TOP_K = 2
RMS_EPS = 1e-6
LN_EPS = 1e-5
GN_EPS = 64e-5
LANES = 128
SUBLANES = 8
VMEM_LIMIT = 56 * 1024 * 1024

_NT = (((1,), (1,)), ((), ()))
_TN = (((0,), (0,)), ((), ()))


def _dot(a, b, dims=None):
    a = a.astype(BF16)
    b = b.astype(BF16)
    if dims is None:
        return jnp.dot(a, b, preferred_element_type=F32)
    return lax.dot_general(a, b, dims, preferred_element_type=F32)


def _dot_split(x, m):
    hi = x.astype(BF16)
    lo = (x - hi.astype(F32)).astype(BF16)
    return (jnp.dot(hi, m, preferred_element_type=F32)
            + jnp.dot(lo, m, preferred_element_type=F32))


def _pack_bf16_pairs(x):
    w = x.shape[1] // 2
    lo = lax.bitcast_convert_type(x[:, :w].astype(BF16).astype(F32), jnp.uint32)
    hi = lax.bitcast_convert_type(x[:, w:].astype(BF16).astype(F32), jnp.uint32)
    return (lo >> 16) | (hi & jnp.uint32(0xFFFF0000))


def _unpack_bf16_pairs(p):
    lo = lax.bitcast_convert_type(p << 16, F32)
    hi = lax.bitcast_convert_type(p & jnp.uint32(0xFFFF0000), F32)
    return jnp.concatenate([lo, hi], axis=1)


def _rwkv_kernel(r_ref, k_ref, v_ref, lora_ref, mu_main_ref, mu_lora_ref, w0_ref, wup_ref, a0_ref, aup_ref,
                 gup_ref, kk_ref, ka_ref, rk_ref, gng_ref, gnb_ref, y_ref,
                 state_ref, prev_main_ref, prev_lora_ref, *, n_heads, d_decay, d_aaa):
    nb, C, d_r = r_ref.shape
    N = RWKV_HEAD
    R = nb * C

    @pl.when(pl.program_id(0) == 0)
    def _():
        state_ref[...] = jnp.zeros_like(state_ref)
        prev_main_ref[...] = jnp.zeros_like(prev_main_ref)
        prev_lora_ref[...] = jnp.zeros_like(prev_lora_ref)

    row = lax.broadcasted_iota(jnp.int32, (R, 1), 0)

    def shifted(x_ref, prev_ref, j, mu):
        x = x_ref[...].reshape(R, x_ref.shape[2])
        xp = pltpu.roll(x, 1, 0)
        for b in range(nb):
            xp = jnp.where(row == b * C, prev_ref[b, j:j + 1, :], xp)
            prev_ref[b, j:j + 1, :] = x[(b + 1) * C - 1:(b + 1) * C, :]
        return x + (xp - x) * mu

    r = shifted(r_ref, prev_main_ref, 0, mu_main_ref[0:1, :])
    k = shifted(k_ref, prev_main_ref, 1, mu_main_ref[1:2, :])
    v = shifted(v_ref, prev_main_ref, 2, mu_main_ref[2:3, :])
    lo = shifted(lora_ref, prev_lora_ref, 0, mu_lora_ref[...])

    wd = lo[:, :d_decay]
    ad = lo[:, d_decay:d_decay + d_aaa]
    gd = lo[:, d_decay + d_aaa:]

    z = w0_ref[...] + _dot(jnp.tanh(wd), wup_ref[...])
    w = -(jnp.maximum(-z, 0.0) + jnp.log(1.0 + jnp.exp(-jnp.abs(z)))) - 0.5
    logd = -jnp.exp(w)
    a = jax.nn.sigmoid(a0_ref[...] + _dot(ad, aup_ref[...]))
    g = _dot(jax.nn.sigmoid(gd), gup_ref[...])

    li = lax.broadcasted_iota(jnp.int32, (LANES, LANES), 0) // N
    lj = lax.broadcasted_iota(jnp.int32, (LANES, LANES), 1) // N
    head_ones = (li == lj).astype(BF16)

    def head_sum(x):
        return jnp.concatenate(
            [_dot_split(x[:, c * LANES:(c + 1) * LANES], head_ones) for c in range(d_r // LANES)], axis=1)

    kkr = k * kk_ref[...]
    kk = kkr / jnp.maximum(jnp.sqrt(head_sum(kkr * kkr)), 1e-12)
    k2 = k * (1.0 + (a - 1.0) * ka_ref[...])
    a_s = -kk
    b_s = kk * a

    ri = lax.broadcasted_iota(jnp.int32, (R, R), 0)
    rj = lax.broadcasted_iota(jnp.int32, (R, R), 1)
    tri = ((rj <= ri) & ((ri // C) == (rj // C))).astype(BF16)
    hi = logd.astype(BF16)
    rem = logd - hi.astype(F32)
    mid = rem.astype(BF16)
    low = (rem - mid.astype(F32)).astype(BF16)
    cum = (jnp.dot(tri, hi, preferred_element_type=F32) + jnp.dot(tri, mid, preferred_element_type=F32)
           + jnp.dot(tri, low, preferred_element_type=F32))
    p_incl = jnp.exp(cum)
    p_excl = jnp.exp(cum - logd)
    p_inv = jnp.exp(-cum)
    p_last = [p_incl[(b + 1) * C - 1:(b + 1) * C, :] for b in range(nb)]
    p_last_rows = jnp.concatenate([jnp.broadcast_to(p, (C, d_r)) for p in p_last], axis=0)

    at = (a_s * p_excl).astype(BF16)
    rt = (r * p_incl).astype(BF16)
    bt = b_s * p_inv
    kt = k2 * p_inv
    bh = (bt * p_last_rows).astype(BF16)
    kh = (kt * p_last_rows).astype(BF16)
    bt = bt.astype(BF16)
    kt = kt.astype(BF16)
    vb = v.astype(BF16)

    ti = lax.broadcasted_iota(jnp.int32, (C, C), 0)
    tj = lax.broadcasted_iota(jnp.int32, (C, C), 1)
    strict = tj < ti
    incl = tj <= ti
    mask2 = jnp.concatenate([strict, incl], axis=0)
    eye = (ti == tj).astype(F32)
    level_masks = []
    s = 1
    while s < C:
        level_masks.append(((ti // (2 * s)) == (tj // (2 * s))) & (((ti // s) % 2) == 1) & (((tj // s) % 2) == 0))
        s *= 2

    units = [(b, h) for b in range(nb) for h in range(n_heads)]
    idx = range(len(units))
    rs = [slice(b * C, (b + 1) * C) for b, _ in units]
    ls = [slice(h * N, (h + 1) * N) for _, h in units]
    ar = [jnp.concatenate([at[rs[i], ls[i]], rt[rs[i], ls[i]]], axis=0) for i in idx]
    g_b = [jnp.where(mask2, _dot(ar[i], bt[rs[i], ls[i]], _NT), 0.0) for i in idx]
    g_k = [jnp.where(mask2, _dot(ar[i], kt[rs[i], ls[i]], _NT), 0.0) for i in idx]
    s0 = [state_ref[i] for i in idx]
    ars = [_dot(ar[i], s0[i], _NT) for i in idx]
    gv = [_dot(g_k[i], vb[rs[i], ls[i]]) for i in idx]
    tinv = [eye + jnp.where(level_masks[0], g_b[i][:C], 0.0) for i in idx]
    for lm in level_masks[1:]:
        xs = [_dot(jnp.where(lm, g_b[i][:C], 0.0), tinv[i]) for i in idx]
        tinv = [tinv[i] + _dot(tinv[i], xs[i]) for i in idx]
    u = [_dot(tinv[i], ars[i][:C] + gv[i][:C]) for i in idx]
    ys = [ars[i][C:] + gv[i][C:] + _dot(g_b[i][C:], u[i]) for i in idx]
    for i in idx:
        uv = jnp.concatenate([u[i].astype(BF16), vb[rs[i], ls[i]]], axis=0)
        bk = jnp.concatenate([bh[rs[i], ls[i]], kh[rs[i], ls[i]]], axis=0)
        state_ref[i] = s0[i] * p_last[units[i][0]][:, ls[i]] + _dot(uv, bk, _TN)
    y = jnp.concatenate([jnp.concatenate(ys[b * n_heads:(b + 1) * n_heads], axis=1) for b in range(nb)],
                        axis=0)

    inv_n = 1.0 / N
    mu_y = head_sum(y) * inv_n
    yc = y - mu_y
    var_y = head_sum(yc * yc) * inv_n
    yn = yc * lax.rsqrt(var_y + GN_EPS) * gng_ref[...] + gnb_ref[...]
    bonus = head_sum(r * k2 * rk_ref[...]) * v
    y_ref[...] = ((yn + bonus) * g).reshape(nb, C, d_r).astype(y_ref.dtype)


def rwkv_mixer(proj_main, proj_lora, col0, shift_mu, w0, w_up, a0, a_up, g_up, k_k, k_a, r_k, gn_g, gn_b,
               *, batch, chunk=64):
    nt = proj_main.shape[0]
    seq = nt // batch
    d_r = w0.shape[-1]
    n_heads = d_r // RWKV_HEAD
    d_lora = proj_lora.shape[1]
    d_decay, d_aaa = w_up.shape[0], a_up.shape[0]
    n_chunks = seq // chunk
    assert seq % chunk == 0 and col0 % d_r == 0
    cb = col0 // d_r
    row2 = lambda t: t.reshape(1, -1).astype(F32)
    mu_main = shift_mu[:3 * d_r].reshape(3, d_r)
    mu_lora = shift_mu[3 * d_r:].reshape(1, d_lora)
    pm3 = proj_main.reshape(batch, seq, proj_main.shape[1])
    pl3 = proj_lora.reshape(batch, seq, d_lora)

    def main_spec(j):
        return pl.BlockSpec((batch, chunk, d_r), lambda c, j=j: (0, c, cb + j))

    full = lambda shape: pl.BlockSpec(shape, lambda c: (0,) * len(shape))
    kern = functools.partial(_rwkv_kernel, n_heads=n_heads, d_decay=d_decay, d_aaa=d_aaa)
    y = pl.pallas_call(
        kern,
        grid=(n_chunks,),
        in_specs=[main_spec(0), main_spec(1), main_spec(2),
                  pl.BlockSpec((batch, chunk, d_lora), lambda c: (0, c, 0)),
                  full((3, d_r)), full((1, d_lora)), full((1, d_r)), full((d_decay, d_r)), full((1, d_r)),
                  full((d_aaa, d_r)), full((g_up.shape[0], d_r)), full((1, d_r)), full((1, d_r)), full((1, d_r)),
                  full((1, d_r)), full((1, d_r))],
        out_specs=pl.BlockSpec((batch, chunk, d_r), lambda c: (0, c, 0)),
        out_shape=jax.ShapeDtypeStruct((batch, seq, d_r), F32),
        scratch_shapes=[pltpu.VMEM((batch * n_heads, RWKV_HEAD, RWKV_HEAD), F32),
                        pltpu.VMEM((batch, 3, d_r), F32),
                        pltpu.VMEM((batch, 1, d_lora), F32)],
        compiler_params=pltpu.CompilerParams(dimension_semantics=("arbitrary",), vmem_limit_bytes=VMEM_LIMIT),
        name="rwkv_mixer",
    )(pm3, pm3, pm3, pl3, mu_main, mu_lora, row2(w0), w_up.astype(BF16), row2(a0),
      a_up.astype(BF16), g_up.astype(BF16), row2(k_k), row2(k_a), row2(r_k), row2(gn_g), row2(gn_b))
    return y.reshape(nt, d_r)


def _rms(x, g):
    return x * lax.rsqrt(jnp.mean(x * x, axis=-1, keepdims=True) + RMS_EPS) * g


def _inproj_kernel(x_ref, g_ref, w_ref, wl_ref, o_ref, ol_ref, u_ref):
    @pl.when(pl.program_id(1) == 0)
    def _():
        u = _rms(x_ref[...], g_ref[...]).astype(BF16)
        u_ref[...] = u
        ol_ref[...] = jnp.dot(u, wl_ref[...], preferred_element_type=F32)

    o_ref[...] = jnp.dot(u_ref[...], w_ref[...].astype(BF16), preferred_element_type=F32)


def in_proj(x2, norm_g, w_in, d_main, *, tm=1024, tn=1024):
    nt, d = x2.shape
    d_lora = w_in.shape[1] - d_main
    assert nt % tm == 0 and d_main % tn == 0
    w_lora = w_in[:, d_main:].astype(BF16)
    return pl.pallas_call(
        _inproj_kernel,
        grid=(nt // tm, d_main // tn),
        in_specs=[pl.BlockSpec((tm, d), lambda i, j: (i, 0)),
                  pl.BlockSpec((1, d), lambda i, j: (0, 0)),
                  pl.BlockSpec((d, tn), lambda i, j: (0, j)),
                  pl.BlockSpec((d, d_lora), lambda i, j: (0, 0))],
        out_specs=[pl.BlockSpec((tm, tn), lambda i, j: (i, j)),
                   pl.BlockSpec((tm, d_lora), lambda i, j: (i, 0))],
        out_shape=[jax.ShapeDtypeStruct((nt, d_main), F32), jax.ShapeDtypeStruct((nt, d_lora), F32)],
        scratch_shapes=[pltpu.VMEM((tm, d), BF16)],
        compiler_params=pltpu.CompilerParams(dimension_semantics=("arbitrary", "arbitrary"),
                                             vmem_limit_bytes=VMEM_LIMIT),
        name="in_proj",
    )(x2, norm_g.reshape(1, d).astype(F32), w_in, w_lora)


CONV_HALO = 32
CONV_ROWS = 32


def _conv_kernel(val_ref, gate_ref, dw_ref, b_ref, g_ref, beta_ref, o_ref, ubuf_ref):
    tt = val_ref.shape[0]
    d_c = val_ref.shape[1]
    t_idx = pl.program_id(1)

    @pl.when(t_idx == 0)
    def _():
        ubuf_ref[0, 0:CONV_HALO, :] = jnp.zeros((CONV_HALO, d_c), F32)

    @pl.when(t_idx > 0)
    def _():
        ubuf_ref[0, 0:CONV_HALO, :] = ubuf_ref[0, tt:tt + CONV_HALO, :]

    ubuf_ref[0, CONV_HALO:CONV_HALO + tt, :] = val_ref[...] * jax.nn.sigmoid(gate_ref[...])
    n_sh = tt + CONV_HALO - SUBLANES
    for s in range(1, SUBLANES):
        for r in range(0, n_sh, CONV_ROWS):
            n = min(CONV_ROWS, n_sh - r)
            ubuf_ref[s, r:r + n, :] = ubuf_ref[0, r + s:r + s + n, :]
    dw = dw_ref[...]
    lead = CONV_HALO - (CONV_WIDTH - 1)
    for c in range(tt // CONV_ROWS):
        r0 = c * CONV_ROWS
        acc = jnp.zeros((CONV_ROWS, d_c), F32)
        for j in range(CONV_WIDTH):
            q, s = divmod(lead + j, SUBLANES)
            r = r0 + q * SUBLANES
            acc = acc + dw[j:j + 1, :] * ubuf_ref[s, r:r + CONV_ROWS, :]
        acc = acc + b_ref[...]
        mu = jnp.mean(acc, axis=-1, keepdims=True)
        cen = acc - mu
        var = jnp.mean(cen * cen, axis=-1, keepdims=True)
        yv = cen * lax.rsqrt(var + LN_EPS) * g_ref[...] + beta_ref[...]
        o_ref[r0:r0 + CONV_ROWS, :] = yv * jax.nn.sigmoid(yv)


def conv_mixer(proj_main, dw, bias, ln_g, ln_b, *, batch, tt=256):
    nt = proj_main.shape[0]
    seq = nt // batch
    d_c = dw.shape[1]
    n_t = seq // tt
    assert seq % tt == 0 and tt % CONV_ROWS == 0
    row = lambda t: t.reshape(1, d_c).astype(F32)
    full = lambda shape: pl.BlockSpec(shape, lambda b, t: (0, 0))
    return pl.pallas_call(
        _conv_kernel,
        grid=(batch, n_t),
        in_specs=[pl.BlockSpec((tt, d_c), lambda b, t: (b * n_t + t, 0)),
                  pl.BlockSpec((tt, d_c), lambda b, t: (b * n_t + t, 1)),
                  full((CONV_WIDTH, d_c)), full((1, d_c)), full((1, d_c)), full((1, d_c))],
        out_specs=pl.BlockSpec((tt, d_c), lambda b, t: (b * n_t + t, 0)),
        out_shape=jax.ShapeDtypeStruct((nt, d_c), F32),
        scratch_shapes=[pltpu.VMEM((SUBLANES, tt + CONV_HALO, d_c), F32)],
        compiler_params=pltpu.CompilerParams(dimension_semantics=("arbitrary", "arbitrary"),
                                             vmem_limit_bytes=VMEM_LIMIT),
        name="conv_mixer",
    )(proj_main, proj_main, dw.astype(F32), row(bias), row(ln_g), row(ln_b))


def _out_router_kernel(x_ref, yc_ref, yr_ref, wc_ref, wr_ref, nf_ref, rw_hi_ref, rw_lo_ref, rb_ref,
                       h_ref, v_ref, eid_ref, gate_ref):
    h = (x_ref[...] + jnp.dot(yc_ref[...].astype(BF16), wc_ref[...], preferred_element_type=F32)
         + jnp.dot(yr_ref[...].astype(BF16), wr_ref[...], preferred_element_type=F32))
    h_ref[...] = h
    v = _rms(h, nf_ref[...])
    v_ref[...] = _pack_bf16_pairs(v)
    v_hi = v.astype(BF16)
    v_lo = (v - v_hi.astype(F32)).astype(BF16)
    logits = (jnp.dot(v_hi, rw_hi_ref[...], preferred_element_type=F32)
              + jnp.dot(v_lo, rw_hi_ref[...], preferred_element_type=F32)
              + jnp.dot(v_hi, rw_lo_ref[...], preferred_element_type=F32)) + rb_ref[...]
    lane = lax.broadcasted_iota(jnp.int32, logits.shape, 1)
    neg = jnp.float32(-jnp.inf)
    big = jnp.int32(LANES)

    def first_max(vals):
        m = jnp.max(vals, axis=-1, keepdims=True)
        return m, jnp.min(jnp.where(vals == m, lane, big), axis=-1, keepdims=True)

    gl = jnp.where(lane < N_GROUPS, logits, neg)
    gmax, grp = first_max(gl)
    p_grp = 1.0 / jnp.sum(jnp.exp(gl - gmax), axis=-1, keepdims=True)
    lo = N_GROUPS + grp * EXPERTS_PER_GROUP
    el = jnp.where((lane >= lo) & (lane < lo + EXPERTS_PER_GROUP), logits, neg)
    m1, i1 = first_max(el)
    m2, i2 = first_max(jnp.where(lane == i1, neg, el))
    e2 = jnp.exp(m2 - m1)
    g1 = p_grp / (1.0 + e2)
    g2 = p_grp * e2 / (1.0 + e2)
    eid_ref[...] = jnp.where(lane == 0, i1 - N_GROUPS, jnp.where(lane == 1, i2 - N_GROUPS, 0))
    gate_ref[...] = jnp.where(lane == 0, g1, jnp.where(lane == 1, g2, 0.0))


def out_router(x2, y_conv, y_rwkv, w_out, norm_ffn, rg_w, rg_b, re_w, re_b, *, tm=256):
    nt, d = x2.shape
    d_c, d_r = y_conv.shape[1], y_rwkv.shape[1]
    assert d_c == d_r and nt % tm == 0
    n_r = N_GROUPS + N_EXPERTS
    rw = jnp.zeros((d, LANES), F32).at[:, :n_r].set(jnp.concatenate([rg_w, re_w], axis=1).astype(F32))
    rw_hi = rw.astype(BF16)
    rw_lo = (rw - rw_hi.astype(F32)).astype(BF16)
    rb = jnp.zeros((1, LANES), F32).at[0, :n_r].set(jnp.concatenate([rg_b, re_b]).astype(F32))
    full = lambda shape: pl.BlockSpec(shape, lambda i: (0, 0))
    rows = lambda w: pl.BlockSpec((tm, w), lambda i: (i, 0))
    return pl.pallas_call(
        _out_router_kernel,
        grid=(nt // tm,),
        in_specs=[rows(d), rows(d_c), rows(d_r),
                  pl.BlockSpec((d_c, d), lambda i: (0, 0)), pl.BlockSpec((d_r, d), lambda i: (1, 0)),
                  full((1, d)), full((d, LANES)), full((d, LANES)), full((1, LANES))],
        out_specs=[rows(d), rows(d // 2), rows(LANES), rows(LANES)],
        out_shape=[jax.ShapeDtypeStruct((nt, d), F32), jax.ShapeDtypeStruct((nt, d // 2), jnp.uint32),
                   jax.ShapeDtypeStruct((nt, LANES), jnp.int32), jax.ShapeDtypeStruct((nt, LANES), F32)],
        compiler_params=pltpu.CompilerParams(dimension_semantics=("arbitrary",), vmem_limit_bytes=VMEM_LIMIT),
        name="out_router",
    )(x2, y_conv, y_rwkv, w_out, w_out, norm_ffn.reshape(1, d).astype(F32), rw_hi, rw_lo, rb)


MOE_ROWS = 256


def slot_plan(expert_id, n_rows_pad):
    flat_e = expert_id.reshape(-1)
    onehot = (flat_e[:, None] == jnp.arange(N_EXPERTS, dtype=jnp.int32)[None, :]).astype(jnp.int32)
    csum = jnp.cumsum(onehot, axis=0)
    rank = jnp.sum(csum * onehot, axis=1) - 1
    counts = csum[-1]
    padded = (counts + MOE_ROWS - 1) // MOE_ROWS * MOE_ROWS
    pend = jnp.cumsum(padded)
    pstarts = pend - padded
    dest = (pstarts[flat_e] + rank).astype(jnp.int32).reshape(expert_id.shape)
    n_blocks = n_rows_pad // MOE_ROWS
    block_row0 = jnp.arange(n_blocks, dtype=jnp.int32) * MOE_ROWS
    block_e = jnp.minimum(jnp.sum((pend[None, :] <= block_row0[:, None]).astype(jnp.int32), axis=1),
                          N_EXPERTS - 1).astype(jnp.int32)
    n_used = (pend[-1:] // MOE_ROWS).astype(jnp.int32)
    return dest, block_e, n_used


def _dispatch_kernel(dest_ref, v_ref, xs_in_ref, xs_ref, sem):
    del xs_in_ref
    step = pl.program_id(0)
    tm = dest_ref.shape[2] // TOP_K
    row0 = step * tm

    def copy(i, k):
        d = dest_ref[0, 0, i * TOP_K + k]
        return pltpu.make_async_copy(v_ref.at[pl.ds(row0 + i, 1), :], xs_ref.at[pl.ds(d, 1), :], sem)

    def start(i, carry):
        for k in range(TOP_K):
            copy(i, k).start()
        return carry

    def wait(i, carry):
        for k in range(TOP_K):
            copy(i, k).wait()
        return carry

    lax.fori_loop(0, tm, start, 0, unroll=8)

    @pl.when(step > 0)
    def _():
        lax.fori_loop(0, tm, wait, 0, unroll=8)

    @pl.when(step == pl.num_programs(0) - 1)
    def _():
        lax.fori_loop(0, tm, wait, 0, unroll=8)


def dispatch(v, dest, n_rows_pad, *, tm=256):
    nt, d = v.shape
    assert nt % tm == 0
    dest3 = dest.reshape(nt // tm, 1, tm * TOP_K)
    xs0 = jnp.zeros((n_rows_pad, d), v.dtype)
    return pl.pallas_call(
        _dispatch_kernel,
        grid=(nt // tm,),
        in_specs=[pl.BlockSpec((1, 1, tm * TOP_K), lambda i: (i, 0, 0), memory_space=pltpu.SMEM),
                  pl.BlockSpec(memory_space=pl.ANY),
                  pl.BlockSpec(memory_space=pl.ANY)],
        out_specs=pl.BlockSpec(memory_space=pl.ANY),
        out_shape=jax.ShapeDtypeStruct((n_rows_pad, d), v.dtype),
        scratch_shapes=[pltpu.SemaphoreType.DMA(())],
        input_output_aliases={2: 0},
        compiler_params=pltpu.CompilerParams(dimension_semantics=("arbitrary",), vmem_limit_bytes=VMEM_LIMIT),
        name="moe_dispatch",
    )(dest3, v, xs0)


def _experts_kernel(be_ref, nu_ref, xs_ref, wg_ref, wu_ref, wd_ref, ys_ref, wg_s, wu_s, wd_s):
    j = pl.program_id(0)
    prev = be_ref[jnp.maximum(j - 1, 0)]

    @pl.when((j == 0) | (be_ref[j] != prev))
    def _():
        wg_s[...] = wg_ref[...].astype(BF16)
        wu_s[...] = wu_ref[...].astype(BF16)
        wd_s[...] = wd_ref[...].astype(BF16)

    @pl.when(j < nu_ref[0])
    def _():
        xb = _unpack_bf16_pairs(xs_ref[...]).astype(BF16)
        gt = jnp.dot(xb, wg_s[...], preferred_element_type=F32)
        up = jnp.dot(xb, wu_s[...], preferred_element_type=F32)
        hid = (gt * jax.nn.sigmoid(gt) * up).astype(BF16)
        ys_ref[...] = _pack_bf16_pairs(jnp.dot(hid, wd_s[...], preferred_element_type=F32))

    @pl.when(j >= nu_ref[0])
    def _():
        ys_ref[...] = jnp.zeros_like(ys_ref)


def experts(xs, block_e, n_used, w_gate, w_up, w_down, n_rows_pad):
    dh = xs.shape[1]
    d, d_e = w_gate.shape[1], w_gate.shape[2]
    assert d == 2 * dh
    n_blocks = n_rows_pad // MOE_ROWS
    grid_spec = pltpu.PrefetchScalarGridSpec(
        num_scalar_prefetch=2,
        grid=(n_blocks,),
        in_specs=[pl.BlockSpec((MOE_ROWS, dh), lambda j, be, nu: (jnp.minimum(j, nu[0] - 1), 0)),
                  pl.BlockSpec((None, d, d_e), lambda j, be, nu: (be[j], 0, 0)),
                  pl.BlockSpec((None, d, d_e), lambda j, be, nu: (be[j], 0, 0)),
                  pl.BlockSpec((None, d_e, d), lambda j, be, nu: (be[j], 0, 0))],
        out_specs=pl.BlockSpec((MOE_ROWS, dh), lambda j, be, nu: (j, 0)),
        scratch_shapes=[pltpu.VMEM((d, d_e), BF16), pltpu.VMEM((d, d_e), BF16), pltpu.VMEM((d_e, d), BF16)],
    )
    return pl.pallas_call(
        _experts_kernel,
        grid_spec=grid_spec,
        out_shape=jax.ShapeDtypeStruct((n_rows_pad, dh), jnp.uint32),
        compiler_params=pltpu.CompilerParams(dimension_semantics=("arbitrary",), vmem_limit_bytes=VMEM_LIMIT),
        name="moe_experts",
    )(block_e, n_used, xs, w_gate, w_up, w_down)


def _combine_kernel(dest_ref, dest_next_ref, h_ref, gate_ref, nfin_ref, ys_ref, o_ref, ybuf, sem):
    step = pl.program_id(0)
    tm = h_ref.shape[0]
    slot = lax.rem(step, 2)

    def copy(d_ref, s, i, k):
        d = d_ref[0, 0, i * TOP_K + k]
        return pltpu.make_async_copy(ys_ref.at[pl.ds(d, 1), :], ybuf.at[s, k, pl.ds(i, 1), :], sem.at[s])

    def start_all(d_ref, s):
        def body(i, carry):
            for k in range(TOP_K):
                copy(d_ref, s, i, k).start(priority=k)
            return carry
        lax.fori_loop(0, tm, body, 0, unroll=8)

    @pl.when(step == 0)
    def _():
        start_all(dest_ref, 0)

    @pl.when(step + 1 < pl.num_programs(0))
    def _():
        start_all(dest_next_ref, 1 - slot)

    def wait_body(i, carry):
        for k in range(TOP_K):
            copy(dest_ref, slot, i, k).wait()
        return carry

    lax.fori_loop(0, tm, wait_body, 0, unroll=8)
    gates = gate_ref[...]
    moe = (gates[:, 0:1] * _unpack_bf16_pairs(ybuf[slot, 0])
           + gates[:, 1:2] * _unpack_bf16_pairs(ybuf[slot, 1]))
    o_ref[...] = _rms(h_ref[...] + moe, nfin_ref[...])


def combine(h, gates, dest, ys, norm_final, *, tm=256):
    nt, d = h.shape
    assert nt % tm == 0 and ys.shape[1] * 2 == d
    n_tiles = nt // tm
    dest3 = dest.reshape(n_tiles, 1, tm * TOP_K)
    return pl.pallas_call(
        _combine_kernel,
        grid=(n_tiles,),
        in_specs=[pl.BlockSpec((1, 1, tm * TOP_K), lambda i: (i, 0, 0), memory_space=pltpu.SMEM),
                  pl.BlockSpec((1, 1, tm * TOP_K), lambda i: (jnp.minimum(i + 1, n_tiles - 1), 0, 0),
                               memory_space=pltpu.SMEM),
                  pl.BlockSpec((tm, d), lambda i: (i, 0)),
                  pl.BlockSpec((tm, LANES), lambda i: (i, 0)),
                  pl.BlockSpec((1, d), lambda i: (0, 0)),
                  pl.BlockSpec(memory_space=pl.ANY)],
        out_specs=pl.BlockSpec((tm, d), lambda i: (i, 0)),
        out_shape=jax.ShapeDtypeStruct((nt, d), F32),
        scratch_shapes=[pltpu.VMEM((2, TOP_K, tm, d // 2), jnp.uint32), pltpu.SemaphoreType.DMA((2,))],
        compiler_params=pltpu.CompilerParams(dimension_semantics=("arbitrary",), vmem_limit_bytes=VMEM_LIMIT),
        name="moe_combine",
    )(dest3, dest3, h, gates, norm_final.reshape(1, d).astype(F32), ys)


def kernel(x, norm_mix, w_in, conv_dw, conv_b, conv_ln_g, conv_ln_b, shift_mu, w0, w_lora_up, a0, a_lora_up, g_lora_up, k_k, k_a, r_k, gn_g, gn_b, w_out, norm_ffn, router_group_w, router_group_b, router_expert_w, router_expert_b, expert_w_gate, expert_w_up, expert_w_down, norm_final):
    B, T, D = x.shape
    depth = w_in.shape[0]
    d_c = conv_dw.shape[2]
    d_r = w0.shape[1]
    d_main = 2 * d_c + 3 * d_r
    nt = B * T
    n_rows_pad = -(-(nt * TOP_K + N_EXPERTS * (MOE_ROWS - 1)) // MOE_ROWS) * MOE_ROWS
    h = x.reshape(nt, D)
    for l in range(depth):
        proj_main, proj_lora = in_proj(h, norm_mix[l], w_in[l], d_main)
        y_conv = conv_mixer(proj_main, conv_dw[l], conv_b[l], conv_ln_g[l], conv_ln_b[l], batch=B)
        y_rwkv = rwkv_mixer(proj_main, proj_lora, 2 * d_c, shift_mu[l], w0[l], w_lora_up[l], a0[l], a_lora_up[l],
                            g_lora_up[l], k_k[l], k_a[l], r_k[l].reshape(-1), gn_g[l], gn_b[l], batch=B)
        h, v, eid, gates = out_router(h, y_conv, y_rwkv, w_out[l].astype(BF16), norm_ffn[l], router_group_w[l],
                                      router_group_b[l], router_expert_w[l], router_expert_b[l])
        dest, block_e, n_used = slot_plan(eid[:, :TOP_K], n_rows_pad)
        xs = dispatch(v, dest, n_rows_pad)
        ys = experts(xs, block_e, n_used, expert_w_gate[l], expert_w_up[l], expert_w_down[l], n_rows_pad)
        assert depth == 1
        h = combine(h, gates, dest, ys, norm_final)
    return h.reshape(B, T, D)
```

```python
import functools

import jax
import jax.numpy as jnp
from jax import lax
from jax.experimental import pallas as pl
from jax.experimental.pallas import tpu as pltpu

F32 = jnp.float32
BF16 = jnp.bfloat16

RWKV_HEAD = 64
CONV_WIDTH = 31
N_GROUPS = 4
EXPERTS_PER_GROUP = 8
N_EXPERTS = N_GROUPS * EXPERTS_PER_GROUP
TOP_K = 2
RMS_EPS = 1e-6
LN_EPS = 1e-5
GN_EPS = 64e-5
LANES = 128
SUBLANES = 8
MXU_DIM = 256
VMEM_LIMIT = 56 * 1024 * 1024

_NT = (((1,), (1,)), ((), ()))
_TN = (((0,), (0,)), ((), ()))


def _dot(a, b, dims=None):
    a = a.astype(BF16)
    b = b.astype(BF16)
    if dims is None:
        return jnp.dot(a, b, preferred_element_type=F32)
    return lax.dot_general(a, b, dims, preferred_element_type=F32)


def _dot_split(x, m):
    hi = x.astype(BF16)
    lo = (x - hi.astype(F32)).astype(BF16)
    return (jnp.dot(hi, m, preferred_element_type=F32)
            + jnp.dot(lo, m, preferred_element_type=F32))


def _pack_bf16_pairs(x):
    w = x.shape[1] // 2
    lo = lax.bitcast_convert_type(x[:, :w].astype(BF16).astype(F32), jnp.uint32)
    hi = lax.bitcast_convert_type(x[:, w:].astype(BF16).astype(F32), jnp.uint32)
    return (lo >> 16) | (hi & jnp.uint32(0xFFFF0000))


def _unpack_bf16_pairs(p):
    lo = lax.bitcast_convert_type(p << 16, F32)
    hi = lax.bitcast_convert_type(p & jnp.uint32(0xFFFF0000), F32)
    return jnp.concatenate([lo, hi], axis=1)


def _rwkv_kernel(r_ref, k_ref, v_ref, lora_ref, mu_main_ref, mu_lora_ref, w0_ref, wup_ref, a0_ref, aup_ref,
                 gup_ref, kk_ref, ka_ref, rk_ref, gng_ref, gnb_ref, y_ref,
                 state_ref, prev_main_ref, prev_lora_ref, *, n_heads, d_decay, d_aaa):
    nb, C, d_r = r_ref.shape
    N = RWKV_HEAD
    R = nb * C

    @pl.when(pl.program_id(0) == 0)
    def _():
        state_ref[...] = jnp.zeros_like(state_ref)
        prev_main_ref[...] = jnp.zeros_like(prev_main_ref)
        prev_lora_ref[...] = jnp.zeros_like(prev_lora_ref)

    row = lax.broadcasted_iota(jnp.int32, (R, 1), 0)

    def shifted(x_ref, prev_ref, j, mu):
        x = x_ref[...].reshape(R, x_ref.shape[2])
        xp = pltpu.roll(x, 1, 0)
        for b in range(nb):
            xp = jnp.where(row == b * C, prev_ref[b, j:j + 1, :], xp)
            prev_ref[b, j:j + 1, :] = x[(b + 1) * C - 1:(b + 1) * C, :]
        return x + (xp - x) * mu

    r = shifted(r_ref, prev_main_ref, 0, mu_main_ref[0:1, :])
    k = shifted(k_ref, prev_main_ref, 1, mu_main_ref[1:2, :])
    v = shifted(v_ref, prev_main_ref, 2, mu_main_ref[2:3, :])
    lo = shifted(lora_ref, prev_lora_ref, 0, mu_lora_ref[...])

    wd = lo[:, :d_decay]
    ad = lo[:, d_decay:d_decay + d_aaa]
    gd = lo[:, d_decay + d_aaa:]

    z = w0_ref[...] + _dot(jnp.tanh(wd), wup_ref[...])
    w = -(jnp.maximum(-z, 0.0) + jnp.log(1.0 + jnp.exp(-jnp.abs(z)))) - 0.5
    logd = -jnp.exp(w)
    a = jax.nn.sigmoid(a0_ref[...] + _dot(ad, aup_ref[...]))
    g = _dot(jax.nn.sigmoid(gd), gup_ref[...])

    li = lax.broadcasted_iota(jnp.int32, (LANES, LANES), 0) // N
    lj = lax.broadcasted_iota(jnp.int32, (LANES, LANES), 1) // N
    head_ones = (li == lj).astype(BF16)

    def head_sum(x):
        return jnp.concatenate(
            [_dot_split(x[:, c * LANES:(c + 1) * LANES], head_ones) for c in range(d_r // LANES)], axis=1)

    kkr = k * kk_ref[...]
    kk = kkr / jnp.maximum(jnp.sqrt(head_sum(kkr * kkr)), 1e-12)
    k2 = k * (1.0 + (a - 1.0) * ka_ref[...])
    a_s = -kk
    b_s = kk * a

    ri = lax.broadcasted_iota(jnp.int32, (R, R), 0)
    rj = lax.broadcasted_iota(jnp.int32, (R, R), 1)
    tri = ((rj <= ri) & ((ri // C) == (rj // C))).astype(BF16)
    hi = logd.astype(BF16)
    rem = logd - hi.astype(F32)
    mid = rem.astype(BF16)
    low = (rem - mid.astype(F32)).astype(BF16)
    cum = (jnp.dot(tri, hi, preferred_element_type=F32) + jnp.dot(tri, mid, preferred_element_type=F32)
           + jnp.dot(tri, low, preferred_element_type=F32))
    p_incl = jnp.exp(cum)
    p_excl = jnp.exp(cum - logd)
    p_inv = jnp.exp(-cum)
    p_last = [p_incl[(b + 1) * C - 1:(b + 1) * C, :] for b in range(nb)]
    p_last_rows = jnp.concatenate([jnp.broadcast_to(p, (C, d_r)) for p in p_last], axis=0)

    at = (a_s * p_excl).astype(BF16)
    rt = (r * p_incl).astype(BF16)
    bt = b_s * p_inv
    kt = k2 * p_inv
    bh = (bt * p_last_rows).astype(BF16)
    kh = (kt * p_last_rows).astype(BF16)
    bt = bt.astype(BF16)
    kt = kt.astype(BF16)
    vb = v.astype(BF16)

    HG = MXU_DIM // N
    W = HG * N
    GC = HG * C
    wi = lax.broadcasted_iota(jnp.int32, (GC, W), 0)
    wj = lax.broadcasted_iota(jnp.int32, (GC, W), 1)
    bd_on = ((wi // C) == (wj // N)).astype(BF16) > 0

    def bdiag(x):
        xb = x.astype(BF16)
        return jnp.where(bd_on, jnp.concatenate([xb] * HG, axis=0), jnp.zeros((), BF16))

    ti = lax.broadcasted_iota(jnp.int32, (2 * C, GC), 0)
    tj = lax.broadcasted_iota(jnp.int32, (2 * C, GC), 1) % C
    mask2 = ((ti < C) & (tj < ti)) | ((ti >= C) & (tj <= ti - C))
    li = lax.broadcasted_iota(jnp.int32, (C, GC), 0)
    lj = lax.broadcasted_iota(jnp.int32, (C, GC), 1) % C
    eye = (li == lj).astype(F32)
    level_masks = []
    s = 1
    while s < C:
        level_masks.append(((li // (2 * s)) == (lj // (2 * s))) & (((li // s) % 2) == 1) & (((lj // s) % 2) == 0))
        s *= 2
    colblk = lax.broadcasted_iota(jnp.int32, (N, W), 1) // N

    units = [(b, g) for b in range(nb) for g in range(n_heads // HG)]
    idx = range(len(units))
    rs = [slice(b * C, (b + 1) * C) for b, _ in units]
    ls = [slice(g * W, (g + 1) * W) for _, g in units]
    ar = [jnp.concatenate([at[rs[i], ls[i]], rt[rs[i], ls[i]]], axis=0) for i in idx]
    g_b = [jnp.where(mask2, _dot(ar[i], bdiag(bt[rs[i], ls[i]]), _NT), 0.0) for i in idx]
    g_k = [jnp.where(mask2, _dot(ar[i], bdiag(kt[rs[i], ls[i]]), _NT), 0.0) for i in idx]
    s0 = [state_ref[i] for i in idx]
    ars = [_dot(ar[i], bdiag(s0[i]), _NT) for i in idx]
    gv = [_dot(g_k[i], bdiag(vb[rs[i], ls[i]])) for i in idx]
    tinv = [eye + jnp.where(level_masks[0], g_b[i][:C], 0.0) for i in idx]
    for lm in level_masks[1:]:
        xs = [_dot(jnp.where(lm, g_b[i][:C], 0.0), bdiag(tinv[i])) for i in idx]
        tinv = [tinv[i] + _dot(tinv[i], bdiag(xs[i])) for i in idx]
    u = [_dot(tinv[i], bdiag(ars[i][:C] + gv[i][:C])) for i in idx]
    ys = [ars[i][C:] + gv[i][C:] + _dot(g_b[i][C:], bdiag(u[i])) for i in idx]
    for i in idx:
        uv = jnp.concatenate([u[i].astype(BF16), vb[rs[i], ls[i]]], axis=0)
        bk = jnp.concatenate([bh[rs[i], ls[i]], kh[rs[i], ls[i]]], axis=0)
        full = _dot(uv, bk, _TN)
        upd = s0[i] * p_last[units[i][0]][:, ls[i]]
        for h in range(HG):
            upd = upd + jnp.where(colblk == h, full[h * N:(h + 1) * N, :], 0.0)
        state_ref[i] = upd
    n_g = n_heads // HG
    y = jnp.concatenate([jnp.concatenate(ys[b * n_g:(b + 1) * n_g], axis=1) for b in range(nb)],
                        axis=0)

    inv_n = 1.0 / N
    mu_y = head_sum(y) * inv_n
    yc = y - mu_y
    var_y = head_sum(yc * yc) * inv_n
    yn = yc * lax.rsqrt(var_y + GN_EPS) * gng_ref[...] + gnb_ref[...]
    bonus = head_sum(r * k2 * rk_ref[...]) * v
    y_ref[...] = ((yn + bonus) * g).reshape(nb, C, d_r).astype(y_ref.dtype)


def rwkv_mixer(proj_main, proj_lora, col0, shift_mu, w0, w_up, a0, a_up, g_up, k_k, k_a, r_k, gn_g, gn_b,
               *, batch, chunk=64):
    nt = proj_main.shape[0]
    seq = nt // batch
    d_r = w0.shape[-1]
    n_heads = d_r // RWKV_HEAD
    d_lora = proj_lora.shape[1]
    d_decay, d_aaa = w_up.shape[0], a_up.shape[0]
    n_chunks = seq // chunk
    assert seq % chunk == 0 and col0 % d_r == 0
    cb = col0 // d_r
    row2 = lambda t: t.reshape(1, -1).astype(F32)
    mu_main = shift_mu[:3 * d_r].reshape(3, d_r)
    mu_lora = shift_mu[3 * d_r:].reshape(1, d_lora)
    pm3 = proj_main.reshape(batch, seq, proj_main.shape[1])
    pl3 = proj_lora.reshape(batch, seq, d_lora)

    def main_spec(j):
        return pl.BlockSpec((batch, chunk, d_r), lambda c, j=j: (0, c, cb + j))

    full = lambda shape: pl.BlockSpec(shape, lambda c: (0,) * len(shape))
    kern = functools.partial(_rwkv_kernel, n_heads=n_heads, d_decay=d_decay, d_aaa=d_aaa)
    y = pl.pallas_call(
        kern,
        grid=(n_chunks,),
        in_specs=[main_spec(0), main_spec(1), main_spec(2),
                  pl.BlockSpec((batch, chunk, d_lora), lambda c: (0, c, 0)),
                  full((3, d_r)), full((1, d_lora)), full((1, d_r)), full((d_decay, d_r)), full((1, d_r)),
                  full((d_aaa, d_r)), full((g_up.shape[0], d_r)), full((1, d_r)), full((1, d_r)), full((1, d_r)),
                  full((1, d_r)), full((1, d_r))],
        out_specs=pl.BlockSpec((batch, chunk, d_r), lambda c: (0, c, 0)),
        out_shape=jax.ShapeDtypeStruct((batch, seq, d_r), F32),
        scratch_shapes=[pltpu.VMEM((batch * n_heads * RWKV_HEAD // MXU_DIM, RWKV_HEAD, MXU_DIM), F32),
                        pltpu.VMEM((batch, 3, d_r), F32),
                        pltpu.VMEM((batch, 1, d_lora), F32)],
        compiler_params=pltpu.CompilerParams(dimension_semantics=("arbitrary",), vmem_limit_bytes=VMEM_LIMIT),
        name="rwkv_mixer",
    )(pm3, pm3, pm3, pl3, mu_main, mu_lora, row2(w0), w_up.astype(BF16), row2(a0),
      a_up.astype(BF16), g_up.astype(BF16), row2(k_k), row2(k_a), row2(r_k), row2(gn_g), row2(gn_b))
    return y.reshape(nt, d_r)


def _rms(x, g):
    return x * lax.rsqrt(jnp.mean(x * x, axis=-1, keepdims=True) + RMS_EPS) * g


def _inproj_kernel(x_ref, g_ref, wt_ref, wlt_ref, o_ref, ol_ref, u_ref):
    @pl.when(pl.program_id(1) == 0)
    def _():
        u = _rms(x_ref[...], g_ref[...]).astype(BF16)
        u_ref[...] = u
        ol_ref[...] = lax.dot_general(u, wlt_ref[...], _NT, preferred_element_type=F32)

    o_ref[...] = lax.dot_general(u_ref[...], wt_ref[...].astype(BF16), _NT, preferred_element_type=F32)


def in_proj(x2, norm_g, w_in, d_main, *, tm=1024, tn=1024):
    nt, d = x2.shape
    d_lora = w_in.shape[1] - d_main
    assert nt % tm == 0 and d_main % tn == 0
    w_t = w_in.T
    w_lora_t = w_t[d_main:].astype(BF16)
    return pl.pallas_call(
        _inproj_kernel,
        grid=(nt // tm, d_main // tn),
        in_specs=[pl.BlockSpec((tm, d), lambda i, j: (i, 0)),
                  pl.BlockSpec((1, d), lambda i, j: (0, 0)),
                  pl.BlockSpec((tn, d), lambda i, j: (j, 0)),
                  pl.BlockSpec((d_lora, d), lambda i, j: (0, 0))],
        out_specs=[pl.BlockSpec((tm, tn), lambda i, j: (i, j)),
                   pl.BlockSpec((tm, d_lora), lambda i, j: (i, 0))],
        out_shape=[jax.ShapeDtypeStruct((nt, d_main), F32), jax.ShapeDtypeStruct((nt, d_lora), F32)],
        scratch_shapes=[pltpu.VMEM((tm, d), BF16)],
        compiler_params=pltpu.CompilerParams(dimension_semantics=("arbitrary", "arbitrary"),
                                             vmem_limit_bytes=VMEM_LIMIT),
        name="in_proj",
    )(x2, norm_g.reshape(1, d).astype(F32), w_t, w_lora_t)


CONV_HALO = 32
CONV_ROWS = 32


def _conv_kernel(val_ref, gate_ref, dw_ref, b_ref, g_ref, beta_ref, o_ref, ubuf_ref):
    tt = val_ref.shape[0]
    d_c = val_ref.shape[1]
    t_idx = pl.program_id(1)

    @pl.when(t_idx == 0)
    def _():
        ubuf_ref[0, 0:CONV_HALO, :] = jnp.zeros((CONV_HALO, d_c), F32)

    @pl.when(t_idx > 0)
    def _():
        ubuf_ref[0, 0:CONV_HALO, :] = ubuf_ref[0, tt:tt + CONV_HALO, :]

    ubuf_ref[0, CONV_HALO:CONV_HALO + tt, :] = val_ref[...] * jax.nn.sigmoid(gate_ref[...])
    n_sh = tt + CONV_HALO - SUBLANES
    for s in range(1, SUBLANES):
        for r in range(0, n_sh, CONV_ROWS):
            n = min(CONV_ROWS, n_sh - r)
            ubuf_ref[s, r:r + n, :] = ubuf_ref[0, r + s:r + s + n, :]
    dw = dw_ref[...]
    lead = CONV_HALO - (CONV_WIDTH - 1)
    for c in range(tt // CONV_ROWS):
        r0 = c * CONV_ROWS
        acc = jnp.zeros((CONV_ROWS, d_c), F32)
        for j in range(CONV_WIDTH):
            q, s = divmod(lead + j, SUBLANES)
            r = r0 + q * SUBLANES
            acc = acc + dw[j:j + 1, :] * ubuf_ref[s, r:r + CONV_ROWS, :]
        acc = acc + b_ref[...]
        mu = jnp.mean(acc, axis=-1, keepdims=True)
        cen = acc - mu
        var = jnp.mean(cen * cen, axis=-1, keepdims=True)
        yv = cen * lax.rsqrt(var + LN_EPS) * g_ref[...] + beta_ref[...]
        o_ref[r0:r0 + CONV_ROWS, :] = yv * jax.nn.sigmoid(yv)


def conv_mixer(proj_main, dw, bias, ln_g, ln_b, *, batch, tt=256):
    nt = proj_main.shape[0]
    seq = nt // batch
    d_c = dw.shape[1]
    n_t = seq // tt
    assert seq % tt == 0 and tt % CONV_ROWS == 0
    row = lambda t: t.reshape(1, d_c).astype(F32)
    full = lambda shape: pl.BlockSpec(shape, lambda b, t: (0, 0))
    return pl.pallas_call(
        _conv_kernel,
        grid=(batch, n_t),
        in_specs=[pl.BlockSpec((tt, d_c), lambda b, t: (b * n_t + t, 0)),
                  pl.BlockSpec((tt, d_c), lambda b, t: (b * n_t + t, 1)),
                  full((CONV_WIDTH, d_c)), full((1, d_c)), full((1, d_c)), full((1, d_c))],
        out_specs=pl.BlockSpec((tt, d_c), lambda b, t: (b * n_t + t, 0)),
        out_shape=jax.ShapeDtypeStruct((nt, d_c), F32),
        scratch_shapes=[pltpu.VMEM((SUBLANES, tt + CONV_HALO, d_c), F32)],
        compiler_params=pltpu.CompilerParams(dimension_semantics=("arbitrary", "arbitrary"),
                                             vmem_limit_bytes=VMEM_LIMIT),
        name="conv_mixer",
    )(proj_main, proj_main, dw.astype(F32), row(bias), row(ln_g), row(ln_b))


def _out_router_kernel(x_ref, yc_ref, yr_ref, wc_ref, wr_ref, nf_ref, rw_hi_ref, rw_lo_ref, rb_ref,
                       h_ref, v_ref, eid_ref, gate_ref):
    h = (x_ref[...] + jnp.dot(yc_ref[...].astype(BF16), wc_ref[...], preferred_element_type=F32)
         + jnp.dot(yr_ref[...].astype(BF16), wr_ref[...], preferred_element_type=F32))
    h_ref[...] = h
    v = _rms(h, nf_ref[...])
    v_ref[...] = _pack_bf16_pairs(v)
    v_hi = v.astype(BF16)
    v_lo = (v - v_hi.astype(F32)).astype(BF16)
    logits = (jnp.dot(v_hi, rw_hi_ref[...], preferred_element_type=F32)
              + jnp.dot(v_lo, rw_hi_ref[...], preferred_element_type=F32)
              + jnp.dot(v_hi, rw_lo_ref[...], preferred_element_type=F32)) + rb_ref[...]
    lane = lax.broadcasted_iota(jnp.int32, logits.shape, 1)
    neg = jnp.float32(-jnp.inf)
    big = jnp.int32(LANES)

    def first_max(vals):
        m = jnp.max(vals, axis=-1, keepdims=True)
        return m, jnp.min(jnp.where(vals == m, lane, big), axis=-1, keepdims=True)

    gl = jnp.where(lane < N_GROUPS, logits, neg)
    gmax, grp = first_max(gl)
    p_grp = 1.0 / jnp.sum(jnp.exp(gl - gmax), axis=-1, keepdims=True)
    lo = N_GROUPS + grp * EXPERTS_PER_GROUP
    el = jnp.where((lane >= lo) & (lane < lo + EXPERTS_PER_GROUP), logits, neg)
    m1, i1 = first_max(el)
    m2, i2 = first_max(jnp.where(lane == i1, neg, el))
    e2 = jnp.exp(m2 - m1)
    g1 = p_grp / (1.0 + e2)
    g2 = p_grp * e2 / (1.0 + e2)
    eid_ref[...] = jnp.where(lane == 0, i1 - N_GROUPS, jnp.where(lane == 1, i2 - N_GROUPS, 0))
    gate_ref[...] = jnp.where(lane == 0, g1, jnp.where(lane == 1, g2, 0.0))


def out_router(x2, y_conv, y_rwkv, w_out, norm_ffn, rg_w, rg_b, re_w, re_b, *, tm=256):
    nt, d = x2.shape
    d_c, d_r = y_conv.shape[1], y_rwkv.shape[1]
    assert d_c == d_r and nt % tm == 0
    n_r = N_GROUPS + N_EXPERTS
    rw = jnp.zeros((d, LANES), F32).at[:, :n_r].set(jnp.concatenate([rg_w, re_w], axis=1).astype(F32))
    rw_hi = rw.astype(BF16)
    rw_lo = (rw - rw_hi.astype(F32)).astype(BF16)
    rb = jnp.zeros((1, LANES), F32).at[0, :n_r].set(jnp.concatenate([rg_b, re_b]).astype(F32))
    full = lambda shape: pl.BlockSpec(shape, lambda i: (0, 0))
    rows = lambda w: pl.BlockSpec((tm, w), lambda i: (i, 0))
    return pl.pallas_call(
        _out_router_kernel,
        grid=(nt // tm,),
        in_specs=[rows(d), rows(d_c), rows(d_r),
                  pl.BlockSpec((d_c, d), lambda i: (0, 0)), pl.BlockSpec((d_r, d), lambda i: (1, 0)),
                  full((1, d)), full((d, LANES)), full((d, LANES)), full((1, LANES))],
        out_specs=[rows(d), rows(d // 2), rows(LANES), rows(LANES)],
        out_shape=[jax.ShapeDtypeStruct((nt, d), F32), jax.ShapeDtypeStruct((nt, d // 2), jnp.uint32),
                   jax.ShapeDtypeStruct((nt, LANES), jnp.int32), jax.ShapeDtypeStruct((nt, LANES), F32)],
        compiler_params=pltpu.CompilerParams(dimension_semantics=("arbitrary",), vmem_limit_bytes=VMEM_LIMIT),
        name="out_router",
    )(x2, y_conv, y_rwkv, w_out, w_out, norm_ffn.reshape(1, d).astype(F32), rw_hi, rw_lo, rb)


MOE_ROWS = 256


def slot_plan(expert_id, n_rows_pad):
    flat_e = expert_id.reshape(-1)
    onehot = (flat_e[:, None] == jnp.arange(N_EXPERTS, dtype=jnp.int32)[None, :]).astype(jnp.int32)
    csum = jnp.cumsum(onehot, axis=0)
    rank = jnp.sum(csum * onehot, axis=1) - 1
    counts = csum[-1]
    padded = (counts + MOE_ROWS - 1) // MOE_ROWS * MOE_ROWS
    pend = jnp.cumsum(padded)
    pstarts = pend - padded
    dest = (pstarts[flat_e] + rank).astype(jnp.int32).reshape(expert_id.shape)
    n_blocks = n_rows_pad // MOE_ROWS
    block_row0 = jnp.arange(n_blocks, dtype=jnp.int32) * MOE_ROWS
    block_e = jnp.minimum(jnp.sum((pend[None, :] <= block_row0[:, None]).astype(jnp.int32), axis=1),
                          N_EXPERTS - 1).astype(jnp.int32)
    n_used = (pend[-1:] // MOE_ROWS).astype(jnp.int32)
    return dest, block_e, n_used


def _dispatch_kernel(dest_ref, v_ref, xs_in_ref, xs_ref, sem):
    del xs_in_ref
    tm = v_ref.shape[0]

    def copy(i, k):
        d = dest_ref[0, 0, i * TOP_K + k]
        return pltpu.make_async_copy(v_ref.at[pl.ds(i, 1), :], xs_ref.at[pl.ds(d, 1), :], sem)

    def start(i, carry):
        for k in range(TOP_K):
            copy(i, k).start(priority=k)
        return carry

    def wait(i, carry):
        for k in range(TOP_K):
            copy(i, k).wait()
        return carry

    lax.fori_loop(0, tm, start, 0, unroll=8)
    lax.fori_loop(0, tm, wait, 0, unroll=8)


def dispatch(v, dest, n_rows_pad, *, tm=1024):
    nt, d = v.shape
    assert nt % tm == 0
    dest3 = dest.reshape(nt // tm, 1, tm * TOP_K)
    xs0 = jnp.zeros((n_rows_pad, d), v.dtype)
    return pl.pallas_call(
        _dispatch_kernel,
        grid=(nt // tm,),
        in_specs=[pl.BlockSpec((1, 1, tm * TOP_K), lambda i: (i, 0, 0), memory_space=pltpu.SMEM),
                  pl.BlockSpec((tm, d), lambda i: (i, 0)),
                  pl.BlockSpec(memory_space=pl.ANY)],
        out_specs=pl.BlockSpec(memory_space=pl.ANY),
        out_shape=jax.ShapeDtypeStruct((n_rows_pad, d), v.dtype),
        scratch_shapes=[pltpu.SemaphoreType.DMA(())],
        input_output_aliases={2: 0},
        compiler_params=pltpu.CompilerParams(dimension_semantics=("arbitrary",), vmem_limit_bytes=VMEM_LIMIT),
        name="moe_dispatch",
    )(dest3, v, xs0)


def _experts_kernel(be_ref, nu_ref, xs_ref, wg_ref, wu_ref, wd_ref, ys_ref, wg_s, wu_s, wd_s):
    j = pl.program_id(0)
    prev = be_ref[jnp.maximum(j - 1, 0)]

    @pl.when((j == 0) | (be_ref[j] != prev))
    def _():
        wg_s[...] = wg_ref[...].astype(BF16)
        wu_s[...] = wu_ref[...].astype(BF16)
        wd_s[...] = wd_ref[...].astype(BF16)

    @pl.when(j < nu_ref[0])
    def _():
        xb = _unpack_bf16_pairs(xs_ref[...]).astype(BF16)
        gt = jnp.dot(xb, wg_s[...], preferred_element_type=F32)
        up = jnp.dot(xb, wu_s[...], preferred_element_type=F32)
        hid = (gt * jax.nn.sigmoid(gt) * up).astype(BF16)
        ys_ref[...] = _pack_bf16_pairs(jnp.dot(hid, wd_s[...], preferred_element_type=F32))

    @pl.when(j >= nu_ref[0])
    def _():
        ys_ref[...] = jnp.zeros_like(ys_ref)


def experts(xs, block_e, n_used, w_gate, w_up, w_down, n_rows_pad):
    dh = xs.shape[1]
    d, d_e = w_gate.shape[1], w_gate.shape[2]
    assert d == 2 * dh
    n_blocks = n_rows_pad // MOE_ROWS
    grid_spec = pltpu.PrefetchScalarGridSpec(
        num_scalar_prefetch=2,
        grid=(n_blocks,),
        in_specs=[pl.BlockSpec((MOE_ROWS, dh), lambda j, be, nu: (jnp.minimum(j, nu[0] - 1), 0)),
                  pl.BlockSpec((None, d, d_e), lambda j, be, nu: (be[j], 0, 0)),
                  pl.BlockSpec((None, d, d_e), lambda j, be, nu: (be[j], 0, 0)),
                  pl.BlockSpec((None, d_e, d), lambda j, be, nu: (be[j], 0, 0))],
        out_specs=pl.BlockSpec((MOE_ROWS, dh), lambda j, be, nu: (j, 0)),
        scratch_shapes=[pltpu.VMEM((d, d_e), BF16), pltpu.VMEM((d, d_e), BF16), pltpu.VMEM((d_e, d), BF16)],
    )
    return pl.pallas_call(
        _experts_kernel,
        grid_spec=grid_spec,
        out_shape=jax.ShapeDtypeStruct((n_rows_pad, dh), jnp.uint32),
        compiler_params=pltpu.CompilerParams(dimension_semantics=("arbitrary",), vmem_limit_bytes=VMEM_LIMIT),
        name="moe_experts",
    )(block_e, n_used, xs, w_gate, w_up, w_down)


def _combine_kernel(dest_ref, dest_next_ref, h_ref, gate_ref, nfin_ref, ys_ref, o_ref, ybuf, sem):
    step = pl.program_id(0)
    tm = h_ref.shape[0]
    slot = lax.rem(step, 2)

    def copy(d_ref, s, i, k):
        d = d_ref[0, 0, i * TOP_K + k]
        return pltpu.make_async_copy(ys_ref.at[pl.ds(d, 1), :], ybuf.at[s, k, pl.ds(i, 1), :], sem.at[s])

    def start_all(d_ref, s):
        def body(i, carry):
            for k in range(TOP_K):
                copy(d_ref, s, i, k).start(priority=k)
            return carry
        lax.fori_loop(0, tm, body, 0, unroll=8)

    @pl.when(step == 0)
    def _():
        start_all(dest_ref, 0)

    @pl.when(step + 1 < pl.num_programs(0))
    def _():
        start_all(dest_next_ref, 1 - slot)

    def wait_body(i, carry):
        for k in range(TOP_K):
            copy(dest_ref, slot, i, k).wait()
        return carry

    lax.fori_loop(0, tm, wait_body, 0, unroll=8)
    gates = gate_ref[...]
    moe = (gates[:, 0:1] * _unpack_bf16_pairs(ybuf[slot, 0])
           + gates[:, 1:2] * _unpack_bf16_pairs(ybuf[slot, 1]))
    o_ref[...] = _rms(h_ref[...] + moe, nfin_ref[...])


def combine(h, gates, dest, ys, norm_final, *, tm=256):
    nt, d = h.shape
    assert nt % tm == 0 and ys.shape[1] * 2 == d
    n_tiles = nt // tm
    dest3 = dest.reshape(n_tiles, 1, tm * TOP_K)
    return pl.pallas_call(
        _combine_kernel,
        grid=(n_tiles,),
        in_specs=[pl.BlockSpec((1, 1, tm * TOP_K), lambda i: (i, 0, 0), memory_space=pltpu.SMEM),
                  pl.BlockSpec((1, 1, tm * TOP_K), lambda i: (jnp.minimum(i + 1, n_tiles - 1), 0, 0),
                               memory_space=pltpu.SMEM),
                  pl.BlockSpec((tm, d), lambda i: (i, 0)),
                  pl.BlockSpec((tm, LANES), lambda i: (i, 0)),
                  pl.BlockSpec((1, d), lambda i: (0, 0)),
                  pl.BlockSpec(memory_space=pl.ANY)],
        out_specs=pl.BlockSpec((tm, d), lambda i: (i, 0)),
        out_shape=jax.ShapeDtypeStruct((nt, d), F32),
        scratch_shapes=[pltpu.VMEM((2, TOP_K, tm, d // 2), jnp.uint32), pltpu.SemaphoreType.DMA((2,))],
        compiler_params=pltpu.CompilerParams(dimension_semantics=("arbitrary",), vmem_limit_bytes=VMEM_LIMIT),
        name="moe_combine",
    )(dest3, dest3, h, gates, norm_final.reshape(1, d).astype(F32), ys)


def kernel(x, norm_mix, w_in, conv_dw, conv_b, conv_ln_g, conv_ln_b, shift_mu, w0, w_lora_up, a0, a_lora_up, g_lora_up, k_k, k_a, r_k, gn_g, gn_b, w_out, norm_ffn, router_group_w, router_group_b, router_expert_w, router_expert_b, expert_w_gate, expert_w_up, expert_w_down, norm_final):
    B, T, D = x.shape
    depth = w_in.shape[0]
    d_c = conv_dw.shape[2]
    d_r = w0.shape[1]
    d_main = 2 * d_c + 3 * d_r
    nt = B * T
    n_rows_pad = -(-(nt * TOP_K + N_EXPERTS * (MOE_ROWS - 1)) // MOE_ROWS) * MOE_ROWS
    h = x.reshape(nt, D)
    for l in range(depth):
        proj_main, proj_lora = in_proj(h, norm_mix[l], w_in[l], d_main)
        y_conv = conv_mixer(proj_main, conv_dw[l], conv_b[l], conv_ln_g[l], conv_ln_b[l], batch=B)
        y_rwkv = rwkv_mixer(proj_main, proj_lora, 2 * d_c, shift_mu[l], w0[l], w_lora_up[l], a0[l], a_lora_up[l],
                            g_lora_up[l], k_k[l], k_a[l], r_k[l].reshape(-1), gn_g[l], gn_b[l], batch=B)
        h, v, eid, gates = out_router(h, y_conv, y_rwkv, w_out[l].astype(BF16), norm_ffn[l], router_group_w[l],
                                      router_group_b[l], router_expert_w[l], router_expert_b[l])
        dest, block_e, n_used = slot_plan(eid[:, :TOP_K], n_rows_pad)
        xs = dispatch(v, dest, n_rows_pad)
        ys = experts(xs, block_e, n_used, expert_w_gate[l], expert_w_up[l], expert_w_down[l], n_rows_pad)
        assert depth == 1
        h = combine(h, gates, dest, ys, norm_final)
    return h.reshape(B, T, D)
```

```python
import functools

import jax
import jax.numpy as jnp
from jax import lax
from jax.experimental import pallas as pl
from jax.experimental.pallas import tpu as pltpu

F32 = jnp.float32
BF16 = jnp.bfloat16

RWKV_HEAD = 64
CONV_WIDTH = 31
N_GROUPS = 4
EXPERTS_PER_GROUP = 8
N_EXPERTS = N_GROUPS * EXPERTS_PER_GROUP
TOP_K = 2
RMS_EPS = 1e-6
LN_EPS = 1e-5
GN_EPS = 64e-5
LANES = 128
SUBLANES = 8
VMEM_LIMIT = 56 * 1024 * 1024

_NT = (((1,), (1,)), ((), ()))
_TN = (((0,), (0,)), ((), ()))


def _dot(a, b, dims=None):
    a = a.astype(BF16)
    b = b.astype(BF16)
    if dims is None:
        return jnp.dot(a, b, preferred_element_type=F32)
    return lax.dot_general(a, b, dims, preferred_element_type=F32)


def _dot_split(x, m):
    hi = x.astype(BF16)
    lo = (x - hi.astype(F32)).astype(BF16)
    return (jnp.dot(hi, m, preferred_element_type=F32)
            + jnp.dot(lo, m, preferred_element_type=F32))


def _pack_bf16_pairs(x):
    w = x.shape[1] // 2
    lo = lax.bitcast_convert_type(x[:, :w].astype(BF16).astype(F32), jnp.uint32)
    hi = lax.bitcast_convert_type(x[:, w:].astype(BF16).astype(F32), jnp.uint32)
    return (lo >> 16) | (hi & jnp.uint32(0xFFFF0000))


def _unpack_bf16_pairs(p):
    lo = lax.bitcast_convert_type(p << 16, F32)
    hi = lax.bitcast_convert_type(p & jnp.uint32(0xFFFF0000), F32)
    return jnp.concatenate([lo, hi], axis=1)


def _rwkv_kernel(r_ref, k_ref, v_ref, lora_ref, mu_main_ref, mu_lora_ref, w0_ref, wup_ref, a0_ref, aup_ref,
                 gup_ref, kk_ref, ka_ref, rk_ref, gng_ref, gnb_ref, y_ref,
                 state_ref, prev_main_ref, prev_lora_ref, *, n_heads, d_decay, d_aaa):
    nb, C, d_r = r_ref.shape
    N = RWKV_HEAD
    R = nb * C

    @pl.when(pl.program_id(0) == 0)
    def _():
        state_ref[...] = jnp.zeros_like(state_ref)
        prev_main_ref[...] = jnp.zeros_like(prev_main_ref)
        prev_lora_ref[...] = jnp.zeros_like(prev_lora_ref)

    row = lax.broadcasted_iota(jnp.int32, (R, 1), 0)

    def shifted(x_ref, prev_ref, j, mu):
        x = x_ref[...].reshape(R, x_ref.shape[2])
        xp = pltpu.roll(x, 1, 0)
        for b in range(nb):
            xp = jnp.where(row == b * C, prev_ref[b, j:j + 1, :], xp)
            prev_ref[b, j:j + 1, :] = x[(b + 1) * C - 1:(b + 1) * C, :]
        return x + (xp - x) * mu

    r = shifted(r_ref, prev_main_ref, 0, mu_main_ref[0:1, :])
    k = shifted(k_ref, prev_main_ref, 1, mu_main_ref[1:2, :])
    v = shifted(v_ref, prev_main_ref, 2, mu_main_ref[2:3, :])
    lo = shifted(lora_ref, prev_lora_ref, 0, mu_lora_ref[...])

    wd = lo[:, :d_decay]
    ad = lo[:, d_decay:d_decay + d_aaa]
    gd = lo[:, d_decay + d_aaa:]

    z = w0_ref[...] + _dot(jnp.tanh(wd), wup_ref[...])
    w = -(jnp.maximum(-z, 0.0) + jnp.log(1.0 + jnp.exp(-jnp.abs(z)))) - 0.5
    logd = -jnp.exp(w)
    a = jax.nn.sigmoid(a0_ref[...] + _dot(ad, aup_ref[...]))
    g = _dot(jax.nn.sigmoid(gd), gup_ref[...])

    li = lax.broadcasted_iota(jnp.int32, (LANES, LANES), 0) // N
    lj = lax.broadcasted_iota(jnp.int32, (LANES, LANES), 1) // N
    head_ones = (li == lj).astype(BF16)

    def head_sum(x):
        return jnp.concatenate(
            [_dot_split(x[:, c * LANES:(c + 1) * LANES], head_ones) for c in range(d_r // LANES)], axis=1)

    kkr = k * kk_ref[...]
    kk = kkr / jnp.maximum(jnp.sqrt(head_sum(kkr * kkr)), 1e-12)
    k2 = k * (1.0 + (a - 1.0) * ka_ref[...])
    a_s = -kk
    b_s = kk * a

    ri = lax.broadcasted_iota(jnp.int32, (R, R), 0)
    rj = lax.broadcasted_iota(jnp.int32, (R, R), 1)
    tri = ((rj <= ri) & ((ri // C) == (rj // C))).astype(BF16)
    hi = logd.astype(BF16)
    rem = logd - hi.astype(F32)
    mid = rem.astype(BF16)
    low = (rem - mid.astype(F32)).astype(BF16)
    cum = (jnp.dot(tri, hi, preferred_element_type=F32) + jnp.dot(tri, mid, preferred_element_type=F32)
           + jnp.dot(tri, low, preferred_element_type=F32))
    p_incl = jnp.exp(cum)
    p_excl = jnp.exp(cum - logd)
    p_inv = jnp.exp(-cum)
    p_last = [p_incl[(b + 1) * C - 1:(b + 1) * C, :] for b in range(nb)]
    p_last_rows = jnp.concatenate([jnp.broadcast_to(p, (C, d_r)) for p in p_last], axis=0)

    at = (a_s * p_excl).astype(BF16)
    rt = (r * p_incl).astype(BF16)
    bt = b_s * p_inv
    kt = k2 * p_inv
    bh = (bt * p_last_rows).astype(BF16)
    kh = (kt * p_last_rows).astype(BF16)
    bt = bt.astype(BF16)
    kt = kt.astype(BF16)
    vb = v.astype(BF16)

    ti = lax.broadcasted_iota(jnp.int32, (C, C), 0)
    tj = lax.broadcasted_iota(jnp.int32, (C, C), 1)
    strict = tj < ti
    incl = tj <= ti
    mask2 = jnp.concatenate([strict, incl], axis=0)
    eye = (ti == tj).astype(F32)
    level_masks = []
    s = 1
    while s < C:
        level_masks.append(((ti // (2 * s)) == (tj // (2 * s))) & (((ti // s) % 2) == 1) & (((tj // s) % 2) == 0))
        s *= 2

    units = [(b, h) for b in range(nb) for h in range(n_heads)]
    idx = range(len(units))
    rs = [slice(b * C, (b + 1) * C) for b, _ in units]
    ls = [slice(h * N, (h + 1) * N) for _, h in units]
    ar = [jnp.concatenate([at[rs[i], ls[i]], rt[rs[i], ls[i]]], axis=0) for i in idx]
    g_b = [jnp.where(mask2, _dot(ar[i], bt[rs[i], ls[i]], _NT), 0.0) for i in idx]
    g_k = [jnp.where(mask2, _dot(ar[i], kt[rs[i], ls[i]], _NT), 0.0) for i in idx]
    s0 = [state_ref[i] for i in idx]
    ars = [_dot(ar[i], s0[i], _NT) for i in idx]
    gv = [_dot(g_k[i], vb[rs[i], ls[i]]) for i in idx]
    tinv = [eye + jnp.where(level_masks[0], g_b[i][:C], 0.0) for i in idx]
    for lm in level_masks[1:]:
        xs = [_dot(jnp.where(lm, g_b[i][:C], 0.0), tinv[i]) for i in idx]
        tinv = [tinv[i] + _dot(tinv[i], xs[i]) for i in idx]
    u = [_dot(tinv[i], ars[i][:C] + gv[i][:C]) for i in idx]
    ys = [ars[i][C:] + gv[i][C:] + _dot(g_b[i][C:], u[i]) for i in idx]
    for i in idx:
        uv = jnp.concatenate([u[i].astype(BF16), vb[rs[i], ls[i]]], axis=0)
        bk = jnp.concatenate([bh[rs[i], ls[i]], kh[rs[i], ls[i]]], axis=0)
        state_ref[i] = s0[i] * p_last[units[i][0]][:, ls[i]] + _dot(uv, bk, _TN)
    y = jnp.concatenate([jnp.concatenate(ys[b * n_heads:(b + 1) * n_heads], axis=1) for b in range(nb)],
                        axis=0)

    inv_n = 1.0 / N
    mu_y = head_sum(y) * inv_n
    yc = y - mu_y
    var_y = head_sum(yc * yc) * inv_n
    yn = yc * lax.rsqrt(var_y + GN_EPS) * gng_ref[...] + gnb_ref[...]
    bonus = head_sum(r * k2 * rk_ref[...]) * v
    y_ref[...] = ((yn + bonus) * g).reshape(nb, C, d_r).astype(y_ref.dtype)


def rwkv_mixer(proj_main, proj_lora, col0, shift_mu, w0, w_up, a0, a_up, g_up, k_k, k_a, r_k, gn_g, gn_b,
               *, batch, chunk=64):
    nt = proj_main.shape[0]
    seq = nt // batch
    d_r = w0.shape[-1]
    n_heads = d_r // RWKV_HEAD
    d_lora = proj_lora.shape[1]
    d_decay, d_aaa = w_up.shape[0], a_up.shape[0]
    n_chunks = seq // chunk
    assert seq % chunk == 0 and col0 % d_r == 0
    cb = col0 // d_r
    row2 = lambda t: t.reshape(1, -1).astype(F32)
    mu_main = shift_mu[:3 * d_r].reshape(3, d_r)
    mu_lora = shift_mu[3 * d_r:].reshape(1, d_lora)
    pm3 = proj_main.reshape(batch, seq, proj_main.shape[1])
    pl3 = proj_lora.reshape(batch, seq, d_lora)

    def main_spec(j):
        return pl.BlockSpec((batch, chunk, d_r), lambda c, j=j: (0, c, cb + j))

    full = lambda shape: pl.BlockSpec(shape, lambda c: (0,) * len(shape))
    kern = functools.partial(_rwkv_kernel, n_heads=n_heads, d_decay=d_decay, d_aaa=d_aaa)
    y = pl.pallas_call(
        kern,
        grid=(n_chunks,),
        in_specs=[main_spec(0), main_spec(1), main_spec(2),
                  pl.BlockSpec((batch, chunk, d_lora), lambda c: (0, c, 0)),
                  full((3, d_r)), full((1, d_lora)), full((1, d_r)), full((d_decay, d_r)), full((1, d_r)),
                  full((d_aaa, d_r)), full((g_up.shape[0], d_r)), full((1, d_r)), full((1, d_r)), full((1, d_r)),
                  full((1, d_r)), full((1, d_r))],
        out_specs=pl.BlockSpec((batch, chunk, d_r), lambda c: (0, c, 0)),
        out_shape=jax.ShapeDtypeStruct((batch, seq, d_r), F32),
        scratch_shapes=[pltpu.VMEM((batch * n_heads, RWKV_HEAD, RWKV_HEAD), F32),
                        pltpu.VMEM((batch, 3, d_r), F32),
                        pltpu.VMEM((batch, 1, d_lora), F32)],
        compiler_params=pltpu.CompilerParams(dimension_semantics=("arbitrary",), vmem_limit_bytes=VMEM_LIMIT),
        name="rwkv_mixer",
    )(pm3, pm3, pm3, pl3, mu_main, mu_lora, row2(w0), w_up.astype(BF16), row2(a0),
      a_up.astype(BF16), g_up.astype(BF16), row2(k_k), row2(k_a), row2(r_k), row2(gn_g), row2(gn_b))
    return y.reshape(nt, d_r)


def _rms(x, g):
    return x * lax.rsqrt(jnp.mean(x * x, axis=-1, keepdims=True) + RMS_EPS) * g


def _inproj_kernel(x_ref, g_ref, wt_ref, wlt_ref, o_ref, ol_ref, u_ref):
    @pl.when(pl.program_id(1) == 0)
    def _():
        u = _rms(x_ref[...], g_ref[...]).astype(BF16)
        u_ref[...] = u
        ol_ref[...] = lax.dot_general(u, wlt_ref[...], _NT, preferred_element_type=F32)

    o_ref[...] = lax.dot_general(u_ref[...], wt_ref[...].astype(BF16), _NT, preferred_element_type=F32)


def in_proj(x2, norm_g, w_in, d_main, *, tm=1024, tn=1024):
    nt, d = x2.shape
    d_lora = w_in.shape[1] - d_main
    assert nt % tm == 0 and d_main % tn == 0
    w_t = w_in.T
    w_lora_t = w_t[d_main:].astype(BF16)
    return pl.pallas_call(
        _inproj_kernel,
        grid=(nt // tm, d_main // tn),
        in_specs=[pl.BlockSpec((tm, d), lambda i, j: (i, 0)),
                  pl.BlockSpec((1, d), lambda i, j: (0, 0)),
                  pl.BlockSpec((tn, d), lambda i, j: (j, 0)),
                  pl.BlockSpec((d_lora, d), lambda i, j: (0, 0))],
        out_specs=[pl.BlockSpec((tm, tn), lambda i, j: (i, j)),
                   pl.BlockSpec((tm, d_lora), lambda i, j: (i, 0))],
        out_shape=[jax.ShapeDtypeStruct((nt, d_main), F32), jax.ShapeDtypeStruct((nt, d_lora), F32)],
        scratch_shapes=[pltpu.VMEM((tm, d), BF16)],
        compiler_params=pltpu.CompilerParams(dimension_semantics=("arbitrary", "arbitrary"),
                                             vmem_limit_bytes=VMEM_LIMIT),
        name="in_proj",
    )(x2, norm_g.reshape(1, d).astype(F32), w_t, w_lora_t)


CONV_HALO = 32
CONV_ROWS = 32


def _conv_kernel(val_ref, gate_ref, dw_ref, b_ref, g_ref, beta_ref, o_ref, ubuf_ref):
    tt = val_ref.shape[0]
    d_c = val_ref.shape[1]
    t_idx = pl.program_id(1)

    @pl.when(t_idx == 0)
    def _():
        ubuf_ref[0, 0:CONV_HALO, :] = jnp.zeros((CONV_HALO, d_c), F32)

    @pl.when(t_idx > 0)
    def _():
        ubuf_ref[0, 0:CONV_HALO, :] = ubuf_ref[0, tt:tt + CONV_HALO, :]

    ubuf_ref[0, CONV_HALO:CONV_HALO + tt, :] = val_ref[...] * jax.nn.sigmoid(gate_ref[...])
    n_sh = tt + CONV_HALO - SUBLANES
    for s in range(1, SUBLANES):
        for r in range(0, n_sh, CONV_ROWS):
            n = min(CONV_ROWS, n_sh - r)
            ubuf_ref[s, r:r + n, :] = ubuf_ref[0, r + s:r + s + n, :]
    dw = dw_ref[...]
    lead = CONV_HALO - (CONV_WIDTH - 1)
    for c in range(tt // CONV_ROWS):
        r0 = c * CONV_ROWS
        acc = jnp.zeros((CONV_ROWS, d_c), F32)
        for j in range(CONV_WIDTH):
            q, s = divmod(lead + j, SUBLANES)
            r = r0 + q * SUBLANES
            acc = acc + dw[j:j + 1, :] * ubuf_ref[s, r:r + CONV_ROWS, :]
        acc = acc + b_ref[...]
        mu = jnp.mean(acc, axis=-1, keepdims=True)
        cen = acc - mu
        var = jnp.mean(cen * cen, axis=-1, keepdims=True)
        yv = cen * lax.rsqrt(var + LN_EPS) * g_ref[...] + beta_ref[...]
        o_ref[r0:r0 + CONV_ROWS, :] = yv * jax.nn.sigmoid(yv)


def conv_mixer(proj_main, dw, bias, ln_g, ln_b, *, batch, tt=256):
    nt = proj_main.shape[0]
    seq = nt // batch
    d_c = dw.shape[1]
    n_t = seq // tt
    assert seq % tt == 0 and tt % CONV_ROWS == 0
    row = lambda t: t.reshape(1, d_c).astype(F32)
    full = lambda shape: pl.BlockSpec(shape, lambda b, t: (0, 0))
    return pl.pallas_call(
        _conv_kernel,
        grid=(batch, n_t),
        in_specs=[pl.BlockSpec((tt, d_c), lambda b, t: (b * n_t + t, 0)),
                  pl.BlockSpec((tt, d_c), lambda b, t: (b * n_t + t, 1)),
                  full((CONV_WIDTH, d_c)), full((1, d_c)), full((1, d_c)), full((1, d_c))],
        out_specs=pl.BlockSpec((tt, d_c), lambda b, t: (b * n_t + t, 0)),
        out_shape=jax.ShapeDtypeStruct((nt, d_c), F32),
        scratch_shapes=[pltpu.VMEM((SUBLANES, tt + CONV_HALO, d_c), F32)],
        compiler_params=pltpu.CompilerParams(dimension_semantics=("arbitrary", "arbitrary"),
                                             vmem_limit_bytes=VMEM_LIMIT),
        name="conv_mixer",
    )(proj_main, proj_main, dw.astype(F32), row(bias), row(ln_g), row(ln_b))


ROUTER_SUB = 512


def _out_router_kernel(x_ref, yc_ref, yr_ref, wc_ref, wr_ref, nf_ref, rw_hi_ref, rw_lo_ref, rb_ref,
                       h_ref, v_ref, eid_ref, gate_ref, count_ref, cnt_ref):
    tm = x_ref.shape[0]
    subs = [slice(r, r + ROUTER_SUB) for r in range(0, tm, ROUTER_SUB)]
    hs = [x_ref[sl, :] + jnp.dot(yc_ref[sl, :].astype(BF16), wc_ref[...], preferred_element_type=F32)
          + jnp.dot(yr_ref[sl, :].astype(BF16), wr_ref[...], preferred_element_type=F32) for sl in subs]

    @pl.when(pl.program_id(0) == 0)
    def _():
        cnt_ref[...] = jnp.zeros_like(cnt_ref)

    lane = lax.broadcasted_iota(jnp.int32, (ROUTER_SUB, LANES), 1)
    neg = jnp.float32(-jnp.inf)
    big = jnp.int32(LANES)
    earlier = (lax.broadcasted_iota(jnp.int32, (ROUTER_SUB, ROUTER_SUB), 1)
               < lax.broadcasted_iota(jnp.int32, (ROUTER_SUB, ROUTER_SUB), 0)).astype(BF16)

    def first_max(vals):
        m = jnp.max(vals, axis=-1, keepdims=True)
        return m, jnp.min(jnp.where(vals == m, lane, big), axis=-1, keepdims=True)

    for sl, h in zip(subs, hs):
        h_ref[sl, :] = h
        v = _rms(h, nf_ref[...])
        v_ref[sl, :] = _pack_bf16_pairs(v)
        v_hi = v.astype(BF16)
        v_lo = (v - v_hi.astype(F32)).astype(BF16)
        logits = (jnp.dot(v_hi, rw_hi_ref[...], preferred_element_type=F32)
                  + jnp.dot(v_lo, rw_hi_ref[...], preferred_element_type=F32)
                  + jnp.dot(v_hi, rw_lo_ref[...], preferred_element_type=F32)) + rb_ref[...]
        gl = jnp.where(lane < N_GROUPS, logits, neg)
        gmax, grp = first_max(gl)
        p_grp = 1.0 / jnp.sum(jnp.exp(gl - gmax), axis=-1, keepdims=True)
        lo = N_GROUPS + grp * EXPERTS_PER_GROUP
        el = jnp.where((lane >= lo) & (lane < lo + EXPERTS_PER_GROUP), logits, neg)
        m1, i1 = first_max(el)
        m2, i2 = first_max(jnp.where(lane == i1, neg, el))
        e2 = jnp.exp(m2 - m1)
        g1 = p_grp / (1.0 + e2)
        g2 = p_grp * e2 / (1.0 + e2)
        gate_ref[sl, :] = jnp.where(lane == 0, g1, jnp.where(lane == 1, g2, 0.0))

        x1 = i1 - N_GROUPS
        x2 = i2 - N_GROUPS
        oh1 = (lane == x1).astype(F32)
        oh2 = (lane == x2).astype(F32)
        before1 = jnp.dot(earlier, oh1.astype(BF16), preferred_element_type=F32)
        before2 = jnp.dot(earlier, oh2.astype(BF16), preferred_element_type=F32)
        carry = cnt_ref[...]
        n1 = jnp.sum(oh1, axis=0, keepdims=True)
        n2 = jnp.sum(oh2, axis=0, keepdims=True)
        rank1 = jnp.sum(oh1 * (before1 + carry), axis=-1, keepdims=True).astype(jnp.int32)
        rank2 = jnp.sum(oh2 * (before2 + carry + n1), axis=-1, keepdims=True).astype(jnp.int32)
        cnt_ref[...] = carry + n1 + n2
        eid_ref[sl, :] = jnp.where(lane == 0, x1, jnp.where(lane == 1, x2, jnp.where(
            lane == 2, rank1, jnp.where(lane == 3, rank2, 0))))
    count_ref[...] = cnt_ref[...].astype(jnp.int32)


def out_router(x2, y_conv, y_rwkv, w_out, norm_ffn, rg_w, rg_b, re_w, re_b, *, tm=512):
    nt, d = x2.shape
    d_c, d_r = y_conv.shape[1], y_rwkv.shape[1]
    assert d_c == d_r and nt % tm == 0 and tm % ROUTER_SUB == 0
    n_r = N_GROUPS + N_EXPERTS
    rw = jnp.zeros((d, LANES), F32).at[:, :n_r].set(jnp.concatenate([rg_w, re_w], axis=1).astype(F32))
    rw_hi = rw.astype(BF16)
    rw_lo = (rw - rw_hi.astype(F32)).astype(BF16)
    rb = jnp.zeros((1, LANES), F32).at[0, :n_r].set(jnp.concatenate([rg_b, re_b]).astype(F32))
    full = lambda shape: pl.BlockSpec(shape, lambda i: (0, 0))
    once = lambda shape, r: pl.BlockSpec(shape, lambda i, r=r: (r, 0), pipeline_mode=pl.Buffered(1))
    rows = lambda w: pl.BlockSpec((tm, w), lambda i: (i, 0))
    return pl.pallas_call(
        _out_router_kernel,
        grid=(nt // tm,),
        in_specs=[rows(d), rows(d_c), rows(d_r), once((d_c, d), 0), once((d_r, d), 1),
                  full((1, d)), full((d, LANES)), full((d, LANES)), full((1, LANES))],
        out_specs=[rows(d), rows(d // 2), rows(LANES), rows(LANES), full((1, LANES))],
        out_shape=[jax.ShapeDtypeStruct((nt, d), F32), jax.ShapeDtypeStruct((nt, d // 2), jnp.uint32),
                   jax.ShapeDtypeStruct((nt, LANES), jnp.int32), jax.ShapeDtypeStruct((nt, LANES), F32),
                   jax.ShapeDtypeStruct((1, LANES), jnp.int32)],
        scratch_shapes=[pltpu.VMEM((1, LANES), F32)],
        compiler_params=pltpu.CompilerParams(dimension_semantics=("arbitrary",), vmem_limit_bytes=VMEM_LIMIT),
        name="out_router",
    )(x2, y_conv, y_rwkv, w_out, w_out, norm_ffn.reshape(1, d).astype(F32), rw_hi, rw_lo, rb)


MOE_ROWS = 256


def slot_plan(routing, counts, n_rows_pad):
    expert_id = routing[:, :TOP_K]
    rank = routing[:, TOP_K:2 * TOP_K]
    padded = (counts + MOE_ROWS - 1) // MOE_ROWS * MOE_ROWS
    pend = jnp.cumsum(padded)
    pstarts = pend - padded
    dest = (pstarts[expert_id] + rank).astype(jnp.int32)
    n_blocks = n_rows_pad // MOE_ROWS
    block_row0 = jnp.arange(n_blocks, dtype=jnp.int32) * MOE_ROWS
    block_e = jnp.minimum(jnp.sum((pend[None, :] <= block_row0[:, None]).astype(jnp.int32), axis=1),
                          N_EXPERTS - 1).astype(jnp.int32)
    n_used = (pend[-1:] // MOE_ROWS).astype(jnp.int32)
    ids = jnp.arange(N_EXPERTS, dtype=jnp.int32)
    owner = jnp.where(counts > 0, ids, N_EXPERTS)
    nxt = lax.cummin(jnp.concatenate([owner[1:], jnp.full((1,), N_EXPERTS, jnp.int32)]), reverse=True)
    next_used = jnp.where(nxt < N_EXPERTS, nxt, -1).astype(jnp.int32)
    used_ord = (jnp.cumsum((counts > 0).astype(jnp.int32)) - 1).astype(jnp.int32)
    return dest, block_e, n_used, next_used, used_ord


def _dispatch_kernel(dest_ref, v_ref, xs_in_ref, xs_ref, sem):
    del xs_in_ref
    tm = v_ref.shape[0]

    def copy(i, k):
        d = dest_ref[0, 0, i * TOP_K + k]
        return pltpu.make_async_copy(v_ref.at[pl.ds(i, 1), :], xs_ref.at[pl.ds(d, 1), :], sem)

    def start(i, carry):
        for k in range(TOP_K):
            copy(i, k).start(priority=k)
        return carry

    def wait(i, carry):
        for k in range(TOP_K):
            copy(i, k).wait()
        return carry

    lax.fori_loop(0, tm, start, 0, unroll=8)
    lax.fori_loop(0, tm, wait, 0, unroll=8)


def dispatch(v, dest, n_rows_pad, *, tm=1024):
    nt, d = v.shape
    assert nt % tm == 0
    dest3 = dest.reshape(nt // tm, 1, tm * TOP_K)
    xs0 = jnp.zeros((n_rows_pad, d), v.dtype)
    return pl.pallas_call(
        _dispatch_kernel,
        grid=(nt // tm,),
        in_specs=[pl.BlockSpec((1, 1, tm * TOP_K), lambda i: (i, 0, 0), memory_space=pltpu.SMEM),
                  pl.BlockSpec((tm, d), lambda i: (i, 0)),
                  pl.BlockSpec(memory_space=pl.ANY)],
        out_specs=pl.BlockSpec(memory_space=pl.ANY),
        out_shape=jax.ShapeDtypeStruct((n_rows_pad, d), v.dtype),
        scratch_shapes=[pltpu.SemaphoreType.DMA(())],
        input_output_aliases={2: 0},
        compiler_params=pltpu.CompilerParams(dimension_semantics=("arbitrary",), vmem_limit_bytes=VMEM_LIMIT),
        name="moe_dispatch",
    )(dest3, v, xs0)


def _experts_kernel(be_ref, nu_ref, nxt_ref, ord_ref, xs_ref, wg_hbm, wu_hbm, wd_hbm, ys_ref,
                    wg_f, wu_f, wd_f, wg_s, wu_s, wd_s, wsem):
    j = pl.program_id(0)
    e = be_ref[j]

    def fetch(ex, slot):
        return [pltpu.make_async_copy(wg_hbm.at[ex], wg_f.at[slot], wsem.at[slot]),
                pltpu.make_async_copy(wu_hbm.at[ex], wu_f.at[slot], wsem.at[slot]),
                pltpu.make_async_copy(wd_hbm.at[ex], wd_f.at[slot], wsem.at[slot])]

    @pl.when(j == 0)
    def _():
        for c in fetch(e, lax.rem(ord_ref[e], 2)):
            c.start()

    first = (j == 0) | (e != be_ref[jnp.maximum(j - 1, 0)])

    @pl.when(first & (j < nu_ref[0]))
    def _():
        slot = lax.rem(ord_ref[e], 2)
        for c in fetch(e, slot):
            c.wait()
        nxt = nxt_ref[e]

        @pl.when(nxt >= 0)
        def _():
            for c in fetch(nxt, 1 - slot):
                c.start()

        wg_s[...] = wg_f[slot].astype(BF16)
        wu_s[...] = wu_f[slot].astype(BF16)
        wd_s[...] = wd_f[slot].astype(BF16)

    @pl.when(j < nu_ref[0])
    def _():
        xb = _unpack_bf16_pairs(xs_ref[...]).astype(BF16)
        gt = jnp.dot(xb, wg_s[...], preferred_element_type=F32)
        up = jnp.dot(xb, wu_s[...], preferred_element_type=F32)
        hid = (gt * jax.nn.sigmoid(gt) * up).astype(BF16)
        ys_ref[...] = _pack_bf16_pairs(jnp.dot(hid, wd_s[...], preferred_element_type=F32))

    @pl.when(j >= nu_ref[0])
    def _():
        ys_ref[...] = jnp.zeros_like(ys_ref)


def experts(xs, block_e, n_used, next_used, used_ord, w_gate, w_up, w_down, n_rows_pad):
    dh = xs.shape[1]
    d, d_e = w_gate.shape[1], w_gate.shape[2]
    assert d == 2 * dh
    n_blocks = n_rows_pad // MOE_ROWS
    hbm = pl.BlockSpec(memory_space=pl.ANY)
    grid_spec = pltpu.PrefetchScalarGridSpec(
        num_scalar_prefetch=4,
        grid=(n_blocks,),
        in_specs=[pl.BlockSpec((MOE_ROWS, dh), lambda j, be, nu, nx, od: (jnp.minimum(j, nu[0] - 1), 0)),
                  hbm, hbm, hbm],
        out_specs=pl.BlockSpec((MOE_ROWS, dh), lambda j, be, nu, nx, od: (j, 0)),
        scratch_shapes=[pltpu.VMEM((2, d, d_e), F32), pltpu.VMEM((2, d, d_e), F32), pltpu.VMEM((2, d_e, d), F32),
                        pltpu.VMEM((d, d_e), BF16), pltpu.VMEM((d, d_e), BF16), pltpu.VMEM((d_e, d), BF16),
                        pltpu.SemaphoreType.DMA((2,))],
    )
    return pl.pallas_call(
        _experts_kernel,
        grid_spec=grid_spec,
        out_shape=jax.ShapeDtypeStruct((n_rows_pad, dh), jnp.uint32),
        compiler_params=pltpu.CompilerParams(dimension_semantics=("arbitrary",), vmem_limit_bytes=VMEM_LIMIT),
        name="moe_experts",
    )(block_e, n_used, next_used, used_ord, xs, w_gate, w_up, w_down)


def _combine_kernel(dest_ref, dest_next_ref, h_ref, gate_ref, nfin_ref, ys_ref, o_ref, ybuf, sem):
    step = pl.program_id(0)
    tm = h_ref.shape[0]
    slot = lax.rem(step, 2)

    def copy(d_ref, s, i, k):
        d = d_ref[0, 0, i * TOP_K + k]
        return pltpu.make_async_copy(ys_ref.at[pl.ds(d, 1), :], ybuf.at[s, k, pl.ds(i, 1), :], sem.at[s])

    def start_all(d_ref, s):
        def body(i, carry):
            for k in range(TOP_K):
                copy(d_ref, s, i, k).start(priority=k)
            return carry
        lax.fori_loop(0, tm, body, 0, unroll=8)

    @pl.when(step == 0)
    def _():
        start_all(dest_ref, 0)

    @pl.when(step + 1 < pl.num_programs(0))
    def _():
        start_all(dest_next_ref, 1 - slot)

    def wait_body(i, carry):
        for k in range(TOP_K):
            copy(dest_ref, slot, i, k).wait()
        return carry

    lax.fori_loop(0, tm, wait_body, 0, unroll=8)
    gates = gate_ref[...]
    moe = (gates[:, 0:1] * _unpack_bf16_pairs(ybuf[slot, 0])
           + gates[:, 1:2] * _unpack_bf16_pairs(ybuf[slot, 1]))
    o_ref[...] = _rms(h_ref[...] + moe, nfin_ref[...])


def combine(h, gates, dest, ys, norm_final, *, tm=256):
    nt, d = h.shape
    assert nt % tm == 0 and ys.shape[1] * 2 == d
    n_tiles = nt // tm
    dest3 = dest.reshape(n_tiles, 1, tm * TOP_K)
    return pl.pallas_call(
        _combine_kernel,
        grid=(n_tiles,),
        in_specs=[pl.BlockSpec((1, 1, tm * TOP_K), lambda i: (i, 0, 0), memory_space=pltpu.SMEM),
                  pl.BlockSpec((1, 1, tm * TOP_K), lambda i: (jnp.minimum(i + 1, n_tiles - 1), 0, 0),
                               memory_space=pltpu.SMEM),
                  pl.BlockSpec((tm, d), lambda i: (i, 0)),
                  pl.BlockSpec((tm, LANES), lambda i: (i, 0)),
                  pl.BlockSpec((1, d), lambda i: (0, 0)),
                  pl.BlockSpec(memory_space=pl.ANY)],
        out_specs=pl.BlockSpec((tm, d), lambda i: (i, 0)),
        out_shape=jax.ShapeDtypeStruct((nt, d), F32),
        scratch_shapes=[pltpu.VMEM((2, TOP_K, tm, d // 2), jnp.uint32), pltpu.SemaphoreType.DMA((2,))],
        compiler_params=pltpu.CompilerParams(dimension_semantics=("arbitrary",), vmem_limit_bytes=VMEM_LIMIT),
        name="moe_combine",
    )(dest3, dest3, h, gates, norm_final.reshape(1, d).astype(F32), ys)


def kernel(x, norm_mix, w_in, conv_dw, conv_b, conv_ln_g, conv_ln_b, shift_mu, w0, w_lora_up, a0, a_lora_up, g_lora_up, k_k, k_a, r_k, gn_g, gn_b, w_out, norm_ffn, router_group_w, router_group_b, router_expert_w, router_expert_b, expert_w_gate, expert_w_up, expert_w_down, norm_final):
    B, T, D = x.shape
    depth = w_in.shape[0]
    d_c = conv_dw.shape[2]
    d_r = w0.shape[1]
    d_main = 2 * d_c + 3 * d_r
    nt = B * T
    n_rows_pad = -(-(nt * TOP_K + N_EXPERTS * (MOE_ROWS - 1)) // MOE_ROWS) * MOE_ROWS
    h = x.reshape(nt, D)
    for l in range(depth):
        proj_main, proj_lora = in_proj(h, norm_mix[l], w_in[l], d_main)
        y_conv = conv_mixer(proj_main, conv_dw[l], conv_b[l], conv_ln_g[l], conv_ln_b[l], batch=B)
        y_rwkv = rwkv_mixer(proj_main, proj_lora, 2 * d_c, shift_mu[l], w0[l], w_lora_up[l], a0[l], a_lora_up[l],
                            g_lora_up[l], k_k[l], k_a[l], r_k[l].reshape(-1), gn_g[l], gn_b[l], batch=B)
        h, v, routing, gates, counts = out_router(h, y_conv, y_rwkv, w_out[l].astype(BF16), norm_ffn[l],
                                                  router_group_w[l], router_group_b[l], router_expert_w[l],
                                                  router_expert_b[l])
        dest, block_e, n_used, next_used, used_ord = slot_plan(routing, counts[0, :N_EXPERTS], n_rows_pad)
        xs = dispatch(v, dest, n_rows_pad)
        ys = experts(xs, block_e, n_used, next_used, used_ord, expert_w_gate[l], expert_w_up[l], expert_w_down[l],
                     n_rows_pad)
        assert depth == 1
        h = combine(h, gates, dest, ys, norm_final)
    return h.reshape(B, T, D)
```

```python
import functools

import jax
import jax.numpy as jnp
from jax import lax
from jax.experimental import pallas as pl
from jax.experimental.pallas import tpu as pltpu

F32 = jnp.float32
BF16 = jnp.bfloat16

RWKV_HEAD = 64
CONV_WIDTH = 31
N_GROUPS = 4
EXPERTS_PER_GROUP = 8
N_EXPERTS = N_GROUPS * EXPERTS_PER_GROUP
TOP_K = 2
RMS_EPS = 1e-6
LN_EPS = 1e-5
GN_EPS = 64e-5
LANES = 128
SUBLANES = 8
VMEM_LIMIT = 56 * 1024 * 1024

_NT = (((1,), (1,)), ((), ()))
_TN = (((0,), (0,)), ((), ()))


def _dot(a, b, dims=None):
    a = a.astype(BF16)
    b = b.astype(BF16)
    if dims is None:
        return jnp.dot(a, b, preferred_element_type=F32)
    return lax.dot_general(a, b, dims, preferred_element_type=F32)


def _dot_split(x, m):
    hi = x.astype(BF16)
    lo = (x - hi.astype(F32)).astype(BF16)
    return (jnp.dot(hi, m, preferred_element_type=F32)
            + jnp.dot(lo, m, preferred_element_type=F32))


def _pack_bf16_pairs(x):
    w = x.shape[1] // 2
    lo = lax.bitcast_convert_type(x[:, :w].astype(BF16).astype(F32), jnp.uint32)
    hi = lax.bitcast_convert_type(x[:, w:].astype(BF16).astype(F32), jnp.uint32)
    return (lo >> 16) | (hi & jnp.uint32(0xFFFF0000))


def _unpack_bf16_pairs(p):
    lo = lax.bitcast_convert_type(p << 16, F32)
    hi = lax.bitcast_convert_type(p & jnp.uint32(0xFFFF0000), F32)
    return jnp.concatenate([lo, hi], axis=1)


def _rwkv_kernel(r_ref, k_ref, v_ref, lora_ref, mu_main_ref, mu_lora_ref, w0_ref, wup_ref, a0_ref, aup_ref,
                 gup_ref, kk_ref, ka_ref, rk_ref, gng_ref, gnb_ref, y_ref,
                 state_ref, prev_main_ref, prev_lora_ref, *, n_heads, d_decay, d_aaa):
    nb, C, d_r = r_ref.shape
    N = RWKV_HEAD
    R = nb * C

    @pl.when(pl.program_id(0) == 0)
    def _():
        state_ref[...] = jnp.zeros_like(state_ref)
        prev_main_ref[...] = jnp.zeros_like(prev_main_ref)
        prev_lora_ref[...] = jnp.zeros_like(prev_lora_ref)

    row = lax.broadcasted_iota(jnp.int32, (R, 1), 0)

    def shifted(x_ref, prev_ref, j, mu):
        x = x_ref[...].reshape(R, x_ref.shape[2])
        xp = pltpu.roll(x, 1, 0)
        for b in range(nb):
            xp = jnp.where(row == b * C, prev_ref[b, j:j + 1, :], xp)
            prev_ref[b, j:j + 1, :] = x[(b + 1) * C - 1:(b + 1) * C, :]
        return x + (xp - x) * mu

    r = shifted(r_ref, prev_main_ref, 0, mu_main_ref[0:1, :])
    k = shifted(k_ref, prev_main_ref, 1, mu_main_ref[1:2, :])
    v = shifted(v_ref, prev_main_ref, 2, mu_main_ref[2:3, :])
    lo = shifted(lora_ref, prev_lora_ref, 0, mu_lora_ref[...])

    wd = lo[:, :d_decay]
    ad = lo[:, d_decay:d_decay + d_aaa]
    gd = lo[:, d_decay + d_aaa:]

    z = w0_ref[...] + _dot(jnp.tanh(wd), wup_ref[...])
    w = -(jnp.maximum(-z, 0.0) + jnp.log(1.0 + jnp.exp(-jnp.abs(z)))) - 0.5
    logd = -jnp.exp(w)
    a = jax.nn.sigmoid(a0_ref[...] + _dot(ad, aup_ref[...]))
    g = _dot(jax.nn.sigmoid(gd), gup_ref[...])

    li = lax.broadcasted_iota(jnp.int32, (LANES, LANES), 0) // N
    lj = lax.broadcasted_iota(jnp.int32, (LANES, LANES), 1) // N
    head_ones = (li == lj).astype(BF16)

    def head_sum(x):
        return jnp.concatenate(
            [_dot_split(x[:, c * LANES:(c + 1) * LANES], head_ones) for c in range(d_r // LANES)], axis=1)

    kkr = k * kk_ref[...]
    kk = kkr / jnp.maximum(jnp.sqrt(head_sum(kkr * kkr)), 1e-12)
    k2 = k * (1.0 + (a - 1.0) * ka_ref[...])
    a_s = -kk
    b_s = kk * a

    ri = lax.broadcasted_iota(jnp.int32, (R, R), 0)
    rj = lax.broadcasted_iota(jnp.int32, (R, R), 1)
    tri = ((rj <= ri) & ((ri // C) == (rj // C))).astype(BF16)
    hi = logd.astype(BF16)
    rem = logd - hi.astype(F32)
    mid = rem.astype(BF16)
    low = (rem - mid.astype(F32)).astype(BF16)
    cum = (jnp.dot(tri, hi, preferred_element_type=F32) + jnp.dot(tri, mid, preferred_element_type=F32)
           + jnp.dot(tri, low, preferred_element_type=F32))
    p_incl = jnp.exp(cum)
    p_excl = jnp.exp(cum - logd)
    p_inv = jnp.exp(-cum)
    p_last = [p_incl[(b + 1) * C - 1:(b + 1) * C, :] for b in range(nb)]
    p_last_rows = jnp.concatenate([jnp.broadcast_to(p, (C, d_r)) for p in p_last], axis=0)

    at = (a_s * p_excl).astype(BF16)
    rt = (r * p_incl).astype(BF16)
    bt = b_s * p_inv
    kt = k2 * p_inv
    bh = (bt * p_last_rows).astype(BF16)
    kh = (kt * p_last_rows).astype(BF16)
    bt = bt.astype(BF16)
    kt = kt.astype(BF16)
    vb = v.astype(BF16)

    ti = lax.broadcasted_iota(jnp.int32, (C, C), 0)
    tj = lax.broadcasted_iota(jnp.int32, (C, C), 1)
    strict = tj < ti
    incl = tj <= ti
    mask2 = jnp.concatenate([strict, incl], axis=0)
    eye = (ti == tj).astype(F32)
    level_masks = []
    s = 1
    while s < C:
        level_masks.append(((ti // (2 * s)) == (tj // (2 * s))) & (((ti // s) % 2) == 1) & (((tj // s) % 2) == 0))
        s *= 2

    units = [(b, h) for b in range(nb) for h in range(n_heads)]
    idx = range(len(units))
    rs = [slice(b * C, (b + 1) * C) for b, _ in units]
    ls = [slice(h * N, (h + 1) * N) for _, h in units]
    ar = [jnp.concatenate([at[rs[i], ls[i]], rt[rs[i], ls[i]]], axis=0) for i in idx]
    g_b = [jnp.where(mask2, _dot(ar[i], bt[rs[i], ls[i]], _NT), 0.0) for i in idx]
    g_k = [jnp.where(mask2, _dot(ar[i], kt[rs[i], ls[i]], _NT), 0.0) for i in idx]
    s0 = [state_ref[i] for i in idx]
    ars = [_dot(ar[i], s0[i], _NT) for i in idx]
    gv = [_dot(g_k[i], vb[rs[i], ls[i]]) for i in idx]
    tinv = [eye + jnp.where(level_masks[0], g_b[i][:C], 0.0) for i in idx]
    for lm in level_masks[1:]:
        xs = [_dot(jnp.where(lm, g_b[i][:C], 0.0), tinv[i]) for i in idx]
        tinv = [tinv[i] + _dot(tinv[i], xs[i]) for i in idx]
    u = [_dot(tinv[i], ars[i][:C] + gv[i][:C]) for i in idx]
    ys = [ars[i][C:] + gv[i][C:] + _dot(g_b[i][C:], u[i]) for i in idx]
    for i in idx:
        uv = jnp.concatenate([u[i].astype(BF16), vb[rs[i], ls[i]]], axis=0)
        bk = jnp.concatenate([bh[rs[i], ls[i]], kh[rs[i], ls[i]]], axis=0)
        state_ref[i] = s0[i] * p_last[units[i][0]][:, ls[i]] + _dot(uv, bk, _TN)
    y = jnp.concatenate([jnp.concatenate(ys[b * n_heads:(b + 1) * n_heads], axis=1) for b in range(nb)],
                        axis=0)

    inv_n = 1.0 / N
    mu_y = head_sum(y) * inv_n
    yc = y - mu_y
    var_y = head_sum(yc * yc) * inv_n
    yn = yc * lax.rsqrt(var_y + GN_EPS) * gng_ref[...] + gnb_ref[...]
    bonus = head_sum(r * k2 * rk_ref[...]) * v
    y_ref[...] = ((yn + bonus) * g).reshape(nb, C, d_r).astype(y_ref.dtype)


def rwkv_mixer(proj_main, proj_lora, col0, shift_mu, w0, w_up, a0, a_up, g_up, k_k, k_a, r_k, gn_g, gn_b,
               *, batch, chunk=64):
    nt = proj_main.shape[0]
    seq = nt // batch
    d_r = w0.shape[-1]
    n_heads = d_r // RWKV_HEAD
    d_lora = proj_lora.shape[1]
    d_decay, d_aaa = w_up.shape[0], a_up.shape[0]
    n_chunks = seq // chunk
    assert seq % chunk == 0 and col0 % d_r == 0
    cb = col0 // d_r
    row2 = lambda t: t.reshape(1, -1).astype(F32)
    mu_main = shift_mu[:3 * d_r].reshape(3, d_r)
    mu_lora = shift_mu[3 * d_r:].reshape(1, d_lora)
    pm3 = proj_main.reshape(batch, seq, proj_main.shape[1])
    pl3 = proj_lora.reshape(batch, seq, d_lora)

    def main_spec(j):
        return pl.BlockSpec((batch, chunk, d_r), lambda c, j=j: (0, c, cb + j))

    full = lambda shape: pl.BlockSpec(shape, lambda c: (0,) * len(shape))
    kern = functools.partial(_rwkv_kernel, n_heads=n_heads, d_decay=d_decay, d_aaa=d_aaa)
    y = pl.pallas_call(
        kern,
        grid=(n_chunks,),
        in_specs=[main_spec(0), main_spec(1), main_spec(2),
                  pl.BlockSpec((batch, chunk, d_lora), lambda c: (0, c, 0)),
                  full((3, d_r)), full((1, d_lora)), full((1, d_r)), full((d_decay, d_r)), full((1, d_r)),
                  full((d_aaa, d_r)), full((g_up.shape[0], d_r)), full((1, d_r)), full((1, d_r)), full((1, d_r)),
                  full((1, d_r)), full((1, d_r))],
        out_specs=pl.BlockSpec((batch, chunk, d_r), lambda c: (0, c, 0)),
        out_shape=jax.ShapeDtypeStruct((batch, seq, d_r), BF16),
        scratch_shapes=[pltpu.VMEM((batch * n_heads, RWKV_HEAD, RWKV_HEAD), F32),
                        pltpu.VMEM((batch, 3, d_r), F32),
                        pltpu.VMEM((batch, 1, d_lora), F32)],
        compiler_params=pltpu.CompilerParams(dimension_semantics=("arbitrary",), vmem_limit_bytes=VMEM_LIMIT),
        name="rwkv_mixer",
    )(pm3, pm3, pm3, pl3, mu_main, mu_lora, row2(w0), w_up.astype(BF16), row2(a0),
      a_up.astype(BF16), g_up.astype(BF16), row2(k_k), row2(k_a), row2(r_k), row2(gn_g), row2(gn_b))
    return y.reshape(nt, d_r)


def _rms(x, g):
    return x * lax.rsqrt(jnp.mean(x * x, axis=-1, keepdims=True) + RMS_EPS) * g


def _inproj_kernel(x_ref, g_ref, wt_ref, wlt_ref, o_ref, ol_ref, u_ref):
    @pl.when(pl.program_id(1) == 0)
    def _():
        u = _rms(x_ref[...], g_ref[...]).astype(BF16)
        u_ref[...] = u
        ol_ref[...] = lax.dot_general(u, wlt_ref[...], _NT, preferred_element_type=F32)

    o_ref[...] = lax.dot_general(u_ref[...], wt_ref[...].astype(BF16), _NT, preferred_element_type=F32)


def in_proj(x2, norm_g, w_in, d_main, *, tm=1024, tn=1024):
    nt, d = x2.shape
    d_lora = w_in.shape[1] - d_main
    assert nt % tm == 0 and d_main % tn == 0
    w_t = w_in.T
    w_lora_t = w_t[d_main:].astype(BF16)
    return pl.pallas_call(
        _inproj_kernel,
        grid=(nt // tm, d_main // tn),
        in_specs=[pl.BlockSpec((tm, d), lambda i, j: (i, 0)),
                  pl.BlockSpec((1, d), lambda i, j: (0, 0)),
                  pl.BlockSpec((tn, d), lambda i, j: (j, 0)),
                  pl.BlockSpec((d_lora, d), lambda i, j: (0, 0))],
        out_specs=[pl.BlockSpec((tm, tn), lambda i, j: (i, j)),
                   pl.BlockSpec((tm, d_lora), lambda i, j: (i, 0))],
        out_shape=[jax.ShapeDtypeStruct((nt, d_main), F32), jax.ShapeDtypeStruct((nt, d_lora), F32)],
        scratch_shapes=[pltpu.VMEM((tm, d), BF16)],
        compiler_params=pltpu.CompilerParams(dimension_semantics=("arbitrary", "arbitrary"),
                                             vmem_limit_bytes=VMEM_LIMIT),
        name="in_proj",
    )(x2, norm_g.reshape(1, d).astype(F32), w_t, w_lora_t)


CONV_HALO = 32
CONV_ROWS = 32


def _conv_kernel(val_ref, gate_ref, dw_ref, b_ref, g_ref, beta_ref, o_ref, ubuf_ref):
    tt = val_ref.shape[0]
    d_c = val_ref.shape[1]
    t_idx = pl.program_id(1)

    @pl.when(t_idx == 0)
    def _():
        ubuf_ref[0, 0:CONV_HALO, :] = jnp.zeros((CONV_HALO, d_c), F32)

    @pl.when(t_idx > 0)
    def _():
        ubuf_ref[0, 0:CONV_HALO, :] = ubuf_ref[0, tt:tt + CONV_HALO, :]

    ubuf_ref[0, CONV_HALO:CONV_HALO + tt, :] = val_ref[...] * jax.nn.sigmoid(gate_ref[...])
    n_sh = tt + CONV_HALO - SUBLANES
    for s in range(1, SUBLANES):
        for r in range(0, n_sh, CONV_ROWS):
            n = min(CONV_ROWS, n_sh - r)
            ubuf_ref[s, r:r + n, :] = ubuf_ref[0, r + s:r + s + n, :]
    dw = dw_ref[...]
    lead = CONV_HALO - (CONV_WIDTH - 1)
    for c in range(tt // CONV_ROWS):
        r0 = c * CONV_ROWS
        acc = jnp.zeros((CONV_ROWS, d_c), F32)
        for j in range(CONV_WIDTH):
            q, s = divmod(lead + j, SUBLANES)
            r = r0 + q * SUBLANES
            acc = acc + dw[j:j + 1, :] * ubuf_ref[s, r:r + CONV_ROWS, :]
        acc = acc + b_ref[...]
        mu = jnp.mean(acc, axis=-1, keepdims=True)
        cen = acc - mu
        var = jnp.mean(cen * cen, axis=-1, keepdims=True)
        yv = cen * lax.rsqrt(var + LN_EPS) * g_ref[...] + beta_ref[...]
        o_ref[r0:r0 + CONV_ROWS, :] = (yv * jax.nn.sigmoid(yv)).astype(o_ref.dtype)


def conv_mixer(proj_main, dw, bias, ln_g, ln_b, *, batch, tt=256):
    nt = proj_main.shape[0]
    seq = nt // batch
    d_c = dw.shape[1]
    n_t = seq // tt
    assert seq % tt == 0 and tt % CONV_ROWS == 0
    row = lambda t: t.reshape(1, d_c).astype(F32)
    full = lambda shape: pl.BlockSpec(shape, lambda b, t: (0, 0))
    return pl.pallas_call(
        _conv_kernel,
        grid=(batch, n_t),
        in_specs=[pl.BlockSpec((tt, d_c), lambda b, t: (b * n_t + t, 0)),
                  pl.BlockSpec((tt, d_c), lambda b, t: (b * n_t + t, 1)),
                  full((CONV_WIDTH, d_c)), full((1, d_c)), full((1, d_c)), full((1, d_c))],
        out_specs=pl.BlockSpec((tt, d_c), lambda b, t: (b * n_t + t, 0)),
        out_shape=jax.ShapeDtypeStruct((nt, d_c), BF16),
        scratch_shapes=[pltpu.VMEM((SUBLANES, tt + CONV_HALO, d_c), F32)],
        compiler_params=pltpu.CompilerParams(dimension_semantics=("arbitrary", "arbitrary"),
                                             vmem_limit_bytes=VMEM_LIMIT),
        name="conv_mixer",
    )(proj_main, proj_main, dw.astype(F32), row(bias), row(ln_g), row(ln_b))


ROUTER_SUB = 512


def _out_router_kernel(x_ref, yc_ref, yr_ref, wc_ref, wr_ref, nf_ref, rw_hi_ref, rw_lo_ref, rb_ref,
                       h_ref, v_ref, eid_ref, gate_ref, count_ref, cnt_ref):
    tm = x_ref.shape[0]
    subs = [slice(r, r + ROUTER_SUB) for r in range(0, tm, ROUTER_SUB)]
    hs = [x_ref[sl, :] + jnp.dot(yc_ref[sl, :], wc_ref[...], preferred_element_type=F32)
          + jnp.dot(yr_ref[sl, :], wr_ref[...], preferred_element_type=F32) for sl in subs]

    @pl.when(pl.program_id(0) == 0)
    def _():
        cnt_ref[...] = jnp.zeros_like(cnt_ref)

    lane = lax.broadcasted_iota(jnp.int32, (ROUTER_SUB, LANES), 1)
    neg = jnp.float32(-jnp.inf)
    big = jnp.int32(LANES)
    earlier = (lax.broadcasted_iota(jnp.int32, (ROUTER_SUB, ROUTER_SUB), 1)
               < lax.broadcasted_iota(jnp.int32, (ROUTER_SUB, ROUTER_SUB), 0)).astype(BF16)

    def first_max(vals):
        m = jnp.max(vals, axis=-1, keepdims=True)
        return m, jnp.min(jnp.where(vals == m, lane, big), axis=-1, keepdims=True)

    for sl, h in zip(subs, hs):
        h_ref[sl, :] = h
        v = _rms(h, nf_ref[...])
        v_ref[sl, :] = _pack_bf16_pairs(v)
        v_hi = v.astype(BF16)
        v_lo = (v - v_hi.astype(F32)).astype(BF16)
        logits = (jnp.dot(v_hi, rw_hi_ref[...], preferred_element_type=F32)
                  + jnp.dot(v_lo, rw_hi_ref[...], preferred_element_type=F32)
                  + jnp.dot(v_hi, rw_lo_ref[...], preferred_element_type=F32)) + rb_ref[...]
        gl = jnp.where(lane < N_GROUPS, logits, neg)
        gmax, grp = first_max(gl)
        p_grp = 1.0 / jnp.sum(jnp.exp(gl - gmax), axis=-1, keepdims=True)
        lo = N_GROUPS + grp * EXPERTS_PER_GROUP
        el = jnp.where((lane >= lo) & (lane < lo + EXPERTS_PER_GROUP), logits, neg)
        m1, i1 = first_max(el)
        m2, i2 = first_max(jnp.where(lane == i1, neg, el))
        e2 = jnp.exp(m2 - m1)
        g1 = p_grp / (1.0 + e2)
        g2 = p_grp * e2 / (1.0 + e2)
        gate_ref[sl, :] = jnp.where(lane == 0, g1, jnp.where(lane == 1, g2, 0.0))

        x1 = i1 - N_GROUPS
        x2 = i2 - N_GROUPS
        oh1 = (lane == x1).astype(F32)
        oh2 = (lane == x2).astype(F32)
        before1 = jnp.dot(earlier, oh1.astype(BF16), preferred_element_type=F32)
        before2 = jnp.dot(earlier, oh2.astype(BF16), preferred_element_type=F32)
        carry = cnt_ref[...]
        n1 = jnp.sum(oh1, axis=0, keepdims=True)
        n2 = jnp.sum(oh2, axis=0, keepdims=True)
        rank1 = jnp.sum(oh1 * (before1 + carry), axis=-1, keepdims=True).astype(jnp.int32)
        rank2 = jnp.sum(oh2 * (before2 + carry + n1), axis=-1, keepdims=True).astype(jnp.int32)
        cnt_ref[...] = carry + n1 + n2
        routing = jnp.where(lane == 0, x1, jnp.where(lane == 1, x2, jnp.where(
            lane == 2, rank1, jnp.where(lane == 3, rank2, 0))))
        eid_ref[:, sl] = routing.T[:SUBLANES, :]
    count_ref[...] = cnt_ref[...].astype(jnp.int32)


def out_router(x2, y_conv, y_rwkv, w_out, norm_ffn, rg_w, rg_b, re_w, re_b, *, tm=512):
    nt, d = x2.shape
    d_c, d_r = y_conv.shape[1], y_rwkv.shape[1]
    assert d_c == d_r and nt % tm == 0 and tm % ROUTER_SUB == 0
    n_r = N_GROUPS + N_EXPERTS
    rw = jnp.zeros((d, LANES), F32).at[:, :n_r].set(jnp.concatenate([rg_w, re_w], axis=1).astype(F32))
    rw_hi = rw.astype(BF16)
    rw_lo = (rw - rw_hi.astype(F32)).astype(BF16)
    rb = jnp.zeros((1, LANES), F32).at[0, :n_r].set(jnp.concatenate([rg_b, re_b]).astype(F32))
    full = lambda shape: pl.BlockSpec(shape, lambda i: (0, 0))
    once = lambda shape, r: pl.BlockSpec(shape, lambda i, r=r: (r, 0), pipeline_mode=pl.Buffered(1))
    rows = lambda w: pl.BlockSpec((tm, w), lambda i: (i, 0))
    return pl.pallas_call(
        _out_router_kernel,
        grid=(nt // tm,),
        in_specs=[rows(d), rows(d_c), rows(d_r), once((d_c, d), 0), once((d_r, d), 1),
                  full((1, d)), full((d, LANES)), full((d, LANES)), full((1, LANES))],
        out_specs=[rows(d), rows(d // 2), pl.BlockSpec((SUBLANES, tm), lambda i: (0, i)), rows(LANES),
                   full((1, LANES))],
        out_shape=[jax.ShapeDtypeStruct((nt, d), F32), jax.ShapeDtypeStruct((nt, d // 2), jnp.uint32),
                   jax.ShapeDtypeStruct((SUBLANES, nt), jnp.int32), jax.ShapeDtypeStruct((nt, LANES), F32),
                   jax.ShapeDtypeStruct((1, LANES), jnp.int32)],
        scratch_shapes=[pltpu.VMEM((1, LANES), F32)],
        compiler_params=pltpu.CompilerParams(dimension_semantics=("arbitrary",), vmem_limit_bytes=VMEM_LIMIT),
        name="out_router",
    )(x2, y_conv, y_rwkv, w_out, w_out, norm_ffn.reshape(1, d).astype(F32), rw_hi, rw_lo, rb)


MOE_ROWS = 256


def slot_plan(routing, counts, n_rows_pad):
    expert_id = routing[:TOP_K]
    rank = routing[TOP_K:2 * TOP_K]
    padded = (counts + MOE_ROWS - 1) // MOE_ROWS * MOE_ROWS
    pend = jnp.cumsum(padded)
    pstarts = pend - padded
    dest = (pstarts[expert_id] + rank).astype(jnp.int32).T
    n_blocks = n_rows_pad // MOE_ROWS
    block_row0 = jnp.arange(n_blocks, dtype=jnp.int32) * MOE_ROWS
    block_e = jnp.minimum(jnp.sum((pend[None, :] <= block_row0[:, None]).astype(jnp.int32), axis=1),
                          N_EXPERTS - 1).astype(jnp.int32)
    n_used = (pend[-1:] // MOE_ROWS).astype(jnp.int32)
    ids = jnp.arange(N_EXPERTS, dtype=jnp.int32)
    owner = jnp.where(counts > 0, ids, N_EXPERTS)
    nxt = lax.cummin(jnp.concatenate([owner[1:], jnp.full((1,), N_EXPERTS, jnp.int32)]), reverse=True)
    next_used = jnp.where(nxt < N_EXPERTS, nxt, -1).astype(jnp.int32)
    used_ord = (jnp.cumsum((counts > 0).astype(jnp.int32)) - 1).astype(jnp.int32)
    pad0 = pstarts + counts
    n_single = jnp.minimum((-pad0) % SUBLANES, pend - pad0)
    tile0 = pad0 + n_single
    n_tile = (pend - tile0) // SUBLANES
    tail = jnp.stack([pend[-1] // MOE_ROWS, n_blocks - pend[-1] // MOE_ROWS, jnp.sum(n_single), jnp.sum(n_tile)])
    fill_plan = jnp.concatenate([pad0, n_single, tile0, n_tile, tail]).astype(jnp.int32)
    return dest, block_e, n_used, next_used, used_ord, fill_plan


def _dispatch_kernel(fill_ref, dest_ref, v_ref, xs_ref, zero_ref, sem, fill_sem):
    tm = v_ref.shape[0]

    def fills():
        e0, e1, e2, e3, e4 = (k * N_EXPERTS for k in range(5))
        single = lambda e, i: pltpu.make_async_copy(
            zero_ref.at[pl.ds(0, 1), :], xs_ref.at[pl.ds(fill_ref[e0 + e] + i, 1), :], fill_sem)
        tile = lambda e, i: pltpu.make_async_copy(
            zero_ref.at[pl.ds(0, SUBLANES), :],
            xs_ref.at[pl.ds(pl.multiple_of(fill_ref[e2 + e] + i * SUBLANES, SUBLANES), SUBLANES), :], fill_sem)
        block = lambda i: pltpu.make_async_copy(
            zero_ref, xs_ref.at[pl.ds(pl.multiple_of((fill_ref[e4] + i) * MOE_ROWS, MOE_ROWS), MOE_ROWS), :],
            fill_sem)
        return single, tile, block, e1, e3, e4

    @pl.when(pl.program_id(0) == 0)
    def _():
        zero_ref[...] = jnp.zeros_like(zero_ref)
        single, tile, block, e1, e3, e4 = fills()
        for e in range(N_EXPERTS):
            lax.fori_loop(0, fill_ref[e1 + e], lambda i, c, e=e: (single(e, i).start(), c)[1], 0)
            lax.fori_loop(0, fill_ref[e3 + e], lambda i, c, e=e: (tile(e, i).start(), c)[1], 0)
        lax.fori_loop(0, fill_ref[e4 + 1], lambda i, c: (block(i).start(), c)[1], 0)

    def copy(i, k):
        d = dest_ref[0, 0, i * TOP_K + k]
        return pltpu.make_async_copy(v_ref.at[pl.ds(i, 1), :], xs_ref.at[pl.ds(d, 1), :], sem)

    def start(i, carry):
        for k in range(TOP_K):
            copy(i, k).start(priority=k)
        return carry

    def wait(i, carry):
        for k in range(TOP_K):
            copy(i, k).wait()
        return carry

    lax.fori_loop(0, tm, start, 0, unroll=8)
    lax.fori_loop(0, tm, wait, 0, unroll=8)

    @pl.when(pl.program_id(0) == 0)
    def _():
        single, tile, block, e1, e3, e4 = fills()
        lax.fori_loop(0, fill_ref[e4 + 2], lambda i, c: (single(0, 0).wait(), c)[1], 0)
        lax.fori_loop(0, fill_ref[e4 + 3], lambda i, c: (tile(0, 0).wait(), c)[1], 0)
        lax.fori_loop(0, fill_ref[e4 + 1], lambda i, c: (block(0).wait(), c)[1], 0)


def dispatch(v, dest, fill_plan, n_rows_pad, *, tm=1024):
    nt, d = v.shape
    assert nt % tm == 0
    dest3 = dest.reshape(nt // tm, 1, tm * TOP_K)
    grid_spec = pltpu.PrefetchScalarGridSpec(
        num_scalar_prefetch=1,
        grid=(nt // tm,),
        in_specs=[pl.BlockSpec((1, 1, tm * TOP_K), lambda i, fp: (i, 0, 0), memory_space=pltpu.SMEM),
                  pl.BlockSpec((tm, d), lambda i, fp: (i, 0))],
        out_specs=pl.BlockSpec(memory_space=pl.ANY),
        scratch_shapes=[pltpu.VMEM((MOE_ROWS, d), v.dtype), pltpu.SemaphoreType.DMA(()),
                        pltpu.SemaphoreType.DMA(())],
    )
    return pl.pallas_call(
        _dispatch_kernel,
        grid_spec=grid_spec,
        out_shape=jax.ShapeDtypeStruct((n_rows_pad, d), v.dtype),
        compiler_params=pltpu.CompilerParams(dimension_semantics=("arbitrary",), vmem_limit_bytes=VMEM_LIMIT),
        name="moe_dispatch",
    )(fill_plan, dest3, v)


def _experts_kernel(be_ref, nu_ref, nxt_ref, ord_ref, xs_ref, wg_hbm, wu_hbm, wd_hbm, ys_ref,
                    wg_f, wu_f, wd_f, wg_s, wu_s, wd_s, wsem):
    j = pl.program_id(0)
    e = be_ref[j]

    def fetch(ex, slot):
        return [pltpu.make_async_copy(wg_hbm.at[ex], wg_f.at[slot], wsem.at[slot]),
                pltpu.make_async_copy(wu_hbm.at[ex], wu_f.at[slot], wsem.at[slot]),
                pltpu.make_async_copy(wd_hbm.at[ex], wd_f.at[slot], wsem.at[slot])]

    @pl.when(j == 0)
    def _():
        for c in fetch(e, lax.rem(ord_ref[e], 2)):
            c.start()

    first = (j == 0) | (e != be_ref[jnp.maximum(j - 1, 0)])

    @pl.when(first & (j < nu_ref[0]))
    def _():
        slot = lax.rem(ord_ref[e], 2)
        for c in fetch(e, slot):
            c.wait()
        nxt = nxt_ref[e]

        @pl.when(nxt >= 0)
        def _():
            for c in fetch(nxt, 1 - slot):
                c.start()

        wg_s[...] = wg_f[slot].astype(BF16)
        wu_s[...] = wu_f[slot].astype(BF16)
        wd_s[...] = wd_f[slot].astype(BF16)

    @pl.when(j < nu_ref[0])
    def _():
        xb = _unpack_bf16_pairs(xs_ref[...]).astype(BF16)
        gt = jnp.dot(xb, wg_s[...], preferred_element_type=F32)
        up = jnp.dot(xb, wu_s[...], preferred_element_type=F32)
        hid = (gt * jax.nn.sigmoid(gt) * up).astype(BF16)
        ys_ref[...] = _pack_bf16_pairs(jnp.dot(hid, wd_s[...], preferred_element_type=F32))

    @pl.when(j >= nu_ref[0])
    def _():
        ys_ref[...] = jnp.zeros_like(ys_ref)


def experts(xs, block_e, n_used, next_used, used_ord, w_gate, w_up, w_down, n_rows_pad):
    dh = xs.shape[1]
    d, d_e = w_gate.shape[1], w_gate.shape[2]
    assert d == 2 * dh
    n_blocks = n_rows_pad // MOE_ROWS
    hbm = pl.BlockSpec(memory_space=pl.ANY)
    grid_spec = pltpu.PrefetchScalarGridSpec(
        num_scalar_prefetch=4,
        grid=(n_blocks,),
        in_specs=[pl.BlockSpec((MOE_ROWS, dh), lambda j, be, nu, nx, od: (jnp.minimum(j, nu[0] - 1), 0)),
                  hbm, hbm, hbm],
        out_specs=pl.BlockSpec((MOE_ROWS, dh), lambda j, be, nu, nx, od: (j, 0)),
        scratch_shapes=[pltpu.VMEM((2, d, d_e), F32), pltpu.VMEM((2, d, d_e), F32), pltpu.VMEM((2, d_e, d), F32),
                        pltpu.VMEM((d, d_e), BF16), pltpu.VMEM((d, d_e), BF16), pltpu.VMEM((d_e, d), BF16),
                        pltpu.SemaphoreType.DMA((2,))],
    )
    return pl.pallas_call(
        _experts_kernel,
        grid_spec=grid_spec,
        out_shape=jax.ShapeDtypeStruct((n_rows_pad, dh), jnp.uint32),
        compiler_params=pltpu.CompilerParams(dimension_semantics=("arbitrary",), vmem_limit_bytes=VMEM_LIMIT),
        name="moe_experts",
    )(block_e, n_used, next_used, used_ord, xs, w_gate, w_up, w_down)


def _combine_kernel(dest_ref, dest_next_ref, h_ref, gate_ref, nfin_ref, ys_ref, o_ref, ybuf, sem):
    step = pl.program_id(0)
    tm = h_ref.shape[0]
    slot = lax.rem(step, 2)

    def copy(d_ref, s, i, k):
        d = d_ref[0, 0, i * TOP_K + k]
        return pltpu.make_async_copy(ys_ref.at[pl.ds(d, 1), :], ybuf.at[s, k, pl.ds(i, 1), :], sem.at[s])

    def start_all(d_ref, s):
        def body(i, carry):
            for k in range(TOP_K):
                copy(d_ref, s, i, k).start(priority=k)
            return carry
        lax.fori_loop(0, tm, body, 0, unroll=8)

    @pl.when(step == 0)
    def _():
        start_all(dest_ref, 0)

    @pl.when(step + 1 < pl.num_programs(0))
    def _():
        start_all(dest_next_ref, 1 - slot)

    def wait_body(i, carry):
        for k in range(TOP_K):
            copy(dest_ref, slot, i, k).wait()
        return carry

    lax.fori_loop(0, tm, wait_body, 0, unroll=8)
    gates = gate_ref[...]
    moe = (gates[:, 0:1] * _unpack_bf16_pairs(ybuf[slot, 0])
           + gates[:, 1:2] * _unpack_bf16_pairs(ybuf[slot, 1]))
    o_ref[...] = _rms(h_ref[...] + moe, nfin_ref[...])


def combine(h, gates, dest, ys, norm_final, *, tm=256):
    nt, d = h.shape
    assert nt % tm == 0 and ys.shape[1] * 2 == d
    n_tiles = nt // tm
    dest3 = dest.reshape(n_tiles, 1, tm * TOP_K)
    return pl.pallas_call(
        _combine_kernel,
        grid=(n_tiles,),
        in_specs=[pl.BlockSpec((1, 1, tm * TOP_K), lambda i: (i, 0, 0), memory_space=pltpu.SMEM),
                  pl.BlockSpec((1, 1, tm * TOP_K), lambda i: (jnp.minimum(i + 1, n_tiles - 1), 0, 0),
                               memory_space=pltpu.SMEM),
                  pl.BlockSpec((tm, d), lambda i: (i, 0)),
                  pl.BlockSpec((tm, LANES), lambda i: (i, 0)),
                  pl.BlockSpec((1, d), lambda i: (0, 0)),
                  pl.BlockSpec(memory_space=pl.ANY)],
        out_specs=pl.BlockSpec((tm, d), lambda i: (i, 0)),
        out_shape=jax.ShapeDtypeStruct((nt, d), F32),
        scratch_shapes=[pltpu.VMEM((2, TOP_K, tm, d // 2), jnp.uint32), pltpu.SemaphoreType.DMA((2,))],
        compiler_params=pltpu.CompilerParams(dimension_semantics=("arbitrary",), vmem_limit_bytes=VMEM_LIMIT),
        name="moe_combine",
    )(dest3, dest3, h, gates, norm_final.reshape(1, d).astype(F32), ys)


def kernel(x, norm_mix, w_in, conv_dw, conv_b, conv_ln_g, conv_ln_b, shift_mu, w0, w_lora_up, a0, a_lora_up, g_lora_up, k_k, k_a, r_k, gn_g, gn_b, w_out, norm_ffn, router_group_w, router_group_b, router_expert_w, router_expert_b, expert_w_gate, expert_w_up, expert_w_down, norm_final):
    B, T, D = x.shape
    depth = w_in.shape[0]
    d_c = conv_dw.shape[2]
    d_r = w0.shape[1]
    d_main = 2 * d_c + 3 * d_r
    nt = B * T
    n_rows_pad = -(-(nt * TOP_K + N_EXPERTS * (MOE_ROWS - 1)) // MOE_ROWS) * MOE_ROWS
    h = x.reshape(nt, D)
    for l in range(depth):
        proj_main, proj_lora = in_proj(h, norm_mix[l], w_in[l], d_main)
        y_conv = conv_mixer(proj_main, conv_dw[l], conv_b[l], conv_ln_g[l], conv_ln_b[l], batch=B)
        y_rwkv = rwkv_mixer(proj_main, proj_lora, 2 * d_c, shift_mu[l], w0[l], w_lora_up[l], a0[l], a_lora_up[l],
                            g_lora_up[l], k_k[l], k_a[l], r_k[l].reshape(-1), gn_g[l], gn_b[l], batch=B)
        h, v, routing, gates, counts = out_router(h, y_conv, y_rwkv, w_out[l].astype(BF16), norm_ffn[l],
                                                  router_group_w[l], router_group_b[l], router_expert_w[l],
                                                  router_expert_b[l])
        dest, block_e, n_used, next_used, used_ord, fill_plan = slot_plan(routing, counts[0, :N_EXPERTS],
                                                                          n_rows_pad)
        xs = dispatch(v, dest, fill_plan, n_rows_pad)
        ys = experts(xs, block_e, n_used, next_used, used_ord, expert_w_gate[l], expert_w_up[l], expert_w_down[l],
                     n_rows_pad)
        assert depth == 1
        h = combine(h, gates, dest, ys, norm_final)
    return h.reshape(B, T, D)
```

```python
import functools

import jax
import jax.numpy as jnp
from jax import lax
from jax.experimental import pallas as pl
from jax.experimental.pallas import tpu as pltpu

F32 = jnp.float32
BF16 = jnp.bfloat16

RWKV_HEAD = 64
CONV_WIDTH = 31
N_GROUPS = 4
EXPERTS_PER_GROUP = 8
N_EXPERTS = N_GROUPS * EXPERTS_PER_GROUP
TOP_K = 2
RMS_EPS = 1e-6
LN_EPS = 1e-5
GN_EPS = 64e-5
LANES = 128
SUBLANES = 8
MXU_DIM = 256
VMEM_LIMIT = 56 * 1024 * 1024

_NT = (((1,), (1,)), ((), ()))
_TN = (((0,), (0,)), ((), ()))


def _dot(a, b, dims=None):
    a = a.astype(BF16)
    b = b.astype(BF16)
    if dims is None:
        return jnp.dot(a, b, preferred_element_type=F32)
    return lax.dot_general(a, b, dims, preferred_element_type=F32)


def _dot_split(x, m):
    hi = x.astype(BF16)
    lo = (x - hi.astype(F32)).astype(BF16)
    return (jnp.dot(hi, m, preferred_element_type=F32)
            + jnp.dot(lo, m, preferred_element_type=F32))


def _pack_bf16_pairs(x):
    w = x.shape[1] // 2
    lo = lax.bitcast_convert_type(x[:, :w].astype(BF16).astype(F32), jnp.uint32)
    hi = lax.bitcast_convert_type(x[:, w:].astype(BF16).astype(F32), jnp.uint32)
    return (lo >> 16) | (hi & jnp.uint32(0xFFFF0000))


def _unpack_bf16_pairs(p):
    lo = lax.bitcast_convert_type(p << 16, F32)
    hi = lax.bitcast_convert_type(p & jnp.uint32(0xFFFF0000), F32)
    return jnp.concatenate([lo, hi], axis=1)


def _rwkv_kernel(r_ref, k_ref, v_ref, lora_ref, mu_main_ref, mu_lora_ref, w0_ref, wup_ref, a0_ref, aup_ref,
                 gup_ref, kk_ref, ka_ref, rk_ref, gng_ref, gnb_ref, y_ref,
                 state_ref, prev_main_ref, prev_lora_ref, *, n_heads, d_decay, d_aaa):
    nb, C, d_r = r_ref.shape
    N = RWKV_HEAD
    R = nb * C

    @pl.when(pl.program_id(0) == 0)
    def _():
        state_ref[...] = jnp.zeros_like(state_ref)
        prev_main_ref[...] = jnp.zeros_like(prev_main_ref)
        prev_lora_ref[...] = jnp.zeros_like(prev_lora_ref)

    row = lax.broadcasted_iota(jnp.int32, (R, 1), 0)

    def shifted(x_ref, prev_ref, j, mu):
        x = x_ref[...].reshape(R, x_ref.shape[2])
        xp = pltpu.roll(x, 1, 0)
        for b in range(nb):
            xp = jnp.where(row == b * C, prev_ref[b, j:j + 1, :], xp)
            prev_ref[b, j:j + 1, :] = x[(b + 1) * C - 1:(b + 1) * C, :]
        return x + (xp - x) * mu

    r = shifted(r_ref, prev_main_ref, 0, mu_main_ref[0:1, :])
    k = shifted(k_ref, prev_main_ref, 1, mu_main_ref[1:2, :])
    v = shifted(v_ref, prev_main_ref, 2, mu_main_ref[2:3, :])
    lo = shifted(lora_ref, prev_lora_ref, 0, mu_lora_ref[...])

    wd = lo[:, :d_decay]
    ad = lo[:, d_decay:d_decay + d_aaa]
    gd = lo[:, d_decay + d_aaa:]

    z = w0_ref[...] + _dot(jnp.tanh(wd), wup_ref[...])
    w = -(jnp.maximum(-z, 0.0) + jnp.log(1.0 + jnp.exp(-jnp.abs(z)))) - 0.5
    logd = -jnp.exp(w)
    a = jax.nn.sigmoid(a0_ref[...] + _dot(ad, aup_ref[...]))
    g = _dot(jax.nn.sigmoid(gd), gup_ref[...])

    li = lax.broadcasted_iota(jnp.int32, (MXU_DIM, MXU_DIM), 0) // N
    lj = lax.broadcasted_iota(jnp.int32, (MXU_DIM, MXU_DIM), 1) // N
    head_ones = (li == lj).astype(BF16)

    def head_sum(x):
        return jnp.concatenate(
            [_dot_split(x[:, c * MXU_DIM:(c + 1) * MXU_DIM], head_ones) for c in range(d_r // MXU_DIM)], axis=1)

    kkr = k * kk_ref[...]
    kk = kkr / jnp.maximum(jnp.sqrt(head_sum(kkr * kkr)), 1e-12)
    k2 = k * (1.0 + (a - 1.0) * ka_ref[...])
    a_s = -kk
    b_s = kk * a

    ri = lax.broadcasted_iota(jnp.int32, (R, R), 0)
    rj = lax.broadcasted_iota(jnp.int32, (R, R), 1)
    tri = ((rj <= ri) & ((ri // C) == (rj // C))).astype(BF16)
    hi = logd.astype(BF16)
    rem = logd - hi.astype(F32)
    mid = rem.astype(BF16)
    low = (rem - mid.astype(F32)).astype(BF16)
    cum = (jnp.dot(tri, hi, preferred_element_type=F32) + jnp.dot(tri, mid, preferred_element_type=F32)
           + jnp.dot(tri, low, preferred_element_type=F32))
    p_incl = jnp.exp(cum)
    p_excl = jnp.exp(cum - logd)
    p_inv = jnp.exp(-cum)
    p_last = [p_incl[(b + 1) * C - 1:(b + 1) * C, :] for b in range(nb)]
    p_last_rows = jnp.concatenate([jnp.broadcast_to(p, (C, d_r)) for p in p_last], axis=0)

    at = (a_s * p_excl).astype(BF16)
    rt = (r * p_incl).astype(BF16)
    bt = b_s * p_inv
    kt = k2 * p_inv
    bh = (bt * p_last_rows).astype(BF16)
    kh = (kt * p_last_rows).astype(BF16)
    bt = bt.astype(BF16)
    kt = kt.astype(BF16)
    vb = v.astype(BF16)

    HG = LANES // N
    W = HG * N
    GC = HG * C
    wi = lax.broadcasted_iota(jnp.int32, (GC, W), 0)
    wj = lax.broadcasted_iota(jnp.int32, (GC, W), 1)
    bd_on = ((wi // C) == (wj // N)).astype(BF16) > 0

    def bdiag(x):
        xb = x.astype(BF16)
        return jnp.where(bd_on, jnp.concatenate([xb] * HG, axis=0), jnp.zeros((), BF16))

    ti = lax.broadcasted_iota(jnp.int32, (2 * C, GC), 0)
    tj = lax.broadcasted_iota(jnp.int32, (2 * C, GC), 1) % C
    mask2 = ((ti < C) & (tj < ti)) | ((ti >= C) & (tj <= ti - C))
    li = lax.broadcasted_iota(jnp.int32, (C, GC), 0)
    lj = lax.broadcasted_iota(jnp.int32, (C, GC), 1) % C
    eye = (li == lj).astype(F32)
    level_masks = []
    s = 1
    while s < C:
        level_masks.append(((li // (2 * s)) == (lj // (2 * s))) & (((li // s) % 2) == 1) & (((lj // s) % 2) == 0))
        s *= 2
    colblk = lax.broadcasted_iota(jnp.int32, (N, W), 1) // N

    units = [(b, g) for b in range(nb) for g in range(n_heads // HG)]
    idx = range(len(units))
    rs = [slice(b * C, (b + 1) * C) for b, _ in units]
    ls = [slice(g * W, (g + 1) * W) for _, g in units]
    ar = [jnp.concatenate([at[rs[i], ls[i]], rt[rs[i], ls[i]]], axis=0) for i in idx]
    g_b = [jnp.where(mask2, _dot(ar[i], bdiag(bt[rs[i], ls[i]]), _NT), 0.0) for i in idx]
    g_k = [jnp.where(mask2, _dot(ar[i], bdiag(kt[rs[i], ls[i]]), _NT), 0.0) for i in idx]
    s0 = [state_ref[i] for i in idx]
    ars = [_dot(ar[i], bdiag(s0[i]), _NT) for i in idx]
    gv = [_dot(g_k[i], bdiag(vb[rs[i], ls[i]])) for i in idx]
    tinv = [eye + jnp.where(level_masks[0], g_b[i][:C], 0.0) for i in idx]
    for lm in level_masks[1:]:
        xs = [_dot(jnp.where(lm, g_b[i][:C], 0.0), bdiag(tinv[i])) for i in idx]
        tinv = [tinv[i] + _dot(tinv[i], bdiag(xs[i])) for i in idx]
    u = [_dot(tinv[i], bdiag(ars[i][:C] + gv[i][:C])) for i in idx]
    ys = [ars[i][C:] + gv[i][C:] + _dot(g_b[i][C:], bdiag(u[i])) for i in idx]
    for i in idx:
        uv = jnp.concatenate([u[i].astype(BF16), vb[rs[i], ls[i]]], axis=0)
        bk = jnp.concatenate([bh[rs[i], ls[i]], kh[rs[i], ls[i]]], axis=0)
        full = _dot(uv, bk, _TN)
        upd = s0[i] * p_last[units[i][0]][:, ls[i]]
        for h in range(HG):
            upd = upd + jnp.where(colblk == h, full[h * N:(h + 1) * N, :], 0.0)
        state_ref[i] = upd
    n_g = n_heads // HG
    y = jnp.concatenate([jnp.concatenate(ys[b * n_g:(b + 1) * n_g], axis=1) for b in range(nb)],
                        axis=0)

    inv_n = 1.0 / N
    mu_y = head_sum(y) * inv_n
    yc = y - mu_y
    var_y = head_sum(yc * yc) * inv_n
    yn = yc * lax.rsqrt(var_y + GN_EPS) * gng_ref[...] + gnb_ref[...]
    bonus = head_sum(r * k2 * rk_ref[...]) * v
    y_ref[...] = ((yn + bonus) * g).reshape(nb, C, d_r).astype(y_ref.dtype)


def rwkv_mixer(proj_main, proj_lora, col0, shift_mu, w0, w_up, a0, a_up, g_up, k_k, k_a, r_k, gn_g, gn_b,
               *, batch, chunk=64):
    nt = proj_main.shape[0]
    seq = nt // batch
    d_r = w0.shape[-1]
    n_heads = d_r // RWKV_HEAD
    d_lora = proj_lora.shape[1]
    d_decay, d_aaa = w_up.shape[0], a_up.shape[0]
    n_chunks = seq // chunk
    assert seq % chunk == 0 and col0 % d_r == 0
    cb = col0 // d_r
    row2 = lambda t: t.reshape(1, -1).astype(F32)
    mu_main = shift_mu[:3 * d_r].reshape(3, d_r)
    mu_lora = shift_mu[3 * d_r:].reshape(1, d_lora)
    pm3 = proj_main.reshape(batch, seq, proj_main.shape[1])
    pl3 = proj_lora.reshape(batch, seq, d_lora)

    def main_spec(j):
        return pl.BlockSpec((batch, chunk, d_r), lambda c, j=j: (0, c, cb + j))

    full = lambda shape: pl.BlockSpec(shape, lambda c: (0,) * len(shape))
    kern = functools.partial(_rwkv_kernel, n_heads=n_heads, d_decay=d_decay, d_aaa=d_aaa)
    y = pl.pallas_call(
        kern,
        grid=(n_chunks,),
        in_specs=[main_spec(0), main_spec(1), main_spec(2),
                  pl.BlockSpec((batch, chunk, d_lora), lambda c: (0, c, 0)),
                  full((3, d_r)), full((1, d_lora)), full((1, d_r)), full((d_decay, d_r)), full((1, d_r)),
                  full((d_aaa, d_r)), full((g_up.shape[0], d_r)), full((1, d_r)), full((1, d_r)), full((1, d_r)),
                  full((1, d_r)), full((1, d_r))],
        out_specs=pl.BlockSpec((batch, chunk, d_r), lambda c: (0, c, 0)),
        out_shape=jax.ShapeDtypeStruct((batch, seq, d_r), BF16),
        scratch_shapes=[pltpu.VMEM((batch * n_heads * RWKV_HEAD // LANES, RWKV_HEAD, LANES), F32),
                        pltpu.VMEM((batch, 3, d_r), F32),
                        pltpu.VMEM((batch, 1, d_lora), F32)],
        compiler_params=pltpu.CompilerParams(dimension_semantics=("arbitrary",), vmem_limit_bytes=VMEM_LIMIT),
        name="rwkv_mixer",
    )(pm3, pm3, pm3, pl3, mu_main, mu_lora, row2(w0), w_up.astype(BF16), row2(a0),
      a_up.astype(BF16), g_up.astype(BF16), row2(k_k), row2(k_a), row2(r_k), row2(gn_g), row2(gn_b))
    return y.reshape(nt, d_r)


def _rms(x, g):
    return x * lax.rsqrt(jnp.mean(x * x, axis=-1, keepdims=True) + RMS_EPS) * g


def _inproj_kernel(x_ref, g_ref, wt_ref, wlt_ref, o_ref, ol_ref, u_ref):
    @pl.when(pl.program_id(1) == 0)
    def _():
        u = _rms(x_ref[...], g_ref[...]).astype(BF16)
        u_ref[...] = u
        ol_ref[...] = lax.dot_general(u, wlt_ref[...], _NT, preferred_element_type=F32)

    o_ref[...] = lax.dot_general(u_ref[...], wt_ref[...].astype(BF16), _NT, preferred_element_type=F32)


def in_proj(x2, norm_g, w_in, d_main, *, tm=1024, tn=1024):
    nt, d = x2.shape
    d_lora = w_in.shape[1] - d_main
    assert nt % tm == 0 and d_main % tn == 0
    w_t = w_in.T
    w_lora_t = w_t[d_main:].astype(BF16)
    return pl.pallas_call(
        _inproj_kernel,
        grid=(nt // tm, d_main // tn),
        in_specs=[pl.BlockSpec((tm, d), lambda i, j: (i, 0)),
                  pl.BlockSpec((1, d), lambda i, j: (0, 0)),
                  pl.BlockSpec((tn, d), lambda i, j: (j, 0)),
                  pl.BlockSpec((d_lora, d), lambda i, j: (0, 0))],
        out_specs=[pl.BlockSpec((tm, tn), lambda i, j: (i, j)),
                   pl.BlockSpec((tm, d_lora), lambda i, j: (i, 0))],
        out_shape=[jax.ShapeDtypeStruct((nt, d_main), F32), jax.ShapeDtypeStruct((nt, d_lora), F32)],
        scratch_shapes=[pltpu.VMEM((tm, d), BF16)],
        compiler_params=pltpu.CompilerParams(dimension_semantics=("arbitrary", "arbitrary"),
                                             vmem_limit_bytes=VMEM_LIMIT),
        name="in_proj",
    )(x2, norm_g.reshape(1, d).astype(F32), w_t, w_lora_t)


CONV_HALO = 32
CONV_ROWS = 32


def _conv_kernel(val_ref, gate_ref, dw_ref, b_ref, g_ref, beta_ref, o_ref, ubuf_ref):
    tt = val_ref.shape[0]
    d_c = val_ref.shape[1]
    t_idx = pl.program_id(1)

    @pl.when(t_idx == 0)
    def _():
        ubuf_ref[0, 0:CONV_HALO, :] = jnp.zeros((CONV_HALO, d_c), F32)

    @pl.when(t_idx > 0)
    def _():
        ubuf_ref[0, 0:CONV_HALO, :] = ubuf_ref[0, tt:tt + CONV_HALO, :]

    ubuf_ref[0, CONV_HALO:CONV_HALO + tt, :] = val_ref[...] * jax.nn.sigmoid(gate_ref[...])
    n_sh = tt + CONV_HALO - SUBLANES
    for s in range(1, SUBLANES):
        for r in range(0, n_sh, CONV_ROWS):
            n = min(CONV_ROWS, n_sh - r)
            ubuf_ref[s, r:r + n, :] = ubuf_ref[0, r + s:r + s + n, :]
    dw = dw_ref[...]
    lead = CONV_HALO - (CONV_WIDTH - 1)
    for c in range(tt // CONV_ROWS):
        r0 = c * CONV_ROWS
        acc = jnp.zeros((CONV_ROWS, d_c), F32)
        for j in range(CONV_WIDTH):
            q, s = divmod(lead + j, SUBLANES)
            r = r0 + q * SUBLANES
            acc = acc + dw[j:j + 1, :] * ubuf_ref[s, r:r + CONV_ROWS, :]
        acc = acc + b_ref[...]
        mu = jnp.mean(acc, axis=-1, keepdims=True)
        cen = acc - mu
        var = jnp.mean(cen * cen, axis=-1, keepdims=True)
        yv = cen * lax.rsqrt(var + LN_EPS) * g_ref[...] + beta_ref[...]
        o_ref[r0:r0 + CONV_ROWS, :] = (yv * jax.nn.sigmoid(yv)).astype(o_ref.dtype)


def conv_mixer(proj_main, dw, bias, ln_g, ln_b, *, batch, tt=256):
    nt = proj_main.shape[0]
    seq = nt // batch
    d_c = dw.shape[1]
    n_t = seq // tt
    assert seq % tt == 0 and tt % CONV_ROWS == 0
    row = lambda t: t.reshape(1, d_c).astype(F32)
    full = lambda shape: pl.BlockSpec(shape, lambda b, t: (0, 0))
    return pl.pallas_call(
        _conv_kernel,
        grid=(batch, n_t),
        in_specs=[pl.BlockSpec((tt, d_c), lambda b, t: (b * n_t + t, 0)),
                  pl.BlockSpec((tt, d_c), lambda b, t: (b * n_t + t, 1)),
                  full((CONV_WIDTH, d_c)), full((1, d_c)), full((1, d_c)), full((1, d_c))],
        out_specs=pl.BlockSpec((tt, d_c), lambda b, t: (b * n_t + t, 0)),
        out_shape=jax.ShapeDtypeStruct((nt, d_c), BF16),
        scratch_shapes=[pltpu.VMEM((SUBLANES, tt + CONV_HALO, d_c), F32)],
        compiler_params=pltpu.CompilerParams(dimension_semantics=("arbitrary", "arbitrary"),
                                             vmem_limit_bytes=VMEM_LIMIT),
        name="conv_mixer",
    )(proj_main, proj_main, dw.astype(F32), row(bias), row(ln_g), row(ln_b))


ROUTER_SUB = 512


def _out_router_kernel(x_ref, yc_ref, yr_ref, wc_ref, wr_ref, nf_ref, rw_hi_ref, rw_lo_ref, rb_ref,
                       h_ref, v_ref, eid_ref, gate_ref, count_ref, cnt_ref):
    tm = x_ref.shape[0]
    subs = [slice(r, r + ROUTER_SUB) for r in range(0, tm, ROUTER_SUB)]
    hs = [x_ref[sl, :] + jnp.dot(yc_ref[sl, :], wc_ref[...], preferred_element_type=F32)
          + jnp.dot(yr_ref[sl, :], wr_ref[...], preferred_element_type=F32) for sl in subs]

    @pl.when(pl.program_id(0) == 0)
    def _():
        cnt_ref[...] = jnp.zeros_like(cnt_ref)

    lane = lax.broadcasted_iota(jnp.int32, (ROUTER_SUB, LANES), 1)
    neg = jnp.float32(-jnp.inf)
    big = jnp.int32(LANES)
    earlier = (lax.broadcasted_iota(jnp.int32, (ROUTER_SUB, ROUTER_SUB), 1)
               < lax.broadcasted_iota(jnp.int32, (ROUTER_SUB, ROUTER_SUB), 0)).astype(BF16)

    def first_max(vals):
        m = jnp.max(vals, axis=-1, keepdims=True)
        return m, jnp.min(jnp.where(vals == m, lane, big), axis=-1, keepdims=True)

    for sl, h in zip(subs, hs):
        h_ref[sl, :] = h
        v = _rms(h, nf_ref[...])
        v_ref[sl, :] = _pack_bf16_pairs(v)
        v_hi = v.astype(BF16)
        v_lo = (v - v_hi.astype(F32)).astype(BF16)
        logits = (jnp.dot(v_hi, rw_hi_ref[...], preferred_element_type=F32)
                  + jnp.dot(v_lo, rw_hi_ref[...], preferred_element_type=F32)
                  + jnp.dot(v_hi, rw_lo_ref[...], preferred_element_type=F32)) + rb_ref[...]
        gl = jnp.where(lane < N_GROUPS, logits, neg)
        gmax, grp = first_max(gl)
        p_grp = 1.0 / jnp.sum(jnp.exp(gl - gmax), axis=-1, keepdims=True)
        lo = N_GROUPS + grp * EXPERTS_PER_GROUP
        el = jnp.where((lane >= lo) & (lane < lo + EXPERTS_PER_GROUP), logits, neg)
        m1, i1 = first_max(el)
        m2, i2 = first_max(jnp.where(lane == i1, neg, el))
        e2 = jnp.exp(m2 - m1)
        g1 = p_grp / (1.0 + e2)
        g2 = p_grp * e2 / (1.0 + e2)
        gate_ref[sl, :] = jnp.where(lane == 0, g1, jnp.where(lane == 1, g2, 0.0))

        x1 = i1 - N_GROUPS
        x2 = i2 - N_GROUPS
        oh1 = (lane == x1).astype(F32)
        oh2 = (lane == x2).astype(F32)
        before1 = jnp.dot(earlier, oh1.astype(BF16), preferred_element_type=F32)
        before2 = jnp.dot(earlier, oh2.astype(BF16), preferred_element_type=F32)
        carry = cnt_ref[...]
        n1 = jnp.sum(oh1, axis=0, keepdims=True)
        n2 = jnp.sum(oh2, axis=0, keepdims=True)
        rank1 = jnp.sum(oh1 * (before1 + carry), axis=-1, keepdims=True).astype(jnp.int32)
        rank2 = jnp.sum(oh2 * (before2 + carry + n1), axis=-1, keepdims=True).astype(jnp.int32)
        cnt_ref[...] = carry + n1 + n2
        routing = jnp.where(lane == 0, x1, jnp.where(lane == 1, x2, jnp.where(
            lane == 2, rank1, jnp.where(lane == 3, rank2, 0))))
        eid_ref[:, sl] = routing.T[:SUBLANES, :]
    count_ref[...] = cnt_ref[...].astype(jnp.int32)


def out_router(x2, y_conv, y_rwkv, w_out, norm_ffn, rg_w, rg_b, re_w, re_b, *, tm=512):
    nt, d = x2.shape
    d_c, d_r = y_conv.shape[1], y_rwkv.shape[1]
    assert d_c == d_r and nt % tm == 0 and tm % ROUTER_SUB == 0
    n_r = N_GROUPS + N_EXPERTS
    rw = jnp.zeros((d, LANES), F32).at[:, :n_r].set(jnp.concatenate([rg_w, re_w], axis=1).astype(F32))
    rw_hi = rw.astype(BF16)
    rw_lo = (rw - rw_hi.astype(F32)).astype(BF16)
    rb = jnp.zeros((1, LANES), F32).at[0, :n_r].set(jnp.concatenate([rg_b, re_b]).astype(F32))
    full = lambda shape: pl.BlockSpec(shape, lambda i: (0, 0))
    once = lambda shape, r: pl.BlockSpec(shape, lambda i, r=r: (r, 0), pipeline_mode=pl.Buffered(1))
    rows = lambda w: pl.BlockSpec((tm, w), lambda i: (i, 0))
    return pl.pallas_call(
        _out_router_kernel,
        grid=(nt // tm,),
        in_specs=[rows(d), rows(d_c), rows(d_r), once((d_c, d), 0), once((d_r, d), 1),
                  full((1, d)), full((d, LANES)), full((d, LANES)), full((1, LANES))],
        out_specs=[rows(d), rows(d // 2), pl.BlockSpec((SUBLANES, tm), lambda i: (0, i)), rows(LANES),
                   full((1, LANES))],
        out_shape=[jax.ShapeDtypeStruct((nt, d), F32), jax.ShapeDtypeStruct((nt, d // 2), jnp.uint32),
                   jax.ShapeDtypeStruct((SUBLANES, nt), jnp.int32), jax.ShapeDtypeStruct((nt, LANES), F32),
                   jax.ShapeDtypeStruct((1, LANES), jnp.int32)],
        scratch_shapes=[pltpu.VMEM((1, LANES), F32)],
        compiler_params=pltpu.CompilerParams(dimension_semantics=("arbitrary",), vmem_limit_bytes=VMEM_LIMIT),
        name="out_router",
    )(x2, y_conv, y_rwkv, w_out, w_out, norm_ffn.reshape(1, d).astype(F32), rw_hi, rw_lo, rb)


MOE_ROWS = 256


def slot_plan(routing, counts, n_rows_pad):
    expert_id = routing[:TOP_K]
    rank = routing[TOP_K:2 * TOP_K]
    padded = (counts + MOE_ROWS - 1) // MOE_ROWS * MOE_ROWS
    pend = jnp.cumsum(padded)
    pstarts = pend - padded
    ids = jnp.arange(N_EXPERTS, dtype=jnp.int32)
    start_of = jnp.sum(jnp.where(expert_id[..., None] == ids, pstarts, 0), axis=-1)
    dest = (start_of + rank).astype(jnp.int32).T
    n_blocks = n_rows_pad // MOE_ROWS
    block_row0 = jnp.arange(n_blocks, dtype=jnp.int32) * MOE_ROWS
    block_e = jnp.minimum(jnp.sum((pend[None, :] <= block_row0[:, None]).astype(jnp.int32), axis=1),
                          N_EXPERTS - 1).astype(jnp.int32)
    n_used = (pend[-1:] // MOE_ROWS).astype(jnp.int32)
    owner = jnp.where(counts > 0, ids, N_EXPERTS)
    nxt = lax.cummin(jnp.concatenate([owner[1:], jnp.full((1,), N_EXPERTS, jnp.int32)]), reverse=True)
    next_used = jnp.where(nxt < N_EXPERTS, nxt, -1).astype(jnp.int32)
    used_ord = (jnp.cumsum((counts > 0).astype(jnp.int32)) - 1).astype(jnp.int32)
    pad0 = pstarts + counts
    n_single = jnp.minimum((-pad0) % SUBLANES, pend - pad0)
    tile0 = pad0 + n_single
    n_tile = (pend - tile0) // SUBLANES
    tail = jnp.stack([pend[-1] // MOE_ROWS, n_blocks - pend[-1] // MOE_ROWS, jnp.sum(n_single), jnp.sum(n_tile)])
    fill_plan = jnp.concatenate([pad0, n_single, tile0, n_tile, tail]).astype(jnp.int32)
    return dest, block_e, n_used, next_used, used_ord, fill_plan


def _dispatch_kernel(fill_ref, dest_ref, v_ref, xs_ref, zero_ref, sem, fill_sem):
    tm = v_ref.shape[0]

    def fills():
        e0, e1, e2, e3, e4 = (k * N_EXPERTS for k in range(5))
        single = lambda e, i: pltpu.make_async_copy(
            zero_ref.at[pl.ds(0, 1), :], xs_ref.at[pl.ds(fill_ref[e0 + e] + i, 1), :], fill_sem)
        tile = lambda e, i: pltpu.make_async_copy(
            zero_ref.at[pl.ds(0, SUBLANES), :],
            xs_ref.at[pl.ds(pl.multiple_of(fill_ref[e2 + e] + i * SUBLANES, SUBLANES), SUBLANES), :], fill_sem)
        block = lambda i: pltpu.make_async_copy(
            zero_ref, xs_ref.at[pl.ds(pl.multiple_of((fill_ref[e4] + i) * MOE_ROWS, MOE_ROWS), MOE_ROWS), :],
            fill_sem)
        return single, tile, block, e1, e3, e4

    @pl.when(pl.program_id(0) == 0)
    def _():
        zero_ref[...] = jnp.zeros_like(zero_ref)
        single, tile, block, e1, e3, e4 = fills()
        for e in range(N_EXPERTS):
            lax.fori_loop(0, fill_ref[e1 + e], lambda i, c, e=e: (single(e, i).start(), c)[1], 0)
            lax.fori_loop(0, fill_ref[e3 + e], lambda i, c, e=e: (tile(e, i).start(), c)[1], 0)
        lax.fori_loop(0, fill_ref[e4 + 1], lambda i, c: (block(i).start(), c)[1], 0)

    def copy(i, k):
        d = dest_ref[0, 0, i * TOP_K + k]
        return pltpu.make_async_copy(v_ref.at[pl.ds(i, 1), :], xs_ref.at[pl.ds(d, 1), :], sem)

    def start(i, carry):
        for k in range(TOP_K):
            copy(i, k).start(priority=k)
        return carry

    def wait(i, carry):
        for k in range(TOP_K):
            copy(i, k).wait()
        return carry

    lax.fori_loop(0, tm, start, 0, unroll=8)
    lax.fori_loop(0, tm, wait, 0, unroll=8)

    @pl.when(pl.program_id(0) == 0)
    def _():
        single, tile, block, e1, e3, e4 = fills()
        lax.fori_loop(0, fill_ref[e4 + 2], lambda i, c: (single(0, 0).wait(), c)[1], 0)
        lax.fori_loop(0, fill_ref[e4 + 3], lambda i, c: (tile(0, 0).wait(), c)[1], 0)
        lax.fori_loop(0, fill_ref[e4 + 1], lambda i, c: (block(0).wait(), c)[1], 0)


def dispatch(v, dest, fill_plan, n_rows_pad, *, tm=1024):
    nt, d = v.shape
    assert nt % tm == 0
    dest3 = dest.reshape(nt // tm, 1, tm * TOP_K)
    grid_spec = pltpu.PrefetchScalarGridSpec(
        num_scalar_prefetch=1,
        grid=(nt // tm,),
        in_specs=[pl.BlockSpec((1, 1, tm * TOP_K), lambda i, fp: (i, 0, 0), memory_space=pltpu.SMEM),
                  pl.BlockSpec((tm, d), lambda i, fp: (i, 0))],
        out_specs=pl.BlockSpec(memory_space=pl.ANY),
        scratch_shapes=[pltpu.VMEM((MOE_ROWS, d), v.dtype), pltpu.SemaphoreType.DMA(()),
                        pltpu.SemaphoreType.DMA(())],
    )
    return pl.pallas_call(
        _dispatch_kernel,
        grid_spec=grid_spec,
        out_shape=jax.ShapeDtypeStruct((n_rows_pad, d), v.dtype),
        compiler_params=pltpu.CompilerParams(dimension_semantics=("arbitrary",), vmem_limit_bytes=VMEM_LIMIT),
        name="moe_dispatch",
    )(fill_plan, dest3, v)


def _experts_kernel(be_ref, nu_ref, nxt_ref, ord_ref, xs_ref, wg_hbm, wu_hbm, wd_hbm, ys_ref,
                    wg_f, wu_f, wd_f, wg_s, wu_s, wd_s, wsem):
    j = pl.program_id(0)
    e = be_ref[j]

    def fetch(ex, slot):
        return [pltpu.make_async_copy(wg_hbm.at[ex], wg_f.at[slot], wsem.at[slot]),
                pltpu.make_async_copy(wu_hbm.at[ex], wu_f.at[slot], wsem.at[slot]),
                pltpu.make_async_copy(wd_hbm.at[ex], wd_f.at[slot], wsem.at[slot])]

    @pl.when(j == 0)
    def _():
        for c in fetch(e, lax.rem(ord_ref[e], 2)):
            c.start()

    first = (j == 0) | (e != be_ref[jnp.maximum(j - 1, 0)])

    @pl.when(first & (j < nu_ref[0]))
    def _():
        slot = lax.rem(ord_ref[e], 2)
        for c in fetch(e, slot):
            c.wait()
        nxt = nxt_ref[e]

        @pl.when(nxt >= 0)
        def _():
            for c in fetch(nxt, 1 - slot):
                c.start()

        wg_s[...] = wg_f[slot].astype(BF16)
        wu_s[...] = wu_f[slot].astype(BF16)
        wd_s[...] = wd_f[slot].astype(BF16)

    @pl.when(j < nu_ref[0])
    def _():
        xb = _unpack_bf16_pairs(xs_ref[...]).astype(BF16)
        gt = jnp.dot(xb, wg_s[...], preferred_element_type=F32)
        up = jnp.dot(xb, wu_s[...], preferred_element_type=F32)
        hid = (gt * jax.nn.sigmoid(gt) * up).astype(BF16)
        ys_ref[...] = _pack_bf16_pairs(jnp.dot(hid, wd_s[...], preferred_element_type=F32))

    @pl.when(j >= nu_ref[0])
    def _():
        ys_ref[...] = jnp.zeros_like(ys_ref)


def experts(xs, block_e, n_used, next_used, used_ord, w_gate, w_up, w_down, n_rows_pad):
    dh = xs.shape[1]
    d, d_e = w_gate.shape[1], w_gate.shape[2]
    assert d == 2 * dh
    n_blocks = n_rows_pad // MOE_ROWS
    hbm = pl.BlockSpec(memory_space=pl.ANY)
    grid_spec = pltpu.PrefetchScalarGridSpec(
        num_scalar_prefetch=4,
        grid=(n_blocks,),
        in_specs=[pl.BlockSpec((MOE_ROWS, dh), lambda j, be, nu, nx, od: (jnp.minimum(j, nu[0] - 1), 0)),
                  hbm, hbm, hbm],
        out_specs=pl.BlockSpec((MOE_ROWS, dh), lambda j, be, nu, nx, od: (j, 0)),
        scratch_shapes=[pltpu.VMEM((2, d, d_e), F32), pltpu.VMEM((2, d, d_e), F32), pltpu.VMEM((2, d_e, d), F32),
                        pltpu.VMEM((d, d_e), BF16), pltpu.VMEM((d, d_e), BF16), pltpu.VMEM((d_e, d), BF16),
                        pltpu.SemaphoreType.DMA((2,))],
    )
    return pl.pallas_call(
        _experts_kernel,
        grid_spec=grid_spec,
        out_shape=jax.ShapeDtypeStruct((n_rows_pad, dh), jnp.uint32),
        compiler_params=pltpu.CompilerParams(dimension_semantics=("arbitrary",), vmem_limit_bytes=VMEM_LIMIT),
        name="moe_experts",
    )(block_e, n_used, next_used, used_ord, xs, w_gate, w_up, w_down)


def _combine_kernel(dest_ref, dest_next_ref, h_ref, gate_ref, nfin_ref, ys_ref, o_ref, ybuf, sem):
    step = pl.program_id(0)
    tm = h_ref.shape[0]
    slot = lax.rem(step, 2)

    def copy(d_ref, s, i, k):
        d = d_ref[0, 0, i * TOP_K + k]
        return pltpu.make_async_copy(ys_ref.at[pl.ds(d, 1), :], ybuf.at[s, k, pl.ds(i, 1), :], sem.at[s])

    def start_all(d_ref, s):
        def body(i, carry):
            for k in range(TOP_K):
                copy(d_ref, s, i, k).start(priority=k)
            return carry
        lax.fori_loop(0, tm, body, 0, unroll=8)

    @pl.when(step == 0)
    def _():
        start_all(dest_ref, 0)

    @pl.when(step + 1 < pl.num_programs(0))
    def _():
        start_all(dest_next_ref, 1 - slot)

    def wait_body(i, carry):
        for k in range(TOP_K):
            copy(dest_ref, slot, i, k).wait()
        return carry

    lax.fori_loop(0, tm, wait_body, 0, unroll=8)
    gates = gate_ref[...]
    moe = (gates[:, 0:1] * _unpack_bf16_pairs(ybuf[slot, 0])
           + gates[:, 1:2] * _unpack_bf16_pairs(ybuf[slot, 1]))
    o_ref[...] = _rms(h_ref[...] + moe, nfin_ref[...])


def combine(h, gates, dest, ys, norm_final, *, tm=256):
    nt, d = h.shape
    assert nt % tm == 0 and ys.shape[1] * 2 == d
    n_tiles = nt // tm
    dest3 = dest.reshape(n_tiles, 1, tm * TOP_K)
    return pl.pallas_call(
        _combine_kernel,
        grid=(n_tiles,),
        in_specs=[pl.BlockSpec((1, 1, tm * TOP_K), lambda i: (i, 0, 0), memory_space=pltpu.SMEM),
                  pl.BlockSpec((1, 1, tm * TOP_K), lambda i: (jnp.minimum(i + 1, n_tiles - 1), 0, 0),
                               memory_space=pltpu.SMEM),
                  pl.BlockSpec((tm, d), lambda i: (i, 0)),
                  pl.BlockSpec((tm, LANES), lambda i: (i, 0)),
                  pl.BlockSpec((1, d), lambda i: (0, 0)),
                  pl.BlockSpec(memory_space=pl.ANY)],
        out_specs=pl.BlockSpec((tm, d), lambda i: (i, 0)),
        out_shape=jax.ShapeDtypeStruct((nt, d), F32),
        scratch_shapes=[pltpu.VMEM((2, TOP_K, tm, d // 2), jnp.uint32), pltpu.SemaphoreType.DMA((2,))],
        compiler_params=pltpu.CompilerParams(dimension_semantics=("arbitrary",), vmem_limit_bytes=VMEM_LIMIT),
        name="moe_combine",
    )(dest3, dest3, h, gates, norm_final.reshape(1, d).astype(F32), ys)


def kernel(x, norm_mix, w_in, conv_dw, conv_b, conv_ln_g, conv_ln_b, shift_mu, w0, w_lora_up, a0, a_lora_up, g_lora_up, k_k, k_a, r_k, gn_g, gn_b, w_out, norm_ffn, router_group_w, router_group_b, router_expert_w, router_expert_b, expert_w_gate, expert_w_up, expert_w_down, norm_final):
    B, T, D = x.shape
    depth = w_in.shape[0]
    d_c = conv_dw.shape[2]
    d_r = w0.shape[1]
    d_main = 2 * d_c + 3 * d_r
    nt = B * T
    n_rows_pad = -(-(nt * TOP_K + N_EXPERTS * (MOE_ROWS - 1)) // MOE_ROWS) * MOE_ROWS
    h = x.reshape(nt, D)
    for l in range(depth):
        proj_main, proj_lora = in_proj(h, norm_mix[l], w_in[l], d_main)
        y_conv = conv_mixer(proj_main, conv_dw[l], conv_b[l], conv_ln_g[l], conv_ln_b[l], batch=B)
        y_rwkv = rwkv_mixer(proj_main, proj_lora, 2 * d_c, shift_mu[l], w0[l], w_lora_up[l], a0[l], a_lora_up[l],
                            g_lora_up[l], k_k[l], k_a[l], r_k[l].reshape(-1), gn_g[l], gn_b[l], batch=B)
        h, v, routing, gates, counts = out_router(h, y_conv, y_rwkv, w_out[l].astype(BF16), norm_ffn[l],
                                                  router_group_w[l], router_group_b[l], router_expert_w[l],
                                                  router_expert_b[l])
        dest, block_e, n_used, next_used, used_ord, fill_plan = slot_plan(routing, counts[0, :N_EXPERTS],
                                                                          n_rows_pad)
        xs = dispatch(v, dest, fill_plan, n_rows_pad)
        ys = experts(xs, block_e, n_used, next_used, used_ord, expert_w_gate[l], expert_w_up[l], expert_w_down[l],
                     n_rows_pad)
        assert depth == 1
        h = combine(h, gates, dest, ys, norm_final)
    return h.reshape(B, T, D)
```

```python
import functools

import jax
import jax.numpy as jnp
from jax import lax
from jax.experimental import pallas as pl
from jax.experimental.pallas import tpu as pltpu

F32 = jnp.float32
BF16 = jnp.bfloat16

RWKV_HEAD = 64
CONV_WIDTH = 31
N_GROUPS = 4
EXPERTS_PER_GROUP = 8
N_EXPERTS = N_GROUPS * EXPERTS_PER_GROUP
TOP_K = 2
RMS_EPS = 1e-6
LN_EPS = 1e-5
GN_EPS = 64e-5
LANES = 128
SUBLANES = 8
MXU_DIM = 256
VMEM_LIMIT = 56 * 1024 * 1024

_NT = (((1,), (1,)), ((), ()))
_TN = (((0,), (0,)), ((), ()))


def _dot(a, b, dims=None):
    a = a.astype(BF16)
    b = b.astype(BF16)
    if dims is None:
        return jnp.dot(a, b, preferred_element_type=F32)
    return lax.dot_general(a, b, dims, preferred_element_type=F32)


def _dot_split(x, m):
    hi = x.astype(BF16)
    lo = (x - hi.astype(F32)).astype(BF16)
    return (jnp.dot(hi, m, preferred_element_type=F32)
            + jnp.dot(lo, m, preferred_element_type=F32))


def _pack_bf16_pairs(x):
    w = x.shape[1] // 2
    lo = lax.bitcast_convert_type(x[:, :w].astype(BF16).astype(F32), jnp.uint32)
    hi = lax.bitcast_convert_type(x[:, w:].astype(BF16).astype(F32), jnp.uint32)
    return (lo >> 16) | (hi & jnp.uint32(0xFFFF0000))


def _unpack_bf16_pairs(p):
    lo = lax.bitcast_convert_type(p << 16, F32)
    hi = lax.bitcast_convert_type(p & jnp.uint32(0xFFFF0000), F32)
    return jnp.concatenate([lo, hi], axis=1)


def _rwkv_kernel(r_ref, k_ref, v_ref, lora_ref, mu_main_ref, mu_lora_ref, w0_ref, wup_ref, a0_ref, aup_ref,
                 gup_ref, kk_ref, ka_ref, rk_ref, gng_ref, gnb_ref, y_ref,
                 state_ref, prev_main_ref, prev_lora_ref, *, n_heads, d_decay, d_aaa, chunk):
    nb, Cb, d_r = r_ref.shape
    C = chunk
    CS = Cb // C
    N = RWKV_HEAD
    R = nb * Cb

    @pl.when(pl.program_id(0) == 0)
    def _():
        state_ref[...] = jnp.zeros_like(state_ref)
        prev_main_ref[...] = jnp.zeros_like(prev_main_ref)
        prev_lora_ref[...] = jnp.zeros_like(prev_lora_ref)

    row = lax.broadcasted_iota(jnp.int32, (R, 1), 0)

    def shifted(x_ref, prev_ref, j, mu):
        x = x_ref[...].reshape(R, x_ref.shape[2])
        xp = pltpu.roll(x, 1, 0)
        for b in range(nb):
            xp = jnp.where(row == b * Cb, prev_ref[b, j:j + 1, :], xp)
            prev_ref[b, j:j + 1, :] = x[(b + 1) * Cb - 1:(b + 1) * Cb, :]
        return x + (xp - x) * mu

    r = shifted(r_ref, prev_main_ref, 0, mu_main_ref[0:1, :])
    k = shifted(k_ref, prev_main_ref, 1, mu_main_ref[1:2, :])
    v = shifted(v_ref, prev_main_ref, 2, mu_main_ref[2:3, :])
    lo = shifted(lora_ref, prev_lora_ref, 0, mu_lora_ref[...])

    wd = lo[:, :d_decay]
    ad = lo[:, d_decay:d_decay + d_aaa]
    gd = lo[:, d_decay + d_aaa:]

    z = w0_ref[...] + _dot(jnp.tanh(wd), wup_ref[...])
    w = -(jnp.maximum(-z, 0.0) + jnp.log(1.0 + jnp.exp(-jnp.abs(z)))) - 0.5
    logd = -jnp.exp(w)
    a = jax.nn.sigmoid(a0_ref[...] + _dot(ad, aup_ref[...]))
    g = _dot(jax.nn.sigmoid(gd), gup_ref[...])

    li = lax.broadcasted_iota(jnp.int32, (MXU_DIM, MXU_DIM), 0) // N
    lj = lax.broadcasted_iota(jnp.int32, (MXU_DIM, MXU_DIM), 1) // N
    head_ones = (li == lj).astype(BF16)

    def head_sum(x):
        return jnp.concatenate(
            [_dot_split(x[:, c * MXU_DIM:(c + 1) * MXU_DIM], head_ones) for c in range(d_r // MXU_DIM)], axis=1)

    kkr = k * kk_ref[...]
    kk = kkr / jnp.maximum(jnp.sqrt(head_sum(kkr * kkr)), 1e-12)
    k2 = k * (1.0 + (a - 1.0) * ka_ref[...])
    a_s = -kk
    b_s = kk * a

    ri = lax.broadcasted_iota(jnp.int32, (R, R), 0)
    rj = lax.broadcasted_iota(jnp.int32, (R, R), 1)
    tri = ((rj <= ri) & ((ri // C) == (rj // C))).astype(BF16)
    hi = logd.astype(BF16)
    rem = logd - hi.astype(F32)
    mid = rem.astype(BF16)
    low = (rem - mid.astype(F32)).astype(BF16)
    cum = (jnp.dot(tri, hi, preferred_element_type=F32) + jnp.dot(tri, mid, preferred_element_type=F32)
           + jnp.dot(tri, low, preferred_element_type=F32))
    p_incl = jnp.exp(cum)
    p_excl = jnp.exp(cum - logd)
    p_inv = jnp.exp(-cum)
    p_last = [p_incl[(q + 1) * C - 1:(q + 1) * C, :] for q in range(nb * CS)]
    p_last_rows = jnp.concatenate([jnp.broadcast_to(p, (C, d_r)) for p in p_last], axis=0)

    at = (a_s * p_excl).astype(BF16)
    rt = (r * p_incl).astype(BF16)
    bt = b_s * p_inv
    kt = k2 * p_inv
    bh = (bt * p_last_rows).astype(BF16)
    kh = (kt * p_last_rows).astype(BF16)
    bt = bt.astype(BF16)
    kt = kt.astype(BF16)
    vb = v.astype(BF16)

    HG = LANES // N
    W = HG * N
    GC = HG * C
    wi = lax.broadcasted_iota(jnp.int32, (GC, W), 0)
    wj = lax.broadcasted_iota(jnp.int32, (GC, W), 1)
    bd_on = ((wi // C) == (wj // N)).astype(BF16) > 0

    def bdiag(x):
        xb = x.astype(BF16)
        return jnp.where(bd_on, jnp.concatenate([xb] * HG, axis=0), jnp.zeros((), BF16))

    ti = lax.broadcasted_iota(jnp.int32, (2 * C, GC), 0)
    tj = lax.broadcasted_iota(jnp.int32, (2 * C, GC), 1) % C
    mask2 = ((ti < C) & (tj < ti)) | ((ti >= C) & (tj <= ti - C))
    li = lax.broadcasted_iota(jnp.int32, (C, GC), 0)
    lj = lax.broadcasted_iota(jnp.int32, (C, GC), 1) % C
    eye = (li == lj).astype(F32)
    level_masks = []
    s = 1
    while s < C:
        level_masks.append(((li // (2 * s)) == (lj // (2 * s))) & (((li // s) % 2) == 1) & (((lj // s) % 2) == 0))
        s *= 2
    colblk = lax.broadcasted_iota(jnp.int32, (N, W), 1) // N

    n_g = n_heads // HG
    units = [(q, g) for q in range(nb * CS) for g in range(n_g)]
    idx = range(len(units))
    rs = [slice(q * C, (q + 1) * C) for q, _ in units]
    ls = [slice(g * W, (g + 1) * W) for _, g in units]
    g_b = [jnp.where(mask2, _dot(jnp.concatenate([at[rs[i], ls[i]], rt[rs[i], ls[i]]], axis=0),
                                 bdiag(bt[rs[i], ls[i]]), _NT), 0.0) for i in idx]
    g_k = [jnp.where(mask2, _dot(jnp.concatenate([at[rs[i], ls[i]], rt[rs[i], ls[i]]], axis=0),
                                 bdiag(kt[rs[i], ls[i]]), _NT), 0.0) for i in idx]
    gv = [_dot(g_k[i], bdiag(vb[rs[i], ls[i]])) for i in idx]
    tinv = [eye + jnp.where(level_masks[0], g_b[i][:C], 0.0) for i in idx]
    for lm in level_masks[1:]:
        xs = [_dot(jnp.where(lm, g_b[i][:C], 0.0), bdiag(tinv[i])) for i in idx]
        tinv = [tinv[i] + _dot(tinv[i], bdiag(xs[i])) for i in idx]
    ta = [_dot(tinv[i], bdiag(at[rs[i], ls[i]])) for i in idx]
    tv = [_dot(tinv[i], bdiag(gv[i][:C])) for i in idx]
    ys = [None] * len(units)
    for b in range(nb):
        state = [state_ref[b * n_g + gi] for gi in range(n_g)]
        for cc in range(CS):
            ids = [(b * CS + cc) * n_g + gi for gi in range(n_g)]
            tr = [_dot(jnp.concatenate([ta[i].astype(BF16), rt[rs[i], ls[i]]], axis=0), bdiag(state[gi]), _NT)
                  for gi, i in enumerate(ids)]
            u = [tr[gi][:C] + tv[i] for gi, i in enumerate(ids)]
            for gi, i in enumerate(ids):
                ys[i] = tr[gi][C:] + gv[i][C:] + _dot(g_b[i][C:], bdiag(u[gi]))
                uv = jnp.concatenate([u[gi].astype(BF16), vb[rs[i], ls[i]]], axis=0)
                bk = jnp.concatenate([bh[rs[i], ls[i]], kh[rs[i], ls[i]]], axis=0)
                full = _dot(uv, bk, _TN)
                upd = state[gi] * p_last[units[i][0]][:, ls[i]]
                for h in range(HG):
                    upd = upd + jnp.where(colblk == h, full[h * N:(h + 1) * N, :], 0.0)
                state[gi] = upd
        for gi in range(n_g):
            state_ref[b * n_g + gi] = state[gi]
    y = jnp.concatenate([jnp.concatenate(ys[q * n_g:(q + 1) * n_g], axis=1) for q in range(nb * CS)],
                        axis=0)

    inv_n = 1.0 / N
    mu_y = head_sum(y) * inv_n
    yc = y - mu_y
    var_y = head_sum(yc * yc) * inv_n
    yn = yc * lax.rsqrt(var_y + GN_EPS) * gng_ref[...] + gnb_ref[...]
    bonus = head_sum(r * k2 * rk_ref[...]) * v
    y_ref[...] = ((yn + bonus) * g).reshape(nb, Cb, d_r).astype(y_ref.dtype)


def rwkv_mixer(proj_main, proj_lora, col0, shift_mu, w0, w_up, a0, a_up, g_up, k_k, k_a, r_k, gn_g, gn_b,
               *, batch, chunk=64, chunks_per_step=2):
    nt = proj_main.shape[0]
    seq = nt // batch
    d_r = w0.shape[-1]
    n_heads = d_r // RWKV_HEAD
    d_lora = proj_lora.shape[1]
    d_decay, d_aaa = w_up.shape[0], a_up.shape[0]
    rows = chunk * chunks_per_step
    n_steps = seq // rows
    assert seq % rows == 0 and col0 % d_r == 0
    cb = col0 // d_r
    row2 = lambda t: t.reshape(1, -1).astype(F32)
    mu_main = shift_mu[:3 * d_r].reshape(3, d_r)
    mu_lora = shift_mu[3 * d_r:].reshape(1, d_lora)
    pm3 = proj_main.reshape(batch, seq, proj_main.shape[1])
    pl3 = proj_lora.reshape(batch, seq, d_lora)

    def main_spec(j):
        return pl.BlockSpec((batch, rows, d_r), lambda c, j=j: (0, c, cb + j))

    full = lambda shape: pl.BlockSpec(shape, lambda c: (0,) * len(shape))
    kern = functools.partial(_rwkv_kernel, n_heads=n_heads, d_decay=d_decay, d_aaa=d_aaa, chunk=chunk)
    y = pl.pallas_call(
        kern,
        grid=(n_steps,),
        in_specs=[main_spec(0), main_spec(1), main_spec(2),
                  pl.BlockSpec((batch, rows, d_lora), lambda c: (0, c, 0)),
                  full((3, d_r)), full((1, d_lora)), full((1, d_r)), full((d_decay, d_r)), full((1, d_r)),
                  full((d_aaa, d_r)), full((g_up.shape[0], d_r)), full((1, d_r)), full((1, d_r)), full((1, d_r)),
                  full((1, d_r)), full((1, d_r))],
        out_specs=pl.BlockSpec((batch, rows, d_r), lambda c: (0, c, 0)),
        out_shape=jax.ShapeDtypeStruct((batch, seq, d_r), BF16),
        scratch_shapes=[pltpu.VMEM((batch * n_heads * RWKV_HEAD // LANES, RWKV_HEAD, LANES), F32),
                        pltpu.VMEM((batch, 3, d_r), F32),
                        pltpu.VMEM((batch, 1, d_lora), F32)],
        compiler_params=pltpu.CompilerParams(dimension_semantics=("arbitrary",), vmem_limit_bytes=VMEM_LIMIT),
        name="rwkv_mixer",
    )(pm3, pm3, pm3, pl3, mu_main, mu_lora, row2(w0), w_up.astype(BF16), row2(a0),
      a_up.astype(BF16), g_up.astype(BF16), row2(k_k), row2(k_a), row2(r_k), row2(gn_g), row2(gn_b))
    return y.reshape(nt, d_r)


def _rms(x, g):
    return x * lax.rsqrt(jnp.mean(x * x, axis=-1, keepdims=True) + RMS_EPS) * g


def _inproj_kernel(x_ref, g_ref, wt_ref, wlt_ref, o_ref, ol_ref, u_ref):
    @pl.when(pl.program_id(1) == 0)
    def _():
        u = _rms(x_ref[...], g_ref[...]).astype(BF16)
        u_ref[...] = u
        ol_ref[...] = lax.dot_general(u, wlt_ref[...], _NT, preferred_element_type=F32)

    o_ref[...] = lax.dot_general(u_ref[...], wt_ref[...].astype(BF16), _NT, preferred_element_type=F32)


def in_proj(x2, norm_g, w_in, d_main, *, tm=1024, tn=1024):
    nt, d = x2.shape
    d_lora = w_in.shape[1] - d_main
    assert nt % tm == 0 and d_main % tn == 0
    w_t = w_in.T
    w_lora_t = w_t[d_main:].astype(BF16)
    return pl.pallas_call(
        _inproj_kernel,
        grid=(nt // tm, d_main // tn),
        in_specs=[pl.BlockSpec((tm, d), lambda i, j: (i, 0)),
                  pl.BlockSpec((1, d), lambda i, j: (0, 0)),
                  pl.BlockSpec((tn, d), lambda i, j: (j, 0)),
                  pl.BlockSpec((d_lora, d), lambda i, j: (0, 0))],
        out_specs=[pl.BlockSpec((tm, tn), lambda i, j: (i, j)),
                   pl.BlockSpec((tm, d_lora), lambda i, j: (i, 0))],
        out_shape=[jax.ShapeDtypeStruct((nt, d_main), F32), jax.ShapeDtypeStruct((nt, d_lora), F32)],
        scratch_shapes=[pltpu.VMEM((tm, d), BF16)],
        compiler_params=pltpu.CompilerParams(dimension_semantics=("arbitrary", "arbitrary"),
                                             vmem_limit_bytes=VMEM_LIMIT),
        name="in_proj",
    )(x2, norm_g.reshape(1, d).astype(F32), w_t, w_lora_t)


CONV_HALO = 32
CONV_ROWS = 32


def _conv_kernel(val_ref, gate_ref, dw_ref, b_ref, g_ref, beta_ref, o_ref, ubuf_ref):
    tt = val_ref.shape[0]
    d_c = val_ref.shape[1]
    t_idx = pl.program_id(1)

    @pl.when(t_idx == 0)
    def _():
        ubuf_ref[0, 0:CONV_HALO, :] = jnp.zeros((CONV_HALO, d_c), F32)

    @pl.when(t_idx > 0)
    def _():
        ubuf_ref[0, 0:CONV_HALO, :] = ubuf_ref[0, tt:tt + CONV_HALO, :]

    ubuf_ref[0, CONV_HALO:CONV_HALO + tt, :] = val_ref[...] * jax.nn.sigmoid(gate_ref[...])
    n_sh = tt + CONV_HALO - SUBLANES
    for s in range(1, SUBLANES):
        for r in range(0, n_sh, CONV_ROWS):
            n = min(CONV_ROWS, n_sh - r)
            ubuf_ref[s, r:r + n, :] = ubuf_ref[0, r + s:r + s + n, :]
    dw = dw_ref[...]
    lead = CONV_HALO - (CONV_WIDTH - 1)
    for c in range(tt // CONV_ROWS):
        r0 = c * CONV_ROWS
        acc = jnp.zeros((CONV_ROWS, d_c), F32)
        for j in range(CONV_WIDTH):
            q, s = divmod(lead + j, SUBLANES)
            r = r0 + q * SUBLANES
            acc = acc + dw[j:j + 1, :] * ubuf_ref[s, r:r + CONV_ROWS, :]
        acc = acc + b_ref[...]
        mu = jnp.mean(acc, axis=-1, keepdims=True)
        cen = acc - mu
        var = jnp.mean(cen * cen, axis=-1, keepdims=True)
        yv = cen * lax.rsqrt(var + LN_EPS) * g_ref[...] + beta_ref[...]
        o_ref[r0:r0 + CONV_ROWS, :] = (yv * jax.nn.sigmoid(yv)).astype(o_ref.dtype)


def conv_mixer(proj_main, dw, bias, ln_g, ln_b, *, batch, tt=256):
    nt = proj_main.shape[0]
    seq = nt // batch
    d_c = dw.shape[1]
    n_t = seq // tt
    assert seq % tt == 0 and tt % CONV_ROWS == 0
    row = lambda t: t.reshape(1, d_c).astype(F32)
    full = lambda shape: pl.BlockSpec(shape, lambda b, t: (0, 0))
    return pl.pallas_call(
        _conv_kernel,
        grid=(batch, n_t),
        in_specs=[pl.BlockSpec((tt, d_c), lambda b, t: (b * n_t + t, 0)),
                  pl.BlockSpec((tt, d_c), lambda b, t: (b * n_t + t, 1)),
                  full((CONV_WIDTH, d_c)), full((1, d_c)), full((1, d_c)), full((1, d_c))],
        out_specs=pl.BlockSpec((tt, d_c), lambda b, t: (b * n_t + t, 0)),
        out_shape=jax.ShapeDtypeStruct((nt, d_c), BF16),
        scratch_shapes=[pltpu.VMEM((SUBLANES, tt + CONV_HALO, d_c), F32)],
        compiler_params=pltpu.CompilerParams(dimension_semantics=("arbitrary", "arbitrary"),
                                             vmem_limit_bytes=VMEM_LIMIT),
        name="conv_mixer",
    )(proj_main, proj_main, dw.astype(F32), row(bias), row(ln_g), row(ln_b))


ROUTER_SUB = 512


def _out_router_kernel(x_ref, yc_ref, yr_ref, wc_ref, wr_ref, nf_ref, rw_hi_ref, rw_lo_ref, rb_ref,
                       h_ref, v_ref, eid_ref, gate_ref, count_ref, cnt_ref):
    tm = x_ref.shape[0]
    subs = [slice(r, r + ROUTER_SUB) for r in range(0, tm, ROUTER_SUB)]
    hs = [x_ref[sl, :] + jnp.dot(yc_ref[sl, :], wc_ref[...], preferred_element_type=F32)
          + jnp.dot(yr_ref[sl, :], wr_ref[...], preferred_element_type=F32) for sl in subs]

    @pl.when(pl.program_id(0) == 0)
    def _():
        cnt_ref[...] = jnp.zeros_like(cnt_ref)

    lane = lax.broadcasted_iota(jnp.int32, (ROUTER_SUB, LANES), 1)
    neg = jnp.float32(-jnp.inf)
    big = jnp.int32(LANES)
    earlier = (lax.broadcasted_iota(jnp.int32, (ROUTER_SUB, ROUTER_SUB), 1)
               < lax.broadcasted_iota(jnp.int32, (ROUTER_SUB, ROUTER_SUB), 0)).astype(BF16)

    def first_max(vals):
        m = jnp.max(vals, axis=-1, keepdims=True)
        return m, jnp.min(jnp.where(vals == m, lane, big), axis=-1, keepdims=True)

    for sl, h in zip(subs, hs):
        h_ref[sl, :] = h
        v = _rms(h, nf_ref[...])
        v_ref[sl, :] = _pack_bf16_pairs(v)
        v_hi = v.astype(BF16)
        v_lo = (v - v_hi.astype(F32)).astype(BF16)
        logits = (jnp.dot(v_hi, rw_hi_ref[...], preferred_element_type=F32)
                  + jnp.dot(v_lo, rw_hi_ref[...], preferred_element_type=F32)
                  + jnp.dot(v_hi, rw_lo_ref[...], preferred_element_type=F32)) + rb_ref[...]
        gl = jnp.where(lane < N_GROUPS, logits, neg)
        gmax, grp = first_max(gl)
        p_grp = 1.0 / jnp.sum(jnp.exp(gl - gmax), axis=-1, keepdims=True)
        lo = N_GROUPS + grp * EXPERTS_PER_GROUP
        el = jnp.where((lane >= lo) & (lane < lo + EXPERTS_PER_GROUP), logits, neg)
        m1, i1 = first_max(el)
        m2, i2 = first_max(jnp.where(lane == i1, neg, el))
        e2 = jnp.exp(m2 - m1)
        g1 = p_grp / (1.0 + e2)
        g2 = p_grp * e2 / (1.0 + e2)
        gate_ref[sl, :] = jnp.where(lane == 0, g1, jnp.where(lane == 1, g2, 0.0))

        x1 = i1 - N_GROUPS
        x2 = i2 - N_GROUPS
        oh1 = (lane == x1).astype(F32)
        oh2 = (lane == x2).astype(F32)
        before1 = jnp.dot(earlier, oh1.astype(BF16), preferred_element_type=F32)
        before2 = jnp.dot(earlier, oh2.astype(BF16), preferred_element_type=F32)
        carry = cnt_ref[...]
        n1 = jnp.sum(oh1, axis=0, keepdims=True)
        n2 = jnp.sum(oh2, axis=0, keepdims=True)
        rank1 = jnp.sum(oh1 * (before1 + carry), axis=-1, keepdims=True).astype(jnp.int32)
        rank2 = jnp.sum(oh2 * (before2 + carry + n1), axis=-1, keepdims=True).astype(jnp.int32)
        cnt_ref[...] = carry + n1 + n2
        routing = jnp.where(lane == 0, x1, jnp.where(lane == 1, x2, jnp.where(
            lane == 2, rank1, jnp.where(lane == 3, rank2, 0))))
        eid_ref[:, sl] = routing.T[:SUBLANES, :]
    count_ref[...] = cnt_ref[...].astype(jnp.int32)


def out_router(x2, y_conv, y_rwkv, w_out, norm_ffn, rg_w, rg_b, re_w, re_b, *, tm=512):
    nt, d = x2.shape
    d_c, d_r = y_conv.shape[1], y_rwkv.shape[1]
    assert d_c == d_r and nt % tm == 0 and tm % ROUTER_SUB == 0
    n_r = N_GROUPS + N_EXPERTS
    rw = jnp.zeros((d, LANES), F32).at[:, :n_r].set(jnp.concatenate([rg_w, re_w], axis=1).astype(F32))
    rw_hi = rw.astype(BF16)
    rw_lo = (rw - rw_hi.astype(F32)).astype(BF16)
    rb = jnp.zeros((1, LANES), F32).at[0, :n_r].set(jnp.concatenate([rg_b, re_b]).astype(F32))
    full = lambda shape: pl.BlockSpec(shape, lambda i: (0, 0))
    once = lambda shape, r: pl.BlockSpec(shape, lambda i, r=r: (r, 0), pipeline_mode=pl.Buffered(1))
    rows = lambda w: pl.BlockSpec((tm, w), lambda i: (i, 0))
    return pl.pallas_call(
        _out_router_kernel,
        grid=(nt // tm,),
        in_specs=[rows(d), rows(d_c), rows(d_r), once((d_c, d), 0), once((d_r, d), 1),
                  full((1, d)), full((d, LANES)), full((d, LANES)), full((1, LANES))],
        out_specs=[rows(d), rows(d // 2), pl.BlockSpec((SUBLANES, tm), lambda i: (0, i)), rows(LANES),
                   full((1, LANES))],
        out_shape=[jax.ShapeDtypeStruct((nt, d), F32), jax.ShapeDtypeStruct((nt, d // 2), jnp.uint32),
                   jax.ShapeDtypeStruct((SUBLANES, nt), jnp.int32), jax.ShapeDtypeStruct((nt, LANES), F32),
                   jax.ShapeDtypeStruct((1, LANES), jnp.int32)],
        scratch_shapes=[pltpu.VMEM((1, LANES), F32)],
        compiler_params=pltpu.CompilerParams(dimension_semantics=("arbitrary",), vmem_limit_bytes=VMEM_LIMIT),
        name="out_router",
    )(x2, y_conv, y_rwkv, w_out, w_out, norm_ffn.reshape(1, d).astype(F32), rw_hi, rw_lo, rb)


MOE_ROWS = 256


def slot_plan(routing, counts, n_rows_pad):
    expert_id = routing[:TOP_K]
    rank = routing[TOP_K:2 * TOP_K]
    padded = (counts + MOE_ROWS - 1) // MOE_ROWS * MOE_ROWS
    pend = jnp.cumsum(padded)
    pstarts = pend - padded
    ids = jnp.arange(N_EXPERTS, dtype=jnp.int32)
    start_of = jnp.sum(jnp.where(expert_id[..., None] == ids, pstarts, 0), axis=-1)
    dest = (start_of + rank).astype(jnp.int32).T
    n_blocks = n_rows_pad // MOE_ROWS
    block_row0 = jnp.arange(n_blocks, dtype=jnp.int32) * MOE_ROWS
    block_e = jnp.minimum(jnp.sum((pend[None, :] <= block_row0[:, None]).astype(jnp.int32), axis=1),
                          N_EXPERTS - 1).astype(jnp.int32)
    n_used = (pend[-1:] // MOE_ROWS).astype(jnp.int32)
    owner = jnp.where(counts > 0, ids, N_EXPERTS)
    nxt = lax.cummin(jnp.concatenate([owner[1:], jnp.full((1,), N_EXPERTS, jnp.int32)]), reverse=True)
    next_used = jnp.where(nxt < N_EXPERTS, nxt, -1).astype(jnp.int32)
    used_ord = (jnp.cumsum((counts > 0).astype(jnp.int32)) - 1).astype(jnp.int32)
    pad0 = pstarts + counts
    n_single = jnp.minimum((-pad0) % SUBLANES, pend - pad0)
    tile0 = pad0 + n_single
    n_tile = (pend - tile0) // SUBLANES
    tail = jnp.stack([pend[-1] // MOE_ROWS, n_blocks - pend[-1] // MOE_ROWS, jnp.sum(n_single), jnp.sum(n_tile)])
    fill_plan = jnp.concatenate([pad0, n_single, tile0, n_tile, tail]).astype(jnp.int32)
    return dest, block_e, n_used, next_used, used_ord, fill_plan


def _dispatch_kernel(fill_ref, dest_ref, v_ref, xs_ref, zero_ref, sem, fill_sem):
    tm = v_ref.shape[0]

    def fills():
        e0, e1, e2, e3, e4 = (k * N_EXPERTS for k in range(5))
        single = lambda e, i: pltpu.make_async_copy(
            zero_ref.at[pl.ds(0, 1), :], xs_ref.at[pl.ds(fill_ref[e0 + e] + i, 1), :], fill_sem)
        tile = lambda e, i: pltpu.make_async_copy(
            zero_ref.at[pl.ds(0, SUBLANES), :],
            xs_ref.at[pl.ds(pl.multiple_of(fill_ref[e2 + e] + i * SUBLANES, SUBLANES), SUBLANES), :], fill_sem)
        block = lambda i: pltpu.make_async_copy(
            zero_ref, xs_ref.at[pl.ds(pl.multiple_of((fill_ref[e4] + i) * MOE_ROWS, MOE_ROWS), MOE_ROWS), :],
            fill_sem)
        return single, tile, block, e1, e3, e4

    @pl.when(pl.program_id(0) == 0)
    def _():
        zero_ref[...] = jnp.zeros_like(zero_ref)
        single, tile, block, e1, e3, e4 = fills()
        for e in range(N_EXPERTS):
            lax.fori_loop(0, fill_ref[e1 + e], lambda i, c, e=e: (single(e, i).start(), c)[1], 0)
            lax.fori_loop(0, fill_ref[e3 + e], lambda i, c, e=e: (tile(e, i).start(), c)[1], 0)
        lax.fori_loop(0, fill_ref[e4 + 1], lambda i, c: (block(i).start(), c)[1], 0)

    def copy(i, k):
        d = dest_ref[0, 0, i * TOP_K + k]
        return pltpu.make_async_copy(v_ref.at[pl.ds(i, 1), :], xs_ref.at[pl.ds(d, 1), :], sem)

    def start(i, carry):
        for k in range(TOP_K):
            copy(i, k).start(priority=k)
        return carry

    def wait(i, carry):
        for k in range(TOP_K):
            copy(i, k).wait()
        return carry

    lax.fori_loop(0, tm, start, 0, unroll=8)
    lax.fori_loop(0, tm, wait, 0, unroll=8)

    @pl.when(pl.program_id(0) == 0)
    def _():
        single, tile, block, e1, e3, e4 = fills()
        lax.fori_loop(0, fill_ref[e4 + 2], lambda i, c: (single(0, 0).wait(), c)[1], 0)
        lax.fori_loop(0, fill_ref[e4 + 3], lambda i, c: (tile(0, 0).wait(), c)[1], 0)
        lax.fori_loop(0, fill_ref[e4 + 1], lambda i, c: (block(0).wait(), c)[1], 0)


def dispatch(v, dest, fill_plan, n_rows_pad, *, tm=1024):
    nt, d = v.shape
    assert nt % tm == 0
    dest3 = dest.reshape(nt // tm, 1, tm * TOP_K)
    grid_spec = pltpu.PrefetchScalarGridSpec(
        num_scalar_prefetch=1,
        grid=(nt // tm,),
        in_specs=[pl.BlockSpec((1, 1, tm * TOP_K), lambda i, fp: (i, 0, 0), memory_space=pltpu.SMEM),
                  pl.BlockSpec((tm, d), lambda i, fp: (i, 0))],
        out_specs=pl.BlockSpec(memory_space=pl.ANY),
        scratch_shapes=[pltpu.VMEM((MOE_ROWS, d), v.dtype), pltpu.SemaphoreType.DMA(()),
                        pltpu.SemaphoreType.DMA(())],
    )
    return pl.pallas_call(
        _dispatch_kernel,
        grid_spec=grid_spec,
        out_shape=jax.ShapeDtypeStruct((n_rows_pad, d), v.dtype),
        compiler_params=pltpu.CompilerParams(dimension_semantics=("arbitrary",), vmem_limit_bytes=VMEM_LIMIT),
        name="moe_dispatch",
    )(fill_plan, dest3, v)


def _experts_kernel(be_ref, nu_ref, nxt_ref, ord_ref, xs_ref, wg_hbm, wu_hbm, wd_hbm, ys_ref,
                    wg_f, wu_f, wd_f, wg_s, wu_s, wd_s, wsem):
    j = pl.program_id(0)
    e = be_ref[j]

    def fetch(ex, slot):
        return [pltpu.make_async_copy(wg_hbm.at[ex], wg_f.at[slot], wsem.at[slot]),
                pltpu.make_async_copy(wu_hbm.at[ex], wu_f.at[slot], wsem.at[slot]),
                pltpu.make_async_copy(wd_hbm.at[ex], wd_f.at[slot], wsem.at[slot])]

    @pl.when(j == 0)
    def _():
        for c in fetch(e, lax.rem(ord_ref[e], 2)):
            c.start()

    first = (j == 0) | (e != be_ref[jnp.maximum(j - 1, 0)])

    @pl.when(first & (j < nu_ref[0]))
    def _():
        slot = lax.rem(ord_ref[e], 2)
        for c in fetch(e, slot):
            c.wait()
        nxt = nxt_ref[e]

        @pl.when(nxt >= 0)
        def _():
            for c in fetch(nxt, 1 - slot):
                c.start()

        wg_s[...] = wg_f[slot].astype(BF16)
        wu_s[...] = wu_f[slot].astype(BF16)
        wd_s[...] = wd_f[slot].astype(BF16)

    @pl.when(j < nu_ref[0])
    def _():
        xb = _unpack_bf16_pairs(xs_ref[...]).astype(BF16)
        gt = jnp.dot(xb, wg_s[...], preferred_element_type=F32)
        up = jnp.dot(xb, wu_s[...], preferred_element_type=F32)
        hid = (gt * jax.nn.sigmoid(gt) * up).astype(BF16)
        ys_ref[...] = _pack_bf16_pairs(jnp.dot(hid, wd_s[...], preferred_element_type=F32))

    @pl.when(j >= nu_ref[0])
    def _():
        ys_ref[...] = jnp.zeros_like(ys_ref)


def experts(xs, block_e, n_used, next_used, used_ord, w_gate, w_up, w_down, n_rows_pad):
    dh = xs.shape[1]
    d, d_e = w_gate.shape[1], w_gate.shape[2]
    assert d == 2 * dh
    n_blocks = n_rows_pad // MOE_ROWS
    hbm = pl.BlockSpec(memory_space=pl.ANY)
    grid_spec = pltpu.PrefetchScalarGridSpec(
        num_scalar_prefetch=4,
        grid=(n_blocks,),
        in_specs=[pl.BlockSpec((MOE_ROWS, dh), lambda j, be, nu, nx, od: (jnp.minimum(j, nu[0] - 1), 0)),
                  hbm, hbm, hbm],
        out_specs=pl.BlockSpec((MOE_ROWS, dh), lambda j, be, nu, nx, od: (j, 0)),
        scratch_shapes=[pltpu.VMEM((2, d, d_e), F32), pltpu.VMEM((2, d, d_e), F32), pltpu.VMEM((2, d_e, d), F32),
                        pltpu.VMEM((d, d_e), BF16), pltpu.VMEM((d, d_e), BF16), pltpu.VMEM((d_e, d), BF16),
                        pltpu.SemaphoreType.DMA((2,))],
    )
    return pl.pallas_call(
        _experts_kernel,
        grid_spec=grid_spec,
        out_shape=jax.ShapeDtypeStruct((n_rows_pad, dh), jnp.uint32),
        compiler_params=pltpu.CompilerParams(dimension_semantics=("arbitrary",), vmem_limit_bytes=VMEM_LIMIT),
        name="moe_experts",
    )(block_e, n_used, next_used, used_ord, xs, w_gate, w_up, w_down)


def _combine_kernel(dest_ref, dest_next_ref, h_ref, gate_ref, nfin_ref, ys_ref, o_ref, ybuf, sem):
    step = pl.program_id(0)
    tm = h_ref.shape[0]
    slot = lax.rem(step, 2)

    def copy(d_ref, s, i, k):
        d = d_ref[0, 0, i * TOP_K + k]
        return pltpu.make_async_copy(ys_ref.at[pl.ds(d, 1), :], ybuf.at[s, k, pl.ds(i, 1), :], sem.at[s])

    def start_all(d_ref, s):
        def body(i, carry):
            for k in range(TOP_K):
                copy(d_ref, s, i, k).start(priority=k)
            return carry
        lax.fori_loop(0, tm, body, 0, unroll=8)

    @pl.when(step == 0)
    def _():
        start_all(dest_ref, 0)

    @pl.when(step + 1 < pl.num_programs(0))
    def _():
        start_all(dest_next_ref, 1 - slot)

    def wait_body(i, carry):
        for k in range(TOP_K):
            copy(dest_ref, slot, i, k).wait()
        return carry

    lax.fori_loop(0, tm, wait_body, 0, unroll=8)
    gates = gate_ref[...]
    moe = (gates[:, 0:1] * _unpack_bf16_pairs(ybuf[slot, 0])
           + gates[:, 1:2] * _unpack_bf16_pairs(ybuf[slot, 1]))
    o_ref[...] = _rms(h_ref[...] + moe, nfin_ref[...])


def combine(h, gates, dest, ys, norm_final, *, tm=512):
    nt, d = h.shape
    assert nt % tm == 0 and ys.shape[1] * 2 == d
    n_tiles = nt // tm
    dest3 = dest.reshape(n_tiles, 1, tm * TOP_K)
    return pl.pallas_call(
        _combine_kernel,
        grid=(n_tiles,),
        in_specs=[pl.BlockSpec((1, 1, tm * TOP_K), lambda i: (i, 0, 0), memory_space=pltpu.SMEM),
                  pl.BlockSpec((1, 1, tm * TOP_K), lambda i: (jnp.minimum(i + 1, n_tiles - 1), 0, 0),
                               memory_space=pltpu.SMEM),
                  pl.BlockSpec((tm, d), lambda i: (i, 0)),
                  pl.BlockSpec((tm, LANES), lambda i: (i, 0)),
                  pl.BlockSpec((1, d), lambda i: (0, 0)),
                  pl.BlockSpec(memory_space=pl.ANY)],
        out_specs=pl.BlockSpec((tm, d), lambda i: (i, 0)),
        out_shape=jax.ShapeDtypeStruct((nt, d), F32),
        scratch_shapes=[pltpu.VMEM((2, TOP_K, tm, d // 2), jnp.uint32), pltpu.SemaphoreType.DMA((2,))],
        compiler_params=pltpu.CompilerParams(dimension_semantics=("arbitrary",), vmem_limit_bytes=VMEM_LIMIT),
        name="moe_combine",
    )(dest3, dest3, h, gates, norm_final.reshape(1, d).astype(F32), ys)


def kernel(x, norm_mix, w_in, conv_dw, conv_b, conv_ln_g, conv_ln_b, shift_mu, w0, w_lora_up, a0, a_lora_up, g_lora_up, k_k, k_a, r_k, gn_g, gn_b, w_out, norm_ffn, router_group_w, router_group_b, router_expert_w, router_expert_b, expert_w_gate, expert_w_up, expert_w_down, norm_final):
    B, T, D = x.shape
    depth = w_in.shape[0]
    d_c = conv_dw.shape[2]
    d_r = w0.shape[1]
    d_main = 2 * d_c + 3 * d_r
    nt = B * T
    n_rows_pad = -(-(nt * TOP_K + N_EXPERTS * (MOE_ROWS - 1)) // MOE_ROWS) * MOE_ROWS
    h = x.reshape(nt, D)
    for l in range(depth):
        proj_main, proj_lora = in_proj(h, norm_mix[l], w_in[l], d_main)
        y_conv = conv_mixer(proj_main, conv_dw[l], conv_b[l], conv_ln_g[l], conv_ln_b[l], batch=B)
        y_rwkv = rwkv_mixer(proj_main, proj_lora, 2 * d_c, shift_mu[l], w0[l], w_lora_up[l], a0[l], a_lora_up[l],
                            g_lora_up[l], k_k[l], k_a[l], r_k[l].reshape(-1), gn_g[l], gn_b[l], batch=B)
        h, v, routing, gates, counts = out_router(h, y_conv, y_rwkv, w_out[l].astype(BF16), norm_ffn[l],
                                                  router_group_w[l], router_group_b[l], router_expert_w[l],
                                                  router_expert_b[l])
        dest, block_e, n_used, next_used, used_ord, fill_plan = slot_plan(routing, counts[0, :N_EXPERTS],
                                                                          n_rows_pad)
        xs = dispatch(v, dest, fill_plan, n_rows_pad)
        ys = experts(xs, block_e, n_used, next_used, used_ord, expert_w_gate[l], expert_w_up[l], expert_w_down[l],
                     n_rows_pad)
        assert depth == 1
        h = combine(h, gates, dest, ys, norm_final)
    return h.reshape(B, T, D)
```

```python
import functools

import jax
import jax.numpy as jnp
from jax import lax
from jax.experimental import pallas as pl
from jax.experimental.pallas import tpu as pltpu

F32 = jnp.float32
BF16 = jnp.bfloat16

RWKV_HEAD = 64
CONV_WIDTH = 31
N_GROUPS = 4
EXPERTS_PER_GROUP = 8
N_EXPERTS = N_GROUPS * EXPERTS_PER_GROUP
TOP_K = 2
RMS_EPS = 1e-6
LN_EPS = 1e-5
GN_EPS = 64e-5
LANES = 128
SUBLANES = 8
MXU_DIM = 256
VMEM_LIMIT = 56 * 1024 * 1024

_NT = (((1,), (1,)), ((), ()))
_TN = (((0,), (0,)), ((), ()))


def _dot(a, b, dims=None):
    a = a.astype(BF16)
    b = b.astype(BF16)
    if dims is None:
        return jnp.dot(a, b, preferred_element_type=F32)
    return lax.dot_general(a, b, dims, preferred_element_type=F32)


def _dot_split(x, m):
    hi = x.astype(BF16)
    lo = (x - hi.astype(F32)).astype(BF16)
    return (jnp.dot(hi, m, preferred_element_type=F32)
            + jnp.dot(lo, m, preferred_element_type=F32))


def _pack_bf16_pairs(x):
    w = x.shape[1] // 2
    lo = lax.bitcast_convert_type(x[:, :w].astype(BF16).astype(F32), jnp.uint32)
    hi = lax.bitcast_convert_type(x[:, w:].astype(BF16).astype(F32), jnp.uint32)
    return (lo >> 16) | (hi & jnp.uint32(0xFFFF0000))


def _unpack_bf16_pairs(p):
    lo = lax.bitcast_convert_type(p << 16, F32)
    hi = lax.bitcast_convert_type(p & jnp.uint32(0xFFFF0000), F32)
    return jnp.concatenate([lo, hi], axis=1)


def _rwkv_kernel(r_ref, k_ref, v_ref, lora_ref, mu_main_ref, mu_lora_ref, w0_ref, wup_ref, a0_ref, aup_ref,
                 gup_ref, kk_ref, ka_ref, rk_ref, gng_ref, gnb_ref, y_ref,
                 state_ref, prev_main_ref, prev_lora_ref, *, n_heads, d_decay, d_aaa, chunk):
    nb, Cb, d_r = r_ref.shape
    C = chunk
    CS = Cb // C
    N = RWKV_HEAD
    R = nb * Cb

    @pl.when(pl.program_id(0) == 0)
    def _():
        state_ref[...] = jnp.zeros_like(state_ref)
        prev_main_ref[...] = jnp.zeros_like(prev_main_ref)
        prev_lora_ref[...] = jnp.zeros_like(prev_lora_ref)

    row = lax.broadcasted_iota(jnp.int32, (R, 1), 0)

    def shifted(x_ref, prev_ref, j, mu):
        x = x_ref[...].reshape(R, x_ref.shape[2])
        xp = pltpu.roll(x, 1, 0)
        for b in range(nb):
            xp = jnp.where(row == b * Cb, prev_ref[b, j:j + 1, :], xp)
            prev_ref[b, j:j + 1, :] = x[(b + 1) * Cb - 1:(b + 1) * Cb, :]
        return x + (xp - x) * mu

    r = shifted(r_ref, prev_main_ref, 0, mu_main_ref[0:1, :])
    k = shifted(k_ref, prev_main_ref, 1, mu_main_ref[1:2, :])
    v = shifted(v_ref, prev_main_ref, 2, mu_main_ref[2:3, :])
    lo = shifted(lora_ref, prev_lora_ref, 0, mu_lora_ref[...])

    wd = lo[:, :d_decay]
    ad = lo[:, d_decay:d_decay + d_aaa]
    gd = lo[:, d_decay + d_aaa:]

    z = w0_ref[...] + _dot(jnp.tanh(wd), wup_ref[...])
    w = -(jnp.maximum(-z, 0.0) + jnp.log(1.0 + jnp.exp(-jnp.abs(z)))) - 0.5
    logd = -jnp.exp(w)
    a = jax.nn.sigmoid(a0_ref[...] + _dot(ad, aup_ref[...]))
    g = _dot(jax.nn.sigmoid(gd), gup_ref[...])

    li = lax.broadcasted_iota(jnp.int32, (MXU_DIM, MXU_DIM), 0) // N
    lj = lax.broadcasted_iota(jnp.int32, (MXU_DIM, MXU_DIM), 1) // N
    head_ones = (li == lj).astype(BF16)

    def head_sum(x):
        return jnp.concatenate(
            [_dot_split(x[:, c * MXU_DIM:(c + 1) * MXU_DIM], head_ones) for c in range(d_r // MXU_DIM)], axis=1)

    kkr = k * kk_ref[...]
    kk = kkr / jnp.maximum(jnp.sqrt(head_sum(kkr * kkr)), 1e-12)
    k2 = k * (1.0 + (a - 1.0) * ka_ref[...])
    a_s = -kk
    b_s = kk * a

    ri = lax.broadcasted_iota(jnp.int32, (R, R), 0)
    rj = lax.broadcasted_iota(jnp.int32, (R, R), 1)
    tri = ((rj <= ri) & ((ri // C) == (rj // C))).astype(BF16)
    hi = logd.astype(BF16)
    rem = logd - hi.astype(F32)
    mid = rem.astype(BF16)
    low = (rem - mid.astype(F32)).astype(BF16)
    cum = (jnp.dot(tri, hi, preferred_element_type=F32) + jnp.dot(tri, mid, preferred_element_type=F32)
           + jnp.dot(tri, low, preferred_element_type=F32))
    p_incl = jnp.exp(cum)
    p_excl = jnp.exp(cum - logd)
    p_inv = jnp.exp(-cum)
    p_last = [p_incl[(q + 1) * C - 1:(q + 1) * C, :] for q in range(nb * CS)]
    p_last_rows = jnp.concatenate([jnp.broadcast_to(p, (C, d_r)) for p in p_last], axis=0)

    at = (a_s * p_excl).astype(BF16)
    rt = (r * p_incl).astype(BF16)
    bt = b_s * p_inv
    kt = k2 * p_inv
    bh = (bt * p_last_rows).astype(BF16)
    kh = (kt * p_last_rows).astype(BF16)
    bt = bt.astype(BF16)
    kt = kt.astype(BF16)
    vb = v.astype(BF16)

    HG = LANES // N
    W = HG * N
    GC = HG * C
    wi = lax.broadcasted_iota(jnp.int32, (GC, W), 0)
    wj = lax.broadcasted_iota(jnp.int32, (GC, W), 1)
    bd_on = ((wi // C) == (wj // N)).astype(BF16) > 0

    def bdiag(x):
        xb = x.astype(BF16)
        return jnp.where(bd_on, jnp.concatenate([xb] * HG, axis=0), jnp.zeros((), BF16))

    ti = lax.broadcasted_iota(jnp.int32, (2 * C, GC), 0)
    tj = lax.broadcasted_iota(jnp.int32, (2 * C, GC), 1) % C
    mask2 = ((ti < C) & (tj < ti)) | ((ti >= C) & (tj <= ti - C))
    li = lax.broadcasted_iota(jnp.int32, (C, GC), 0)
    lj = lax.broadcasted_iota(jnp.int32, (C, GC), 1) % C
    eye = (li == lj).astype(F32)
    level_masks = []
    s = 1
    while s < C:
        level_masks.append(((li // (2 * s)) == (lj // (2 * s))) & (((li // s) % 2) == 1) & (((lj // s) % 2) == 0))
        s *= 2
    colblk = lax.broadcasted_iota(jnp.int32, (N, W), 1) // N

    n_g = n_heads // HG
    units = [(q, g) for q in range(nb * CS) for g in range(n_g)]
    idx = range(len(units))
    rs = [slice(q * C, (q + 1) * C) for q, _ in units]
    ls = [slice(g * W, (g + 1) * W) for _, g in units]
    g_b = [jnp.where(mask2, _dot(jnp.concatenate([at[rs[i], ls[i]], rt[rs[i], ls[i]]], axis=0),
                                 bdiag(bt[rs[i], ls[i]]), _NT), 0.0) for i in idx]
    g_k = [jnp.where(mask2, _dot(jnp.concatenate([at[rs[i], ls[i]], rt[rs[i], ls[i]]], axis=0),
                                 bdiag(kt[rs[i], ls[i]]), _NT), 0.0) for i in idx]
    gv = [_dot(g_k[i], bdiag(vb[rs[i], ls[i]])) for i in idx]
    tinv = [eye + jnp.where(level_masks[0], g_b[i][:C], 0.0) for i in idx]
    for lm in level_masks[1:]:
        xs = [_dot(jnp.where(lm, g_b[i][:C], 0.0), bdiag(tinv[i])) for i in idx]
        tinv = [tinv[i] + _dot(tinv[i], bdiag(xs[i])) for i in idx]
    ta = [_dot(tinv[i], bdiag(at[rs[i], ls[i]])) for i in idx]
    tv = [_dot(tinv[i], bdiag(gv[i][:C])) for i in idx]
    ys = [None] * len(units)
    for b in range(nb):
        state = [state_ref[b * n_g + gi] for gi in range(n_g)]
        for cc in range(CS):
            ids = [(b * CS + cc) * n_g + gi for gi in range(n_g)]
            tr = [_dot(jnp.concatenate([ta[i].astype(BF16), rt[rs[i], ls[i]]], axis=0), bdiag(state[gi]), _NT)
                  for gi, i in enumerate(ids)]
            u = [tr[gi][:C] + tv[i] for gi, i in enumerate(ids)]
            for gi, i in enumerate(ids):
                ys[i] = tr[gi][C:] + gv[i][C:] + _dot(g_b[i][C:], bdiag(u[gi]))
                uv = jnp.concatenate([u[gi].astype(BF16), vb[rs[i], ls[i]]], axis=0)
                bk = jnp.concatenate([bh[rs[i], ls[i]], kh[rs[i], ls[i]]], axis=0)
                full = _dot(uv, bk, _TN)
                upd = state[gi] * p_last[units[i][0]][:, ls[i]]
                for h in range(HG):
                    upd = upd + jnp.where(colblk == h, full[h * N:(h + 1) * N, :], 0.0)
                state[gi] = upd
        for gi in range(n_g):
            state_ref[b * n_g + gi] = state[gi]
    y = jnp.concatenate([jnp.concatenate(ys[q * n_g:(q + 1) * n_g], axis=1) for q in range(nb * CS)],
                        axis=0)

    inv_n = 1.0 / N
    mu_y = head_sum(y) * inv_n
    yc = y - mu_y
    var_y = head_sum(yc * yc) * inv_n
    yn = yc * lax.rsqrt(var_y + GN_EPS) * gng_ref[...] + gnb_ref[...]
    bonus = head_sum(r * k2 * rk_ref[...]) * v
    y_ref[...] = ((yn + bonus) * g).reshape(nb, Cb, d_r).astype(y_ref.dtype)


def rwkv_mixer(proj_main, proj_lora, col0, shift_mu, w0, w_up, a0, a_up, g_up, k_k, k_a, r_k, gn_g, gn_b,
               *, batch, chunk=64, chunks_per_step=2):
    nt = proj_main.shape[0]
    seq = nt // batch
    d_r = w0.shape[-1]
    n_heads = d_r // RWKV_HEAD
    d_lora = proj_lora.shape[1]
    d_decay, d_aaa = w_up.shape[0], a_up.shape[0]
    rows = chunk * chunks_per_step
    n_steps = seq // rows
    assert seq % rows == 0 and col0 % d_r == 0
    cb = col0 // d_r
    row2 = lambda t: t.reshape(1, -1).astype(F32)
    mu_main = shift_mu[:3 * d_r].reshape(3, d_r)
    mu_lora = shift_mu[3 * d_r:].reshape(1, d_lora)
    pm3 = proj_main.reshape(batch, seq, proj_main.shape[1])
    pl3 = proj_lora.reshape(batch, seq, d_lora)

    def main_spec(j):
        return pl.BlockSpec((batch, rows, d_r), lambda c, j=j: (0, c, cb + j))

    full = lambda shape: pl.BlockSpec(shape, lambda c: (0,) * len(shape))
    kern = functools.partial(_rwkv_kernel, n_heads=n_heads, d_decay=d_decay, d_aaa=d_aaa, chunk=chunk)
    y = pl.pallas_call(
        kern,
        grid=(n_steps,),
        in_specs=[main_spec(0), main_spec(1), main_spec(2),
                  pl.BlockSpec((batch, rows, d_lora), lambda c: (0, c, 0)),
                  full((3, d_r)), full((1, d_lora)), full((1, d_r)), full((d_decay, d_r)), full((1, d_r)),
                  full((d_aaa, d_r)), full((g_up.shape[0], d_r)), full((1, d_r)), full((1, d_r)), full((1, d_r)),
                  full((1, d_r)), full((1, d_r))],
        out_specs=pl.BlockSpec((batch, rows, d_r), lambda c: (0, c, 0)),
        out_shape=jax.ShapeDtypeStruct((batch, seq, d_r), BF16),
        scratch_shapes=[pltpu.VMEM((batch * n_heads * RWKV_HEAD // LANES, RWKV_HEAD, LANES), F32),
                        pltpu.VMEM((batch, 3, d_r), F32),
                        pltpu.VMEM((batch, 1, d_lora), F32)],
        compiler_params=pltpu.CompilerParams(dimension_semantics=("arbitrary",), vmem_limit_bytes=VMEM_LIMIT),
        name="rwkv_mixer",
    )(pm3, pm3, pm3, pl3, mu_main, mu_lora, row2(w0), w_up.astype(BF16), row2(a0),
      a_up.astype(BF16), g_up.astype(BF16), row2(k_k), row2(k_a), row2(r_k), row2(gn_g), row2(gn_b))
    return y.reshape(nt, d_r)


def _rms(x, g):
    return x * lax.rsqrt(jnp.mean(x * x, axis=-1, keepdims=True) + RMS_EPS) * g


def _inproj_kernel(x_ref, g_ref, wt_ref, wlt_ref, o_ref, ol_ref, u_ref):
    @pl.when(pl.program_id(1) == 0)
    def _():
        u = _rms(x_ref[...], g_ref[...]).astype(BF16)
        u_ref[...] = u
        ol_ref[...] = lax.dot_general(u, wlt_ref[...], _NT, preferred_element_type=F32)

    o_ref[...] = lax.dot_general(u_ref[...], wt_ref[...].astype(BF16), _NT, preferred_element_type=F32)


def in_proj(x2, norm_g, w_in, d_main, *, tm=1024, tn=1024):
    nt, d = x2.shape
    d_lora = w_in.shape[1] - d_main
    assert nt % tm == 0 and d_main % tn == 0
    w_t = w_in.T
    w_lora_t = w_t[d_main:].astype(BF16)
    return pl.pallas_call(
        _inproj_kernel,
        grid=(nt // tm, d_main // tn),
        in_specs=[pl.BlockSpec((tm, d), lambda i, j: (i, 0)),
                  pl.BlockSpec((1, d), lambda i, j: (0, 0)),
                  pl.BlockSpec((tn, d), lambda i, j: (j, 0)),
                  pl.BlockSpec((d_lora, d), lambda i, j: (0, 0))],
        out_specs=[pl.BlockSpec((tm, tn), lambda i, j: (i, j)),
                   pl.BlockSpec((tm, d_lora), lambda i, j: (i, 0))],
        out_shape=[jax.ShapeDtypeStruct((nt, d_main), F32), jax.ShapeDtypeStruct((nt, d_lora), F32)],
        scratch_shapes=[pltpu.VMEM((tm, d), BF16)],
        compiler_params=pltpu.CompilerParams(dimension_semantics=("arbitrary", "arbitrary"),
                                             vmem_limit_bytes=VMEM_LIMIT),
        name="in_proj",
    )(x2, norm_g.reshape(1, d).astype(F32), w_t, w_lora_t)


CONV_HALO = 32
CONV_ROWS = 32


def _conv_kernel(val_ref, gate_ref, dw_ref, b_ref, g_ref, beta_ref, o_ref, ubuf_ref):
    tt = val_ref.shape[0]
    d_c = val_ref.shape[1]
    t_idx = pl.program_id(1)

    @pl.when(t_idx == 0)
    def _():
        ubuf_ref[0, 0:CONV_HALO, :] = jnp.zeros((CONV_HALO, d_c), F32)

    @pl.when(t_idx > 0)
    def _():
        ubuf_ref[0, 0:CONV_HALO, :] = ubuf_ref[0, tt:tt + CONV_HALO, :]

    ubuf_ref[0, CONV_HALO:CONV_HALO + tt, :] = val_ref[...] * jax.nn.sigmoid(gate_ref[...])
    n_sh = tt + CONV_HALO - SUBLANES
    for s in range(1, SUBLANES):
        for r in range(0, n_sh, CONV_ROWS):
            n = min(CONV_ROWS, n_sh - r)
            ubuf_ref[s, r:r + n, :] = ubuf_ref[0, r + s:r + s + n, :]
    dw = dw_ref[...]
    lead = CONV_HALO - (CONV_WIDTH - 1)
    for c in range(tt // CONV_ROWS):
        r0 = c * CONV_ROWS
        acc = jnp.zeros((CONV_ROWS, d_c), F32)
        for j in range(CONV_WIDTH):
            q, s = divmod(lead + j, SUBLANES)
            r = r0 + q * SUBLANES
            acc = acc + dw[j:j + 1, :] * ubuf_ref[s, r:r + CONV_ROWS, :]
        acc = acc + b_ref[...]
        mu = jnp.mean(acc, axis=-1, keepdims=True)
        cen = acc - mu
        var = jnp.mean(cen * cen, axis=-1, keepdims=True)
        yv = cen * lax.rsqrt(var + LN_EPS) * g_ref[...] + beta_ref[...]
        o_ref[r0:r0 + CONV_ROWS, :] = (yv * jax.nn.sigmoid(yv)).astype(o_ref.dtype)


def conv_mixer(proj_main, dw, bias, ln_g, ln_b, *, batch, tt=256):
    nt = proj_main.shape[0]
    seq = nt // batch
    d_c = dw.shape[1]
    n_t = seq // tt
    assert seq % tt == 0 and tt % CONV_ROWS == 0
    row = lambda t: t.reshape(1, d_c).astype(F32)
    full = lambda shape: pl.BlockSpec(shape, lambda b, t: (0, 0))
    return pl.pallas_call(
        _conv_kernel,
        grid=(batch, n_t),
        in_specs=[pl.BlockSpec((tt, d_c), lambda b, t: (b * n_t + t, 0)),
                  pl.BlockSpec((tt, d_c), lambda b, t: (b * n_t + t, 1)),
                  full((CONV_WIDTH, d_c)), full((1, d_c)), full((1, d_c)), full((1, d_c))],
        out_specs=pl.BlockSpec((tt, d_c), lambda b, t: (b * n_t + t, 0)),
        out_shape=jax.ShapeDtypeStruct((nt, d_c), BF16),
        scratch_shapes=[pltpu.VMEM((SUBLANES, tt + CONV_HALO, d_c), F32)],
        compiler_params=pltpu.CompilerParams(dimension_semantics=("arbitrary", "arbitrary"),
                                             vmem_limit_bytes=VMEM_LIMIT),
        name="conv_mixer",
    )(proj_main, proj_main, dw.astype(F32), row(bias), row(ln_g), row(ln_b))


ROUTER_SUB = 512


def _out_router_kernel(x_ref, yc_ref, yr_ref, wc_ref, wr_ref, nf_ref, rw_hi_ref, rw_lo_ref, rb_ref,
                       h_ref, v_ref, eid_ref, gate_ref, count_ref, cnt_ref):
    tm = x_ref.shape[0]
    subs = [slice(r, r + ROUTER_SUB) for r in range(0, tm, ROUTER_SUB)]
    hs = [x_ref[sl, :] + jnp.dot(yc_ref[sl, :], wc_ref[...], preferred_element_type=F32)
          + jnp.dot(yr_ref[sl, :], wr_ref[...], preferred_element_type=F32) for sl in subs]

    @pl.when(pl.program_id(0) == 0)
    def _():
        cnt_ref[...] = jnp.zeros_like(cnt_ref)

    lane = lax.broadcasted_iota(jnp.int32, (ROUTER_SUB, LANES), 1)
    neg = jnp.float32(-jnp.inf)
    big = jnp.int32(LANES)
    earlier = (lax.broadcasted_iota(jnp.int32, (ROUTER_SUB, ROUTER_SUB), 1)
               < lax.broadcasted_iota(jnp.int32, (ROUTER_SUB, ROUTER_SUB), 0)).astype(BF16)

    def first_max(vals):
        m = jnp.max(vals, axis=-1, keepdims=True)
        return m, jnp.min(jnp.where(vals == m, lane, big), axis=-1, keepdims=True)

    for sl, h in zip(subs, hs):
        h_ref[sl, :] = h
        v = _rms(h, nf_ref[...])
        v_ref[sl, :] = _pack_bf16_pairs(v)
        v_hi = v.astype(BF16)
        v_lo = (v - v_hi.astype(F32)).astype(BF16)
        logits = (jnp.dot(v_hi, rw_hi_ref[...], preferred_element_type=F32)
                  + jnp.dot(v_lo, rw_hi_ref[...], preferred_element_type=F32)
                  + jnp.dot(v_hi, rw_lo_ref[...], preferred_element_type=F32)) + rb_ref[...]
        gl = jnp.where(lane < N_GROUPS, logits, neg)
        gmax, grp = first_max(gl)
        p_grp = 1.0 / jnp.sum(jnp.exp(gl - gmax), axis=-1, keepdims=True)
        lo = N_GROUPS + grp * EXPERTS_PER_GROUP
        el = jnp.where((lane >= lo) & (lane < lo + EXPERTS_PER_GROUP), logits, neg)
        m1, i1 = first_max(el)
        m2, i2 = first_max(jnp.where(lane == i1, neg, el))
        e2 = jnp.exp(m2 - m1)
        g1 = p_grp / (1.0 + e2)
        g2 = p_grp * e2 / (1.0 + e2)
        gate_ref[sl, :] = jnp.where(lane == 0, g1, jnp.where(lane == 1, g2, 0.0))

        x1 = i1 - N_GROUPS
        x2 = i2 - N_GROUPS
        oh1 = (lane == x1).astype(F32)
        oh2 = (lane == x2).astype(F32)
        before1 = jnp.dot(earlier, oh1.astype(BF16), preferred_element_type=F32)
        before2 = jnp.dot(earlier, oh2.astype(BF16), preferred_element_type=F32)
        carry = cnt_ref[...]
        n1 = jnp.sum(oh1, axis=0, keepdims=True)
        n2 = jnp.sum(oh2, axis=0, keepdims=True)
        rank1 = jnp.sum(oh1 * (before1 + carry), axis=-1, keepdims=True).astype(jnp.int32)
        rank2 = jnp.sum(oh2 * (before2 + carry + n1), axis=-1, keepdims=True).astype(jnp.int32)
        cnt_ref[...] = carry + n1 + n2
        routing = jnp.where(lane == 0, x1, jnp.where(lane == 1, x2, jnp.where(
            lane == 2, rank1, jnp.where(lane == 3, rank2, 0))))
        eid_ref[:, sl] = routing.T[:SUBLANES, :]
    count_ref[...] = cnt_ref[...].astype(jnp.int32)


def out_router(x2, y_conv, y_rwkv, w_out, norm_ffn, rg_w, rg_b, re_w, re_b, *, tm=512):
    nt, d = x2.shape
    d_c, d_r = y_conv.shape[1], y_rwkv.shape[1]
    assert d_c == d_r and nt % tm == 0 and tm % ROUTER_SUB == 0
    n_r = N_GROUPS + N_EXPERTS
    rw = jnp.zeros((d, LANES), F32).at[:, :n_r].set(jnp.concatenate([rg_w, re_w], axis=1).astype(F32))
    rw_hi = rw.astype(BF16)
    rw_lo = (rw - rw_hi.astype(F32)).astype(BF16)
    rb = jnp.zeros((1, LANES), F32).at[0, :n_r].set(jnp.concatenate([rg_b, re_b]).astype(F32))
    full = lambda shape: pl.BlockSpec(shape, lambda i: (0, 0))
    once = lambda shape, r: pl.BlockSpec(shape, lambda i, r=r: (r, 0), pipeline_mode=pl.Buffered(1))
    rows = lambda w: pl.BlockSpec((tm, w), lambda i: (i, 0))
    return pl.pallas_call(
        _out_router_kernel,
        grid=(nt // tm,),
        in_specs=[rows(d), rows(d_c), rows(d_r), once((d_c, d), 0), once((d_r, d), 1),
                  full((1, d)), full((d, LANES)), full((d, LANES)), full((1, LANES))],
        out_specs=[rows(d), rows(d // 2), pl.BlockSpec((SUBLANES, tm), lambda i: (0, i)), rows(LANES),
                   full((1, LANES))],
        out_shape=[jax.ShapeDtypeStruct((nt, d), F32), jax.ShapeDtypeStruct((nt, d // 2), jnp.uint32),
                   jax.ShapeDtypeStruct((SUBLANES, nt), jnp.int32), jax.ShapeDtypeStruct((nt, LANES), F32),
                   jax.ShapeDtypeStruct((1, LANES), jnp.int32)],
        scratch_shapes=[pltpu.VMEM((1, LANES), F32)],
        compiler_params=pltpu.CompilerParams(dimension_semantics=("arbitrary",), vmem_limit_bytes=VMEM_LIMIT),
        name="out_router",
    )(x2, y_conv, y_rwkv, w_out, w_out, norm_ffn.reshape(1, d).astype(F32), rw_hi, rw_lo, rb)


MOE_ROWS = 256


def slot_plan(routing, counts, n_rows_pad):
    expert_id = routing[:TOP_K]
    rank = routing[TOP_K:2 * TOP_K]
    padded = (counts + MOE_ROWS - 1) // MOE_ROWS * MOE_ROWS
    pend = jnp.cumsum(padded)
    pstarts = pend - padded
    ids = jnp.arange(N_EXPERTS, dtype=jnp.int32)
    start_of = jnp.sum(jnp.where(expert_id[..., None] == ids, pstarts, 0), axis=-1)
    dest = (start_of + rank).astype(jnp.int32).T
    n_blocks = n_rows_pad // MOE_ROWS
    block_row0 = jnp.arange(n_blocks, dtype=jnp.int32) * MOE_ROWS
    block_e = jnp.minimum(jnp.sum((pend[None, :] <= block_row0[:, None]).astype(jnp.int32), axis=1),
                          N_EXPERTS - 1).astype(jnp.int32)
    n_used = (pend[-1:] // MOE_ROWS).astype(jnp.int32)
    owner = jnp.where(counts > 0, ids, N_EXPERTS)
    nxt = lax.cummin(jnp.concatenate([owner[1:], jnp.full((1,), N_EXPERTS, jnp.int32)]), reverse=True)
    next_used = jnp.where(nxt < N_EXPERTS, nxt, -1).astype(jnp.int32)
    used_ord = (jnp.cumsum((counts > 0).astype(jnp.int32)) - 1).astype(jnp.int32)
    pad0 = pstarts + counts
    n_single = jnp.minimum((-pad0) % SUBLANES, pend - pad0)
    tile0 = pad0 + n_single
    n_tile = (pend - tile0) // SUBLANES
    tail = jnp.stack([pend[-1] // MOE_ROWS, n_blocks - pend[-1] // MOE_ROWS, jnp.sum(n_single), jnp.sum(n_tile)])
    fill_plan = jnp.concatenate([pad0, n_single, tile0, n_tile, tail]).astype(jnp.int32)
    return dest, block_e, n_used, next_used, used_ord, fill_plan


def _dispatch_kernel(fill_ref, dest_ref, v_ref, xs_ref, zero_ref, sem, fill_sem):
    tm = v_ref.shape[0]

    def fills():
        e0, e1, e2, e3, e4 = (k * N_EXPERTS for k in range(5))
        single = lambda e, i: pltpu.make_async_copy(
            zero_ref.at[pl.ds(0, 1), :], xs_ref.at[pl.ds(fill_ref[e0 + e] + i, 1), :], fill_sem)
        tile = lambda e, i: pltpu.make_async_copy(
            zero_ref.at[pl.ds(0, SUBLANES), :],
            xs_ref.at[pl.ds(pl.multiple_of(fill_ref[e2 + e] + i * SUBLANES, SUBLANES), SUBLANES), :], fill_sem)
        block = lambda i: pltpu.make_async_copy(
            zero_ref, xs_ref.at[pl.ds(pl.multiple_of((fill_ref[e4] + i) * MOE_ROWS, MOE_ROWS), MOE_ROWS), :],
            fill_sem)
        return single, tile, block, e1, e3, e4

    @pl.when(pl.program_id(0) == 0)
    def _():
        zero_ref[...] = jnp.zeros_like(zero_ref)
        single, tile, block, e1, e3, e4 = fills()
        for e in range(N_EXPERTS):
            lax.fori_loop(0, fill_ref[e1 + e], lambda i, c, e=e: (single(e, i).start(), c)[1], 0)
            lax.fori_loop(0, fill_ref[e3 + e], lambda i, c, e=e: (tile(e, i).start(), c)[1], 0)
        lax.fori_loop(0, fill_ref[e4 + 1], lambda i, c: (block(i).start(), c)[1], 0)

    def copy(i, k):
        d = dest_ref[0, 0, i * TOP_K + k]
        return pltpu.make_async_copy(v_ref.at[pl.ds(i, 1), :], xs_ref.at[pl.ds(d, 1), :], sem)

    def start(i, carry):
        for k in range(TOP_K):
            copy(i, k).start(priority=k)
        return carry

    lax.fori_loop(0, tm, start, 0, unroll=8)
    for k in range(TOP_K):
        pltpu.make_async_copy(v_ref, xs_ref.at[pl.ds(0, tm), :], sem).wait()

    @pl.when(pl.program_id(0) == 0)
    def _():
        single, tile, block, e1, e3, e4 = fills()
        lax.fori_loop(0, fill_ref[e4 + 2], lambda i, c: (single(0, 0).wait(), c)[1], 0)
        lax.fori_loop(0, fill_ref[e4 + 3], lambda i, c: (tile(0, 0).wait(), c)[1], 0)
        lax.fori_loop(0, fill_ref[e4 + 1], lambda i, c: (block(0).wait(), c)[1], 0)


def dispatch(v, dest, fill_plan, n_rows_pad, *, tm=1024):
    nt, d = v.shape
    assert nt % tm == 0
    dest3 = dest.reshape(nt // tm, 1, tm * TOP_K)
    grid_spec = pltpu.PrefetchScalarGridSpec(
        num_scalar_prefetch=1,
        grid=(nt // tm,),
        in_specs=[pl.BlockSpec((1, 1, tm * TOP_K), lambda i, fp: (i, 0, 0), memory_space=pltpu.SMEM),
                  pl.BlockSpec((tm, d), lambda i, fp: (i, 0))],
        out_specs=pl.BlockSpec(memory_space=pl.ANY),
        scratch_shapes=[pltpu.VMEM((MOE_ROWS, d), v.dtype), pltpu.SemaphoreType.DMA(()),
                        pltpu.SemaphoreType.DMA(())],
    )
    return pl.pallas_call(
        _dispatch_kernel,
        grid_spec=grid_spec,
        out_shape=jax.ShapeDtypeStruct((n_rows_pad, d), v.dtype),
        compiler_params=pltpu.CompilerParams(dimension_semantics=("arbitrary",), vmem_limit_bytes=VMEM_LIMIT),
        name="moe_dispatch",
    )(fill_plan, dest3, v)


def _experts_kernel(be_ref, nu_ref, nxt_ref, ord_ref, xs_ref, wg_hbm, wu_hbm, wd_hbm, ys_ref,
                    wg_f, wu_f, wd_f, wg_s, wu_s, wd_s, wsem):
    j = pl.program_id(0)
    e = be_ref[j]

    def fetch(ex, slot):
        return [pltpu.make_async_copy(wg_hbm.at[ex], wg_f.at[slot], wsem.at[slot]),
                pltpu.make_async_copy(wu_hbm.at[ex], wu_f.at[slot], wsem.at[slot]),
                pltpu.make_async_copy(wd_hbm.at[ex], wd_f.at[slot], wsem.at[slot])]

    @pl.when(j == 0)
    def _():
        for c in fetch(e, lax.rem(ord_ref[e], 2)):
            c.start()

    first = (j == 0) | (e != be_ref[jnp.maximum(j - 1, 0)])

    @pl.when(first & (j < nu_ref[0]))
    def _():
        slot = lax.rem(ord_ref[e], 2)
        for c in fetch(e, slot):
            c.wait()
        nxt = nxt_ref[e]

        @pl.when(nxt >= 0)
        def _():
            for c in fetch(nxt, 1 - slot):
                c.start()

        wg_s[...] = wg_f[slot].astype(BF16)
        wu_s[...] = wu_f[slot].astype(BF16)
        wd_s[...] = wd_f[slot].astype(BF16)

    @pl.when(j < nu_ref[0])
    def _():
        xb = _unpack_bf16_pairs(xs_ref[...]).astype(BF16)
        gt = jnp.dot(xb, wg_s[...], preferred_element_type=F32)
        up = jnp.dot(xb, wu_s[...], preferred_element_type=F32)
        hid = (gt * jax.nn.sigmoid(gt) * up).astype(BF16)
        ys_ref[...] = _pack_bf16_pairs(jnp.dot(hid, wd_s[...], preferred_element_type=F32))

    @pl.when(j >= nu_ref[0])
    def _():
        ys_ref[...] = jnp.zeros_like(ys_ref)


def experts(xs, block_e, n_used, next_used, used_ord, w_gate, w_up, w_down, n_rows_pad):
    dh = xs.shape[1]
    d, d_e = w_gate.shape[1], w_gate.shape[2]
    assert d == 2 * dh
    n_blocks = n_rows_pad // MOE_ROWS
    hbm = pl.BlockSpec(memory_space=pl.ANY)
    grid_spec = pltpu.PrefetchScalarGridSpec(
        num_scalar_prefetch=4,
        grid=(n_blocks,),
        in_specs=[pl.BlockSpec((MOE_ROWS, dh), lambda j, be, nu, nx, od: (jnp.minimum(j, nu[0] - 1), 0)),
                  hbm, hbm, hbm],
        out_specs=pl.BlockSpec((MOE_ROWS, dh), lambda j, be, nu, nx, od: (j, 0)),
        scratch_shapes=[pltpu.VMEM((2, d, d_e), F32), pltpu.VMEM((2, d, d_e), F32), pltpu.VMEM((2, d_e, d), F32),
                        pltpu.VMEM((d, d_e), BF16), pltpu.VMEM((d, d_e), BF16), pltpu.VMEM((d_e, d), BF16),
                        pltpu.SemaphoreType.DMA((2,))],
    )
    return pl.pallas_call(
        _experts_kernel,
        grid_spec=grid_spec,
        out_shape=jax.ShapeDtypeStruct((n_rows_pad, dh), jnp.uint32),
        compiler_params=pltpu.CompilerParams(dimension_semantics=("arbitrary",), vmem_limit_bytes=VMEM_LIMIT),
        name="moe_experts",
    )(block_e, n_used, next_used, used_ord, xs, w_gate, w_up, w_down)


def _combine_kernel(dest_ref, dest_next_ref, h_ref, gate_ref, nfin_ref, ys_ref, o_ref, ybuf, sem):
    step = pl.program_id(0)
    tm = h_ref.shape[0]
    slot = lax.rem(step, 2)

    def copy(d_ref, s, i, k):
        d = d_ref[0, 0, i * TOP_K + k]
        return pltpu.make_async_copy(ys_ref.at[pl.ds(d, 1), :], ybuf.at[s, k, pl.ds(i, 1), :], sem.at[s])

    def start_all(d_ref, s):
        def body(i, carry):
            for k in range(TOP_K):
                copy(d_ref, s, i, k).start(priority=k)
            return carry
        lax.fori_loop(0, tm, body, 0, unroll=8)

    @pl.when(step == 0)
    def _():
        start_all(dest_ref, 0)

    @pl.when(step + 1 < pl.num_programs(0))
    def _():
        start_all(dest_next_ref, 1 - slot)

    for k in range(TOP_K):
        pltpu.make_async_copy(ys_ref.at[pl.ds(0, tm), :], ybuf.at[slot, k], sem.at[slot]).wait()
    gates = gate_ref[...]
    moe = (gates[:, 0:1] * _unpack_bf16_pairs(ybuf[slot, 0])
           + gates[:, 1:2] * _unpack_bf16_pairs(ybuf[slot, 1]))
    o_ref[...] = _rms(h_ref[...] + moe, nfin_ref[...])


def combine(h, gates, dest, ys, norm_final, *, tm=512):
    nt, d = h.shape
    assert nt % tm == 0 and ys.shape[1] * 2 == d
    n_tiles = nt // tm
    dest3 = dest.reshape(n_tiles, 1, tm * TOP_K)
    return pl.pallas_call(
        _combine_kernel,
        grid=(n_tiles,),
        in_specs=[pl.BlockSpec((1, 1, tm * TOP_K), lambda i: (i, 0, 0), memory_space=pltpu.SMEM),
                  pl.BlockSpec((1, 1, tm * TOP_K), lambda i: (jnp.minimum(i + 1, n_tiles - 1), 0, 0),
                               memory_space=pltpu.SMEM),
                  pl.BlockSpec((tm, d), lambda i: (i, 0)),
                  pl.BlockSpec((tm, LANES), lambda i: (i, 0)),
                  pl.BlockSpec((1, d), lambda i: (0, 0)),
                  pl.BlockSpec(memory_space=pl.ANY)],
        out_specs=pl.BlockSpec((tm, d), lambda i: (i, 0)),
        out_shape=jax.ShapeDtypeStruct((nt, d), F32),
        scratch_shapes=[pltpu.VMEM((2, TOP_K, tm, d // 2), jnp.uint32), pltpu.SemaphoreType.DMA((2,))],
        compiler_params=pltpu.CompilerParams(dimension_semantics=("arbitrary",), vmem_limit_bytes=VMEM_LIMIT),
        name="moe_combine",
    )(dest3, dest3, h, gates, norm_final.reshape(1, d).astype(F32), ys)


def kernel(x, norm_mix, w_in, conv_dw, conv_b, conv_ln_g, conv_ln_b, shift_mu, w0, w_lora_up, a0, a_lora_up, g_lora_up, k_k, k_a, r_k, gn_g, gn_b, w_out, norm_ffn, router_group_w, router_group_b, router_expert_w, router_expert_b, expert_w_gate, expert_w_up, expert_w_down, norm_final):
    B, T, D = x.shape
    depth = w_in.shape[0]
    d_c = conv_dw.shape[2]
    d_r = w0.shape[1]
    d_main = 2 * d_c + 3 * d_r
    nt = B * T
    n_rows_pad = -(-(nt * TOP_K + N_EXPERTS * (MOE_ROWS - 1)) // MOE_ROWS) * MOE_ROWS
    h = x.reshape(nt, D)
    for l in range(depth):
        proj_main, proj_lora = in_proj(h, norm_mix[l], w_in[l], d_main)
        y_conv = conv_mixer(proj_main, conv_dw[l], conv_b[l], conv_ln_g[l], conv_ln_b[l], batch=B)
        y_rwkv = rwkv_mixer(proj_main, proj_lora, 2 * d_c, shift_mu[l], w0[l], w_lora_up[l], a0[l], a_lora_up[l],
                            g_lora_up[l], k_k[l], k_a[l], r_k[l].reshape(-1), gn_g[l], gn_b[l], batch=B)
        h, v, routing, gates, counts = out_router(h, y_conv, y_rwkv, w_out[l].astype(BF16), norm_ffn[l],
                                                  router_group_w[l], router_group_b[l], router_expert_w[l],
                                                  router_expert_b[l])
        dest, block_e, n_used, next_used, used_ord, fill_plan = slot_plan(routing, counts[0, :N_EXPERTS],
                                                                          n_rows_pad)
        xs = dispatch(v, dest, fill_plan, n_rows_pad)
        ys = experts(xs, block_e, n_used, next_used, used_ord, expert_w_gate[l], expert_w_up[l], expert_w_down[l],
                     n_rows_pad)
        assert depth == 1
        h = combine(h, gates, dest, ys, norm_final)
    return h.reshape(B, T, D)
```

```python
import functools

import jax
import jax.numpy as jnp
from jax import lax
from jax.experimental import pallas as pl
from jax.experimental.pallas import tpu as pltpu

F32 = jnp.float32
BF16 = jnp.bfloat16

RWKV_HEAD = 64
CONV_WIDTH = 31
N_GROUPS = 4
EXPERTS_PER_GROUP = 8
N_EXPERTS = N_GROUPS * EXPERTS_PER_GROUP
TOP_K = 2
RMS_EPS = 1e-6
LN_EPS = 1e-5
GN_EPS = 64e-5
LANES = 128
SUBLANES = 8
MXU_DIM = 256
VMEM_LIMIT = 56 * 1024 * 1024

_NT = (((1,), (1,)), ((), ()))
_TN = (((0,), (0,)), ((), ()))


def _dot(a, b, dims=None):
    a = a.astype(BF16)
    b = b.astype(BF16)
    if dims is None:
        return jnp.dot(a, b, preferred_element_type=F32)
    return lax.dot_general(a, b, dims, preferred_element_type=F32)


def _dot_split(x, m):
    hi = x.astype(BF16)
    lo = (x - hi.astype(F32)).astype(BF16)
    return (jnp.dot(hi, m, preferred_element_type=F32)
            + jnp.dot(lo, m, preferred_element_type=F32))


def _pack_bf16_pairs(x):
    w = x.shape[1] // 2
    lo = lax.bitcast_convert_type(x[:, :w].astype(BF16).astype(F32), jnp.uint32)
    hi = lax.bitcast_convert_type(x[:, w:].astype(BF16).astype(F32), jnp.uint32)
    return (lo >> 16) | (hi & jnp.uint32(0xFFFF0000))


def _unpack_bf16_pairs(p):
    lo = lax.bitcast_convert_type(p << 16, F32)
    hi = lax.bitcast_convert_type(p & jnp.uint32(0xFFFF0000), F32)
    return jnp.concatenate([lo, hi], axis=1)


def _rwkv_kernel(r_ref, k_ref, v_ref, lora_ref, mu_main_ref, mu_lora_ref, w0_ref, wup_ref, a0_ref, aup_ref,
                 gup_ref, kk_ref, ka_ref, rk_ref, gng_ref, gnb_ref, y_ref,
                 state_ref, prev_main_ref, prev_lora_ref, *, n_heads, d_decay, d_aaa, chunk):
    nb, Cb, d_r = r_ref.shape
    C = chunk
    CS = Cb // C
    N = RWKV_HEAD
    R = nb * Cb

    @pl.when(pl.program_id(0) == 0)
    def _():
        state_ref[...] = jnp.zeros_like(state_ref)
        prev_main_ref[...] = jnp.zeros_like(prev_main_ref)
        prev_lora_ref[...] = jnp.zeros_like(prev_lora_ref)

    row = lax.broadcasted_iota(jnp.int32, (R, 1), 0)

    def shifted(x_ref, prev_ref, j, mu):
        x = x_ref[...].reshape(R, x_ref.shape[2])
        xp = pltpu.roll(x, 1, 0)
        for b in range(nb):
            xp = jnp.where(row == b * Cb, prev_ref[b, j:j + 1, :], xp)
            prev_ref[b, j:j + 1, :] = x[(b + 1) * Cb - 1:(b + 1) * Cb, :]
        return x + (xp - x) * mu

    r = shifted(r_ref, prev_main_ref, 0, mu_main_ref[0:1, :])
    k = shifted(k_ref, prev_main_ref, 1, mu_main_ref[1:2, :])
    v = shifted(v_ref, prev_main_ref, 2, mu_main_ref[2:3, :])
    lo = shifted(lora_ref, prev_lora_ref, 0, mu_lora_ref[...])

    wd = lo[:, :d_decay]
    ad = lo[:, d_decay:d_decay + d_aaa]
    gd = lo[:, d_decay + d_aaa:]

    z = w0_ref[...] + _dot(jnp.tanh(wd), wup_ref[...])
    w = -(jnp.maximum(-z, 0.0) + jnp.log(1.0 + jnp.exp(-jnp.abs(z)))) - 0.5
    logd = -jnp.exp(w)
    a = jax.nn.sigmoid(a0_ref[...] + _dot(ad, aup_ref[...]))
    g = _dot(jax.nn.sigmoid(gd), gup_ref[...])

    li = lax.broadcasted_iota(jnp.int32, (MXU_DIM, MXU_DIM), 0) // N
    lj = lax.broadcasted_iota(jnp.int32, (MXU_DIM, MXU_DIM), 1) // N
    head_ones = (li == lj).astype(BF16)

    def head_sum(x):
        return jnp.concatenate(
            [_dot_split(x[:, c * MXU_DIM:(c + 1) * MXU_DIM], head_ones) for c in range(d_r // MXU_DIM)], axis=1)

    kkr = k * kk_ref[...]
    kk = kkr / jnp.maximum(jnp.sqrt(head_sum(kkr * kkr)), 1e-12)
    k2 = k * (1.0 + (a - 1.0) * ka_ref[...])
    a_s = -kk
    b_s = kk * a

    ri = lax.broadcasted_iota(jnp.int32, (R, R), 0)
    rj = lax.broadcasted_iota(jnp.int32, (R, R), 1)
    tri = ((rj <= ri) & ((ri // C) == (rj // C))).astype(BF16)
    hi = logd.astype(BF16)
    rem = logd - hi.astype(F32)
    mid = rem.astype(BF16)
    low = (rem - mid.astype(F32)).astype(BF16)
    cum = (jnp.dot(tri, hi, preferred_element_type=F32) + jnp.dot(tri, mid, preferred_element_type=F32)
           + jnp.dot(tri, low, preferred_element_type=F32))
    p_incl = jnp.exp(cum)
    p_excl = jnp.exp(cum - logd)
    p_inv = jnp.exp(-cum)
    p_last = [p_incl[(q + 1) * C - 1:(q + 1) * C, :] for q in range(nb * CS)]
    p_last_rows = jnp.concatenate([jnp.broadcast_to(p, (C, d_r)) for p in p_last], axis=0)

    at = (a_s * p_excl).astype(BF16)
    rt = (r * p_incl).astype(BF16)
    bt = b_s * p_inv
    kt = k2 * p_inv
    bh = (bt * p_last_rows).astype(BF16)
    kh = (kt * p_last_rows).astype(BF16)
    bt = bt.astype(BF16)
    kt = kt.astype(BF16)
    vb = v.astype(BF16)

    HG = LANES // N
    W = HG * N
    GC = HG * C
    wi = lax.broadcasted_iota(jnp.int32, (GC, W), 0)
    wj = lax.broadcasted_iota(jnp.int32, (GC, W), 1)
    bd_on = ((wi // C) == (wj // N)).astype(BF16) > 0

    def bdiag(x):
        xb = x.astype(BF16)
        return jnp.where(bd_on, jnp.concatenate([xb] * HG, axis=0), jnp.zeros((), BF16))

    ti = lax.broadcasted_iota(jnp.int32, (2 * C, GC), 0)
    tj = lax.broadcasted_iota(jnp.int32, (2 * C, GC), 1) % C
    mask2 = ((ti < C) & (tj < ti)) | ((ti >= C) & (tj <= ti - C))
    li = lax.broadcasted_iota(jnp.int32, (C, GC), 0)
    lj = lax.broadcasted_iota(jnp.int32, (C, GC), 1) % C
    eye = (li == lj).astype(F32)
    level_masks = []
    s = 1
    while s < C:
        level_masks.append(((li // (2 * s)) == (lj // (2 * s))) & (((li // s) % 2) == 1) & (((lj // s) % 2) == 0))
        s *= 2
    colblk = lax.broadcasted_iota(jnp.int32, (N, W), 1) // N

    n_g = n_heads // HG
    units = [(q, g) for q in range(nb * CS) for g in range(n_g)]
    idx = range(len(units))
    rs = [slice(q * C, (q + 1) * C) for q, _ in units]
    ls = [slice(g * W, (g + 1) * W) for _, g in units]
    g_b = [jnp.where(mask2, _dot(jnp.concatenate([at[rs[i], ls[i]], rt[rs[i], ls[i]]], axis=0),
                                 bdiag(bt[rs[i], ls[i]]), _NT), 0.0) for i in idx]
    g_k = [jnp.where(mask2, _dot(jnp.concatenate([at[rs[i], ls[i]], rt[rs[i], ls[i]]], axis=0),
                                 bdiag(kt[rs[i], ls[i]]), _NT), 0.0) for i in idx]
    gv = [_dot(g_k[i], bdiag(vb[rs[i], ls[i]])) for i in idx]
    tinv = [eye + jnp.where(level_masks[0], g_b[i][:C], 0.0) for i in idx]
    for lm in level_masks[1:]:
        xs = [_dot(jnp.where(lm, g_b[i][:C], 0.0), bdiag(tinv[i])) for i in idx]
        tinv = [tinv[i] + _dot(tinv[i], bdiag(xs[i])) for i in idx]
    ta = [_dot(tinv[i], bdiag(at[rs[i], ls[i]])) for i in idx]
    tv = [_dot(tinv[i], bdiag(gv[i][:C])) for i in idx]
    ys = [None] * len(units)
    for b in range(nb):
        state = [state_ref[b * n_g + gi] for gi in range(n_g)]
        for cc in range(CS):
            ids = [(b * CS + cc) * n_g + gi for gi in range(n_g)]
            tr = [_dot(jnp.concatenate([ta[i].astype(BF16), rt[rs[i], ls[i]]], axis=0), bdiag(state[gi]), _NT)
                  for gi, i in enumerate(ids)]
            u = [tr[gi][:C] + tv[i] for gi, i in enumerate(ids)]
            for gi, i in enumerate(ids):
                ys[i] = tr[gi][C:] + gv[i][C:] + _dot(g_b[i][C:], bdiag(u[gi]))
                uv = jnp.concatenate([u[gi].astype(BF16), vb[rs[i], ls[i]]], axis=0)
                bk = jnp.concatenate([bh[rs[i], ls[i]], kh[rs[i], ls[i]]], axis=0)
                full = _dot(uv, bk, _TN)
                upd = state[gi] * p_last[units[i][0]][:, ls[i]]
                for h in range(HG):
                    upd = upd + jnp.where(colblk == h, full[h * N:(h + 1) * N, :], 0.0)
                state[gi] = upd
        for gi in range(n_g):
            state_ref[b * n_g + gi] = state[gi]
    y = jnp.concatenate([jnp.concatenate(ys[q * n_g:(q + 1) * n_g], axis=1) for q in range(nb * CS)],
                        axis=0)

    inv_n = 1.0 / N
    mu_y = head_sum(y) * inv_n
    yc = y - mu_y
    var_y = head_sum(yc * yc) * inv_n
    yn = yc * lax.rsqrt(var_y + GN_EPS) * gng_ref[...] + gnb_ref[...]
    bonus = head_sum(r * k2 * rk_ref[...]) * v
    y_ref[...] = ((yn + bonus) * g).reshape(nb, Cb, d_r).astype(y_ref.dtype)


def rwkv_mixer(proj_main, proj_lora, col0, shift_mu, w0, w_up, a0, a_up, g_up, k_k, k_a, r_k, gn_g, gn_b,
               *, batch, chunk=64, chunks_per_step=2):
    nt = proj_main.shape[0]
    seq = nt // batch
    d_r = w0.shape[-1]
    n_heads = d_r // RWKV_HEAD
    d_lora = proj_lora.shape[1]
    d_decay, d_aaa = w_up.shape[0], a_up.shape[0]
    rows = chunk * chunks_per_step
    n_steps = seq // rows
    assert seq % rows == 0 and col0 % d_r == 0
    cb = col0 // d_r
    row2 = lambda t: t.reshape(1, -1).astype(F32)
    mu_main = shift_mu[:3 * d_r].reshape(3, d_r)
    mu_lora = shift_mu[3 * d_r:].reshape(1, d_lora)
    pm3 = proj_main.reshape(batch, seq, proj_main.shape[1])
    pl3 = proj_lora.reshape(batch, seq, d_lora)

    def main_spec(j):
        return pl.BlockSpec((batch, rows, d_r), lambda c, j=j: (0, c, cb + j))

    full = lambda shape: pl.BlockSpec(shape, lambda c: (0,) * len(shape))
    kern = functools.partial(_rwkv_kernel, n_heads=n_heads, d_decay=d_decay, d_aaa=d_aaa, chunk=chunk)
    y = pl.pallas_call(
        kern,
        grid=(n_steps,),
        in_specs=[main_spec(0), main_spec(1), main_spec(2),
                  pl.BlockSpec((batch, rows, d_lora), lambda c: (0, c, 0)),
                  full((3, d_r)), full((1, d_lora)), full((1, d_r)), full((d_decay, d_r)), full((1, d_r)),
                  full((d_aaa, d_r)), full((g_up.shape[0], d_r)), full((1, d_r)), full((1, d_r)), full((1, d_r)),
                  full((1, d_r)), full((1, d_r))],
        out_specs=pl.BlockSpec((batch, rows, d_r), lambda c: (0, c, 0)),
        out_shape=jax.ShapeDtypeStruct((batch, seq, d_r), BF16),
        scratch_shapes=[pltpu.VMEM((batch * n_heads * RWKV_HEAD // LANES, RWKV_HEAD, LANES), F32),
                        pltpu.VMEM((batch, 3, d_r), F32),
                        pltpu.VMEM((batch, 1, d_lora), F32)],
        compiler_params=pltpu.CompilerParams(dimension_semantics=("arbitrary",), vmem_limit_bytes=VMEM_LIMIT),
        name="rwkv_mixer",
    )(pm3, pm3, pm3, pl3, mu_main, mu_lora, row2(w0), w_up.astype(BF16), row2(a0),
      a_up.astype(BF16), g_up.astype(BF16), row2(k_k), row2(k_a), row2(r_k), row2(gn_g), row2(gn_b))
    return y.reshape(nt, d_r)


def _rms(x, g):
    return x * lax.rsqrt(jnp.mean(x * x, axis=-1, keepdims=True) + RMS_EPS) * g


def _inproj_kernel(x_ref, g_ref, wt_ref, wlt_ref, o_ref, ol_ref, u_ref):
    @pl.when(pl.program_id(1) == 0)
    def _():
        u = _rms(x_ref[...], g_ref[...]).astype(BF16)
        u_ref[...] = u
        ol_ref[...] = lax.dot_general(u, wlt_ref[...], _NT, preferred_element_type=F32)

    o_ref[...] = lax.dot_general(u_ref[...], wt_ref[...].astype(BF16), _NT, preferred_element_type=F32)


def in_proj(x2, norm_g, w_in, d_main, *, tm=1024, tn=1024):
    nt, d = x2.shape
    d_lora = w_in.shape[1] - d_main
    assert nt % tm == 0 and d_main % tn == 0
    w_t = w_in.T
    w_lora_t = w_t[d_main:].astype(BF16)
    return pl.pallas_call(
        _inproj_kernel,
        grid=(nt // tm, d_main // tn),
        in_specs=[pl.BlockSpec((tm, d), lambda i, j: (i, 0)),
                  pl.BlockSpec((1, d), lambda i, j: (0, 0)),
                  pl.BlockSpec((tn, d), lambda i, j: (j, 0)),
                  pl.BlockSpec((d_lora, d), lambda i, j: (0, 0))],
        out_specs=[pl.BlockSpec((tm, tn), lambda i, j: (i, j)),
                   pl.BlockSpec((tm, d_lora), lambda i, j: (i, 0))],
        out_shape=[jax.ShapeDtypeStruct((nt, d_main), F32), jax.ShapeDtypeStruct((nt, d_lora), F32)],
        scratch_shapes=[pltpu.VMEM((tm, d), BF16)],
        compiler_params=pltpu.CompilerParams(dimension_semantics=("arbitrary", "arbitrary"),
                                             vmem_limit_bytes=VMEM_LIMIT),
        name="in_proj",
    )(x2, norm_g.reshape(1, d).astype(F32), w_t, w_lora_t)


CONV_HALO = 32
CONV_ROWS = 32


def _conv_kernel(val_ref, gate_ref, dw_ref, b_ref, g_ref, beta_ref, o_ref, ubuf_ref):
    tt = val_ref.shape[0]
    d_c = val_ref.shape[1]
    t_idx = pl.program_id(1)

    @pl.when(t_idx == 0)
    def _():
        ubuf_ref[0, 0:CONV_HALO, :] = jnp.zeros((CONV_HALO, d_c), F32)

    @pl.when(t_idx > 0)
    def _():
        ubuf_ref[0, 0:CONV_HALO, :] = ubuf_ref[0, tt:tt + CONV_HALO, :]

    ubuf_ref[0, CONV_HALO:CONV_HALO + tt, :] = val_ref[...] * jax.nn.sigmoid(gate_ref[...])
    n_sh = tt + CONV_HALO - SUBLANES
    for s in range(1, SUBLANES):
        for r in range(0, n_sh, CONV_ROWS):
            n = min(CONV_ROWS, n_sh - r)
            ubuf_ref[s, r:r + n, :] = ubuf_ref[0, r + s:r + s + n, :]
    dw = dw_ref[...]
    lead = CONV_HALO - (CONV_WIDTH - 1)
    for c in range(tt // CONV_ROWS):
        r0 = c * CONV_ROWS
        acc = jnp.zeros((CONV_ROWS, d_c), F32)
        for j in range(CONV_WIDTH):
            q, s = divmod(lead + j, SUBLANES)
            r = r0 + q * SUBLANES
            acc = acc + dw[j:j + 1, :] * ubuf_ref[s, r:r + CONV_ROWS, :]
        acc = acc + b_ref[...]
        mu = jnp.mean(acc, axis=-1, keepdims=True)
        cen = acc - mu
        var = jnp.mean(cen * cen, axis=-1, keepdims=True)
        yv = cen * lax.rsqrt(var + LN_EPS) * g_ref[...] + beta_ref[...]
        o_ref[r0:r0 + CONV_ROWS, :] = (yv * jax.nn.sigmoid(yv)).astype(o_ref.dtype)


def conv_mixer(proj_main, dw, bias, ln_g, ln_b, *, batch, tt=256):
    nt = proj_main.shape[0]
    seq = nt // batch
    d_c = dw.shape[1]
    n_t = seq // tt
    assert seq % tt == 0 and tt % CONV_ROWS == 0
    row = lambda t: t.reshape(1, d_c).astype(F32)
    full = lambda shape: pl.BlockSpec(shape, lambda b, t: (0, 0))
    return pl.pallas_call(
        _conv_kernel,
        grid=(batch, n_t),
        in_specs=[pl.BlockSpec((tt, d_c), lambda b, t: (b * n_t + t, 0)),
                  pl.BlockSpec((tt, d_c), lambda b, t: (b * n_t + t, 1)),
                  full((CONV_WIDTH, d_c)), full((1, d_c)), full((1, d_c)), full((1, d_c))],
        out_specs=pl.BlockSpec((tt, d_c), lambda b, t: (b * n_t + t, 0)),
        out_shape=jax.ShapeDtypeStruct((nt, d_c), BF16),
        scratch_shapes=[pltpu.VMEM((SUBLANES, tt + CONV_HALO, d_c), F32)],
        compiler_params=pltpu.CompilerParams(dimension_semantics=("arbitrary", "arbitrary"),
                                             vmem_limit_bytes=VMEM_LIMIT),
        name="conv_mixer",
    )(proj_main, proj_main, dw.astype(F32), row(bias), row(ln_g), row(ln_b))


def _out_router_kernel(x_ref, yc_ref, yr_ref, wc_ref, wr_ref, nf_ref, rw_hi_ref, rw_lo_ref, rb_ref,
                       h_ref, v_ref, eid_ref, gate_ref, count_ref, cnt_ref):
    tm = x_ref.shape[0]
    h = (x_ref[...] + jnp.dot(yc_ref[...], wc_ref[...], preferred_element_type=F32)
         + jnp.dot(yr_ref[...], wr_ref[...], preferred_element_type=F32))
    h_ref[...] = h

    @pl.when(pl.program_id(0) == 0)
    def _():
        cnt_ref[...] = jnp.zeros_like(cnt_ref)

    v = _rms(h, nf_ref[...])
    v_ref[...] = _pack_bf16_pairs(v)
    v_hi = v.astype(BF16)
    v_lo = (v - v_hi.astype(F32)).astype(BF16)
    logits = (jnp.dot(v_hi, rw_hi_ref[...], preferred_element_type=F32)
              + jnp.dot(v_lo, rw_hi_ref[...], preferred_element_type=F32)
              + jnp.dot(v_hi, rw_lo_ref[...], preferred_element_type=F32)) + rb_ref[...]
    lane = lax.broadcasted_iota(jnp.int32, (tm, LANES), 1)
    neg = jnp.float32(-jnp.inf)
    big = jnp.int32(LANES)

    def first_max(vals):
        m = jnp.max(vals, axis=-1, keepdims=True)
        return m, jnp.min(jnp.where(vals == m, lane, big), axis=-1, keepdims=True)

    gl = jnp.where(lane < N_GROUPS, logits, neg)
    gmax, grp = first_max(gl)
    p_grp = 1.0 / jnp.sum(jnp.exp(gl - gmax), axis=-1, keepdims=True)
    lo = N_GROUPS + grp * EXPERTS_PER_GROUP
    el = jnp.where((lane >= lo) & (lane < lo + EXPERTS_PER_GROUP), logits, neg)
    m1, i1 = first_max(el)
    m2, i2 = first_max(jnp.where(lane == i1, neg, el))
    e2 = jnp.exp(m2 - m1)
    g1 = p_grp / (1.0 + e2)
    g2 = p_grp * e2 / (1.0 + e2)
    gate_ref[...] = jnp.where(lane == 0, g1, jnp.where(lane == 1, g2, 0.0))

    x1 = i1 - N_GROUPS
    x2 = i2 - N_GROUPS
    oh1 = (lane == x1).astype(F32)
    oh2 = (lane == x2).astype(F32)
    earlier = (lax.broadcasted_iota(jnp.int32, (tm, tm), 1)
               < lax.broadcasted_iota(jnp.int32, (tm, tm), 0)).astype(BF16)
    before1 = jnp.dot(earlier, oh1.astype(BF16), preferred_element_type=F32)
    before2 = jnp.dot(earlier, oh2.astype(BF16), preferred_element_type=F32)
    carry = cnt_ref[...]
    n1 = jnp.sum(oh1, axis=0, keepdims=True)
    n2 = jnp.sum(oh2, axis=0, keepdims=True)
    rank1 = jnp.sum(oh1 * (before1 + carry), axis=-1, keepdims=True).astype(jnp.int32)
    rank2 = jnp.sum(oh2 * (before2 + carry + n1), axis=-1, keepdims=True).astype(jnp.int32)
    cnt_ref[...] = carry + n1 + n2
    count_ref[...] = (carry + n1 + n2).astype(jnp.int32)
    routing = jnp.where(lane == 0, x1, jnp.where(lane == 1, x2, jnp.where(
        lane == 2, rank1, jnp.where(lane == 3, rank2, 0))))
    eid_ref[...] = routing.T[:SUBLANES, :]


def out_router(x2, y_conv, y_rwkv, w_out, norm_ffn, rg_w, rg_b, re_w, re_b, *, tm=512):
    nt, d = x2.shape
    d_c, d_r = y_conv.shape[1], y_rwkv.shape[1]
    assert d_c == d_r and nt % tm == 0
    n_r = N_GROUPS + N_EXPERTS
    rw = jnp.zeros((d, LANES), F32).at[:, :n_r].set(jnp.concatenate([rg_w, re_w], axis=1).astype(F32))
    rw_hi = rw.astype(BF16)
    rw_lo = (rw - rw_hi.astype(F32)).astype(BF16)
    rb = jnp.zeros((1, LANES), F32).at[0, :n_r].set(jnp.concatenate([rg_b, re_b]).astype(F32))
    full = lambda shape: pl.BlockSpec(shape, lambda i: (0, 0))
    once = lambda shape, r: pl.BlockSpec(shape, lambda i, r=r: (r, 0), pipeline_mode=pl.Buffered(1))
    rows = lambda w: pl.BlockSpec((tm, w), lambda i: (i, 0))
    return pl.pallas_call(
        _out_router_kernel,
        grid=(nt // tm,),
        in_specs=[rows(d), rows(d_c), rows(d_r), once((d_c, d), 0), once((d_r, d), 1),
                  full((1, d)), full((d, LANES)), full((d, LANES)), full((1, LANES))],
        out_specs=[rows(d), rows(d // 2), pl.BlockSpec((SUBLANES, tm), lambda i: (0, i)), rows(LANES),
                   full((1, LANES))],
        out_shape=[jax.ShapeDtypeStruct((nt, d), F32), jax.ShapeDtypeStruct((nt, d // 2), jnp.uint32),
                   jax.ShapeDtypeStruct((SUBLANES, nt), jnp.int32), jax.ShapeDtypeStruct((nt, LANES), F32),
                   jax.ShapeDtypeStruct((1, LANES), jnp.int32)],
        scratch_shapes=[pltpu.VMEM((1, LANES), F32)],
        compiler_params=pltpu.CompilerParams(dimension_semantics=("arbitrary",), vmem_limit_bytes=VMEM_LIMIT),
        name="out_router",
    )(x2, y_conv, y_rwkv, w_out, w_out, norm_ffn.reshape(1, d).astype(F32), rw_hi, rw_lo, rb)


MOE_ROWS = 256


def slot_plan(routing, counts, n_rows_pad):
    expert_id = routing[:TOP_K]
    rank = routing[TOP_K:2 * TOP_K]
    padded = (counts + MOE_ROWS - 1) // MOE_ROWS * MOE_ROWS
    pend = jnp.cumsum(padded)
    pstarts = pend - padded
    ids = jnp.arange(N_EXPERTS, dtype=jnp.int32)
    start_of = jnp.sum(jnp.where(expert_id[..., None] == ids, pstarts, 0), axis=-1)
    dest = (start_of + rank).astype(jnp.int32).T
    n_blocks = n_rows_pad // MOE_ROWS
    block_row0 = jnp.arange(n_blocks, dtype=jnp.int32) * MOE_ROWS
    block_e = jnp.minimum(jnp.sum((pend[None, :] <= block_row0[:, None]).astype(jnp.int32), axis=1),
                          N_EXPERTS - 1).astype(jnp.int32)
    n_used = (pend[-1:] // MOE_ROWS).astype(jnp.int32)
    owner = jnp.where(counts > 0, ids, N_EXPERTS)
    nxt = lax.cummin(jnp.concatenate([owner[1:], jnp.full((1,), N_EXPERTS, jnp.int32)]), reverse=True)
    next_used = jnp.where(nxt < N_EXPERTS, nxt, -1).astype(jnp.int32)
    used_ord = (jnp.cumsum((counts > 0).astype(jnp.int32)) - 1).astype(jnp.int32)
    pad0 = pstarts + counts
    n_single = jnp.minimum((-pad0) % SUBLANES, pend - pad0)
    tile0 = pad0 + n_single
    n_tile = (pend - tile0) // SUBLANES
    tail = jnp.stack([pend[-1] // MOE_ROWS, n_blocks - pend[-1] // MOE_ROWS, jnp.sum(n_single), jnp.sum(n_tile)])
    fill_plan = jnp.concatenate([pad0, n_single, tile0, n_tile, tail]).astype(jnp.int32)
    return dest, block_e, n_used, next_used, used_ord, fill_plan


def _dispatch_kernel(fill_ref, dest_ref, v_ref, xs_ref, zero_ref, sem, fill_sem):
    tm = v_ref.shape[0]

    def fills():
        e0, e1, e2, e3, e4 = (k * N_EXPERTS for k in range(5))
        single = lambda e, i: pltpu.make_async_copy(
            zero_ref.at[pl.ds(0, 1), :], xs_ref.at[pl.ds(fill_ref[e0 + e] + i, 1), :], fill_sem)
        tile = lambda e, i: pltpu.make_async_copy(
            zero_ref.at[pl.ds(0, SUBLANES), :],
            xs_ref.at[pl.ds(pl.multiple_of(fill_ref[e2 + e] + i * SUBLANES, SUBLANES), SUBLANES), :], fill_sem)
        block = lambda i: pltpu.make_async_copy(
            zero_ref, xs_ref.at[pl.ds(pl.multiple_of((fill_ref[e4] + i) * MOE_ROWS, MOE_ROWS), MOE_ROWS), :],
            fill_sem)
        return single, tile, block, e1, e3, e4

    @pl.when(pl.program_id(0) == 0)
    def _():
        zero_ref[...] = jnp.zeros_like(zero_ref)
        single, tile, block, e1, e3, e4 = fills()
        for e in range(N_EXPERTS):
            lax.fori_loop(0, fill_ref[e1 + e], lambda i, c, e=e: (single(e, i).start(), c)[1], 0)
            lax.fori_loop(0, fill_ref[e3 + e], lambda i, c, e=e: (tile(e, i).start(), c)[1], 0)
        lax.fori_loop(0, fill_ref[e4 + 1], lambda i, c: (block(i).start(), c)[1], 0)

    def copy(i, k):
        d = dest_ref[0, 0, i * TOP_K + k]
        return pltpu.make_async_copy(v_ref.at[pl.ds(i, 1), :], xs_ref.at[pl.ds(d, 1), :], sem)

    def start(i, carry):
        for k in range(TOP_K):
            copy(i, k).start(priority=k)
        return carry

    lax.fori_loop(0, tm, start, 0, unroll=8)
    for k in range(TOP_K):
        pltpu.make_async_copy(v_ref, xs_ref.at[pl.ds(0, tm), :], sem).wait()

    @pl.when(pl.program_id(0) == 0)
    def _():
        single, tile, block, e1, e3, e4 = fills()
        lax.fori_loop(0, fill_ref[e4 + 2], lambda i, c: (single(0, 0).wait(), c)[1], 0)
        lax.fori_loop(0, fill_ref[e4 + 3], lambda i, c: (tile(0, 0).wait(), c)[1], 0)
        lax.fori_loop(0, fill_ref[e4 + 1], lambda i, c: (block(0).wait(), c)[1], 0)


def dispatch(v, dest, fill_plan, n_rows_pad, *, tm=1024):
    nt, d = v.shape
    assert nt % tm == 0
    dest3 = dest.reshape(nt // tm, 1, tm * TOP_K)
    grid_spec = pltpu.PrefetchScalarGridSpec(
        num_scalar_prefetch=1,
        grid=(nt // tm,),
        in_specs=[pl.BlockSpec((1, 1, tm * TOP_K), lambda i, fp: (i, 0, 0), memory_space=pltpu.SMEM),
                  pl.BlockSpec((tm, d), lambda i, fp: (i, 0))],
        out_specs=pl.BlockSpec(memory_space=pl.ANY),
        scratch_shapes=[pltpu.VMEM((MOE_ROWS, d), v.dtype), pltpu.SemaphoreType.DMA(()),
                        pltpu.SemaphoreType.DMA(())],
    )
    return pl.pallas_call(
        _dispatch_kernel,
        grid_spec=grid_spec,
        out_shape=jax.ShapeDtypeStruct((n_rows_pad, d), v.dtype),
        compiler_params=pltpu.CompilerParams(dimension_semantics=("arbitrary",), vmem_limit_bytes=VMEM_LIMIT),
        name="moe_dispatch",
    )(fill_plan, dest3, v)


def _experts_kernel(be_ref, nu_ref, nxt_ref, ord_ref, xs_ref, wg_hbm, wu_hbm, wd_hbm, ys_ref,
                    wg_f, wu_f, wd_f, wg_s, wu_s, wd_s, wsem):
    j = pl.program_id(0)
    e = be_ref[j]

    def fetch(ex, slot):
        return [pltpu.make_async_copy(wg_hbm.at[ex], wg_f.at[slot], wsem.at[slot]),
                pltpu.make_async_copy(wu_hbm.at[ex], wu_f.at[slot], wsem.at[slot]),
                pltpu.make_async_copy(wd_hbm.at[ex], wd_f.at[slot], wsem.at[slot])]

    @pl.when(j == 0)
    def _():
        for c in fetch(e, lax.rem(ord_ref[e], 2)):
            c.start(priority=1)

    first = (j == 0) | (e != be_ref[jnp.maximum(j - 1, 0)])

    @pl.when(first & (j < nu_ref[0]))
    def _():
        slot = lax.rem(ord_ref[e], 2)
        for c in fetch(e, slot):
            c.wait()
        nxt = nxt_ref[e]

        @pl.when(nxt >= 0)
        def _():
            for c in fetch(nxt, 1 - slot):
                c.start(priority=1)

        wg_s[...] = wg_f[slot].astype(BF16)
        wu_s[...] = wu_f[slot].astype(BF16)
        wd_s[...] = wd_f[slot].astype(BF16)

    @pl.when(j < nu_ref[0])
    def _():
        xb = _unpack_bf16_pairs(xs_ref[...]).astype(BF16)
        gt = jnp.dot(xb, wg_s[...], preferred_element_type=F32)
        up = jnp.dot(xb, wu_s[...], preferred_element_type=F32)
        hid = (gt * jax.nn.sigmoid(gt) * up).astype(BF16)
        ys_ref[...] = _pack_bf16_pairs(jnp.dot(hid, wd_s[...], preferred_element_type=F32))

    @pl.when(j >= nu_ref[0])
    def _():
        ys_ref[...] = jnp.zeros_like(ys_ref)


def experts(xs, block_e, n_used, next_used, used_ord, w_gate, w_up, w_down, n_rows_pad):
    dh = xs.shape[1]
    d, d_e = w_gate.shape[1], w_gate.shape[2]
    assert d == 2 * dh
    n_blocks = n_rows_pad // MOE_ROWS
    hbm = pl.BlockSpec(memory_space=pl.ANY)
    grid_spec = pltpu.PrefetchScalarGridSpec(
        num_scalar_prefetch=4,
        grid=(n_blocks,),
        in_specs=[pl.BlockSpec((MOE_ROWS, dh), lambda j, be, nu, nx, od: (jnp.minimum(j, nu[0] - 1), 0)),
                  hbm, hbm, hbm],
        out_specs=pl.BlockSpec((MOE_ROWS, dh), lambda j, be, nu, nx, od: (j, 0)),
        scratch_shapes=[pltpu.VMEM((2, d, d_e), F32), pltpu.VMEM((2, d, d_e), F32), pltpu.VMEM((2, d_e, d), F32),
                        pltpu.VMEM((d, d_e), BF16), pltpu.VMEM((d, d_e), BF16), pltpu.VMEM((d_e, d), BF16),
                        pltpu.SemaphoreType.DMA((2,))],
    )
    return pl.pallas_call(
        _experts_kernel,
        grid_spec=grid_spec,
        out_shape=jax.ShapeDtypeStruct((n_rows_pad, dh), jnp.uint32),
        compiler_params=pltpu.CompilerParams(dimension_semantics=("arbitrary",), vmem_limit_bytes=VMEM_LIMIT),
        name="moe_experts",
    )(block_e, n_used, next_used, used_ord, xs, w_gate, w_up, w_down)


def _combine_kernel(dest_ref, dest_next_ref, h_ref, gate_ref, nfin_ref, ys_ref, o_ref, ybuf, sem):
    step = pl.program_id(0)
    tm = h_ref.shape[0]
    slot = lax.rem(step, 2)

    def copy(d_ref, s, i, k):
        d = d_ref[0, 0, i * TOP_K + k]
        return pltpu.make_async_copy(ys_ref.at[pl.ds(d, 1), :], ybuf.at[s, k, pl.ds(i, 1), :], sem.at[s])

    def start_all(d_ref, s):
        def body(i, carry):
            for k in range(TOP_K):
                copy(d_ref, s, i, k).start(priority=k)
            return carry
        lax.fori_loop(0, tm, body, 0, unroll=8)

    @pl.when(step == 0)
    def _():
        start_all(dest_ref, 0)

    @pl.when(step + 1 < pl.num_programs(0))
    def _():
        start_all(dest_next_ref, 1 - slot)

    for k in range(TOP_K):
        pltpu.make_async_copy(ys_ref.at[pl.ds(0, tm), :], ybuf.at[slot, k], sem.at[slot]).wait()
    gates = gate_ref[...]
    moe = (gates[:, 0:1] * _unpack_bf16_pairs(ybuf[slot, 0])
           + gates[:, 1:2] * _unpack_bf16_pairs(ybuf[slot, 1]))
    o_ref[...] = _rms(h_ref[...] + moe, nfin_ref[...])


def combine(h, gates, dest, ys, norm_final, *, tm=512):
    nt, d = h.shape
    assert nt % tm == 0 and ys.shape[1] * 2 == d
    n_tiles = nt // tm
    dest3 = dest.reshape(n_tiles, 1, tm * TOP_K)
    return pl.pallas_call(
        _combine_kernel,
        grid=(n_tiles,),
        in_specs=[pl.BlockSpec((1, 1, tm * TOP_K), lambda i: (i, 0, 0), memory_space=pltpu.SMEM),
                  pl.BlockSpec((1, 1, tm * TOP_K), lambda i: (jnp.minimum(i + 1, n_tiles - 1), 0, 0),
                               memory_space=pltpu.SMEM),
                  pl.BlockSpec((tm, d), lambda i: (i, 0)),
                  pl.BlockSpec((tm, LANES), lambda i: (i, 0)),
                  pl.BlockSpec((1, d), lambda i: (0, 0)),
                  pl.BlockSpec(memory_space=pl.ANY)],
        out_specs=pl.BlockSpec((tm, d), lambda i: (i, 0)),
        out_shape=jax.ShapeDtypeStruct((nt, d), F32),
        scratch_shapes=[pltpu.VMEM((2, TOP_K, tm, d // 2), jnp.uint32), pltpu.SemaphoreType.DMA((2,))],
        compiler_params=pltpu.CompilerParams(dimension_semantics=("arbitrary",), vmem_limit_bytes=VMEM_LIMIT),
        name="moe_combine",
    )(dest3, dest3, h, gates, norm_final.reshape(1, d).astype(F32), ys)


def kernel(x, norm_mix, w_in, conv_dw, conv_b, conv_ln_g, conv_ln_b, shift_mu, w0, w_lora_up, a0, a_lora_up, g_lora_up, k_k, k_a, r_k, gn_g, gn_b, w_out, norm_ffn, router_group_w, router_group_b, router_expert_w, router_expert_b, expert_w_gate, expert_w_up, expert_w_down, norm_final):
    B, T, D = x.shape
    depth = w_in.shape[0]
    d_c = conv_dw.shape[2]
    d_r = w0.shape[1]
    d_main = 2 * d_c + 3 * d_r
    nt = B * T
    n_rows_pad = -(-(nt * TOP_K + N_EXPERTS * (MOE_ROWS - 1)) // MOE_ROWS) * MOE_ROWS
    h = x.reshape(nt, D)
    for l in range(depth):
        proj_main, proj_lora = in_proj(h, norm_mix[l], w_in[l], d_main)
        y_conv = conv_mixer(proj_main, conv_dw[l], conv_b[l], conv_ln_g[l], conv_ln_b[l], batch=B)
        y_rwkv = rwkv_mixer(proj_main, proj_lora, 2 * d_c, shift_mu[l], w0[l], w_lora_up[l], a0[l], a_lora_up[l],
                            g_lora_up[l], k_k[l], k_a[l], r_k[l].reshape(-1), gn_g[l], gn_b[l], batch=B)
        h, v, routing, gates, counts = out_router(h, y_conv, y_rwkv, w_out[l].astype(BF16), norm_ffn[l],
                                                  router_group_w[l], router_group_b[l], router_expert_w[l],
                                                  router_expert_b[l])
        dest, block_e, n_used, next_used, used_ord, fill_plan = slot_plan(routing, counts[0, :N_EXPERTS],
                                                                          n_rows_pad)
        xs = dispatch(v, dest, fill_plan, n_rows_pad)
        ys = experts(xs, block_e, n_used, next_used, used_ord, expert_w_gate[l], expert_w_up[l], expert_w_down[l],
                     n_rows_pad)
        assert depth == 1
        h = combine(h, gates, dest, ys, norm_final)
    return h.reshape(B, T, D)
```

```python
import functools

import jax
import jax.numpy as jnp
from jax import lax
from jax.experimental import pallas as pl
from jax.experimental.pallas import tpu as pltpu

F32 = jnp.float32
BF16 = jnp.bfloat16

RWKV_HEAD = 64
CONV_WIDTH = 31
N_GROUPS = 4
EXPERTS_PER_GROUP = 8
N_EXPERTS = N_GROUPS * EXPERTS_PER_GROUP
TOP_K = 2
RMS_EPS = 1e-6
LN_EPS = 1e-5
GN_EPS = 64e-5
LANES = 128
SUBLANES = 8
MXU_DIM = 256
VMEM_LIMIT = 56 * 1024 * 1024

_NT = (((1,), (1,)), ((), ()))
_TN = (((0,), (0,)), ((), ()))


def _dot(a, b, dims=None):
    a = a.astype(BF16)
    b = b.astype(BF16)
    if dims is None:
        return jnp.dot(a, b, preferred_element_type=F32)
    return lax.dot_general(a, b, dims, preferred_element_type=F32)


def _dot_split(x, m):
    hi = x.astype(BF16)
    lo = (x - hi.astype(F32)).astype(BF16)
    return (jnp.dot(hi, m, preferred_element_type=F32)
            + jnp.dot(lo, m, preferred_element_type=F32))


def _pack_bf16_pairs(x):
    w = x.shape[1] // 2
    lo = lax.bitcast_convert_type(x[:, :w].astype(BF16).astype(F32), jnp.uint32)
    hi = lax.bitcast_convert_type(x[:, w:].astype(BF16).astype(F32), jnp.uint32)
    return (lo >> 16) | (hi & jnp.uint32(0xFFFF0000))


def _unpack_bf16_pairs(p):
    lo = lax.bitcast_convert_type(p << 16, F32)
    hi = lax.bitcast_convert_type(p & jnp.uint32(0xFFFF0000), F32)
    return jnp.concatenate([lo, hi], axis=1)


def _rwkv_kernel(r_ref, k_ref, v_ref, lora_ref, mu_main_ref, mu_lora_ref, w0_ref, wup_ref, a0_ref, aup_ref,
                 gup_ref, kk_ref, ka_ref, rk_ref, gng_ref, gnb_ref, y_ref,
                 state_ref, prev_main_ref, prev_lora_ref, *, n_heads, d_decay, d_aaa, chunk):
    nb, Cb, d_r = r_ref.shape
    C = chunk
    CS = Cb // C
    N = RWKV_HEAD
    R = nb * Cb

    @pl.when(pl.program_id(0) == 0)
    def _():
        state_ref[...] = jnp.zeros_like(state_ref)
        prev_main_ref[...] = jnp.zeros_like(prev_main_ref)
        prev_lora_ref[...] = jnp.zeros_like(prev_lora_ref)

    row = lax.broadcasted_iota(jnp.int32, (R, 1), 0)

    def shifted(x_ref, prev_ref, j, mu):
        x = x_ref[...].reshape(R, x_ref.shape[2])
        xp = pltpu.roll(x, 1, 0)
        for b in range(nb):
            xp = jnp.where(row == b * Cb, prev_ref[b, j:j + 1, :], xp)
            prev_ref[b, j:j + 1, :] = x[(b + 1) * Cb - 1:(b + 1) * Cb, :]
        return x + (xp - x) * mu

    r = shifted(r_ref, prev_main_ref, 0, mu_main_ref[0:1, :])
    k = shifted(k_ref, prev_main_ref, 1, mu_main_ref[1:2, :])
    v = shifted(v_ref, prev_main_ref, 2, mu_main_ref[2:3, :])
    lo = shifted(lora_ref, prev_lora_ref, 0, mu_lora_ref[...])

    wd = lo[:, :d_decay]
    ad = lo[:, d_decay:d_decay + d_aaa]
    gd = lo[:, d_decay + d_aaa:]

    z = w0_ref[...] + _dot(jnp.tanh(wd), wup_ref[...])
    w = -(jnp.maximum(-z, 0.0) + jnp.log(1.0 + jnp.exp(-jnp.abs(z)))) - 0.5
    logd = -jnp.exp(w)
    a = jax.nn.sigmoid(a0_ref[...] + _dot(ad, aup_ref[...]))
    g = _dot(jax.nn.sigmoid(gd), gup_ref[...])

    li = lax.broadcasted_iota(jnp.int32, (MXU_DIM, MXU_DIM), 0) // N
    lj = lax.broadcasted_iota(jnp.int32, (MXU_DIM, MXU_DIM), 1) // N
    head_ones = (li == lj).astype(BF16)

    def head_sum(x):
        return jnp.concatenate(
            [_dot_split(x[:, c * MXU_DIM:(c + 1) * MXU_DIM], head_ones) for c in range(d_r // MXU_DIM)], axis=1)

    kkr = k * kk_ref[...]
    kk = kkr / jnp.maximum(jnp.sqrt(head_sum(kkr * kkr)), 1e-12)
    k2 = k * (1.0 + (a - 1.0) * ka_ref[...])
    a_s = -kk
    b_s = kk * a

    ri = lax.broadcasted_iota(jnp.int32, (R, R), 0)
    rj = lax.broadcasted_iota(jnp.int32, (R, R), 1)
    tri = ((rj <= ri) & ((ri // C) == (rj // C))).astype(BF16)
    hi = logd.astype(BF16)
    rem = logd - hi.astype(F32)
    mid = rem.astype(BF16)
    low = (rem - mid.astype(F32)).astype(BF16)
    cum = (jnp.dot(tri, hi, preferred_element_type=F32) + jnp.dot(tri, mid, preferred_element_type=F32)
           + jnp.dot(tri, low, preferred_element_type=F32))
    p_incl = jnp.exp(cum)
    p_excl = jnp.exp(cum - logd)
    p_inv = jnp.exp(-cum)
    p_last = [p_incl[(q + 1) * C - 1:(q + 1) * C, :] for q in range(nb * CS)]
    p_last_rows = jnp.concatenate([jnp.broadcast_to(p, (C, d_r)) for p in p_last], axis=0)

    at = (a_s * p_excl).astype(BF16)
    rt = (r * p_incl).astype(BF16)
    bt = b_s * p_inv
    kt = k2 * p_inv
    bh = (bt * p_last_rows).astype(BF16)
    kh = (kt * p_last_rows).astype(BF16)
    bt = bt.astype(BF16)
    kt = kt.astype(BF16)
    vb = v.astype(BF16)

    HG = LANES // N
    W = HG * N
    GC = HG * C
    wi = lax.broadcasted_iota(jnp.int32, (GC, W), 0)
    wj = lax.broadcasted_iota(jnp.int32, (GC, W), 1)
    bd_on = ((wi // C) == (wj // N)).astype(BF16) > 0

    def bdiag(x):
        xb = x.astype(BF16)
        return jnp.where(bd_on, jnp.concatenate([xb] * HG, axis=0), jnp.zeros((), BF16))

    ti = lax.broadcasted_iota(jnp.int32, (2 * C, GC), 0)
    tj = lax.broadcasted_iota(jnp.int32, (2 * C, GC), 1) % C
    mask2 = ((ti < C) & (tj < ti)) | ((ti >= C) & (tj <= ti - C))
    li = lax.broadcasted_iota(jnp.int32, (C, GC), 0)
    lj = lax.broadcasted_iota(jnp.int32, (C, GC), 1) % C
    eye = (li == lj).astype(F32)
    level_masks = []
    s = 1
    while s < C:
        level_masks.append(((li // (2 * s)) == (lj // (2 * s))) & (((li // s) % 2) == 1) & (((lj // s) % 2) == 0))
        s *= 2
    colblk = lax.broadcasted_iota(jnp.int32, (N, W), 1) // N

    n_g = n_heads // HG
    units = [(q, g) for q in range(nb * CS) for g in range(n_g)]
    idx = range(len(units))
    rs = [slice(q * C, (q + 1) * C) for q, _ in units]
    ls = [slice(g * W, (g + 1) * W) for _, g in units]
    g_b = [jnp.where(mask2, _dot(jnp.concatenate([at[rs[i], ls[i]], rt[rs[i], ls[i]]], axis=0),
                                 bdiag(bt[rs[i], ls[i]]), _NT), 0.0) for i in idx]
    g_k = [jnp.where(mask2, _dot(jnp.concatenate([at[rs[i], ls[i]], rt[rs[i], ls[i]]], axis=0),
                                 bdiag(kt[rs[i], ls[i]]), _NT), 0.0) for i in idx]
    gv = [_dot(g_k[i], bdiag(vb[rs[i], ls[i]])) for i in idx]
    tinv = [eye + jnp.where(level_masks[0], g_b[i][:C], 0.0) for i in idx]
    for lm in level_masks[1:]:
        xs = [_dot(jnp.where(lm, g_b[i][:C], 0.0), bdiag(tinv[i])) for i in idx]
        tinv = [tinv[i] + _dot(tinv[i], bdiag(xs[i])) for i in idx]
    ta = [_dot(tinv[i], bdiag(at[rs[i], ls[i]])) for i in idx]
    tv = [_dot(tinv[i], bdiag(gv[i][:C])) for i in idx]
    ys = [None] * len(units)
    for b in range(nb):
        state = [state_ref[b * n_g + gi] for gi in range(n_g)]
        for cc in range(CS):
            ids = [(b * CS + cc) * n_g + gi for gi in range(n_g)]
            tr = [_dot(jnp.concatenate([ta[i].astype(BF16), rt[rs[i], ls[i]]], axis=0), bdiag(state[gi]), _NT)
                  for gi, i in enumerate(ids)]
            u = [tr[gi][:C] + tv[i] for gi, i in enumerate(ids)]
            for gi, i in enumerate(ids):
                ys[i] = tr[gi][C:] + gv[i][C:] + _dot(g_b[i][C:], bdiag(u[gi]))
                uv = jnp.concatenate([u[gi].astype(BF16), vb[rs[i], ls[i]]], axis=0)
                bk = jnp.concatenate([bh[rs[i], ls[i]], kh[rs[i], ls[i]]], axis=0)
                full = _dot(uv, bk, _TN)
                upd = state[gi] * p_last[units[i][0]][:, ls[i]]
                for h in range(HG):
                    upd = upd + jnp.where(colblk == h, full[h * N:(h + 1) * N, :], 0.0)
                state[gi] = upd
        for gi in range(n_g):
            state_ref[b * n_g + gi] = state[gi]
    y = jnp.concatenate([jnp.concatenate(ys[q * n_g:(q + 1) * n_g], axis=1) for q in range(nb * CS)],
                        axis=0)

    inv_n = 1.0 / N
    mu_y = head_sum(y) * inv_n
    yc = y - mu_y
    var_y = head_sum(yc * yc) * inv_n
    yn = yc * lax.rsqrt(var_y + GN_EPS) * gng_ref[...] + gnb_ref[...]
    bonus = head_sum(r * k2 * rk_ref[...]) * v
    y_ref[...] = ((yn + bonus) * g).reshape(nb, Cb, d_r).astype(y_ref.dtype)


def rwkv_mixer(proj_main, proj_lora, col0, shift_mu, w0, w_up, a0, a_up, g_up, k_k, k_a, r_k, gn_g, gn_b,
               *, batch, chunk=64, chunks_per_step=2):
    nt = proj_main.shape[0]
    seq = nt // batch
    d_r = w0.shape[-1]
    n_heads = d_r // RWKV_HEAD
    d_lora = proj_lora.shape[1]
    d_decay, d_aaa = w_up.shape[0], a_up.shape[0]
    rows = chunk * chunks_per_step
    n_steps = seq // rows
    assert seq % rows == 0 and col0 % d_r == 0
    cb = col0 // d_r
    row2 = lambda t: t.reshape(1, -1).astype(F32)
    mu_main = shift_mu[:3 * d_r].reshape(3, d_r)
    mu_lora = shift_mu[3 * d_r:].reshape(1, d_lora)
    pm3 = proj_main.reshape(batch, seq, proj_main.shape[1])
    pl3 = proj_lora.reshape(batch, seq, d_lora)

    def main_spec(j):
        return pl.BlockSpec((batch, rows, d_r), lambda c, j=j: (0, c, cb + j))

    full = lambda shape: pl.BlockSpec(shape, lambda c: (0,) * len(shape))
    kern = functools.partial(_rwkv_kernel, n_heads=n_heads, d_decay=d_decay, d_aaa=d_aaa, chunk=chunk)
    y = pl.pallas_call(
        kern,
        grid=(n_steps,),
        in_specs=[main_spec(0), main_spec(1), main_spec(2),
                  pl.BlockSpec((batch, rows, d_lora), lambda c: (0, c, 0)),
                  full((3, d_r)), full((1, d_lora)), full((1, d_r)), full((d_decay, d_r)), full((1, d_r)),
                  full((d_aaa, d_r)), full((g_up.shape[0], d_r)), full((1, d_r)), full((1, d_r)), full((1, d_r)),
                  full((1, d_r)), full((1, d_r))],
        out_specs=pl.BlockSpec((batch, rows, d_r), lambda c: (0, c, 0)),
        out_shape=jax.ShapeDtypeStruct((batch, seq, d_r), BF16),
        scratch_shapes=[pltpu.VMEM((batch * n_heads * RWKV_HEAD // LANES, RWKV_HEAD, LANES), F32),
                        pltpu.VMEM((batch, 3, d_r), F32),
                        pltpu.VMEM((batch, 1, d_lora), F32)],
        compiler_params=pltpu.CompilerParams(dimension_semantics=("arbitrary",), vmem_limit_bytes=VMEM_LIMIT),
        name="rwkv_mixer",
    )(pm3, pm3, pm3, pl3, mu_main, mu_lora, row2(w0), w_up.astype(BF16), row2(a0),
      a_up.astype(BF16), g_up.astype(BF16), row2(k_k), row2(k_a), row2(r_k), row2(gn_g), row2(gn_b))
    return y.reshape(nt, d_r)


def _rms(x, g):
    return x * lax.rsqrt(jnp.mean(x * x, axis=-1, keepdims=True) + RMS_EPS) * g


def _inproj_kernel(x_ref, g_ref, wt_hbm, wlt_ref, o_ref, ol_ref, u_ref, w_f, wsem):
    i, j = pl.program_id(0), pl.program_id(1)
    n_i, n_j = pl.num_programs(0), pl.num_programs(1)
    tn = w_f.shape[1]
    step = i * n_j + j
    slot = lax.rem(step, 2)

    def fetch(jt, s):
        return pltpu.make_async_copy(wt_hbm.at[pl.ds(pl.multiple_of(jt * tn, tn), tn), :], w_f.at[s], wsem.at[s])

    @pl.when(step == 0)
    def _():
        fetch(0, 0).start(priority=1)

    @pl.when(step + 1 < n_i * n_j)
    def _():
        fetch(lax.rem(j + 1, n_j), 1 - slot).start(priority=1)

    @pl.when(j == 0)
    def _():
        u = _rms(x_ref[...], g_ref[...]).astype(BF16)
        u_ref[...] = u
        ol_ref[...] = lax.dot_general(u, wlt_ref[...], _NT, preferred_element_type=F32)

    fetch(j, slot).wait()
    o_ref[...] = lax.dot_general(u_ref[...], w_f[slot].astype(BF16), _NT, preferred_element_type=F32)


def in_proj(x2, norm_g, w_in, d_main, *, tm=1024, tn=1024):
    nt, d = x2.shape
    d_lora = w_in.shape[1] - d_main
    assert nt % tm == 0 and d_main % tn == 0
    w_t = w_in.T
    w_lora_t = w_t[d_main:].astype(BF16)
    return pl.pallas_call(
        _inproj_kernel,
        grid=(nt // tm, d_main // tn),
        in_specs=[pl.BlockSpec((tm, d), lambda i, j: (i, 0)),
                  pl.BlockSpec((1, d), lambda i, j: (0, 0)),
                  pl.BlockSpec(memory_space=pl.ANY),
                  pl.BlockSpec((d_lora, d), lambda i, j: (0, 0))],
        out_specs=[pl.BlockSpec((tm, tn), lambda i, j: (i, j)),
                   pl.BlockSpec((tm, d_lora), lambda i, j: (i, 0))],
        out_shape=[jax.ShapeDtypeStruct((nt, d_main), F32), jax.ShapeDtypeStruct((nt, d_lora), F32)],
        scratch_shapes=[pltpu.VMEM((tm, d), BF16), pltpu.VMEM((2, tn, d), F32), pltpu.SemaphoreType.DMA((2,))],
        compiler_params=pltpu.CompilerParams(dimension_semantics=("arbitrary", "arbitrary"),
                                             vmem_limit_bytes=VMEM_LIMIT),
        name="in_proj",
    )(x2, norm_g.reshape(1, d).astype(F32), w_t, w_lora_t)


CONV_HALO = 32
CONV_ROWS = 32


def _conv_kernel(val_ref, gate_ref, dw_ref, b_ref, g_ref, beta_ref, o_ref, ubuf_ref):
    tt = val_ref.shape[0]
    d_c = val_ref.shape[1]
    t_idx = pl.program_id(1)

    @pl.when(t_idx == 0)
    def _():
        ubuf_ref[0, 0:CONV_HALO, :] = jnp.zeros((CONV_HALO, d_c), F32)

    @pl.when(t_idx > 0)
    def _():
        ubuf_ref[0, 0:CONV_HALO, :] = ubuf_ref[0, tt:tt + CONV_HALO, :]

    ubuf_ref[0, CONV_HALO:CONV_HALO + tt, :] = val_ref[...] * jax.nn.sigmoid(gate_ref[...])
    n_sh = tt + CONV_HALO - SUBLANES
    for s in range(1, SUBLANES):
        for r in range(0, n_sh, CONV_ROWS):
            n = min(CONV_ROWS, n_sh - r)
            ubuf_ref[s, r:r + n, :] = ubuf_ref[0, r + s:r + s + n, :]
    dw = dw_ref[...]
    lead = CONV_HALO - (CONV_WIDTH - 1)
    for c in range(tt // CONV_ROWS):
        r0 = c * CONV_ROWS
        acc = jnp.zeros((CONV_ROWS, d_c), F32)
        for j in range(CONV_WIDTH):
            q, s = divmod(lead + j, SUBLANES)
            r = r0 + q * SUBLANES
            acc = acc + dw[j:j + 1, :] * ubuf_ref[s, r:r + CONV_ROWS, :]
        acc = acc + b_ref[...]
        mu = jnp.mean(acc, axis=-1, keepdims=True)
        cen = acc - mu
        var = jnp.mean(cen * cen, axis=-1, keepdims=True)
        yv = cen * lax.rsqrt(var + LN_EPS) * g_ref[...] + beta_ref[...]
        o_ref[r0:r0 + CONV_ROWS, :] = (yv * jax.nn.sigmoid(yv)).astype(o_ref.dtype)


def conv_mixer(proj_main, dw, bias, ln_g, ln_b, *, batch, tt=256):
    nt = proj_main.shape[0]
    seq = nt // batch
    d_c = dw.shape[1]
    n_t = seq // tt
    assert seq % tt == 0 and tt % CONV_ROWS == 0
    row = lambda t: t.reshape(1, d_c).astype(F32)
    full = lambda shape: pl.BlockSpec(shape, lambda b, t: (0, 0))
    return pl.pallas_call(
        _conv_kernel,
        grid=(batch, n_t),
        in_specs=[pl.BlockSpec((tt, d_c), lambda b, t: (b * n_t + t, 0)),
                  pl.BlockSpec((tt, d_c), lambda b, t: (b * n_t + t, 1)),
                  full((CONV_WIDTH, d_c)), full((1, d_c)), full((1, d_c)), full((1, d_c))],
        out_specs=pl.BlockSpec((tt, d_c), lambda b, t: (b * n_t + t, 0)),
        out_shape=jax.ShapeDtypeStruct((nt, d_c), BF16),
        scratch_shapes=[pltpu.VMEM((SUBLANES, tt + CONV_HALO, d_c), F32)],
        compiler_params=pltpu.CompilerParams(dimension_semantics=("arbitrary", "arbitrary"),
                                             vmem_limit_bytes=VMEM_LIMIT),
        name="conv_mixer",
    )(proj_main, proj_main, dw.astype(F32), row(bias), row(ln_g), row(ln_b))


def _out_router_kernel(x_ref, yc_ref, yr_ref, wc_ref, wr_ref, nf_ref, rw_hi_ref, rw_lo_ref, rb_ref,
                       h_ref, v_ref, eid_ref, gate_ref, count_ref, cnt_ref):
    tm = x_ref.shape[0]
    h = (x_ref[...] + jnp.dot(yc_ref[...], wc_ref[...], preferred_element_type=F32)
         + jnp.dot(yr_ref[...], wr_ref[...], preferred_element_type=F32))
    h_ref[...] = h

    @pl.when(pl.program_id(0) == 0)
    def _():
        cnt_ref[...] = jnp.zeros_like(cnt_ref)

    v = _rms(h, nf_ref[...])
    v_ref[...] = _pack_bf16_pairs(v)
    v_hi = v.astype(BF16)
    v_lo = (v - v_hi.astype(F32)).astype(BF16)
    logits = (jnp.dot(v_hi, rw_hi_ref[...], preferred_element_type=F32)
              + jnp.dot(v_lo, rw_hi_ref[...], preferred_element_type=F32)
              + jnp.dot(v_hi, rw_lo_ref[...], preferred_element_type=F32)) + rb_ref[...]
    lane = lax.broadcasted_iota(jnp.int32, (tm, LANES), 1)
    neg = jnp.float32(-jnp.inf)
    big = jnp.int32(LANES)

    def first_max(vals):
        m = jnp.max(vals, axis=-1, keepdims=True)
        return m, jnp.min(jnp.where(vals == m, lane, big), axis=-1, keepdims=True)

    gl = jnp.where(lane < N_GROUPS, logits, neg)
    gmax, grp = first_max(gl)
    p_grp = 1.0 / jnp.sum(jnp.exp(gl - gmax), axis=-1, keepdims=True)
    lo = N_GROUPS + grp * EXPERTS_PER_GROUP
    el = jnp.where((lane >= lo) & (lane < lo + EXPERTS_PER_GROUP), logits, neg)
    m1, i1 = first_max(el)
    m2, i2 = first_max(jnp.where(lane == i1, neg, el))
    e2 = jnp.exp(m2 - m1)
    g1 = p_grp / (1.0 + e2)
    g2 = p_grp * e2 / (1.0 + e2)
    gate_ref[...] = jnp.where(lane == 0, g1, jnp.where(lane == 1, g2, 0.0))

    x1 = i1 - N_GROUPS
    x2 = i2 - N_GROUPS
    oh1 = (lane == x1).astype(F32)
    oh2 = (lane == x2).astype(F32)
    earlier = (lax.broadcasted_iota(jnp.int32, (tm, tm), 1)
               < lax.broadcasted_iota(jnp.int32, (tm, tm), 0)).astype(BF16)
    before1 = jnp.dot(earlier, oh1.astype(BF16), preferred_element_type=F32)
    before2 = jnp.dot(earlier, oh2.astype(BF16), preferred_element_type=F32)
    carry = cnt_ref[...]
    n1 = jnp.sum(oh1, axis=0, keepdims=True)
    n2 = jnp.sum(oh2, axis=0, keepdims=True)
    rank1 = jnp.sum(oh1 * (before1 + carry), axis=-1, keepdims=True).astype(jnp.int32)
    rank2 = jnp.sum(oh2 * (before2 + carry + n1), axis=-1, keepdims=True).astype(jnp.int32)
    cnt_ref[...] = carry + n1 + n2
    count_ref[...] = (carry + n1 + n2).astype(jnp.int32)
    routing = jnp.where(lane == 0, x1, jnp.where(lane == 1, x2, jnp.where(
        lane == 2, rank1, jnp.where(lane == 3, rank2, 0))))
    eid_ref[...] = routing.T[:SUBLANES, :]


def out_router(x2, y_conv, y_rwkv, w_out, norm_ffn, rg_w, rg_b, re_w, re_b, *, tm=512):
    nt, d = x2.shape
    d_c, d_r = y_conv.shape[1], y_rwkv.shape[1]
    assert d_c == d_r and nt % tm == 0
    n_r = N_GROUPS + N_EXPERTS
    rw = jnp.zeros((d, LANES), F32).at[:, :n_r].set(jnp.concatenate([rg_w, re_w], axis=1).astype(F32))
    rw_hi = rw.astype(BF16)
    rw_lo = (rw - rw_hi.astype(F32)).astype(BF16)
    rb = jnp.zeros((1, LANES), F32).at[0, :n_r].set(jnp.concatenate([rg_b, re_b]).astype(F32))
    full = lambda shape: pl.BlockSpec(shape, lambda i: (0, 0))
    once = lambda shape, r: pl.BlockSpec(shape, lambda i, r=r: (r, 0), pipeline_mode=pl.Buffered(1))
    rows = lambda w: pl.BlockSpec((tm, w), lambda i: (i, 0))
    return pl.pallas_call(
        _out_router_kernel,
        grid=(nt // tm,),
        in_specs=[rows(d), rows(d_c), rows(d_r), once((d_c, d), 0), once((d_r, d), 1),
                  full((1, d)), full((d, LANES)), full((d, LANES)), full((1, LANES))],
        out_specs=[rows(d), rows(d // 2), pl.BlockSpec((SUBLANES, tm), lambda i: (0, i)), rows(LANES),
                   full((1, LANES))],
        out_shape=[jax.ShapeDtypeStruct((nt, d), F32), jax.ShapeDtypeStruct((nt, d // 2), jnp.uint32),
                   jax.ShapeDtypeStruct((SUBLANES, nt), jnp.int32), jax.ShapeDtypeStruct((nt, LANES), F32),
                   jax.ShapeDtypeStruct((1, LANES), jnp.int32)],
        scratch_shapes=[pltpu.VMEM((1, LANES), F32)],
        compiler_params=pltpu.CompilerParams(dimension_semantics=("arbitrary",), vmem_limit_bytes=VMEM_LIMIT),
        name="out_router",
    )(x2, y_conv, y_rwkv, w_out, w_out, norm_ffn.reshape(1, d).astype(F32), rw_hi, rw_lo, rb)


MOE_ROWS = 256


def slot_plan(routing, counts, n_rows_pad):
    expert_id = routing[:TOP_K]
    rank = routing[TOP_K:2 * TOP_K]
    padded = (counts + MOE_ROWS - 1) // MOE_ROWS * MOE_ROWS
    pend = jnp.cumsum(padded)
    pstarts = pend - padded
    ids = jnp.arange(N_EXPERTS, dtype=jnp.int32)
    start_of = jnp.sum(jnp.where(expert_id[..., None] == ids, pstarts, 0), axis=-1)
    dest = (start_of + rank).astype(jnp.int32).T
    n_blocks = n_rows_pad // MOE_ROWS
    block_row0 = jnp.arange(n_blocks, dtype=jnp.int32) * MOE_ROWS
    block_e = jnp.minimum(jnp.sum((pend[None, :] <= block_row0[:, None]).astype(jnp.int32), axis=1),
                          N_EXPERTS - 1).astype(jnp.int32)
    n_used = (pend[-1:] // MOE_ROWS).astype(jnp.int32)
    owner = jnp.where(counts > 0, ids, N_EXPERTS)
    nxt = lax.cummin(jnp.concatenate([owner[1:], jnp.full((1,), N_EXPERTS, jnp.int32)]), reverse=True)
    next_used = jnp.where(nxt < N_EXPERTS, nxt, -1).astype(jnp.int32)
    used_ord = (jnp.cumsum((counts > 0).astype(jnp.int32)) - 1).astype(jnp.int32)
    pad0 = pstarts + counts
    n_single = jnp.minimum((-pad0) % SUBLANES, pend - pad0)
    tile0 = pad0 + n_single
    n_tile = (pend - tile0) // SUBLANES
    tail = jnp.stack([pend[-1] // MOE_ROWS, n_blocks - pend[-1] // MOE_ROWS, jnp.sum(n_single), jnp.sum(n_tile)])
    fill_plan = jnp.concatenate([pad0, n_single, tile0, n_tile, tail]).astype(jnp.int32)
    return dest, block_e, n_used, next_used, used_ord, fill_plan


def _dispatch_kernel(fill_ref, dest_ref, v_ref, xs_ref, zero_ref, sem, fill_sem):
    tm = v_ref.shape[0]

    def fills():
        e0, e1, e2, e3, e4 = (k * N_EXPERTS for k in range(5))
        single = lambda e, i: pltpu.make_async_copy(
            zero_ref.at[pl.ds(0, 1), :], xs_ref.at[pl.ds(fill_ref[e0 + e] + i, 1), :], fill_sem)
        tile = lambda e, i: pltpu.make_async_copy(
            zero_ref.at[pl.ds(0, SUBLANES), :],
            xs_ref.at[pl.ds(pl.multiple_of(fill_ref[e2 + e] + i * SUBLANES, SUBLANES), SUBLANES), :], fill_sem)
        block = lambda i: pltpu.make_async_copy(
            zero_ref, xs_ref.at[pl.ds(pl.multiple_of((fill_ref[e4] + i) * MOE_ROWS, MOE_ROWS), MOE_ROWS), :],
            fill_sem)
        return single, tile, block, e1, e3, e4

    @pl.when(pl.program_id(0) == 0)
    def _():
        zero_ref[...] = jnp.zeros_like(zero_ref)
        single, tile, block, e1, e3, e4 = fills()
        for e in range(N_EXPERTS):
            lax.fori_loop(0, fill_ref[e1 + e], lambda i, c, e=e: (single(e, i).start(), c)[1], 0)
            lax.fori_loop(0, fill_ref[e3 + e], lambda i, c, e=e: (tile(e, i).start(), c)[1], 0)
        lax.fori_loop(0, fill_ref[e4 + 1], lambda i, c: (block(i).start(), c)[1], 0)

    def copy(i, k):
        d = dest_ref[0, 0, i * TOP_K + k]
        return pltpu.make_async_copy(v_ref.at[pl.ds(i, 1), :], xs_ref.at[pl.ds(d, 1), :], sem)

    def start(i, carry):
        for k in range(TOP_K):
            copy(i, k).start(priority=k)
        return carry

    lax.fori_loop(0, tm, start, 0, unroll=8)
    for k in range(TOP_K):
        pltpu.make_async_copy(v_ref, xs_ref.at[pl.ds(0, tm), :], sem).wait()

    @pl.when(pl.program_id(0) == 0)
    def _():
        single, tile, block, e1, e3, e4 = fills()
        lax.fori_loop(0, fill_ref[e4 + 2], lambda i, c: (single(0, 0).wait(), c)[1], 0)
        lax.fori_loop(0, fill_ref[e4 + 3], lambda i, c: (tile(0, 0).wait(), c)[1], 0)
        lax.fori_loop(0, fill_ref[e4 + 1], lambda i, c: (block(0).wait(), c)[1], 0)


def dispatch(v, dest, fill_plan, n_rows_pad, *, tm=1024):
    nt, d = v.shape
    assert nt % tm == 0
    dest3 = dest.reshape(nt // tm, 1, tm * TOP_K)
    grid_spec = pltpu.PrefetchScalarGridSpec(
        num_scalar_prefetch=1,
        grid=(nt // tm,),
        in_specs=[pl.BlockSpec((1, 1, tm * TOP_K), lambda i, fp: (i, 0, 0), memory_space=pltpu.SMEM),
                  pl.BlockSpec((tm, d), lambda i, fp: (i, 0))],
        out_specs=pl.BlockSpec(memory_space=pl.ANY),
        scratch_shapes=[pltpu.VMEM((MOE_ROWS, d), v.dtype), pltpu.SemaphoreType.DMA(()),
                        pltpu.SemaphoreType.DMA(())],
    )
    return pl.pallas_call(
        _dispatch_kernel,
        grid_spec=grid_spec,
        out_shape=jax.ShapeDtypeStruct((n_rows_pad, d), v.dtype),
        compiler_params=pltpu.CompilerParams(dimension_semantics=("arbitrary",), vmem_limit_bytes=VMEM_LIMIT),
        name="moe_dispatch",
    )(fill_plan, dest3, v)


def _experts_kernel(be_ref, nu_ref, nxt_ref, ord_ref, xs_ref, wg_hbm, wu_hbm, wd_hbm, ys_ref,
                    wg_f, wu_f, wd_f, wg_s, wu_s, wd_s, wsem):
    j = pl.program_id(0)
    e = be_ref[j]

    def fetch(ex, slot):
        return [pltpu.make_async_copy(wg_hbm.at[ex], wg_f.at[slot], wsem.at[slot]),
                pltpu.make_async_copy(wu_hbm.at[ex], wu_f.at[slot], wsem.at[slot]),
                pltpu.make_async_copy(wd_hbm.at[ex], wd_f.at[slot], wsem.at[slot])]

    @pl.when(j == 0)
    def _():
        for c in fetch(e, lax.rem(ord_ref[e], 2)):
            c.start(priority=1)

    first = (j == 0) | (e != be_ref[jnp.maximum(j - 1, 0)])

    @pl.when(first & (j < nu_ref[0]))
    def _():
        slot = lax.rem(ord_ref[e], 2)
        for c in fetch(e, slot):
            c.wait()
        nxt = nxt_ref[e]

        @pl.when(nxt >= 0)
        def _():
            for c in fetch(nxt, 1 - slot):
                c.start(priority=1)

        wg_s[...] = wg_f[slot].astype(BF16)
        wu_s[...] = wu_f[slot].astype(BF16)
        wd_s[...] = wd_f[slot].astype(BF16)

    @pl.when(j < nu_ref[0])
    def _():
        xb = _unpack_bf16_pairs(xs_ref[...]).astype(BF16)
        gt = jnp.dot(xb, wg_s[...], preferred_element_type=F32)
        up = jnp.dot(xb, wu_s[...], preferred_element_type=F32)
        hid = (gt * jax.nn.sigmoid(gt) * up).astype(BF16)
        ys_ref[...] = _pack_bf16_pairs(jnp.dot(hid, wd_s[...], preferred_element_type=F32))

    @pl.when(j >= nu_ref[0])
    def _():
        ys_ref[...] = jnp.zeros_like(ys_ref)


def experts(xs, block_e, n_used, next_used, used_ord, w_gate, w_up, w_down, n_rows_pad):
    dh = xs.shape[1]
    d, d_e = w_gate.shape[1], w_gate.shape[2]
    assert d == 2 * dh
    n_blocks = n_rows_pad // MOE_ROWS
    hbm = pl.BlockSpec(memory_space=pl.ANY)
    grid_spec = pltpu.PrefetchScalarGridSpec(
        num_scalar_prefetch=4,
        grid=(n_blocks,),
        in_specs=[pl.BlockSpec((MOE_ROWS, dh), lambda j, be, nu, nx, od: (jnp.minimum(j, nu[0] - 1), 0)),
                  hbm, hbm, hbm],
        out_specs=pl.BlockSpec((MOE_ROWS, dh), lambda j, be, nu, nx, od: (j, 0)),
        scratch_shapes=[pltpu.VMEM((2, d, d_e), F32), pltpu.VMEM((2, d, d_e), F32), pltpu.VMEM((2, d_e, d), F32),
                        pltpu.VMEM((d, d_e), BF16), pltpu.VMEM((d, d_e), BF16), pltpu.VMEM((d_e, d), BF16),
                        pltpu.SemaphoreType.DMA((2,))],
    )
    return pl.pallas_call(
        _experts_kernel,
        grid_spec=grid_spec,
        out_shape=jax.ShapeDtypeStruct((n_rows_pad, dh), jnp.uint32),
        compiler_params=pltpu.CompilerParams(dimension_semantics=("arbitrary",), vmem_limit_bytes=VMEM_LIMIT),
        name="moe_experts",
    )(block_e, n_used, next_used, used_ord, xs, w_gate, w_up, w_down)


def _combine_kernel(dest_ref, dest_next_ref, h_ref, gate_ref, nfin_ref, ys_ref, o_ref, ybuf, sem):
    step = pl.program_id(0)
    tm = h_ref.shape[0]
    slot = lax.rem(step, 2)

    def copy(d_ref, s, i, k):
        d = d_ref[0, 0, i * TOP_K + k]
        return pltpu.make_async_copy(ys_ref.at[pl.ds(d, 1), :], ybuf.at[s, k, pl.ds(i, 1), :], sem.at[s])

    def start_all(d_ref, s):
        def body(i, carry):
            for k in range(TOP_K):
                copy(d_ref, s, i, k).start(priority=k)
            return carry
        lax.fori_loop(0, tm, body, 0, unroll=8)

    @pl.when(step == 0)
    def _():
        start_all(dest_ref, 0)

    @pl.when(step + 1 < pl.num_programs(0))
    def _():
        start_all(dest_next_ref, 1 - slot)

    for k in range(TOP_K):
        pltpu.make_async_copy(ys_ref.at[pl.ds(0, tm), :], ybuf.at[slot, k], sem.at[slot]).wait()
    gates = gate_ref[...]
    moe = (gates[:, 0:1] * _unpack_bf16_pairs(ybuf[slot, 0])
           + gates[:, 1:2] * _unpack_bf16_pairs(ybuf[slot, 1]))
    o_ref[...] = _rms(h_ref[...] + moe, nfin_ref[...])


def combine(h, gates, dest, ys, norm_final, *, tm=512):
    nt, d = h.shape
    assert nt % tm == 0 and ys.shape[1] * 2 == d
    n_tiles = nt // tm
    dest3 = dest.reshape(n_tiles, 1, tm * TOP_K)
    return pl.pallas_call(
        _combine_kernel,
        grid=(n_tiles,),
        in_specs=[pl.BlockSpec((1, 1, tm * TOP_K), lambda i: (i, 0, 0), memory_space=pltpu.SMEM),
                  pl.BlockSpec((1, 1, tm * TOP_K), lambda i: (jnp.minimum(i + 1, n_tiles - 1), 0, 0),
                               memory_space=pltpu.SMEM),
                  pl.BlockSpec((tm, d), lambda i: (i, 0)),
                  pl.BlockSpec((tm, LANES), lambda i: (i, 0)),
                  pl.BlockSpec((1, d), lambda i: (0, 0)),
                  pl.BlockSpec(memory_space=pl.ANY)],
        out_specs=pl.BlockSpec((tm, d), lambda i: (i, 0)),
        out_shape=jax.ShapeDtypeStruct((nt, d), F32),
        scratch_shapes=[pltpu.VMEM((2, TOP_K, tm, d // 2), jnp.uint32), pltpu.SemaphoreType.DMA((2,))],
        compiler_params=pltpu.CompilerParams(dimension_semantics=("arbitrary",), vmem_limit_bytes=VMEM_LIMIT),
        name="moe_combine",
    )(dest3, dest3, h, gates, norm_final.reshape(1, d).astype(F32), ys)


def kernel(x, norm_mix, w_in, conv_dw, conv_b, conv_ln_g, conv_ln_b, shift_mu, w0, w_lora_up, a0, a_lora_up, g_lora_up, k_k, k_a, r_k, gn_g, gn_b, w_out, norm_ffn, router_group_w, router_group_b, router_expert_w, router_expert_b, expert_w_gate, expert_w_up, expert_w_down, norm_final):
    B, T, D = x.shape
    depth = w_in.shape[0]
    d_c = conv_dw.shape[2]
    d_r = w0.shape[1]
    d_main = 2 * d_c + 3 * d_r
    nt = B * T
    n_rows_pad = -(-(nt * TOP_K + N_EXPERTS * (MOE_ROWS - 1)) // MOE_ROWS) * MOE_ROWS
    h = x.reshape(nt, D)
    for l in range(depth):
        proj_main, proj_lora = in_proj(h, norm_mix[l], w_in[l], d_main)
        y_conv = conv_mixer(proj_main, conv_dw[l], conv_b[l], conv_ln_g[l], conv_ln_b[l], batch=B)
        y_rwkv = rwkv_mixer(proj_main, proj_lora, 2 * d_c, shift_mu[l], w0[l], w_lora_up[l], a0[l], a_lora_up[l],
                            g_lora_up[l], k_k[l], k_a[l], r_k[l].reshape(-1), gn_g[l], gn_b[l], batch=B)
        h, v, routing, gates, counts = out_router(h, y_conv, y_rwkv, w_out[l].astype(BF16), norm_ffn[l],
                                                  router_group_w[l], router_group_b[l], router_expert_w[l],
                                                  router_expert_b[l])
        dest, block_e, n_used, next_used, used_ord, fill_plan = slot_plan(routing, counts[0, :N_EXPERTS],
                                                                          n_rows_pad)
        xs = dispatch(v, dest, fill_plan, n_rows_pad)
        ys = experts(xs, block_e, n_used, next_used, used_ord, expert_w_gate[l], expert_w_up[l], expert_w_down[l],
                     n_rows_pad)
        assert depth == 1
        h = combine(h, gates, dest, ys, norm_final)
    return h.reshape(B, T, D)
```

```python
import functools

import jax
import jax.numpy as jnp
from jax import lax
from jax.experimental import pallas as pl
from jax.experimental.pallas import tpu as pltpu

F32 = jnp.float32
BF16 = jnp.bfloat16

RWKV_HEAD = 64
CONV_WIDTH = 31
N_GROUPS = 4
EXPERTS_PER_GROUP = 8
N_EXPERTS = N_GROUPS * EXPERTS_PER_GROUP
TOP_K = 2
RMS_EPS = 1e-6
LN_EPS = 1e-5
GN_EPS = 64e-5
LANES = 128
SUBLANES = 8
MXU_DIM = 256
VMEM_LIMIT = 56 * 1024 * 1024

_NT = (((1,), (1,)), ((), ()))
_TN = (((0,), (0,)), ((), ()))


def _dot(a, b, dims=None):
    a = a.astype(BF16)
    b = b.astype(BF16)
    if dims is None:
        return jnp.dot(a, b, preferred_element_type=F32)
    return lax.dot_general(a, b, dims, preferred_element_type=F32)


def _dot_split(x, m):
    hi = x.astype(BF16)
    lo = (x - hi.astype(F32)).astype(BF16)
    return (jnp.dot(hi, m, preferred_element_type=F32)
            + jnp.dot(lo, m, preferred_element_type=F32))


def _pack_bf16_pairs(x):
    w = x.shape[1] // 2
    lo = lax.bitcast_convert_type(x[:, :w].astype(BF16).astype(F32), jnp.uint32)
    hi = lax.bitcast_convert_type(x[:, w:].astype(BF16).astype(F32), jnp.uint32)
    return (lo >> 16) | (hi & jnp.uint32(0xFFFF0000))


def _unpack_bf16_pairs(p):
    lo = lax.bitcast_convert_type(p << 16, F32)
    hi = lax.bitcast_convert_type(p & jnp.uint32(0xFFFF0000), F32)
    return jnp.concatenate([lo, hi], axis=1)


def _rwkv_kernel(r_ref, k_ref, v_ref, lora_ref, mu_main_ref, mu_lora_ref, w0_ref, wup_ref, a0_ref, aup_ref,
                 gup_ref, kk_ref, ka_ref, rk_ref, gng_ref, gnb_ref, y_ref,
                 state_ref, prev_main_ref, prev_lora_ref, *, n_heads, d_decay, d_aaa, chunk):
    nb, Cb, d_r = r_ref.shape
    C = chunk
    CS = Cb // C
    N = RWKV_HEAD
    R = nb * Cb

    @pl.when(pl.program_id(0) == 0)
    def _():
        state_ref[...] = jnp.zeros_like(state_ref)
        prev_main_ref[...] = jnp.zeros_like(prev_main_ref)
        prev_lora_ref[...] = jnp.zeros_like(prev_lora_ref)

    row = lax.broadcasted_iota(jnp.int32, (R, 1), 0)

    def shifted(x_ref, prev_ref, j, mu):
        x = x_ref[...].reshape(R, x_ref.shape[2])
        xp = pltpu.roll(x, 1, 0)
        for b in range(nb):
            xp = jnp.where(row == b * Cb, prev_ref[b, j:j + 1, :], xp)
            prev_ref[b, j:j + 1, :] = x[(b + 1) * Cb - 1:(b + 1) * Cb, :]
        return x + (xp - x) * mu

    r = shifted(r_ref, prev_main_ref, 0, mu_main_ref[0:1, :])
    k = shifted(k_ref, prev_main_ref, 1, mu_main_ref[1:2, :])
    v = shifted(v_ref, prev_main_ref, 2, mu_main_ref[2:3, :])
    lo = shifted(lora_ref, prev_lora_ref, 0, mu_lora_ref[...])

    wd = lo[:, :d_decay]
    ad = lo[:, d_decay:d_decay + d_aaa]
    gd = lo[:, d_decay + d_aaa:]

    z = w0_ref[...] + _dot(jnp.tanh(wd), wup_ref[...])
    w = -(jnp.maximum(-z, 0.0) + jnp.log(1.0 + jnp.exp(-jnp.abs(z)))) - 0.5
    logd = -jnp.exp(w)
    a = jax.nn.sigmoid(a0_ref[...] + _dot(ad, aup_ref[...]))
    g = _dot(jax.nn.sigmoid(gd), gup_ref[...])

    li = lax.broadcasted_iota(jnp.int32, (MXU_DIM, MXU_DIM), 0) // N
    lj = lax.broadcasted_iota(jnp.int32, (MXU_DIM, MXU_DIM), 1) // N
    head_ones = (li == lj).astype(BF16)

    def head_sum(x):
        return jnp.concatenate(
            [_dot_split(x[:, c * MXU_DIM:(c + 1) * MXU_DIM], head_ones) for c in range(d_r // MXU_DIM)], axis=1)

    kkr = k * kk_ref[...]
    kk = kkr / jnp.maximum(jnp.sqrt(head_sum(kkr * kkr)), 1e-12)
    k2 = k * (1.0 + (a - 1.0) * ka_ref[...])
    a_s = -kk
    b_s = kk * a

    ri = lax.broadcasted_iota(jnp.int32, (R, R), 0)
    rj = lax.broadcasted_iota(jnp.int32, (R, R), 1)
    tri = ((rj <= ri) & ((ri // C) == (rj // C))).astype(BF16)
    hi = logd.astype(BF16)
    rem = logd - hi.astype(F32)
    mid = rem.astype(BF16)
    low = (rem - mid.astype(F32)).astype(BF16)
    cum = (jnp.dot(tri, hi, preferred_element_type=F32) + jnp.dot(tri, mid, preferred_element_type=F32)
           + jnp.dot(tri, low, preferred_element_type=F32))
    p_incl = jnp.exp(cum)
    p_excl = jnp.exp(cum - logd)
    p_inv = jnp.exp(-cum)
    p_last = [p_incl[(q + 1) * C - 1:(q + 1) * C, :] for q in range(nb * CS)]
    p_last_rows = jnp.concatenate([jnp.broadcast_to(p, (C, d_r)) for p in p_last], axis=0)

    at = (a_s * p_excl).astype(BF16)
    rt = (r * p_incl).astype(BF16)
    bt = b_s * p_inv
    kt = k2 * p_inv
    bh = (bt * p_last_rows).astype(BF16)
    kh = (kt * p_last_rows).astype(BF16)
    bt = bt.astype(BF16)
    kt = kt.astype(BF16)
    vb = v.astype(BF16)

    HG = LANES // N
    W = HG * N
    GC = HG * C
    wi = lax.broadcasted_iota(jnp.int32, (GC, W), 0)
    wj = lax.broadcasted_iota(jnp.int32, (GC, W), 1)
    bd_on = ((wi // C) == (wj // N)).astype(BF16) > 0

    def bdiag(x):
        xb = x.astype(BF16)
        return jnp.where(bd_on, jnp.concatenate([xb] * HG, axis=0), jnp.zeros((), BF16))

    ti = lax.broadcasted_iota(jnp.int32, (2 * C, GC), 0)
    tj = lax.broadcasted_iota(jnp.int32, (2 * C, GC), 1) % C
    mask2 = ((ti < C) & (tj < ti)) | ((ti >= C) & (tj <= ti - C))
    li = lax.broadcasted_iota(jnp.int32, (C, GC), 0)
    lj = lax.broadcasted_iota(jnp.int32, (C, GC), 1) % C
    eye = (li == lj).astype(F32)
    level_masks = []
    s = 1
    while s < C:
        level_masks.append(((li // (2 * s)) == (lj // (2 * s))) & (((li // s) % 2) == 1) & (((lj // s) % 2) == 0))
        s *= 2
    colblk = lax.broadcasted_iota(jnp.int32, (N, W), 1) // N

    n_g = n_heads // HG
    units = [(q, g) for q in range(nb * CS) for g in range(n_g)]
    idx = range(len(units))
    rs = [slice(q * C, (q + 1) * C) for q, _ in units]
    ls = [slice(g * W, (g + 1) * W) for _, g in units]
    g_b = [jnp.where(mask2, _dot(jnp.concatenate([at[rs[i], ls[i]], rt[rs[i], ls[i]]], axis=0),
                                 bdiag(bt[rs[i], ls[i]]), _NT), 0.0) for i in idx]
    g_k = [jnp.where(mask2, _dot(jnp.concatenate([at[rs[i], ls[i]], rt[rs[i], ls[i]]], axis=0),
                                 bdiag(kt[rs[i], ls[i]]), _NT), 0.0) for i in idx]
    gv = [_dot(g_k[i], bdiag(vb[rs[i], ls[i]])) for i in idx]
    tinv = [eye + jnp.where(level_masks[0], g_b[i][:C], 0.0) for i in idx]
    for lm in level_masks[1:]:
        xs = [_dot(jnp.where(lm, g_b[i][:C], 0.0), bdiag(tinv[i])) for i in idx]
        tinv = [tinv[i] + _dot(tinv[i], bdiag(xs[i])) for i in idx]
    ta = [_dot(tinv[i], bdiag(at[rs[i], ls[i]])) for i in idx]
    tv = [_dot(tinv[i], bdiag(gv[i][:C])) for i in idx]
    ys = [None] * len(units)
    for b in range(nb):
        state = [state_ref[b * n_g + gi] for gi in range(n_g)]
        for cc in range(CS):
            ids = [(b * CS + cc) * n_g + gi for gi in range(n_g)]
            tr = [_dot(jnp.concatenate([ta[i].astype(BF16), rt[rs[i], ls[i]]], axis=0), bdiag(state[gi]), _NT)
                  for gi, i in enumerate(ids)]
            u = [tr[gi][:C] + tv[i] for gi, i in enumerate(ids)]
            for gi, i in enumerate(ids):
                ys[i] = tr[gi][C:] + gv[i][C:] + _dot(g_b[i][C:], bdiag(u[gi]))
                uv = jnp.concatenate([u[gi].astype(BF16), vb[rs[i], ls[i]]], axis=0)
                bk = jnp.concatenate([bh[rs[i], ls[i]], kh[rs[i], ls[i]]], axis=0)
                full = _dot(uv, bk, _TN)
                upd = state[gi] * p_last[units[i][0]][:, ls[i]]
                for h in range(HG):
                    upd = upd + jnp.where(colblk == h, full[h * N:(h + 1) * N, :], 0.0)
                state[gi] = upd
        for gi in range(n_g):
            state_ref[b * n_g + gi] = state[gi]
    y = jnp.concatenate([jnp.concatenate(ys[q * n_g:(q + 1) * n_g], axis=1) for q in range(nb * CS)],
                        axis=0)

    inv_n = 1.0 / N
    mu_y = head_sum(y) * inv_n
    yc = y - mu_y
    var_y = head_sum(yc * yc) * inv_n
    yn = yc * lax.rsqrt(var_y + GN_EPS) * gng_ref[...] + gnb_ref[...]
    bonus = head_sum(r * k2 * rk_ref[...]) * v
    y_ref[...] = ((yn + bonus) * g).reshape(nb, Cb, d_r).astype(y_ref.dtype)


def rwkv_mixer(proj_main, proj_lora, col0, shift_mu, w0, w_up, a0, a_up, g_up, k_k, k_a, r_k, gn_g, gn_b,
               *, batch, chunk=64, chunks_per_step=2):
    nt = proj_main.shape[0]
    seq = nt // batch
    d_r = w0.shape[-1]
    n_heads = d_r // RWKV_HEAD
    d_lora = proj_lora.shape[1]
    d_decay, d_aaa = w_up.shape[0], a_up.shape[0]
    rows = chunk * chunks_per_step
    n_steps = seq // rows
    assert seq % rows == 0 and col0 % d_r == 0
    cb = col0 // d_r
    row2 = lambda t: t.reshape(1, -1).astype(F32)
    mu_main = shift_mu[:3 * d_r].reshape(3, d_r)
    mu_lora = shift_mu[3 * d_r:].reshape(1, d_lora)
    pm3 = proj_main.reshape(batch, seq, proj_main.shape[1])
    pl3 = proj_lora.reshape(batch, seq, d_lora)

    def main_spec(j):
        return pl.BlockSpec((batch, rows, d_r), lambda c, j=j: (0, c, cb + j))

    full = lambda shape: pl.BlockSpec(shape, lambda c: (0,) * len(shape))
    kern = functools.partial(_rwkv_kernel, n_heads=n_heads, d_decay=d_decay, d_aaa=d_aaa, chunk=chunk)
    y = pl.pallas_call(
        kern,
        grid=(n_steps,),
        in_specs=[main_spec(0), main_spec(1), main_spec(2),
                  pl.BlockSpec((batch, rows, d_lora), lambda c: (0, c, 0)),
                  full((3, d_r)), full((1, d_lora)), full((1, d_r)), full((d_decay, d_r)), full((1, d_r)),
                  full((d_aaa, d_r)), full((g_up.shape[0], d_r)), full((1, d_r)), full((1, d_r)), full((1, d_r)),
                  full((1, d_r)), full((1, d_r))],
        out_specs=pl.BlockSpec((batch, rows, d_r), lambda c: (0, c, 0)),
        out_shape=jax.ShapeDtypeStruct((batch, seq, d_r), BF16),
        scratch_shapes=[pltpu.VMEM((batch * n_heads * RWKV_HEAD // LANES, RWKV_HEAD, LANES), F32),
                        pltpu.VMEM((batch, 3, d_r), F32),
                        pltpu.VMEM((batch, 1, d_lora), F32)],
        compiler_params=pltpu.CompilerParams(dimension_semantics=("arbitrary",), vmem_limit_bytes=VMEM_LIMIT),
        name="rwkv_mixer",
    )(pm3, pm3, pm3, pl3, mu_main, mu_lora, row2(w0), w_up.astype(BF16), row2(a0),
      a_up.astype(BF16), g_up.astype(BF16), row2(k_k), row2(k_a), row2(r_k), row2(gn_g), row2(gn_b))
    return y.reshape(nt, d_r)


def _rms(x, g):
    return x * lax.rsqrt(jnp.mean(x * x, axis=-1, keepdims=True) + RMS_EPS) * g


def _inproj_kernel(x_ref, g_ref, wt_hbm, wlt_ref, o_ref, ol_ref, u_ref, w_f, wsem):
    i, j = pl.program_id(0), pl.program_id(1)
    n_i, n_j = pl.num_programs(0), pl.num_programs(1)
    tn = w_f.shape[1]
    step = i * n_j + j
    slot = lax.rem(step, 2)

    def fetch(jt, s):
        return pltpu.make_async_copy(wt_hbm.at[pl.ds(pl.multiple_of(jt * tn, tn), tn), :], w_f.at[s], wsem.at[s])

    @pl.when(step == 0)
    def _():
        fetch(0, 0).start(priority=1)

    @pl.when(step + 1 < n_i * n_j)
    def _():
        fetch(lax.rem(j + 1, n_j), 1 - slot).start(priority=1)

    @pl.when(j == 0)
    def _():
        u = _rms(x_ref[...], g_ref[...]).astype(BF16)
        u_ref[...] = u
        ol_ref[...] = lax.dot_general(u, wlt_ref[...], _NT, preferred_element_type=F32)

    fetch(j, slot).wait()
    o_ref[...] = lax.dot_general(u_ref[...], w_f[slot].astype(BF16), _NT, preferred_element_type=F32)


def in_proj(x2, norm_g, w_in, d_main, *, tm=1024, tn=1024):
    nt, d = x2.shape
    d_lora = w_in.shape[1] - d_main
    assert nt % tm == 0 and d_main % tn == 0
    w_t = w_in.T
    w_lora_t = w_t[d_main:].astype(BF16)
    return pl.pallas_call(
        _inproj_kernel,
        grid=(nt // tm, d_main // tn),
        in_specs=[pl.BlockSpec((tm, d), lambda i, j: (i, 0)),
                  pl.BlockSpec((1, d), lambda i, j: (0, 0)),
                  pl.BlockSpec(memory_space=pl.ANY),
                  pl.BlockSpec((d_lora, d), lambda i, j: (0, 0))],
        out_specs=[pl.BlockSpec((tm, tn), lambda i, j: (i, j)),
                   pl.BlockSpec((tm, d_lora), lambda i, j: (i, 0))],
        out_shape=[jax.ShapeDtypeStruct((nt, d_main), F32), jax.ShapeDtypeStruct((nt, d_lora), F32)],
        scratch_shapes=[pltpu.VMEM((tm, d), BF16), pltpu.VMEM((2, tn, d), F32), pltpu.SemaphoreType.DMA((2,))],
        compiler_params=pltpu.CompilerParams(dimension_semantics=("arbitrary", "arbitrary"),
                                             vmem_limit_bytes=VMEM_LIMIT),
        name="in_proj",
    )(x2, norm_g.reshape(1, d).astype(F32), w_t, w_lora_t)


CONV_HALO = 32
CONV_ROWS = 32


def _conv_kernel(val_ref, gate_ref, dw_ref, b_ref, g_ref, beta_ref, o_ref, ubuf_ref):
    tt = val_ref.shape[0]
    d_c = val_ref.shape[1]
    t_idx = pl.program_id(1)

    @pl.when(t_idx == 0)
    def _():
        ubuf_ref[0, 0:CONV_HALO, :] = jnp.zeros((CONV_HALO, d_c), F32)

    @pl.when(t_idx > 0)
    def _():
        ubuf_ref[0, 0:CONV_HALO, :] = ubuf_ref[0, tt:tt + CONV_HALO, :]

    ubuf_ref[0, CONV_HALO:CONV_HALO + tt, :] = val_ref[...] * jax.nn.sigmoid(gate_ref[...])
    n_sh = tt + CONV_HALO - SUBLANES
    for s in range(1, SUBLANES):
        for r in range(0, n_sh, CONV_ROWS):
            n = min(CONV_ROWS, n_sh - r)
            ubuf_ref[s, r:r + n, :] = ubuf_ref[0, r + s:r + s + n, :]
    dw = dw_ref[...]
    lead = CONV_HALO - (CONV_WIDTH - 1)
    for c in range(tt // CONV_ROWS):
        r0 = c * CONV_ROWS
        acc = jnp.zeros((CONV_ROWS, d_c), F32)
        for j in range(CONV_WIDTH):
            q, s = divmod(lead + j, SUBLANES)
            r = r0 + q * SUBLANES
            acc = acc + dw[j:j + 1, :] * ubuf_ref[s, r:r + CONV_ROWS, :]
        acc = acc + b_ref[...]
        mu = jnp.mean(acc, axis=-1, keepdims=True)
        cen = acc - mu
        var = jnp.mean(cen * cen, axis=-1, keepdims=True)
        yv = cen * lax.rsqrt(var + LN_EPS) * g_ref[...] + beta_ref[...]
        o_ref[r0:r0 + CONV_ROWS, :] = (yv * jax.nn.sigmoid(yv)).astype(o_ref.dtype)


def conv_mixer(proj_main, dw, bias, ln_g, ln_b, *, batch, tt=256):
    nt = proj_main.shape[0]
    seq = nt // batch
    d_c = dw.shape[1]
    n_t = seq // tt
    assert seq % tt == 0 and tt % CONV_ROWS == 0
    row = lambda t: t.reshape(1, d_c).astype(F32)
    full = lambda shape: pl.BlockSpec(shape, lambda b, t: (0, 0))
    return pl.pallas_call(
        _conv_kernel,
        grid=(batch, n_t),
        in_specs=[pl.BlockSpec((tt, d_c), lambda b, t: (b * n_t + t, 0)),
                  pl.BlockSpec((tt, d_c), lambda b, t: (b * n_t + t, 1)),
                  full((CONV_WIDTH, d_c)), full((1, d_c)), full((1, d_c)), full((1, d_c))],
        out_specs=pl.BlockSpec((tt, d_c), lambda b, t: (b * n_t + t, 0)),
        out_shape=jax.ShapeDtypeStruct((nt, d_c), BF16),
        scratch_shapes=[pltpu.VMEM((SUBLANES, tt + CONV_HALO, d_c), F32)],
        compiler_params=pltpu.CompilerParams(dimension_semantics=("arbitrary", "arbitrary"),
                                             vmem_limit_bytes=VMEM_LIMIT),
        name="conv_mixer",
    )(proj_main, proj_main, dw.astype(F32), row(bias), row(ln_g), row(ln_b))


def _out_router_kernel(x_ref, yc_ref, yr_ref, wc_ref, wr_ref, nf_ref, rw_hi_ref, rw_lo_ref, rb_ref,
                       h_ref, v_ref, eid_ref, gate_ref, count_ref, cnt_ref):
    tm = x_ref.shape[0]
    h = (x_ref[...] + jnp.dot(yc_ref[...], wc_ref[...], preferred_element_type=F32)
         + jnp.dot(yr_ref[...], wr_ref[...], preferred_element_type=F32))
    h_ref[...] = h

    @pl.when(pl.program_id(0) == 0)
    def _():
        cnt_ref[...] = jnp.zeros_like(cnt_ref)

    v = _rms(h, nf_ref[...])
    v_ref[...] = _pack_bf16_pairs(v)
    v_hi = v.astype(BF16)
    v_lo = (v - v_hi.astype(F32)).astype(BF16)
    logits = (jnp.dot(v_hi, rw_hi_ref[...], preferred_element_type=F32)
              + jnp.dot(v_lo, rw_hi_ref[...], preferred_element_type=F32)
              + jnp.dot(v_hi, rw_lo_ref[...], preferred_element_type=F32)) + rb_ref[...]
    lane = lax.broadcasted_iota(jnp.int32, (tm, LANES), 1)
    neg = jnp.float32(-jnp.inf)
    big = jnp.int32(LANES)

    def first_max(vals):
        m = jnp.max(vals, axis=-1, keepdims=True)
        return m, jnp.min(jnp.where(vals == m, lane, big), axis=-1, keepdims=True)

    gl = jnp.where(lane < N_GROUPS, logits, neg)
    gmax, grp = first_max(gl)
    p_grp = 1.0 / jnp.sum(jnp.exp(gl - gmax), axis=-1, keepdims=True)
    lo = N_GROUPS + grp * EXPERTS_PER_GROUP
    el = jnp.where((lane >= lo) & (lane < lo + EXPERTS_PER_GROUP), logits, neg)
    m1, i1 = first_max(el)
    m2, i2 = first_max(jnp.where(lane == i1, neg, el))
    e2 = jnp.exp(m2 - m1)
    g1 = p_grp / (1.0 + e2)
    g2 = p_grp * e2 / (1.0 + e2)
    gate_ref[...] = jnp.where(lane == 0, g1, jnp.where(lane == 1, g2, 0.0))

    x1 = i1 - N_GROUPS
    x2 = i2 - N_GROUPS
    oh1 = (lane == x1).astype(F32)
    oh2 = (lane == x2).astype(F32)
    earlier = (lax.broadcasted_iota(jnp.int32, (tm, tm), 1)
               < lax.broadcasted_iota(jnp.int32, (tm, tm), 0)).astype(BF16)
    before1 = jnp.dot(earlier, oh1.astype(BF16), preferred_element_type=F32)
    before2 = jnp.dot(earlier, oh2.astype(BF16), preferred_element_type=F32)
    carry = cnt_ref[...]
    n1 = jnp.sum(oh1, axis=0, keepdims=True)
    n2 = jnp.sum(oh2, axis=0, keepdims=True)
    rank1 = jnp.sum(oh1 * (before1 + carry), axis=-1, keepdims=True).astype(jnp.int32)
    rank2 = jnp.sum(oh2 * (before2 + carry + n1), axis=-1, keepdims=True).astype(jnp.int32)
    cnt_ref[...] = carry + n1 + n2
    count_ref[...] = (carry + n1 + n2).astype(jnp.int32)
    routing = jnp.where(lane == 0, x1, jnp.where(lane == 1, x2, jnp.where(
        lane == 2, rank1, jnp.where(lane == 3, rank2, 0))))
    eid_ref[...] = routing.T[:SUBLANES, :]


def out_router(x2, y_conv, y_rwkv, w_out, norm_ffn, rg_w, rg_b, re_w, re_b, *, tm=512):
    nt, d = x2.shape
    d_c, d_r = y_conv.shape[1], y_rwkv.shape[1]
    assert d_c == d_r and nt % tm == 0
    n_r = N_GROUPS + N_EXPERTS
    rw = jnp.zeros((d, LANES), F32).at[:, :n_r].set(jnp.concatenate([rg_w, re_w], axis=1).astype(F32))
    rw_hi = rw.astype(BF16)
    rw_lo = (rw - rw_hi.astype(F32)).astype(BF16)
    rb = jnp.zeros((1, LANES), F32).at[0, :n_r].set(jnp.concatenate([rg_b, re_b]).astype(F32))
    full = lambda shape: pl.BlockSpec(shape, lambda i: (0, 0))
    once = lambda shape, r: pl.BlockSpec(shape, lambda i, r=r: (r, 0), pipeline_mode=pl.Buffered(1))
    rows = lambda w: pl.BlockSpec((tm, w), lambda i: (i, 0))
    return pl.pallas_call(
        _out_router_kernel,
        grid=(nt // tm,),
        in_specs=[rows(d), rows(d_c), rows(d_r), once((d_c, d), 0), once((d_r, d), 1),
                  full((1, d)), full((d, LANES)), full((d, LANES)), full((1, LANES))],
        out_specs=[rows(d), rows(d // 2), pl.BlockSpec((SUBLANES, tm), lambda i: (0, i)), rows(LANES),
                   full((1, LANES))],
        out_shape=[jax.ShapeDtypeStruct((nt, d), F32), jax.ShapeDtypeStruct((nt, d // 2), jnp.uint32),
                   jax.ShapeDtypeStruct((SUBLANES, nt), jnp.int32), jax.ShapeDtypeStruct((nt, LANES), F32),
                   jax.ShapeDtypeStruct((1, LANES), jnp.int32)],
        scratch_shapes=[pltpu.VMEM((1, LANES), F32)],
        compiler_params=pltpu.CompilerParams(dimension_semantics=("arbitrary",), vmem_limit_bytes=VMEM_LIMIT),
        name="out_router",
    )(x2, y_conv, y_rwkv, w_out, w_out, norm_ffn.reshape(1, d).astype(F32), rw_hi, rw_lo, rb)


MOE_ROWS = 256
W_SLOTS = 3


def slot_plan(routing, counts, n_rows_pad):
    expert_id = routing[:TOP_K]
    rank = routing[TOP_K:2 * TOP_K]
    padded = (counts + MOE_ROWS - 1) // MOE_ROWS * MOE_ROWS
    pend = jnp.cumsum(padded)
    pstarts = pend - padded
    ids = jnp.arange(N_EXPERTS, dtype=jnp.int32)
    start_of = jnp.sum(jnp.where(expert_id[..., None] == ids, pstarts, 0), axis=-1)
    dest = (start_of + rank).astype(jnp.int32).T
    n_blocks = n_rows_pad // MOE_ROWS
    block_row0 = jnp.arange(n_blocks, dtype=jnp.int32) * MOE_ROWS
    block_e = jnp.minimum(jnp.sum((pend[None, :] <= block_row0[:, None]).astype(jnp.int32), axis=1),
                          N_EXPERTS - 1).astype(jnp.int32)
    n_used = (pend[-1:] // MOE_ROWS).astype(jnp.int32)
    owner = jnp.where(counts > 0, ids, N_EXPERTS)
    nxt = lax.cummin(jnp.concatenate([owner[1:], jnp.full((1,), N_EXPERTS, jnp.int32)]), reverse=True)
    next_used = jnp.where(nxt < N_EXPERTS, nxt, -1).astype(jnp.int32)
    hops = [next_used]
    for _ in range(W_SLOTS - 2):
        hops.append(jnp.where(hops[-1] >= 0, next_used[jnp.maximum(hops[-1], 0)], -1))
    next_used = jnp.stack(hops)
    used_ord = (jnp.cumsum((counts > 0).astype(jnp.int32)) - 1).astype(jnp.int32)
    pad0 = pstarts + counts
    n_single = jnp.minimum((-pad0) % SUBLANES, pend - pad0)
    tile0 = pad0 + n_single
    n_tile = (pend - tile0) // SUBLANES
    tail = jnp.stack([pend[-1] // MOE_ROWS, n_blocks - pend[-1] // MOE_ROWS, jnp.sum(n_single), jnp.sum(n_tile)])
    fill_plan = jnp.concatenate([pad0, n_single, tile0, n_tile, tail]).astype(jnp.int32)
    return dest, block_e, n_used, next_used, used_ord, fill_plan


def _dispatch_kernel(fill_ref, dest_ref, v_ref, xs_ref, zero_ref, sem, fill_sem):
    tm = v_ref.shape[0]

    def fills():
        e0, e1, e2, e3, e4 = (k * N_EXPERTS for k in range(5))
        single = lambda e, i: pltpu.make_async_copy(
            zero_ref.at[pl.ds(0, 1), :], xs_ref.at[pl.ds(fill_ref[e0 + e] + i, 1), :], fill_sem)
        tile = lambda e, i: pltpu.make_async_copy(
            zero_ref.at[pl.ds(0, SUBLANES), :],
            xs_ref.at[pl.ds(pl.multiple_of(fill_ref[e2 + e] + i * SUBLANES, SUBLANES), SUBLANES), :], fill_sem)
        block = lambda i: pltpu.make_async_copy(
            zero_ref, xs_ref.at[pl.ds(pl.multiple_of((fill_ref[e4] + i) * MOE_ROWS, MOE_ROWS), MOE_ROWS), :],
            fill_sem)
        return single, tile, block, e1, e3, e4

    @pl.when(pl.program_id(0) == 0)
    def _():
        zero_ref[...] = jnp.zeros_like(zero_ref)
        single, tile, block, e1, e3, e4 = fills()
        for e in range(N_EXPERTS):
            lax.fori_loop(0, fill_ref[e1 + e], lambda i, c, e=e: (single(e, i).start(), c)[1], 0)
            lax.fori_loop(0, fill_ref[e3 + e], lambda i, c, e=e: (tile(e, i).start(), c)[1], 0)
        lax.fori_loop(0, fill_ref[e4 + 1], lambda i, c: (block(i).start(), c)[1], 0)

    def copy(i, k):
        d = dest_ref[0, 0, i * TOP_K + k]
        return pltpu.make_async_copy(v_ref.at[pl.ds(i, 1), :], xs_ref.at[pl.ds(d, 1), :], sem)

    def start(i, carry):
        for k in range(TOP_K):
            copy(i, k).start(priority=k)
        return carry

    lax.fori_loop(0, tm, start, 0, unroll=8)
    for k in range(TOP_K):
        pltpu.make_async_copy(v_ref, xs_ref.at[pl.ds(0, tm), :], sem).wait()

    @pl.when(pl.program_id(0) == 0)
    def _():
        single, tile, block, e1, e3, e4 = fills()
        lax.fori_loop(0, fill_ref[e4 + 2], lambda i, c: (single(0, 0).wait(), c)[1], 0)
        lax.fori_loop(0, fill_ref[e4 + 3], lambda i, c: (tile(0, 0).wait(), c)[1], 0)
        lax.fori_loop(0, fill_ref[e4 + 1], lambda i, c: (block(0).wait(), c)[1], 0)


def dispatch(v, dest, fill_plan, n_rows_pad, *, tm=1024):
    nt, d = v.shape
    assert nt % tm == 0
    dest3 = dest.reshape(nt // tm, 1, tm * TOP_K)
    grid_spec = pltpu.PrefetchScalarGridSpec(
        num_scalar_prefetch=1,
        grid=(nt // tm,),
        in_specs=[pl.BlockSpec((1, 1, tm * TOP_K), lambda i, fp: (i, 0, 0), memory_space=pltpu.SMEM),
                  pl.BlockSpec((tm, d), lambda i, fp: (i, 0))],
        out_specs=pl.BlockSpec(memory_space=pl.ANY),
        scratch_shapes=[pltpu.VMEM((MOE_ROWS, d), v.dtype), pltpu.SemaphoreType.DMA(()),
                        pltpu.SemaphoreType.DMA(())],
    )
    return pl.pallas_call(
        _dispatch_kernel,
        grid_spec=grid_spec,
        out_shape=jax.ShapeDtypeStruct((n_rows_pad, d), v.dtype),
        compiler_params=pltpu.CompilerParams(dimension_semantics=("arbitrary",), vmem_limit_bytes=VMEM_LIMIT),
        name="moe_dispatch",
    )(fill_plan, dest3, v)


def _experts_kernel(be_ref, nu_ref, nxt_ref, ord_ref, xs_ref, wg_hbm, wu_hbm, wd_hbm, ys_ref,
                    wg_f, wu_f, wd_f, wg_s, wu_s, wd_s, wsem):
    j = pl.program_id(0)
    e = be_ref[j]
    ahead = W_SLOTS - 1

    def fetch(ex, slot):
        return [pltpu.make_async_copy(wg_hbm.at[ex], wg_f.at[slot], wsem.at[slot]),
                pltpu.make_async_copy(wu_hbm.at[ex], wu_f.at[slot], wsem.at[slot]),
                pltpu.make_async_copy(wd_hbm.at[ex], wd_f.at[slot], wsem.at[slot])]

    @pl.when(j == 0)
    def _():
        ex = e
        for k in range(ahead):
            @pl.when(ex >= 0)
            def _(ex=ex):
                for c in fetch(ex, lax.rem(ord_ref[ex], W_SLOTS)):
                    c.start(priority=1)
            ex = jnp.where(ex >= 0, nxt_ref[0, jnp.maximum(ex, 0)], -1)

    first = (j == 0) | (e != be_ref[jnp.maximum(j - 1, 0)])

    @pl.when(first & (j < nu_ref[0]))
    def _():
        slot = lax.rem(ord_ref[e], W_SLOTS)
        for c in fetch(e, slot):
            c.wait()
        nxt = nxt_ref[ahead - 1, e]

        @pl.when(nxt >= 0)
        def _():
            for c in fetch(nxt, lax.rem(ord_ref[nxt], W_SLOTS)):
                c.start(priority=1)

        wg_s[...] = wg_f[slot].astype(BF16)
        wu_s[...] = wu_f[slot].astype(BF16)
        wd_s[...] = wd_f[slot].astype(BF16)

    @pl.when(j < nu_ref[0])
    def _():
        xb = _unpack_bf16_pairs(xs_ref[...]).astype(BF16)
        gt = jnp.dot(xb, wg_s[...], preferred_element_type=F32)
        up = jnp.dot(xb, wu_s[...], preferred_element_type=F32)
        hid = (gt * jax.nn.sigmoid(gt) * up).astype(BF16)
        ys_ref[...] = _pack_bf16_pairs(jnp.dot(hid, wd_s[...], preferred_element_type=F32))

    @pl.when(j >= nu_ref[0])
    def _():
        ys_ref[...] = jnp.zeros_like(ys_ref)


def experts(xs, block_e, n_used, next_used, used_ord, w_gate, w_up, w_down, n_rows_pad):
    dh = xs.shape[1]
    d, d_e = w_gate.shape[1], w_gate.shape[2]
    assert d == 2 * dh
    n_blocks = n_rows_pad // MOE_ROWS
    hbm = pl.BlockSpec(memory_space=pl.ANY)
    grid_spec = pltpu.PrefetchScalarGridSpec(
        num_scalar_prefetch=4,
        grid=(n_blocks,),
        in_specs=[pl.BlockSpec((MOE_ROWS, dh), lambda j, be, nu, nx, od: (jnp.minimum(j, nu[0] - 1), 0)),
                  hbm, hbm, hbm],
        out_specs=pl.BlockSpec((MOE_ROWS, dh), lambda j, be, nu, nx, od: (j, 0)),
        scratch_shapes=[pltpu.VMEM((W_SLOTS, d, d_e), F32), pltpu.VMEM((W_SLOTS, d, d_e), F32),
                        pltpu.VMEM((W_SLOTS, d_e, d), F32),
                        pltpu.VMEM((d, d_e), BF16), pltpu.VMEM((d, d_e), BF16), pltpu.VMEM((d_e, d), BF16),
                        pltpu.SemaphoreType.DMA((W_SLOTS,))],
    )
    return pl.pallas_call(
        _experts_kernel,
        grid_spec=grid_spec,
        out_shape=jax.ShapeDtypeStruct((n_rows_pad, dh), jnp.uint32),
        compiler_params=pltpu.CompilerParams(dimension_semantics=("arbitrary",), vmem_limit_bytes=VMEM_LIMIT),
        name="moe_experts",
    )(block_e, n_used, next_used, used_ord, xs, w_gate, w_up, w_down)


def _combine_kernel(dest_ref, dest_next_ref, h_ref, gate_ref, nfin_ref, ys_ref, o_ref, ybuf, sem):
    step = pl.program_id(0)
    tm = h_ref.shape[0]
    slot = lax.rem(step, 2)

    def copy(d_ref, s, i, k):
        d = d_ref[0, 0, i * TOP_K + k]
        return pltpu.make_async_copy(ys_ref.at[pl.ds(d, 1), :], ybuf.at[s, k, pl.ds(i, 1), :], sem.at[s])

    def start_all(d_ref, s):
        def body(i, carry):
            for k in range(TOP_K):
                copy(d_ref, s, i, k).start(priority=k)
            return carry
        lax.fori_loop(0, tm, body, 0, unroll=8)

    @pl.when(step == 0)
    def _():
        start_all(dest_ref, 0)

    @pl.when(step + 1 < pl.num_programs(0))
    def _():
        start_all(dest_next_ref, 1 - slot)

    for k in range(TOP_K):
        pltpu.make_async_copy(ys_ref.at[pl.ds(0, tm), :], ybuf.at[slot, k], sem.at[slot]).wait()
    gates = gate_ref[...]
    moe = (gates[:, 0:1] * _unpack_bf16_pairs(ybuf[slot, 0])
           + gates[:, 1:2] * _unpack_bf16_pairs(ybuf[slot, 1]))
    o_ref[...] = _rms(h_ref[...] + moe, nfin_ref[...])


def combine(h, gates, dest, ys, norm_final, *, tm=512):
    nt, d = h.shape
    assert nt % tm == 0 and ys.shape[1] * 2 == d
    n_tiles = nt // tm
    dest3 = dest.reshape(n_tiles, 1, tm * TOP_K)
    return pl.pallas_call(
        _combine_kernel,
        grid=(n_tiles,),
        in_specs=[pl.BlockSpec((1, 1, tm * TOP_K), lambda i: (i, 0, 0), memory_space=pltpu.SMEM),
                  pl.BlockSpec((1, 1, tm * TOP_K), lambda i: (jnp.minimum(i + 1, n_tiles - 1), 0, 0),
                               memory_space=pltpu.SMEM),
                  pl.BlockSpec((tm, d), lambda i: (i, 0)),
                  pl.BlockSpec((tm, LANES), lambda i: (i, 0)),
                  pl.BlockSpec((1, d), lambda i: (0, 0)),
                  pl.BlockSpec(memory_space=pl.ANY)],
        out_specs=pl.BlockSpec((tm, d), lambda i: (i, 0)),
        out_shape=jax.ShapeDtypeStruct((nt, d), F32),
        scratch_shapes=[pltpu.VMEM((2, TOP_K, tm, d // 2), jnp.uint32), pltpu.SemaphoreType.DMA((2,))],
        compiler_params=pltpu.CompilerParams(dimension_semantics=("arbitrary",), vmem_limit_bytes=VMEM_LIMIT),
        name="moe_combine",
    )(dest3, dest3, h, gates, norm_final.reshape(1, d).astype(F32), ys)


def kernel(x, norm_mix, w_in, conv_dw, conv_b, conv_ln_g, conv_ln_b, shift_mu, w0, w_lora_up, a0, a_lora_up, g_lora_up, k_k, k_a, r_k, gn_g, gn_b, w_out, norm_ffn, router_group_w, router_group_b, router_expert_w, router_expert_b, expert_w_gate, expert_w_up, expert_w_down, norm_final):
    B, T, D = x.shape
    depth = w_in.shape[0]
    d_c = conv_dw.shape[2]
    d_r = w0.shape[1]
    d_main = 2 * d_c + 3 * d_r
    nt = B * T
    n_rows_pad = -(-(nt * TOP_K + N_EXPERTS * (MOE_ROWS - 1)) // MOE_ROWS) * MOE_ROWS
    h = x.reshape(nt, D)
    for l in range(depth):
        proj_main, proj_lora = in_proj(h, norm_mix[l], w_in[l], d_main)
        y_conv = conv_mixer(proj_main, conv_dw[l], conv_b[l], conv_ln_g[l], conv_ln_b[l], batch=B)
        y_rwkv = rwkv_mixer(proj_main, proj_lora, 2 * d_c, shift_mu[l], w0[l], w_lora_up[l], a0[l], a_lora_up[l],
                            g_lora_up[l], k_k[l], k_a[l], r_k[l].reshape(-1), gn_g[l], gn_b[l], batch=B)
        h, v, routing, gates, counts = out_router(h, y_conv, y_rwkv, w_out[l].astype(BF16), norm_ffn[l],
                                                  router_group_w[l], router_group_b[l], router_expert_w[l],
                                                  router_expert_b[l])
        dest, block_e, n_used, next_used, used_ord, fill_plan = slot_plan(routing, counts[0, :N_EXPERTS],
                                                                          n_rows_pad)
        xs = dispatch(v, dest, fill_plan, n_rows_pad)
        ys = experts(xs, block_e, n_used, next_used, used_ord, expert_w_gate[l], expert_w_up[l], expert_w_down[l],
                     n_rows_pad)
        assert depth == 1
        h = combine(h, gates, dest, ys, norm_final)
    return h.reshape(B, T, D)
```

```python
import functools

import jax
import jax.numpy as jnp
from jax import lax
from jax.experimental import pallas as pl
from jax.experimental.pallas import tpu as pltpu

F32 = jnp.float32
BF16 = jnp.bfloat16

RWKV_HEAD = 64
CONV_WIDTH = 31
N_GROUPS = 4
EXPERTS_PER_GROUP = 8
N_EXPERTS = N_GROUPS * EXPERTS_PER_GROUP
TOP_K = 2
RMS_EPS = 1e-6
LN_EPS = 1e-5
GN_EPS = 64e-5
LANES = 128
SUBLANES = 8
MXU_DIM = 256
VMEM_LIMIT = 56 * 1024 * 1024
VMEM_LIMIT_IN_PROJ = 60 * 1024 * 1024

_NT = (((1,), (1,)), ((), ()))
_TN = (((0,), (0,)), ((), ()))


def _dot(a, b, dims=None):
    a = a.astype(BF16)
    b = b.astype(BF16)
    if dims is None:
        return jnp.dot(a, b, preferred_element_type=F32)
    return lax.dot_general(a, b, dims, preferred_element_type=F32)


def _dot_split(x, m):
    hi = x.astype(BF16)
    lo = (x - hi.astype(F32)).astype(BF16)
    return (jnp.dot(hi, m, preferred_element_type=F32)
            + jnp.dot(lo, m, preferred_element_type=F32))


def _pack_bf16_pairs(x):
    w = x.shape[1] // 2
    lo = lax.bitcast_convert_type(x[:, :w].astype(BF16).astype(F32), jnp.uint32)
    hi = lax.bitcast_convert_type(x[:, w:].astype(BF16).astype(F32), jnp.uint32)
    return (lo >> 16) | (hi & jnp.uint32(0xFFFF0000))


def _unpack_bf16_pairs(p):
    lo = lax.bitcast_convert_type(p << 16, F32)
    hi = lax.bitcast_convert_type(p & jnp.uint32(0xFFFF0000), F32)
    return jnp.concatenate([lo, hi], axis=1)


def _rwkv_kernel(r_ref, k_ref, v_ref, lora_ref, mu_main_ref, mu_lora_ref, w0_ref, wup_ref, a0_ref, aup_ref,
                 gup_ref, kk_ref, ka_ref, rk_ref, gng_ref, gnb_ref, y_ref,
                 state_ref, prev_main_ref, prev_lora_ref, *, n_heads, d_decay, d_aaa, chunk):
    nb, Cb, d_r = r_ref.shape
    C = chunk
    CS = Cb // C
    N = RWKV_HEAD
    R = nb * Cb

    @pl.when(pl.program_id(0) == 0)
    def _():
        state_ref[...] = jnp.zeros_like(state_ref)
        prev_main_ref[...] = jnp.zeros_like(prev_main_ref)
        prev_lora_ref[...] = jnp.zeros_like(prev_lora_ref)

    row = lax.broadcasted_iota(jnp.int32, (R, 1), 0)

    def shifted(x_ref, prev_ref, j, mu):
        x = x_ref[...].reshape(R, x_ref.shape[2])
        xp = pltpu.roll(x, 1, 0)
        for b in range(nb):
            xp = jnp.where(row == b * Cb, prev_ref[b, j:j + 1, :], xp)
            prev_ref[b, j:j + 1, :] = x[(b + 1) * Cb - 1:(b + 1) * Cb, :]
        return x + (xp - x) * mu

    r = shifted(r_ref, prev_main_ref, 0, mu_main_ref[0:1, :])
    k = shifted(k_ref, prev_main_ref, 1, mu_main_ref[1:2, :])
    v = shifted(v_ref, prev_main_ref, 2, mu_main_ref[2:3, :])
    lo = shifted(lora_ref, prev_lora_ref, 0, mu_lora_ref[...])

    wd = lo[:, :d_decay]
    ad = lo[:, d_decay:d_decay + d_aaa]
    gd = lo[:, d_decay + d_aaa:]

    z = w0_ref[...] + _dot(jnp.tanh(wd), wup_ref[...])
    w = -(jnp.maximum(-z, 0.0) + jnp.log(1.0 + jnp.exp(-jnp.abs(z)))) - 0.5
    logd = -jnp.exp(w)
    a = jax.nn.sigmoid(a0_ref[...] + _dot(ad, aup_ref[...]))
    g = _dot(jax.nn.sigmoid(gd), gup_ref[...])

    li = lax.broadcasted_iota(jnp.int32, (MXU_DIM, MXU_DIM), 0) // N
    lj = lax.broadcasted_iota(jnp.int32, (MXU_DIM, MXU_DIM), 1) // N
    head_ones = (li == lj).astype(BF16)

    def head_sum(x):
        return jnp.concatenate(
            [_dot_split(x[:, c * MXU_DIM:(c + 1) * MXU_DIM], head_ones) for c in range(d_r // MXU_DIM)], axis=1)

    kkr = k * kk_ref[...]
    kk = kkr / jnp.maximum(jnp.sqrt(head_sum(kkr * kkr)), 1e-12)
    k2 = k * (1.0 + (a - 1.0) * ka_ref[...])
    a_s = -kk
    b_s = kk * a

    ri = lax.broadcasted_iota(jnp.int32, (R, R), 0)
    rj = lax.broadcasted_iota(jnp.int32, (R, R), 1)
    tri = ((rj <= ri) & ((ri // C) == (rj // C))).astype(BF16)
    hi = logd.astype(BF16)
    rem = logd - hi.astype(F32)
    mid = rem.astype(BF16)
    low = (rem - mid.astype(F32)).astype(BF16)
    cum = (jnp.dot(tri, hi, preferred_element_type=F32) + jnp.dot(tri, mid, preferred_element_type=F32)
           + jnp.dot(tri, low, preferred_element_type=F32))
    p_incl = jnp.exp(cum)
    p_excl = jnp.exp(cum - logd)
    p_inv = jnp.exp(-cum)
    p_last = [p_incl[(q + 1) * C - 1:(q + 1) * C, :] for q in range(nb * CS)]
    p_last_rows = jnp.concatenate([jnp.broadcast_to(p, (C, d_r)) for p in p_last], axis=0)

    at = (a_s * p_excl).astype(BF16)
    rt = (r * p_incl).astype(BF16)
    bt = b_s * p_inv
    kt = k2 * p_inv
    bh = (bt * p_last_rows).astype(BF16)
    kh = (kt * p_last_rows).astype(BF16)
    bt = bt.astype(BF16)
    kt = kt.astype(BF16)
    vb = v.astype(BF16)

    HG = LANES // N
    W = HG * N
    GC = HG * C
    wi = lax.broadcasted_iota(jnp.int32, (GC, W), 0)
    wj = lax.broadcasted_iota(jnp.int32, (GC, W), 1)
    bd_on = ((wi // C) == (wj // N)).astype(BF16) > 0

    def bdiag(x):
        xb = x.astype(BF16)
        return jnp.where(bd_on, jnp.concatenate([xb] * HG, axis=0), jnp.zeros((), BF16))

    ti = lax.broadcasted_iota(jnp.int32, (2 * C, GC), 0)
    tj = lax.broadcasted_iota(jnp.int32, (2 * C, GC), 1) % C
    mask2 = ((ti < C) & (tj < ti)) | ((ti >= C) & (tj <= ti - C))
    li = lax.broadcasted_iota(jnp.int32, (C, GC), 0)
    lj = lax.broadcasted_iota(jnp.int32, (C, GC), 1) % C
    eye = (li == lj).astype(F32)
    level_masks = []
    s = 1
    while s < C:
        level_masks.append(((li // (2 * s)) == (lj // (2 * s))) & (((li // s) % 2) == 1) & (((lj // s) % 2) == 0))
        s *= 2
    colblk = lax.broadcasted_iota(jnp.int32, (N, W), 1) // N

    n_g = n_heads // HG
    units = [(q, g) for q in range(nb * CS) for g in range(n_g)]
    idx = range(len(units))
    rs = [slice(q * C, (q + 1) * C) for q, _ in units]
    ls = [slice(g * W, (g + 1) * W) for _, g in units]
    g_b = [jnp.where(mask2, _dot(jnp.concatenate([at[rs[i], ls[i]], rt[rs[i], ls[i]]], axis=0),
                                 bdiag(bt[rs[i], ls[i]]), _NT), 0.0) for i in idx]
    g_k = [jnp.where(mask2, _dot(jnp.concatenate([at[rs[i], ls[i]], rt[rs[i], ls[i]]], axis=0),
                                 bdiag(kt[rs[i], ls[i]]), _NT), 0.0) for i in idx]
    gv = [_dot(g_k[i], bdiag(vb[rs[i], ls[i]])) for i in idx]
    tinv = [eye + jnp.where(level_masks[0], g_b[i][:C], 0.0) for i in idx]
    for lm in level_masks[1:]:
        xs = [_dot(jnp.where(lm, g_b[i][:C], 0.0), bdiag(tinv[i])) for i in idx]
        tinv = [tinv[i] + _dot(tinv[i], bdiag(xs[i])) for i in idx]
    ta = [_dot(tinv[i], bdiag(at[rs[i], ls[i]])) for i in idx]
    tv = [_dot(tinv[i], bdiag(gv[i][:C])) for i in idx]
    ys = [None] * len(units)
    for b in range(nb):
        state = [state_ref[b * n_g + gi] for gi in range(n_g)]
        for cc in range(CS):
            ids = [(b * CS + cc) * n_g + gi for gi in range(n_g)]
            tr = [_dot(jnp.concatenate([ta[i].astype(BF16), rt[rs[i], ls[i]]], axis=0), bdiag(state[gi]), _NT)
                  for gi, i in enumerate(ids)]
            u = [tr[gi][:C] + tv[i] for gi, i in enumerate(ids)]
            for gi, i in enumerate(ids):
                ys[i] = tr[gi][C:] + gv[i][C:] + _dot(g_b[i][C:], bdiag(u[gi]))
                uv = jnp.concatenate([u[gi].astype(BF16), vb[rs[i], ls[i]]], axis=0)
                bk = jnp.concatenate([bh[rs[i], ls[i]], kh[rs[i], ls[i]]], axis=0)
                full = _dot(uv, bk, _TN)
                upd = state[gi] * p_last[units[i][0]][:, ls[i]]
                for h in range(HG):
                    upd = upd + jnp.where(colblk == h, full[h * N:(h + 1) * N, :], 0.0)
                state[gi] = upd
        for gi in range(n_g):
            state_ref[b * n_g + gi] = state[gi]
    y = jnp.concatenate([jnp.concatenate(ys[q * n_g:(q + 1) * n_g], axis=1) for q in range(nb * CS)],
                        axis=0)

    inv_n = 1.0 / N
    mu_y = head_sum(y) * inv_n
    yc = y - mu_y
    var_y = head_sum(yc * yc) * inv_n
    yn = yc * lax.rsqrt(var_y + GN_EPS) * gng_ref[...] + gnb_ref[...]
    bonus = head_sum(r * k2 * rk_ref[...]) * v
    y_ref[...] = ((yn + bonus) * g).reshape(nb, Cb, d_r).astype(y_ref.dtype)


def rwkv_mixer(proj_main, proj_lora, col0, shift_mu, w0, w_up, a0, a_up, g_up, k_k, k_a, r_k, gn_g, gn_b,
               *, batch, chunk=64, chunks_per_step=2):
    nt = proj_main.shape[0]
    seq = nt // batch
    d_r = w0.shape[-1]
    n_heads = d_r // RWKV_HEAD
    d_lora = proj_lora.shape[1]
    d_decay, d_aaa = w_up.shape[0], a_up.shape[0]
    rows = chunk * chunks_per_step
    n_steps = seq // rows
    assert seq % rows == 0 and col0 % d_r == 0
    cb = col0 // d_r
    row2 = lambda t: t.reshape(1, -1).astype(F32)
    mu_main = shift_mu[:3 * d_r].reshape(3, d_r)
    mu_lora = shift_mu[3 * d_r:].reshape(1, d_lora)
    pm3 = proj_main.reshape(batch, seq, proj_main.shape[1])
    pl3 = proj_lora.reshape(batch, seq, d_lora)

    def main_spec(j):
        return pl.BlockSpec((batch, rows, d_r), lambda c, j=j: (0, c, cb + j))

    full = lambda shape: pl.BlockSpec(shape, lambda c: (0,) * len(shape))
    kern = functools.partial(_rwkv_kernel, n_heads=n_heads, d_decay=d_decay, d_aaa=d_aaa, chunk=chunk)
    y = pl.pallas_call(
        kern,
        grid=(n_steps,),
        in_specs=[main_spec(0), main_spec(1), main_spec(2),
                  pl.BlockSpec((batch, rows, d_lora), lambda c: (0, c, 0)),
                  full((3, d_r)), full((1, d_lora)), full((1, d_r)), full((d_decay, d_r)), full((1, d_r)),
                  full((d_aaa, d_r)), full((g_up.shape[0], d_r)), full((1, d_r)), full((1, d_r)), full((1, d_r)),
                  full((1, d_r)), full((1, d_r))],
        out_specs=pl.BlockSpec((batch, rows, d_r), lambda c: (0, c, 0)),
        out_shape=jax.ShapeDtypeStruct((batch, seq, d_r), BF16),
        scratch_shapes=[pltpu.VMEM((batch * n_heads * RWKV_HEAD // LANES, RWKV_HEAD, LANES), F32),
                        pltpu.VMEM((batch, 3, d_r), F32),
                        pltpu.VMEM((batch, 1, d_lora), F32)],
        compiler_params=pltpu.CompilerParams(dimension_semantics=("arbitrary",), vmem_limit_bytes=VMEM_LIMIT),
        name="rwkv_mixer",
    )(pm3, pm3, pm3, pl3, mu_main, mu_lora, row2(w0), w_up.astype(BF16), row2(a0),
      a_up.astype(BF16), g_up.astype(BF16), row2(k_k), row2(k_a), row2(r_k), row2(gn_g), row2(gn_b))
    return y.reshape(nt, d_r)


def _rms(x, g):
    return x * lax.rsqrt(jnp.mean(x * x, axis=-1, keepdims=True) + RMS_EPS) * g


def _inproj_kernel(x_hbm, g_ref, wt_hbm, wlt_ref, o_ref, ol_ref, x_f, u_ref, w_f, xsem, wsem):
    i, j = pl.program_id(0), pl.program_id(1)
    n_i, n_j = pl.num_programs(0), pl.num_programs(1)
    tm = x_f.shape[0]
    tn = w_f.shape[1]
    step = i * n_j + j
    slot = lax.rem(step, 2)

    def fetch_w(jt, s):
        return pltpu.make_async_copy(wt_hbm.at[pl.ds(pl.multiple_of(jt * tn, tn), tn), :], w_f.at[s], wsem.at[s])

    def fetch_x(it):
        return pltpu.make_async_copy(x_hbm.at[pl.ds(pl.multiple_of(it * tm, tm), tm), :], x_f, xsem)

    @pl.when(step == 0)
    def _():
        fetch_x(0).start()
        fetch_w(0, 0).start(priority=1)

    @pl.when(step + 1 < n_i * n_j)
    def _():
        fetch_w(lax.rem(j + 1, n_j), 1 - slot).start(priority=1)

    @pl.when(j == 0)
    def _():
        fetch_x(i).wait()
        u = _rms(x_f[...], g_ref[...]).astype(BF16)
        u_ref[...] = u

        @pl.when(i + 1 < n_i)
        def _():
            fetch_x(i + 1).start()

        ol_ref[...] = lax.dot_general(u, wlt_ref[...], _NT, preferred_element_type=F32)

    fetch_w(j, slot).wait()
    o_ref[...] = lax.dot_general(u_ref[...], w_f[slot].astype(BF16), _NT, preferred_element_type=F32)


def in_proj(x2, norm_g, w_in, d_main, *, tm=2048, tn=512):
    nt, d = x2.shape
    d_lora = w_in.shape[1] - d_main
    assert nt % tm == 0 and d_main % tn == 0
    w_t = w_in.T
    w_lora_t = w_t[d_main:].astype(BF16)
    hbm = pl.BlockSpec(memory_space=pl.ANY)
    return pl.pallas_call(
        _inproj_kernel,
        grid=(nt // tm, d_main // tn),
        in_specs=[hbm, pl.BlockSpec((1, d), lambda i, j: (0, 0)), hbm,
                  pl.BlockSpec((d_lora, d), lambda i, j: (0, 0))],
        out_specs=[pl.BlockSpec((tm, tn), lambda i, j: (i, j)),
                   pl.BlockSpec((tm, d_lora), lambda i, j: (i, 0))],
        out_shape=[jax.ShapeDtypeStruct((nt, d_main), F32), jax.ShapeDtypeStruct((nt, d_lora), F32)],
        scratch_shapes=[pltpu.VMEM((tm, d), F32), pltpu.VMEM((tm, d), BF16), pltpu.VMEM((2, tn, d), F32),
                        pltpu.SemaphoreType.DMA(()), pltpu.SemaphoreType.DMA((2,))],
        compiler_params=pltpu.CompilerParams(dimension_semantics=("arbitrary", "arbitrary"),
                                             vmem_limit_bytes=VMEM_LIMIT_IN_PROJ),
        name="in_proj",
    )(x2, norm_g.reshape(1, d).astype(F32), w_t, w_lora_t)


CONV_HALO = 32
CONV_ROWS = 32


def _conv_kernel(val_ref, gate_ref, dw_ref, b_ref, g_ref, beta_ref, o_ref, ubuf_ref):
    tt = val_ref.shape[0]
    d_c = val_ref.shape[1]
    t_idx = pl.program_id(1)

    @pl.when(t_idx == 0)
    def _():
        ubuf_ref[0, 0:CONV_HALO, :] = jnp.zeros((CONV_HALO, d_c), F32)

    @pl.when(t_idx > 0)
    def _():
        ubuf_ref[0, 0:CONV_HALO, :] = ubuf_ref[0, tt:tt + CONV_HALO, :]

    ubuf_ref[0, CONV_HALO:CONV_HALO + tt, :] = val_ref[...] * jax.nn.sigmoid(gate_ref[...])
    n_sh = tt + CONV_HALO - SUBLANES
    for s in range(1, SUBLANES):
        for r in range(0, n_sh, CONV_ROWS):
            n = min(CONV_ROWS, n_sh - r)
            ubuf_ref[s, r:r + n, :] = ubuf_ref[0, r + s:r + s + n, :]
    dw = dw_ref[...]
    lead = CONV_HALO - (CONV_WIDTH - 1)
    for c in range(tt // CONV_ROWS):
        r0 = c * CONV_ROWS
        acc = jnp.zeros((CONV_ROWS, d_c), F32)
        for j in range(CONV_WIDTH):
            q, s = divmod(lead + j, SUBLANES)
            r = r0 + q * SUBLANES
            acc = acc + dw[j:j + 1, :] * ubuf_ref[s, r:r + CONV_ROWS, :]
        acc = acc + b_ref[...]
        mu = jnp.mean(acc, axis=-1, keepdims=True)
        cen = acc - mu
        var = jnp.mean(cen * cen, axis=-1, keepdims=True)
        yv = cen * lax.rsqrt(var + LN_EPS) * g_ref[...] + beta_ref[...]
        o_ref[r0:r0 + CONV_ROWS, :] = (yv * jax.nn.sigmoid(yv)).astype(o_ref.dtype)


def conv_mixer(proj_main, dw, bias, ln_g, ln_b, *, batch, tt=512):
    nt = proj_main.shape[0]
    seq = nt // batch
    d_c = dw.shape[1]
    n_t = seq // tt
    assert seq % tt == 0 and tt % CONV_ROWS == 0
    row = lambda t: t.reshape(1, d_c).astype(F32)
    full = lambda shape: pl.BlockSpec(shape, lambda b, t: (0, 0))
    return pl.pallas_call(
        _conv_kernel,
        grid=(batch, n_t),
        in_specs=[pl.BlockSpec((tt, d_c), lambda b, t: (b * n_t + t, 0)),
                  pl.BlockSpec((tt, d_c), lambda b, t: (b * n_t + t, 1)),
                  full((CONV_WIDTH, d_c)), full((1, d_c)), full((1, d_c)), full((1, d_c))],
        out_specs=pl.BlockSpec((tt, d_c), lambda b, t: (b * n_t + t, 0)),
        out_shape=jax.ShapeDtypeStruct((nt, d_c), BF16),
        scratch_shapes=[pltpu.VMEM((SUBLANES, tt + CONV_HALO, d_c), F32)],
        compiler_params=pltpu.CompilerParams(dimension_semantics=("arbitrary", "arbitrary"),
                                             vmem_limit_bytes=VMEM_LIMIT),
        name="conv_mixer",
    )(proj_main, proj_main, dw.astype(F32), row(bias), row(ln_g), row(ln_b))


def _out_router_kernel(x_ref, yc_ref, yr_ref, wc_ref, wr_ref, nf_ref, rw_hi_ref, rw_lo_ref, rb_ref,
                       h_ref, v_ref, eid_ref, gate_ref, count_ref, cnt_ref):
    tm = x_ref.shape[0]
    h = (x_ref[...] + jnp.dot(yc_ref[...], wc_ref[...], preferred_element_type=F32)
         + jnp.dot(yr_ref[...], wr_ref[...], preferred_element_type=F32))
    h_ref[...] = h

    @pl.when(pl.program_id(0) == 0)
    def _():
        cnt_ref[...] = jnp.zeros_like(cnt_ref)

    v = _rms(h, nf_ref[...])
    v_ref[...] = _pack_bf16_pairs(v)
    v_hi = v.astype(BF16)
    v_lo = (v - v_hi.astype(F32)).astype(BF16)
    logits = (jnp.dot(v_hi, rw_hi_ref[...], preferred_element_type=F32)
              + jnp.dot(v_lo, rw_hi_ref[...], preferred_element_type=F32)
              + jnp.dot(v_hi, rw_lo_ref[...], preferred_element_type=F32)) + rb_ref[...]
    lane = lax.broadcasted_iota(jnp.int32, (tm, LANES), 1)
    neg = jnp.float32(-jnp.inf)
    big = jnp.int32(LANES)

    def first_max(vals):
        m = jnp.max(vals, axis=-1, keepdims=True)
        return m, jnp.min(jnp.where(vals == m, lane, big), axis=-1, keepdims=True)

    gl = jnp.where(lane < N_GROUPS, logits, neg)
    gmax, grp = first_max(gl)
    p_grp = 1.0 / jnp.sum(jnp.exp(gl - gmax), axis=-1, keepdims=True)
    lo = N_GROUPS + grp * EXPERTS_PER_GROUP
    el = jnp.where((lane >= lo) & (lane < lo + EXPERTS_PER_GROUP), logits, neg)
    m1, i1 = first_max(el)
    m2, i2 = first_max(jnp.where(lane == i1, neg, el))
    e2 = jnp.exp(m2 - m1)
    g1 = p_grp / (1.0 + e2)
    g2 = p_grp * e2 / (1.0 + e2)
    gate_ref[...] = jnp.where(lane == 0, g1, jnp.where(lane == 1, g2, 0.0))

    x1 = i1 - N_GROUPS
    x2 = i2 - N_GROUPS
    oh1 = (lane == x1).astype(F32)
    oh2 = (lane == x2).astype(F32)
    earlier = (lax.broadcasted_iota(jnp.int32, (tm, tm), 1)
               < lax.broadcasted_iota(jnp.int32, (tm, tm), 0)).astype(BF16)
    before1 = jnp.dot(earlier, oh1.astype(BF16), preferred_element_type=F32)
    before2 = jnp.dot(earlier, oh2.astype(BF16), preferred_element_type=F32)
    carry = cnt_ref[...]
    n1 = jnp.sum(oh1, axis=0, keepdims=True)
    n2 = jnp.sum(oh2, axis=0, keepdims=True)
    rank1 = jnp.sum(oh1 * (before1 + carry), axis=-1, keepdims=True).astype(jnp.int32)
    rank2 = jnp.sum(oh2 * (before2 + carry + n1), axis=-1, keepdims=True).astype(jnp.int32)
    cnt_ref[...] = carry + n1 + n2
    count_ref[...] = (carry + n1 + n2).astype(jnp.int32)
    routing = jnp.where(lane == 0, x1, jnp.where(lane == 1, x2, jnp.where(
        lane == 2, rank1, jnp.where(lane == 3, rank2, 0))))
    eid_ref[...] = routing.T[:SUBLANES, :]


def out_router(x2, y_conv, y_rwkv, w_out, norm_ffn, rg_w, rg_b, re_w, re_b, *, tm=512):
    nt, d = x2.shape
    d_c, d_r = y_conv.shape[1], y_rwkv.shape[1]
    assert d_c == d_r and nt % tm == 0
    n_r = N_GROUPS + N_EXPERTS
    rw = jnp.zeros((d, LANES), F32).at[:, :n_r].set(jnp.concatenate([rg_w, re_w], axis=1).astype(F32))
    rw_hi = rw.astype(BF16)
    rw_lo = (rw - rw_hi.astype(F32)).astype(BF16)
    rb = jnp.zeros((1, LANES), F32).at[0, :n_r].set(jnp.concatenate([rg_b, re_b]).astype(F32))
    full = lambda shape: pl.BlockSpec(shape, lambda i: (0, 0))
    once = lambda shape, r: pl.BlockSpec(shape, lambda i, r=r: (r, 0), pipeline_mode=pl.Buffered(1))
    rows = lambda w: pl.BlockSpec((tm, w), lambda i: (i, 0))
    return pl.pallas_call(
        _out_router_kernel,
        grid=(nt // tm,),
        in_specs=[rows(d), rows(d_c), rows(d_r), once((d_c, d), 0), once((d_r, d), 1),
                  full((1, d)), full((d, LANES)), full((d, LANES)), full((1, LANES))],
        out_specs=[rows(d), rows(d // 2), pl.BlockSpec((SUBLANES, tm), lambda i: (0, i)), rows(LANES),
                   full((1, LANES))],
        out_shape=[jax.ShapeDtypeStruct((nt, d), F32), jax.ShapeDtypeStruct((nt, d // 2), jnp.uint32),
                   jax.ShapeDtypeStruct((SUBLANES, nt), jnp.int32), jax.ShapeDtypeStruct((nt, LANES), F32),
                   jax.ShapeDtypeStruct((1, LANES), jnp.int32)],
        scratch_shapes=[pltpu.VMEM((1, LANES), F32)],
        compiler_params=pltpu.CompilerParams(dimension_semantics=("arbitrary",), vmem_limit_bytes=VMEM_LIMIT),
        name="out_router",
    )(x2, y_conv, y_rwkv, w_out, w_out, norm_ffn.reshape(1, d).astype(F32), rw_hi, rw_lo, rb)


MOE_ROWS = 256
W_SLOTS = 3


def slot_plan(routing, counts, n_rows_pad):
    expert_id = routing[:TOP_K]
    rank = routing[TOP_K:2 * TOP_K]
    padded = (counts + MOE_ROWS - 1) // MOE_ROWS * MOE_ROWS
    pend = jnp.cumsum(padded)
    pstarts = pend - padded
    ids = jnp.arange(N_EXPERTS, dtype=jnp.int32)
    start_of = jnp.sum(jnp.where(expert_id[..., None] == ids, pstarts, 0), axis=-1)
    dest = (start_of + rank).astype(jnp.int32).T
    n_blocks = n_rows_pad // MOE_ROWS
    block_row0 = jnp.arange(n_blocks, dtype=jnp.int32) * MOE_ROWS
    block_e = jnp.minimum(jnp.sum((pend[None, :] <= block_row0[:, None]).astype(jnp.int32), axis=1),
                          N_EXPERTS - 1).astype(jnp.int32)
    n_used = (pend[-1:] // MOE_ROWS).astype(jnp.int32)
    owner = jnp.where(counts > 0, ids, N_EXPERTS)
    nxt = lax.cummin(jnp.concatenate([owner[1:], jnp.full((1,), N_EXPERTS, jnp.int32)]), reverse=True)
    next_used = jnp.where(nxt < N_EXPERTS, nxt, -1).astype(jnp.int32)
    hops = [next_used]
    for _ in range(W_SLOTS - 2):
        hops.append(jnp.where(hops[-1] >= 0, next_used[jnp.maximum(hops[-1], 0)], -1))
    next_used = jnp.stack(hops)
    used_ord = (jnp.cumsum((counts > 0).astype(jnp.int32)) - 1).astype(jnp.int32)
    pad0 = pstarts + counts
    n_single = jnp.minimum((-pad0) % SUBLANES, pend - pad0)
    tile0 = pad0 + n_single
    n_tile = (pend - tile0) // SUBLANES
    tail = jnp.stack([pend[-1] // MOE_ROWS, n_blocks - pend[-1] // MOE_ROWS, jnp.sum(n_single), jnp.sum(n_tile)])
    fill_plan = jnp.concatenate([pad0, n_single, tile0, n_tile, tail]).astype(jnp.int32)
    return dest, block_e, n_used, next_used, used_ord, fill_plan


def _dispatch_kernel(fill_ref, dest_ref, v_ref, xs_ref, zero_ref, sem, fill_sem):
    tm = v_ref.shape[0]

    def fills():
        e0, e1, e2, e3, e4 = (k * N_EXPERTS for k in range(5))
        single = lambda e, i: pltpu.make_async_copy(
            zero_ref.at[pl.ds(0, 1), :], xs_ref.at[pl.ds(fill_ref[e0 + e] + i, 1), :], fill_sem)
        tile = lambda e, i: pltpu.make_async_copy(
            zero_ref.at[pl.ds(0, SUBLANES), :],
            xs_ref.at[pl.ds(pl.multiple_of(fill_ref[e2 + e] + i * SUBLANES, SUBLANES), SUBLANES), :], fill_sem)
        block = lambda i: pltpu.make_async_copy(
            zero_ref, xs_ref.at[pl.ds(pl.multiple_of((fill_ref[e4] + i) * MOE_ROWS, MOE_ROWS), MOE_ROWS), :],
            fill_sem)
        return single, tile, block, e1, e3, e4

    @pl.when(pl.program_id(0) == 0)
    def _():
        zero_ref[...] = jnp.zeros_like(zero_ref)
        single, tile, block, e1, e3, e4 = fills()
        for e in range(N_EXPERTS):
            lax.fori_loop(0, fill_ref[e1 + e], lambda i, c, e=e: (single(e, i).start(), c)[1], 0)
            lax.fori_loop(0, fill_ref[e3 + e], lambda i, c, e=e: (tile(e, i).start(), c)[1], 0)
        lax.fori_loop(0, fill_ref[e4 + 1], lambda i, c: (block(i).start(), c)[1], 0)

    def copy(i, k):
        d = dest_ref[0, 0, i * TOP_K + k]
        return pltpu.make_async_copy(v_ref.at[pl.ds(i, 1), :], xs_ref.at[pl.ds(d, 1), :], sem)

    def start(i, carry):
        for k in range(TOP_K):
            copy(i, k).start(priority=k)
        return carry

    lax.fori_loop(0, tm, start, 0, unroll=8)
    for k in range(TOP_K):
        pltpu.make_async_copy(v_ref, xs_ref.at[pl.ds(0, tm), :], sem).wait()

    @pl.when(pl.program_id(0) == 0)
    def _():
        single, tile, block, e1, e3, e4 = fills()
        lax.fori_loop(0, fill_ref[e4 + 2], lambda i, c: (single(0, 0).wait(), c)[1], 0)
        lax.fori_loop(0, fill_ref[e4 + 3], lambda i, c: (tile(0, 0).wait(), c)[1], 0)
        lax.fori_loop(0, fill_ref[e4 + 1], lambda i, c: (block(0).wait(), c)[1], 0)


def dispatch(v, dest, fill_plan, n_rows_pad, *, tm=2048):
    nt, d = v.shape
    assert nt % tm == 0
    dest3 = dest.reshape(nt // tm, 1, tm * TOP_K)
    grid_spec = pltpu.PrefetchScalarGridSpec(
        num_scalar_prefetch=1,
        grid=(nt // tm,),
        in_specs=[pl.BlockSpec((1, 1, tm * TOP_K), lambda i, fp: (i, 0, 0), memory_space=pltpu.SMEM),
                  pl.BlockSpec((tm, d), lambda i, fp: (i, 0))],
        out_specs=pl.BlockSpec(memory_space=pl.ANY),
        scratch_shapes=[pltpu.VMEM((MOE_ROWS, d), v.dtype), pltpu.SemaphoreType.DMA(()),
                        pltpu.SemaphoreType.DMA(())],
    )
    return pl.pallas_call(
        _dispatch_kernel,
        grid_spec=grid_spec,
        out_shape=jax.ShapeDtypeStruct((n_rows_pad, d), v.dtype),
        compiler_params=pltpu.CompilerParams(dimension_semantics=("arbitrary",), vmem_limit_bytes=VMEM_LIMIT),
        name="moe_dispatch",
    )(fill_plan, dest3, v)


def _experts_kernel(be_ref, nu_ref, nxt_ref, ord_ref, xs_ref, wg_hbm, wu_hbm, wd_hbm, ys_ref,
                    wg_f, wu_f, wd_f, wg_s, wu_s, wd_s, wsem):
    j = pl.program_id(0)
    e = be_ref[j]
    ahead = W_SLOTS - 1

    def fetch(ex, slot):
        return [pltpu.make_async_copy(wg_hbm.at[ex], wg_f.at[slot], wsem.at[slot]),
                pltpu.make_async_copy(wu_hbm.at[ex], wu_f.at[slot], wsem.at[slot]),
                pltpu.make_async_copy(wd_hbm.at[ex], wd_f.at[slot], wsem.at[slot])]

    @pl.when(j == 0)
    def _():
        ex = e
        for k in range(ahead):
            @pl.when(ex >= 0)
            def _(ex=ex):
                for c in fetch(ex, lax.rem(ord_ref[ex], W_SLOTS)):
                    c.start(priority=1)
            ex = jnp.where(ex >= 0, nxt_ref[0, jnp.maximum(ex, 0)], -1)

    first = (j == 0) | (e != be_ref[jnp.maximum(j - 1, 0)])

    @pl.when(first & (j < nu_ref[0]))
    def _():
        slot = lax.rem(ord_ref[e], W_SLOTS)
        for c in fetch(e, slot):
            c.wait()
        nxt = nxt_ref[ahead - 1, e]

        @pl.when(nxt >= 0)
        def _():
            for c in fetch(nxt, lax.rem(ord_ref[nxt], W_SLOTS)):
                c.start(priority=1)

        wg_s[...] = wg_f[slot].astype(BF16)
        wu_s[...] = wu_f[slot].astype(BF16)
        wd_s[...] = wd_f[slot].astype(BF16)

    @pl.when(j < nu_ref[0])
    def _():
        xb = _unpack_bf16_pairs(xs_ref[...]).astype(BF16)
        gt = jnp.dot(xb, wg_s[...], preferred_element_type=F32)
        up = jnp.dot(xb, wu_s[...], preferred_element_type=F32)
        hid = (gt * jax.nn.sigmoid(gt) * up).astype(BF16)
        ys_ref[...] = _pack_bf16_pairs(jnp.dot(hid, wd_s[...], preferred_element_type=F32))

    @pl.when(j >= nu_ref[0])
    def _():
        ys_ref[...] = jnp.zeros_like(ys_ref)


def experts(xs, block_e, n_used, next_used, used_ord, w_gate, w_up, w_down, n_rows_pad):
    dh = xs.shape[1]
    d, d_e = w_gate.shape[1], w_gate.shape[2]
    assert d == 2 * dh
    n_blocks = n_rows_pad // MOE_ROWS
    hbm = pl.BlockSpec(memory_space=pl.ANY)
    grid_spec = pltpu.PrefetchScalarGridSpec(
        num_scalar_prefetch=4,
        grid=(n_blocks,),
        in_specs=[pl.BlockSpec((MOE_ROWS, dh), lambda j, be, nu, nx, od: (jnp.minimum(j, nu[0] - 1), 0)),
                  hbm, hbm, hbm],
        out_specs=pl.BlockSpec((MOE_ROWS, dh), lambda j, be, nu, nx, od: (j, 0)),
        scratch_shapes=[pltpu.VMEM((W_SLOTS, d, d_e), F32), pltpu.VMEM((W_SLOTS, d, d_e), F32),
                        pltpu.VMEM((W_SLOTS, d_e, d), F32),
                        pltpu.VMEM((d, d_e), BF16), pltpu.VMEM((d, d_e), BF16), pltpu.VMEM((d_e, d), BF16),
                        pltpu.SemaphoreType.DMA((W_SLOTS,))],
    )
    return pl.pallas_call(
        _experts_kernel,
        grid_spec=grid_spec,
        out_shape=jax.ShapeDtypeStruct((n_rows_pad, dh), jnp.uint32),
        compiler_params=pltpu.CompilerParams(dimension_semantics=("arbitrary",), vmem_limit_bytes=VMEM_LIMIT),
        name="moe_experts",
    )(block_e, n_used, next_used, used_ord, xs, w_gate, w_up, w_down)


def _combine_kernel(dest_ref, dest_next_ref, h_ref, gate_ref, nfin_ref, ys_ref, o_ref, ybuf, sem):
    step = pl.program_id(0)
    tm = h_ref.shape[0]
    slot = lax.rem(step, 2)

    def copy(d_ref, s, i, k):
        d = d_ref[0, 0, i * TOP_K + k]
        return pltpu.make_async_copy(ys_ref.at[pl.ds(d, 1), :], ybuf.at[s, k, pl.ds(i, 1), :], sem.at[s])

    def start_all(d_ref, s):
        def body(i, carry):
            for k in range(TOP_K):
                copy(d_ref, s, i, k).start(priority=1)
            return carry
        lax.fori_loop(0, tm, body, 0, unroll=8)

    @pl.when(step == 0)
    def _():
        start_all(dest_ref, 0)

    @pl.when(step + 1 < pl.num_programs(0))
    def _():
        start_all(dest_next_ref, 1 - slot)

    for k in range(TOP_K):
        pltpu.make_async_copy(ys_ref.at[pl.ds(0, tm), :], ybuf.at[slot, k], sem.at[slot]).wait()
    gates = gate_ref[...]
    moe = (gates[:, 0:1] * _unpack_bf16_pairs(ybuf[slot, 0])
           + gates[:, 1:2] * _unpack_bf16_pairs(ybuf[slot, 1]))
    o_ref[...] = _rms(h_ref[...] + moe, nfin_ref[...])


def combine(h, gates, dest, ys, norm_final, *, tm=512):
    nt, d = h.shape
    assert nt % tm == 0 and ys.shape[1] * 2 == d
    n_tiles = nt // tm
    dest3 = dest.reshape(n_tiles, 1, tm * TOP_K)
    return pl.pallas_call(
        _combine_kernel,
        grid=(n_tiles,),
        in_specs=[pl.BlockSpec((1, 1, tm * TOP_K), lambda i: (i, 0, 0), memory_space=pltpu.SMEM),
                  pl.BlockSpec((1, 1, tm * TOP_K), lambda i: (jnp.minimum(i + 1, n_tiles - 1), 0, 0),
                               memory_space=pltpu.SMEM),
                  pl.BlockSpec((tm, d), lambda i: (i, 0)),
                  pl.BlockSpec((tm, LANES), lambda i: (i, 0)),
                  pl.BlockSpec((1, d), lambda i: (0, 0)),
                  pl.BlockSpec(memory_space=pl.ANY)],
        out_specs=pl.BlockSpec((tm, d), lambda i: (i, 0)),
        out_shape=jax.ShapeDtypeStruct((nt, d), F32),
        scratch_shapes=[pltpu.VMEM((2, TOP_K, tm, d // 2), jnp.uint32), pltpu.SemaphoreType.DMA((2,))],
        compiler_params=pltpu.CompilerParams(dimension_semantics=("arbitrary",), vmem_limit_bytes=VMEM_LIMIT),
        name="moe_combine",
    )(dest3, dest3, h, gates, norm_final.reshape(1, d).astype(F32), ys)


def kernel(x, norm_mix, w_in, conv_dw, conv_b, conv_ln_g, conv_ln_b, shift_mu, w0, w_lora_up, a0, a_lora_up, g_lora_up, k_k, k_a, r_k, gn_g, gn_b, w_out, norm_ffn, router_group_w, router_group_b, router_expert_w, router_expert_b, expert_w_gate, expert_w_up, expert_w_down, norm_final):
    B, T, D = x.shape
    depth = w_in.shape[0]
    d_c = conv_dw.shape[2]
    d_r = w0.shape[1]
    d_main = 2 * d_c + 3 * d_r
    nt = B * T
    n_rows_pad = -(-(nt * TOP_K + N_EXPERTS * (MOE_ROWS - 1)) // MOE_ROWS) * MOE_ROWS
    h = x.reshape(nt, D)
    for l in range(depth):
        proj_main, proj_lora = in_proj(h, norm_mix[l], w_in[l], d_main)
        y_conv = conv_mixer(proj_main, conv_dw[l], conv_b[l], conv_ln_g[l], conv_ln_b[l], batch=B)
        y_rwkv = rwkv_mixer(proj_main, proj_lora, 2 * d_c, shift_mu[l], w0[l], w_lora_up[l], a0[l], a_lora_up[l],
                            g_lora_up[l], k_k[l], k_a[l], r_k[l].reshape(-1), gn_g[l], gn_b[l], batch=B)
        h, v, routing, gates, counts = out_router(h, y_conv, y_rwkv, w_out[l].astype(BF16), norm_ffn[l],
                                                  router_group_w[l], router_group_b[l], router_expert_w[l],
                                                  router_expert_b[l])
        dest, block_e, n_used, next_used, used_ord, fill_plan = slot_plan(routing, counts[0, :N_EXPERTS],
                                                                          n_rows_pad)
        xs = dispatch(v, dest, fill_plan, n_rows_pad)
        ys = experts(xs, block_e, n_used, next_used, used_ord, expert_w_gate[l], expert_w_up[l], expert_w_down[l],
                     n_rows_pad)
        assert depth == 1
        h = combine(h, gates, dest, ys, norm_final)
    return h.reshape(B, T, D)
```

```python
import functools

import jax
import jax.numpy as jnp
from jax import lax
from jax.experimental import pallas as pl
from jax.experimental.pallas import tpu as pltpu

F32 = jnp.float32
BF16 = jnp.bfloat16

RWKV_HEAD = 64
CONV_WIDTH = 31
N_GROUPS = 4
EXPERTS_PER_GROUP = 8
N_EXPERTS = N_GROUPS * EXPERTS_PER_GROUP
TOP_K = 2
RMS_EPS = 1e-6
LN_EPS = 1e-5
GN_EPS = 64e-5
LANES = 128
SUBLANES = 8
MXU_DIM = 256
VMEM_LIMIT = 56 * 1024 * 1024
VMEM_LIMIT_IN_PROJ = 60 * 1024 * 1024

_NT = (((1,), (1,)), ((), ()))
_TN = (((0,), (0,)), ((), ()))


def _dot(a, b, dims=None):
    a = a.astype(BF16)
    b = b.astype(BF16)
    if dims is None:
        return jnp.dot(a, b, preferred_element_type=F32)
    return lax.dot_general(a, b, dims, preferred_element_type=F32)


def _dot_split(x, m):
    hi = x.astype(BF16)
    lo = (x - hi.astype(F32)).astype(BF16)
    return (jnp.dot(hi, m, preferred_element_type=F32)
            + jnp.dot(lo, m, preferred_element_type=F32))


def _pack_bf16_pairs(x):
    w = x.shape[1] // 2
    lo = lax.bitcast_convert_type(x[:, :w].astype(BF16).astype(F32), jnp.uint32)
    hi = lax.bitcast_convert_type(x[:, w:].astype(BF16).astype(F32), jnp.uint32)
    return (lo >> 16) | (hi & jnp.uint32(0xFFFF0000))


def _unpack_bf16_pairs(p):
    lo = lax.bitcast_convert_type(p << 16, F32)
    hi = lax.bitcast_convert_type(p & jnp.uint32(0xFFFF0000), F32)
    return jnp.concatenate([lo, hi], axis=1)


def _rwkv_kernel(r_ref, k_ref, v_ref, lora_ref, mu_main_ref, mu_lora_ref, w0_ref, wup_ref, a0_ref, aup_ref,
                 gup_ref, kk_ref, ka_ref, rk_ref, gng_ref, gnb_ref, y_ref,
                 state_ref, prev_main_ref, prev_lora_ref, *, n_heads, d_decay, d_aaa, chunk):
    nb, Cb, d_r = r_ref.shape
    C = chunk
    CS = Cb // C
    N = RWKV_HEAD
    R = nb * Cb

    @pl.when(pl.program_id(0) == 0)
    def _():
        state_ref[...] = jnp.zeros_like(state_ref)
        prev_main_ref[...] = jnp.zeros_like(prev_main_ref)
        prev_lora_ref[...] = jnp.zeros_like(prev_lora_ref)

    row = lax.broadcasted_iota(jnp.int32, (R, 1), 0)

    def shifted(x_ref, prev_ref, j, mu):
        x = x_ref[...].reshape(R, x_ref.shape[2])
        xp = pltpu.roll(x, 1, 0)
        for b in range(nb):
            xp = jnp.where(row == b * Cb, prev_ref[b, j:j + 1, :], xp)
            prev_ref[b, j:j + 1, :] = x[(b + 1) * Cb - 1:(b + 1) * Cb, :]
        return x + (xp - x) * mu

    r = shifted(r_ref, prev_main_ref, 0, mu_main_ref[0:1, :])
    k = shifted(k_ref, prev_main_ref, 1, mu_main_ref[1:2, :])
    v = shifted(v_ref, prev_main_ref, 2, mu_main_ref[2:3, :])
    lo = shifted(lora_ref, prev_lora_ref, 0, mu_lora_ref[...])

    wd = lo[:, :d_decay]
    ad = lo[:, d_decay:d_decay + d_aaa]
    gd = lo[:, d_decay + d_aaa:]

    z = w0_ref[...] + _dot(jnp.tanh(wd), wup_ref[...])
    w = -(jnp.maximum(-z, 0.0) + jnp.log(1.0 + jnp.exp(-jnp.abs(z)))) - 0.5
    logd = -jnp.exp(w)
    a = jax.nn.sigmoid(a0_ref[...] + _dot(ad, aup_ref[...]))
    g = _dot(jax.nn.sigmoid(gd), gup_ref[...])

    li = lax.broadcasted_iota(jnp.int32, (MXU_DIM, MXU_DIM), 0) // N
    lj = lax.broadcasted_iota(jnp.int32, (MXU_DIM, MXU_DIM), 1) // N
    head_ones = (li == lj).astype(BF16)

    def head_sum(x, split=True):
        one = _dot_split if split else _dot
        return jnp.concatenate(
            [one(x[:, c * MXU_DIM:(c + 1) * MXU_DIM], head_ones) for c in range(d_r // MXU_DIM)], axis=1)

    kkr = k * kk_ref[...]
    kk = kkr / jnp.maximum(jnp.sqrt(head_sum(kkr * kkr, split=False)), 1e-12)
    k2 = k * (1.0 + (a - 1.0) * ka_ref[...])
    a_s = -kk
    b_s = kk * a

    ri = lax.broadcasted_iota(jnp.int32, (R, R), 0)
    rj = lax.broadcasted_iota(jnp.int32, (R, R), 1)
    tri = ((rj <= ri) & ((ri // C) == (rj // C))).astype(BF16)
    hi = logd.astype(BF16)
    rem = logd - hi.astype(F32)
    mid = rem.astype(BF16)
    low = (rem - mid.astype(F32)).astype(BF16)
    cum = (jnp.dot(tri, hi, preferred_element_type=F32) + jnp.dot(tri, mid, preferred_element_type=F32)
           + jnp.dot(tri, low, preferred_element_type=F32))
    p_incl = jnp.exp(cum)
    p_excl = jnp.exp(cum - logd)
    p_inv = jnp.exp(-cum)
    p_last = [p_incl[(q + 1) * C - 1:(q + 1) * C, :] for q in range(nb * CS)]
    p_last_rows = jnp.concatenate([jnp.broadcast_to(p, (C, d_r)) for p in p_last], axis=0)

    at = (a_s * p_excl).astype(BF16)
    rt = (r * p_incl).astype(BF16)
    bt = b_s * p_inv
    kt = k2 * p_inv
    bh = (bt * p_last_rows).astype(BF16)
    kh = (kt * p_last_rows).astype(BF16)
    bt = bt.astype(BF16)
    kt = kt.astype(BF16)
    vb = v.astype(BF16)

    HG = LANES // N
    W = HG * N
    GC = HG * C
    wi = lax.broadcasted_iota(jnp.int32, (GC, W), 0)
    wj = lax.broadcasted_iota(jnp.int32, (GC, W), 1)
    bd_on = ((wi // C) == (wj // N)).astype(BF16) > 0

    def bdiag(x):
        xb = x.astype(BF16)
        return jnp.where(bd_on, jnp.concatenate([xb] * HG, axis=0), jnp.zeros((), BF16))

    ti = lax.broadcasted_iota(jnp.int32, (2 * C, GC), 0)
    tj = lax.broadcasted_iota(jnp.int32, (2 * C, GC), 1) % C
    mask2 = ((ti < C) & (tj < ti)) | ((ti >= C) & (tj <= ti - C))
    li = lax.broadcasted_iota(jnp.int32, (C, GC), 0)
    lj = lax.broadcasted_iota(jnp.int32, (C, GC), 1) % C
    eye = (li == lj).astype(F32)
    level_masks = []
    s = 1
    while s < C:
        level_masks.append(((li // (2 * s)) == (lj // (2 * s))) & (((li // s) % 2) == 1) & (((lj // s) % 2) == 0))
        s *= 2
    colblk = lax.broadcasted_iota(jnp.int32, (N, W), 1) // N

    n_g = n_heads // HG
    units = [(q, g) for q in range(nb * CS) for g in range(n_g)]
    idx = range(len(units))
    rs = [slice(q * C, (q + 1) * C) for q, _ in units]
    ls = [slice(g * W, (g + 1) * W) for _, g in units]
    g_b = [jnp.where(mask2, _dot(jnp.concatenate([at[rs[i], ls[i]], rt[rs[i], ls[i]]], axis=0),
                                 bdiag(bt[rs[i], ls[i]]), _NT), 0.0) for i in idx]
    g_k = [jnp.where(mask2, _dot(jnp.concatenate([at[rs[i], ls[i]], rt[rs[i], ls[i]]], axis=0),
                                 bdiag(kt[rs[i], ls[i]]), _NT), 0.0) for i in idx]
    gv = [_dot(g_k[i], bdiag(vb[rs[i], ls[i]])) for i in idx]
    tinv = [eye + jnp.where(level_masks[0], g_b[i][:C], 0.0) for i in idx]
    for lm in level_masks[1:]:
        xs = [_dot(jnp.where(lm, g_b[i][:C], 0.0), bdiag(tinv[i])) for i in idx]
        tinv = [tinv[i] + _dot(tinv[i], bdiag(xs[i])) for i in idx]
    ta = [_dot(tinv[i], bdiag(at[rs[i], ls[i]])) for i in idx]
    tv = [_dot(tinv[i], bdiag(gv[i][:C])) for i in idx]
    ys = [None] * len(units)
    for b in range(nb):
        state = [state_ref[b * n_g + gi] for gi in range(n_g)]
        for cc in range(CS):
            ids = [(b * CS + cc) * n_g + gi for gi in range(n_g)]
            tr = [_dot(jnp.concatenate([ta[i].astype(BF16), rt[rs[i], ls[i]]], axis=0), bdiag(state[gi]), _NT)
                  for gi, i in enumerate(ids)]
            u = [tr[gi][:C] + tv[i] for gi, i in enumerate(ids)]
            for gi, i in enumerate(ids):
                ys[i] = tr[gi][C:] + gv[i][C:] + _dot(g_b[i][C:], bdiag(u[gi]))
                uv = jnp.concatenate([u[gi].astype(BF16), vb[rs[i], ls[i]]], axis=0)
                bk = jnp.concatenate([bh[rs[i], ls[i]], kh[rs[i], ls[i]]], axis=0)
                full = _dot(uv, bk, _TN)
                upd = state[gi] * p_last[units[i][0]][:, ls[i]]
                for h in range(HG):
                    upd = upd + jnp.where(colblk == h, full[h * N:(h + 1) * N, :], 0.0)
                state[gi] = upd
        for gi in range(n_g):
            state_ref[b * n_g + gi] = state[gi]
    y = jnp.concatenate([jnp.concatenate(ys[q * n_g:(q + 1) * n_g], axis=1) for q in range(nb * CS)],
                        axis=0)

    inv_n = 1.0 / N
    mu_y = head_sum(y) * inv_n
    yc = y - mu_y
    var_y = head_sum(yc * yc, split=False) * inv_n
    yn = yc * lax.rsqrt(var_y + GN_EPS) * gng_ref[...] + gnb_ref[...]
    bonus = head_sum(r * k2 * rk_ref[...]) * v
    y_ref[...] = ((yn + bonus) * g).reshape(nb, Cb, d_r).astype(y_ref.dtype)


def rwkv_mixer(proj_main, proj_lora, col0, shift_mu, w0, w_up, a0, a_up, g_up, k_k, k_a, r_k, gn_g, gn_b,
               *, batch, chunk=64, chunks_per_step=2):
    nt = proj_main.shape[0]
    seq = nt // batch
    d_r = w0.shape[-1]
    n_heads = d_r // RWKV_HEAD
    d_lora = proj_lora.shape[1]
    d_decay, d_aaa = w_up.shape[0], a_up.shape[0]
    rows = chunk * chunks_per_step
    n_steps = seq // rows
    assert seq % rows == 0 and col0 % d_r == 0
    cb = col0 // d_r
    row2 = lambda t: t.reshape(1, -1).astype(F32)
    mu_main = shift_mu[:3 * d_r].reshape(3, d_r)
    mu_lora = shift_mu[3 * d_r:].reshape(1, d_lora)
    pm3 = proj_main.reshape(batch, seq, proj_main.shape[1])
    pl3 = proj_lora.reshape(batch, seq, d_lora)

    def main_spec(j):
        return pl.BlockSpec((batch, rows, d_r), lambda c, j=j: (0, c, cb + j))

    full = lambda shape: pl.BlockSpec(shape, lambda c: (0,) * len(shape))
    kern = functools.partial(_rwkv_kernel, n_heads=n_heads, d_decay=d_decay, d_aaa=d_aaa, chunk=chunk)
    y = pl.pallas_call(
        kern,
        grid=(n_steps,),
        in_specs=[main_spec(0), main_spec(1), main_spec(2),
                  pl.BlockSpec((batch, rows, d_lora), lambda c: (0, c, 0)),
                  full((3, d_r)), full((1, d_lora)), full((1, d_r)), full((d_decay, d_r)), full((1, d_r)),
                  full((d_aaa, d_r)), full((g_up.shape[0], d_r)), full((1, d_r)), full((1, d_r)), full((1, d_r)),
                  full((1, d_r)), full((1, d_r))],
        out_specs=pl.BlockSpec((batch, rows, d_r), lambda c: (0, c, 0)),
        out_shape=jax.ShapeDtypeStruct((batch, seq, d_r), BF16),
        scratch_shapes=[pltpu.VMEM((batch * n_heads * RWKV_HEAD // LANES, RWKV_HEAD, LANES), F32),
                        pltpu.VMEM((batch, 3, d_r), F32),
                        pltpu.VMEM((batch, 1, d_lora), F32)],
        compiler_params=pltpu.CompilerParams(dimension_semantics=("arbitrary",), vmem_limit_bytes=VMEM_LIMIT),
        name="rwkv_mixer",
    )(pm3, pm3, pm3, pl3, mu_main, mu_lora, row2(w0), w_up.astype(BF16), row2(a0),
      a_up.astype(BF16), g_up.astype(BF16), row2(k_k), row2(k_a), row2(r_k), row2(gn_g), row2(gn_b))
    return y.reshape(nt, d_r)


def _rms(x, g):
    return x * lax.rsqrt(jnp.mean(x * x, axis=-1, keepdims=True) + RMS_EPS) * g


def _inproj_kernel(x_hbm, g_ref, wt_hbm, wlt_ref, o_ref, ol_ref, x_f, u_ref, w_f, xsem, wsem):
    i, j = pl.program_id(0), pl.program_id(1)
    n_i, n_j = pl.num_programs(0), pl.num_programs(1)
    tm = x_f.shape[0]
    tn = w_f.shape[1]
    step = i * n_j + j
    slot = lax.rem(step, 2)

    def fetch_w(jt, s):
        return pltpu.make_async_copy(wt_hbm.at[pl.ds(pl.multiple_of(jt * tn, tn), tn), :], w_f.at[s], wsem.at[s])

    def fetch_x(it):
        return pltpu.make_async_copy(x_hbm.at[pl.ds(pl.multiple_of(it * tm, tm), tm), :], x_f, xsem)

    @pl.when(step == 0)
    def _():
        fetch_x(0).start()
        fetch_w(0, 0).start(priority=1)

    @pl.when(step + 1 < n_i * n_j)
    def _():
        fetch_w(lax.rem(j + 1, n_j), 1 - slot).start(priority=1)

    @pl.when(j == 0)
    def _():
        fetch_x(i).wait()
        u = _rms(x_f[...], g_ref[...]).astype(BF16)
        u_ref[...] = u

        @pl.when(i + 1 < n_i)
        def _():
            fetch_x(i + 1).start()

        ol_ref[...] = lax.dot_general(u, wlt_ref[...], _NT, preferred_element_type=F32)

    fetch_w(j, slot).wait()
    o_ref[...] = lax.dot_general(u_ref[...], w_f[slot].astype(BF16), _NT, preferred_element_type=F32)


def in_proj(x2, norm_g, w_in, d_main, *, tm=2048, tn=512):
    nt, d = x2.shape
    d_lora = w_in.shape[1] - d_main
    assert nt % tm == 0 and d_main % tn == 0
    w_t = w_in.T
    w_lora_t = w_t[d_main:].astype(BF16)
    hbm = pl.BlockSpec(memory_space=pl.ANY)
    return pl.pallas_call(
        _inproj_kernel,
        grid=(nt // tm, d_main // tn),
        in_specs=[hbm, pl.BlockSpec((1, d), lambda i, j: (0, 0)), hbm,
                  pl.BlockSpec((d_lora, d), lambda i, j: (0, 0))],
        out_specs=[pl.BlockSpec((tm, tn), lambda i, j: (i, j)),
                   pl.BlockSpec((tm, d_lora), lambda i, j: (i, 0))],
        out_shape=[jax.ShapeDtypeStruct((nt, d_main), F32), jax.ShapeDtypeStruct((nt, d_lora), F32)],
        scratch_shapes=[pltpu.VMEM((tm, d), F32), pltpu.VMEM((tm, d), BF16), pltpu.VMEM((2, tn, d), F32),
                        pltpu.SemaphoreType.DMA(()), pltpu.SemaphoreType.DMA((2,))],
        compiler_params=pltpu.CompilerParams(dimension_semantics=("arbitrary", "arbitrary"),
                                             vmem_limit_bytes=VMEM_LIMIT_IN_PROJ),
        name="in_proj",
    )(x2, norm_g.reshape(1, d).astype(F32), w_t, w_lora_t)


CONV_HALO = 32
CONV_ROWS = 32


def _conv_kernel(val_ref, gate_ref, dw_ref, b_ref, g_ref, beta_ref, o_ref, ubuf_ref):
    tt = val_ref.shape[0]
    d_c = val_ref.shape[1]
    t_idx = pl.program_id(1)

    @pl.when(t_idx == 0)
    def _():
        ubuf_ref[0, 0:CONV_HALO, :] = jnp.zeros((CONV_HALO, d_c), F32)

    @pl.when(t_idx > 0)
    def _():
        ubuf_ref[0, 0:CONV_HALO, :] = ubuf_ref[0, tt:tt + CONV_HALO, :]

    ubuf_ref[0, CONV_HALO:CONV_HALO + tt, :] = val_ref[...] * jax.nn.sigmoid(gate_ref[...])
    n_sh = tt + CONV_HALO - SUBLANES
    for s in range(1, SUBLANES):
        for r in range(0, n_sh, CONV_ROWS):
            n = min(CONV_ROWS, n_sh - r)
            ubuf_ref[s, r:r + n, :] = ubuf_ref[0, r + s:r + s + n, :]
    dw = dw_ref[...]
    lead = CONV_HALO - (CONV_WIDTH - 1)
    for c in range(tt // CONV_ROWS):
        r0 = c * CONV_ROWS
        acc = jnp.zeros((CONV_ROWS, d_c), F32)
        for j in range(CONV_WIDTH):
            q, s = divmod(lead + j, SUBLANES)
            r = r0 + q * SUBLANES
            acc = acc + dw[j:j + 1, :] * ubuf_ref[s, r:r + CONV_ROWS, :]
        acc = acc + b_ref[...]
        mu = jnp.mean(acc, axis=-1, keepdims=True)
        cen = acc - mu
        var = jnp.mean(cen * cen, axis=-1, keepdims=True)
        yv = cen * lax.rsqrt(var + LN_EPS) * g_ref[...] + beta_ref[...]
        o_ref[r0:r0 + CONV_ROWS, :] = (yv * jax.nn.sigmoid(yv)).astype(o_ref.dtype)


def conv_mixer(proj_main, dw, bias, ln_g, ln_b, *, batch, tt=512):
    nt = proj_main.shape[0]
    seq = nt // batch
    d_c = dw.shape[1]
    n_t = seq // tt
    assert seq % tt == 0 and tt % CONV_ROWS == 0
    row = lambda t: t.reshape(1, d_c).astype(F32)
    full = lambda shape: pl.BlockSpec(shape, lambda b, t: (0, 0))
    return pl.pallas_call(
        _conv_kernel,
        grid=(batch, n_t),
        in_specs=[pl.BlockSpec((tt, d_c), lambda b, t: (b * n_t + t, 0)),
                  pl.BlockSpec((tt, d_c), lambda b, t: (b * n_t + t, 1)),
                  full((CONV_WIDTH, d_c)), full((1, d_c)), full((1, d_c)), full((1, d_c))],
        out_specs=pl.BlockSpec((tt, d_c), lambda b, t: (b * n_t + t, 0)),
        out_shape=jax.ShapeDtypeStruct((nt, d_c), BF16),
        scratch_shapes=[pltpu.VMEM((SUBLANES, tt + CONV_HALO, d_c), F32)],
        compiler_params=pltpu.CompilerParams(dimension_semantics=("arbitrary", "arbitrary"),
                                             vmem_limit_bytes=VMEM_LIMIT),
        name="conv_mixer",
    )(proj_main, proj_main, dw.astype(F32), row(bias), row(ln_g), row(ln_b))


def _out_router_kernel(x_ref, yc_ref, yr_ref, wc_ref, wr_ref, nf_ref, rw_hi_ref, rw_lo_ref, rb_ref,
                       h_ref, v_ref, eid_ref, gate_ref, count_ref, cnt_ref):
    tm = x_ref.shape[0]
    h = (x_ref[...] + jnp.dot(yc_ref[...], wc_ref[...], preferred_element_type=F32)
         + jnp.dot(yr_ref[...], wr_ref[...], preferred_element_type=F32))
    h_ref[...] = h

    @pl.when(pl.program_id(0) == 0)
    def _():
        cnt_ref[...] = jnp.zeros_like(cnt_ref)

    v = _rms(h, nf_ref[...])
    v_ref[...] = _pack_bf16_pairs(v)
    v_hi = v.astype(BF16)
    v_lo = (v - v_hi.astype(F32)).astype(BF16)
    logits = (jnp.dot(v_hi, rw_hi_ref[...], preferred_element_type=F32)
              + jnp.dot(v_lo, rw_hi_ref[...], preferred_element_type=F32)
              + jnp.dot(v_hi, rw_lo_ref[...], preferred_element_type=F32)) + rb_ref[...]
    lane = lax.broadcasted_iota(jnp.int32, (tm, LANES), 1)
    neg = jnp.float32(-jnp.inf)
    big = jnp.int32(LANES)

    def first_max(vals):
        m = jnp.max(vals, axis=-1, keepdims=True)
        return m, jnp.min(jnp.where(vals == m, lane, big), axis=-1, keepdims=True)

    gl = jnp.where(lane < N_GROUPS, logits, neg)
    gmax, grp = first_max(gl)
    p_grp = 1.0 / jnp.sum(jnp.exp(gl - gmax), axis=-1, keepdims=True)
    lo = N_GROUPS + grp * EXPERTS_PER_GROUP
    el = jnp.where((lane >= lo) & (lane < lo + EXPERTS_PER_GROUP), logits, neg)
    m1, i1 = first_max(el)
    m2, i2 = first_max(jnp.where(lane == i1, neg, el))
    e2 = jnp.exp(m2 - m1)
    g1 = p_grp / (1.0 + e2)
    g2 = p_grp * e2 / (1.0 + e2)
    gate_ref[...] = jnp.where(lane == 0, g1, jnp.where(lane == 1, g2, 0.0))

    x1 = i1 - N_GROUPS
    x2 = i2 - N_GROUPS
    oh1 = (lane == x1).astype(F32)
    oh2 = (lane == x2).astype(F32)
    earlier = (lax.broadcasted_iota(jnp.int32, (tm, tm), 1)
               < lax.broadcasted_iota(jnp.int32, (tm, tm), 0)).astype(BF16)
    before1 = jnp.dot(earlier, oh1.astype(BF16), preferred_element_type=F32)
    before2 = jnp.dot(earlier, oh2.astype(BF16), preferred_element_type=F32)
    carry = cnt_ref[...]
    n1 = jnp.sum(oh1, axis=0, keepdims=True)
    n2 = jnp.sum(oh2, axis=0, keepdims=True)
    rank1 = jnp.sum(oh1 * (before1 + carry), axis=-1, keepdims=True).astype(jnp.int32)
    rank2 = jnp.sum(oh2 * (before2 + carry + n1), axis=-1, keepdims=True).astype(jnp.int32)
    cnt_ref[...] = carry + n1 + n2
    count_ref[...] = (carry + n1 + n2).astype(jnp.int32)
    routing = jnp.where(lane == 0, x1, jnp.where(lane == 1, x2, jnp.where(
        lane == 2, rank1, jnp.where(lane == 3, rank2, 0))))
    eid_ref[...] = routing.T[:SUBLANES, :]


def out_router(x2, y_conv, y_rwkv, w_out, norm_ffn, rg_w, rg_b, re_w, re_b, *, tm=512):
    nt, d = x2.shape
    d_c, d_r = y_conv.shape[1], y_rwkv.shape[1]
    assert d_c == d_r and nt % tm == 0
    n_r = N_GROUPS + N_EXPERTS
    rw = jnp.zeros((d, LANES), F32).at[:, :n_r].set(jnp.concatenate([rg_w, re_w], axis=1).astype(F32))
    rw_hi = rw.astype(BF16)
    rw_lo = (rw - rw_hi.astype(F32)).astype(BF16)
    rb = jnp.zeros((1, LANES), F32).at[0, :n_r].set(jnp.concatenate([rg_b, re_b]).astype(F32))
    full = lambda shape: pl.BlockSpec(shape, lambda i: (0, 0))
    once = lambda shape, r: pl.BlockSpec(shape, lambda i, r=r: (r, 0), pipeline_mode=pl.Buffered(1))
    rows = lambda w: pl.BlockSpec((tm, w), lambda i: (i, 0))
    return pl.pallas_call(
        _out_router_kernel,
        grid=(nt // tm,),
        in_specs=[rows(d), rows(d_c), rows(d_r), once((d_c, d), 0), once((d_r, d), 1),
                  full((1, d)), full((d, LANES)), full((d, LANES)), full((1, LANES))],
        out_specs=[rows(d), rows(d // 2), pl.BlockSpec((SUBLANES, tm), lambda i: (0, i)), rows(LANES),
                   full((1, LANES))],
        out_shape=[jax.ShapeDtypeStruct((nt, d), F32), jax.ShapeDtypeStruct((nt, d // 2), jnp.uint32),
                   jax.ShapeDtypeStruct((SUBLANES, nt), jnp.int32), jax.ShapeDtypeStruct((nt, LANES), F32),
                   jax.ShapeDtypeStruct((1, LANES), jnp.int32)],
        scratch_shapes=[pltpu.VMEM((1, LANES), F32)],
        compiler_params=pltpu.CompilerParams(dimension_semantics=("arbitrary",), vmem_limit_bytes=VMEM_LIMIT),
        name="out_router",
    )(x2, y_conv, y_rwkv, w_out, w_out, norm_ffn.reshape(1, d).astype(F32), rw_hi, rw_lo, rb)


MOE_ROWS = 256
W_SLOTS = 3


def slot_plan(routing, counts, n_rows_pad):
    expert_id = routing[:TOP_K]
    rank = routing[TOP_K:2 * TOP_K]
    padded = (counts + MOE_ROWS - 1) // MOE_ROWS * MOE_ROWS
    pend = jnp.cumsum(padded)
    pstarts = pend - padded
    ids = jnp.arange(N_EXPERTS, dtype=jnp.int32)
    start_of = jnp.sum(jnp.where(expert_id[..., None] == ids, pstarts, 0), axis=-1)
    dest = (start_of + rank).astype(jnp.int32).T
    n_blocks = n_rows_pad // MOE_ROWS
    block_row0 = jnp.arange(n_blocks, dtype=jnp.int32) * MOE_ROWS
    block_e = jnp.minimum(jnp.sum((pend[None, :] <= block_row0[:, None]).astype(jnp.int32), axis=1),
                          N_EXPERTS - 1).astype(jnp.int32)
    n_used = (pend[-1:] // MOE_ROWS).astype(jnp.int32)
    owner = jnp.where(counts > 0, ids, N_EXPERTS)
    nxt = lax.cummin(jnp.concatenate([owner[1:], jnp.full((1,), N_EXPERTS, jnp.int32)]), reverse=True)
    next_used = jnp.where(nxt < N_EXPERTS, nxt, -1).astype(jnp.int32)
    hops = [next_used]
    for _ in range(W_SLOTS - 2):
        hops.append(jnp.where(hops[-1] >= 0, next_used[jnp.maximum(hops[-1], 0)], -1))
    next_used = jnp.stack(hops)
    used_ord = (jnp.cumsum((counts > 0).astype(jnp.int32)) - 1).astype(jnp.int32)
    pad0 = pstarts + counts
    n_single = jnp.minimum((-pad0) % SUBLANES, pend - pad0)
    tile0 = pad0 + n_single
    n_tile = (pend - tile0) // SUBLANES
    tail = jnp.stack([pend[-1] // MOE_ROWS, n_blocks - pend[-1] // MOE_ROWS, jnp.sum(n_single), jnp.sum(n_tile)])
    fill_plan = jnp.concatenate([pad0, n_single, tile0, n_tile, tail]).astype(jnp.int32)
    return dest, block_e, n_used, next_used, used_ord, fill_plan


def _dispatch_kernel(fill_ref, dest_ref, v_ref, xs_ref, zero_ref, sem, fill_sem):
    tm = v_ref.shape[0]

    def fills():
        e0, e1, e2, e3, e4 = (k * N_EXPERTS for k in range(5))
        single = lambda e, i: pltpu.make_async_copy(
            zero_ref.at[pl.ds(0, 1), :], xs_ref.at[pl.ds(fill_ref[e0 + e] + i, 1), :], fill_sem)
        tile = lambda e, i: pltpu.make_async_copy(
            zero_ref.at[pl.ds(0, SUBLANES), :],
            xs_ref.at[pl.ds(pl.multiple_of(fill_ref[e2 + e] + i * SUBLANES, SUBLANES), SUBLANES), :], fill_sem)
        block = lambda i: pltpu.make_async_copy(
            zero_ref, xs_ref.at[pl.ds(pl.multiple_of((fill_ref[e4] + i) * MOE_ROWS, MOE_ROWS), MOE_ROWS), :],
            fill_sem)
        return single, tile, block, e1, e3, e4

    @pl.when(pl.program_id(0) == 0)
    def _():
        zero_ref[...] = jnp.zeros_like(zero_ref)
        single, tile, block, e1, e3, e4 = fills()
        for e in range(N_EXPERTS):
            lax.fori_loop(0, fill_ref[e1 + e], lambda i, c, e=e: (single(e, i).start(), c)[1], 0)
            lax.fori_loop(0, fill_ref[e3 + e], lambda i, c, e=e: (tile(e, i).start(), c)[1], 0)
        lax.fori_loop(0, fill_ref[e4 + 1], lambda i, c: (block(i).start(), c)[1], 0)

    def copy(i, k):
        d = dest_ref[0, 0, i * TOP_K + k]
        return pltpu.make_async_copy(v_ref.at[pl.ds(i, 1), :], xs_ref.at[pl.ds(d, 1), :], sem)

    def start(i, carry):
        for k in range(TOP_K):
            copy(i, k).start(priority=k)
        return carry

    lax.fori_loop(0, tm, start, 0, unroll=8)
    for k in range(TOP_K):
        pltpu.make_async_copy(v_ref, xs_ref.at[pl.ds(0, tm), :], sem).wait()

    @pl.when(pl.program_id(0) == 0)
    def _():
        single, tile, block, e1, e3, e4 = fills()
        lax.fori_loop(0, fill_ref[e4 + 2], lambda i, c: (single(0, 0).wait(), c)[1], 0)
        lax.fori_loop(0, fill_ref[e4 + 3], lambda i, c: (tile(0, 0).wait(), c)[1], 0)
        lax.fori_loop(0, fill_ref[e4 + 1], lambda i, c: (block(0).wait(), c)[1], 0)


def dispatch(v, dest, fill_plan, n_rows_pad, *, tm=2048):
    nt, d = v.shape
    assert nt % tm == 0
    dest3 = dest.reshape(nt // tm, 1, tm * TOP_K)
    grid_spec = pltpu.PrefetchScalarGridSpec(
        num_scalar_prefetch=1,
        grid=(nt // tm,),
        in_specs=[pl.BlockSpec((1, 1, tm * TOP_K), lambda i, fp: (i, 0, 0), memory_space=pltpu.SMEM),
                  pl.BlockSpec((tm, d), lambda i, fp: (i, 0))],
        out_specs=pl.BlockSpec(memory_space=pl.ANY),
        scratch_shapes=[pltpu.VMEM((MOE_ROWS, d), v.dtype), pltpu.SemaphoreType.DMA(()),
                        pltpu.SemaphoreType.DMA(())],
    )
    return pl.pallas_call(
        _dispatch_kernel,
        grid_spec=grid_spec,
        out_shape=jax.ShapeDtypeStruct((n_rows_pad, d), v.dtype),
        compiler_params=pltpu.CompilerParams(dimension_semantics=("arbitrary",), vmem_limit_bytes=VMEM_LIMIT),
        name="moe_dispatch",
    )(fill_plan, dest3, v)


def _experts_kernel(be_ref, nu_ref, nxt_ref, ord_ref, xs_ref, wg_hbm, wu_hbm, wd_hbm, ys_ref,
                    wg_f, wu_f, wd_f, wg_s, wu_s, wd_s, wsem):
    j = pl.program_id(0)
    e = be_ref[j]
    ahead = W_SLOTS - 1

    def fetch(ex, slot):
        return [pltpu.make_async_copy(wg_hbm.at[ex], wg_f.at[slot], wsem.at[slot]),
                pltpu.make_async_copy(wu_hbm.at[ex], wu_f.at[slot], wsem.at[slot]),
                pltpu.make_async_copy(wd_hbm.at[ex], wd_f.at[slot], wsem.at[slot])]

    @pl.when(j == 0)
    def _():
        ex = e
        for k in range(ahead):
            @pl.when(ex >= 0)
            def _(ex=ex):
                for c in fetch(ex, lax.rem(ord_ref[ex], W_SLOTS)):
                    c.start(priority=1)
            ex = jnp.where(ex >= 0, nxt_ref[0, jnp.maximum(ex, 0)], -1)

    first = (j == 0) | (e != be_ref[jnp.maximum(j - 1, 0)])

    @pl.when(first & (j < nu_ref[0]))
    def _():
        slot = lax.rem(ord_ref[e], W_SLOTS)
        for c in fetch(e, slot):
            c.wait()
        nxt = nxt_ref[ahead - 1, e]

        @pl.when(nxt >= 0)
        def _():
            for c in fetch(nxt, lax.rem(ord_ref[nxt], W_SLOTS)):
                c.start(priority=1)

        wg_s[...] = wg_f[slot].astype(BF16)
        wu_s[...] = wu_f[slot].astype(BF16)
        wd_s[...] = wd_f[slot].astype(BF16)

    @pl.when(j < nu_ref[0])
    def _():
        xb = _unpack_bf16_pairs(xs_ref[...]).astype(BF16)
        gt = jnp.dot(xb, wg_s[...], preferred_element_type=F32)
        up = jnp.dot(xb, wu_s[...], preferred_element_type=F32)
        hid = (gt * jax.nn.sigmoid(gt) * up).astype(BF16)
        ys_ref[...] = _pack_bf16_pairs(jnp.dot(hid, wd_s[...], preferred_element_type=F32))

    @pl.when(j >= nu_ref[0])
    def _():
        ys_ref[...] = jnp.zeros_like(ys_ref)


def experts(xs, block_e, n_used, next_used, used_ord, w_gate, w_up, w_down, n_rows_pad):
    dh = xs.shape[1]
    d, d_e = w_gate.shape[1], w_gate.shape[2]
    assert d == 2 * dh
    n_blocks = n_rows_pad // MOE_ROWS
    hbm = pl.BlockSpec(memory_space=pl.ANY)
    grid_spec = pltpu.PrefetchScalarGridSpec(
        num_scalar_prefetch=4,
        grid=(n_blocks,),
        in_specs=[pl.BlockSpec((MOE_ROWS, dh), lambda j, be, nu, nx, od: (jnp.minimum(j, nu[0] - 1), 0)),
                  hbm, hbm, hbm],
        out_specs=pl.BlockSpec((MOE_ROWS, dh), lambda j, be, nu, nx, od: (j, 0)),
        scratch_shapes=[pltpu.VMEM((W_SLOTS, d, d_e), F32), pltpu.VMEM((W_SLOTS, d, d_e), F32),
                        pltpu.VMEM((W_SLOTS, d_e, d), F32),
                        pltpu.VMEM((d, d_e), BF16), pltpu.VMEM((d, d_e), BF16), pltpu.VMEM((d_e, d), BF16),
                        pltpu.SemaphoreType.DMA((W_SLOTS,))],
    )
    return pl.pallas_call(
        _experts_kernel,
        grid_spec=grid_spec,
        out_shape=jax.ShapeDtypeStruct((n_rows_pad, dh), jnp.uint32),
        compiler_params=pltpu.CompilerParams(dimension_semantics=("arbitrary",), vmem_limit_bytes=VMEM_LIMIT),
        name="moe_experts",
    )(block_e, n_used, next_used, used_ord, xs, w_gate, w_up, w_down)


def _combine_kernel(dest_ref, dest_next_ref, h_ref, gate_ref, nfin_ref, ys_ref, o_ref, ybuf, sem):
    step = pl.program_id(0)
    tm = h_ref.shape[0]
    slot = lax.rem(step, 2)

    def copy(d_ref, s, i, k):
        d = d_ref[0, 0, i * TOP_K + k]
        return pltpu.make_async_copy(ys_ref.at[pl.ds(d, 1), :], ybuf.at[s, k, pl.ds(i, 1), :], sem.at[s])

    def start_all(d_ref, s):
        def body(i, carry):
            for k in range(TOP_K):
                copy(d_ref, s, i, k).start(priority=1)
            return carry
        lax.fori_loop(0, tm, body, 0, unroll=8)

    @pl.when(step == 0)
    def _():
        start_all(dest_ref, 0)

    @pl.when(step + 1 < pl.num_programs(0))
    def _():
        start_all(dest_next_ref, 1 - slot)

    for k in range(TOP_K):
        pltpu.make_async_copy(ys_ref.at[pl.ds(0, tm), :], ybuf.at[slot, k], sem.at[slot]).wait()
    gates = gate_ref[...]
    moe = (gates[:, 0:1] * _unpack_bf16_pairs(ybuf[slot, 0])
           + gates[:, 1:2] * _unpack_bf16_pairs(ybuf[slot, 1]))
    o_ref[...] = _rms(h_ref[...] + moe, nfin_ref[...])


def combine(h, gates, dest, ys, norm_final, *, tm=512):
    nt, d = h.shape
    assert nt % tm == 0 and ys.shape[1] * 2 == d
    n_tiles = nt // tm
    dest3 = dest.reshape(n_tiles, 1, tm * TOP_K)
    return pl.pallas_call(
        _combine_kernel,
        grid=(n_tiles,),
        in_specs=[pl.BlockSpec((1, 1, tm * TOP_K), lambda i: (i, 0, 0), memory_space=pltpu.SMEM),
                  pl.BlockSpec((1, 1, tm * TOP_K), lambda i: (jnp.minimum(i + 1, n_tiles - 1), 0, 0),
                               memory_space=pltpu.SMEM),
                  pl.BlockSpec((tm, d), lambda i: (i, 0)),
                  pl.BlockSpec((tm, LANES), lambda i: (i, 0)),
                  pl.BlockSpec((1, d), lambda i: (0, 0)),
                  pl.BlockSpec(memory_space=pl.ANY)],
        out_specs=pl.BlockSpec((tm, d), lambda i: (i, 0)),
        out_shape=jax.ShapeDtypeStruct((nt, d), F32),
        scratch_shapes=[pltpu.VMEM((2, TOP_K, tm, d // 2), jnp.uint32), pltpu.SemaphoreType.DMA((2,))],
        compiler_params=pltpu.CompilerParams(dimension_semantics=("arbitrary",), vmem_limit_bytes=VMEM_LIMIT),
        name="moe_combine",
    )(dest3, dest3, h, gates, norm_final.reshape(1, d).astype(F32), ys)


def kernel(x, norm_mix, w_in, conv_dw, conv_b, conv_ln_g, conv_ln_b, shift_mu, w0, w_lora_up, a0, a_lora_up, g_lora_up, k_k, k_a, r_k, gn_g, gn_b, w_out, norm_ffn, router_group_w, router_group_b, router_expert_w, router_expert_b, expert_w_gate, expert_w_up, expert_w_down, norm_final):
    B, T, D = x.shape
    depth = w_in.shape[0]
    d_c = conv_dw.shape[2]
    d_r = w0.shape[1]
    d_main = 2 * d_c + 3 * d_r
    nt = B * T
    n_rows_pad = -(-(nt * TOP_K + N_EXPERTS * (MOE_ROWS - 1)) // MOE_ROWS) * MOE_ROWS
    h = x.reshape(nt, D)
    for l in range(depth):
        proj_main, proj_lora = in_proj(h, norm_mix[l], w_in[l], d_main)
        y_conv = conv_mixer(proj_main, conv_dw[l], conv_b[l], conv_ln_g[l], conv_ln_b[l], batch=B)
        y_rwkv = rwkv_mixer(proj_main, proj_lora, 2 * d_c, shift_mu[l], w0[l], w_lora_up[l], a0[l], a_lora_up[l],
                            g_lora_up[l], k_k[l], k_a[l], r_k[l].reshape(-1), gn_g[l], gn_b[l], batch=B)
        h, v, routing, gates, counts = out_router(h, y_conv, y_rwkv, w_out[l].astype(BF16), norm_ffn[l],
                                                  router_group_w[l], router_group_b[l], router_expert_w[l],
                                                  router_expert_b[l])
        dest, block_e, n_used, next_used, used_ord, fill_plan = slot_plan(routing, counts[0, :N_EXPERTS],
                                                                          n_rows_pad)
        xs = dispatch(v, dest, fill_plan, n_rows_pad)
        ys = experts(xs, block_e, n_used, next_used, used_ord, expert_w_gate[l], expert_w_up[l], expert_w_down[l],
                     n_rows_pad)
        assert depth == 1
        h = combine(h, gates, dest, ys, norm_final)
    return h.reshape(B, T, D)
```

```python
import functools

import jax
import jax.numpy as jnp
from jax import lax
from jax.experimental import pallas as pl
from jax.experimental.pallas import tpu as pltpu

F32 = jnp.float32
BF16 = jnp.bfloat16

RWKV_HEAD = 64
CONV_WIDTH = 31
N_GROUPS = 4
EXPERTS_PER_GROUP = 8
N_EXPERTS = N_GROUPS * EXPERTS_PER_GROUP
TOP_K = 2
RMS_EPS = 1e-6
LN_EPS = 1e-5
GN_EPS = 64e-5
LANES = 128
SUBLANES = 8
MXU_DIM = 256
VMEM_LIMIT = 56 * 1024 * 1024
VMEM_LIMIT_IN_PROJ = 60 * 1024 * 1024

_NT = (((1,), (1,)), ((), ()))
_TN = (((0,), (0,)), ((), ()))


def _dot(a, b, dims=None):
    a = a.astype(BF16)
    b = b.astype(BF16)
    if dims is None:
        return jnp.dot(a, b, preferred_element_type=F32)
    return lax.dot_general(a, b, dims, preferred_element_type=F32)


def _pack_bf16_pairs(x):
    w = x.shape[1] // 2
    lo = lax.bitcast_convert_type(x[:, :w].astype(BF16).astype(F32), jnp.uint32)
    hi = lax.bitcast_convert_type(x[:, w:].astype(BF16).astype(F32), jnp.uint32)
    return (lo >> 16) | (hi & jnp.uint32(0xFFFF0000))


def _unpack_bf16_pairs(p):
    lo = lax.bitcast_convert_type(p << 16, F32)
    hi = lax.bitcast_convert_type(p & jnp.uint32(0xFFFF0000), F32)
    return jnp.concatenate([lo, hi], axis=1)


def _rwkv_kernel(r_ref, k_ref, v_ref, lora_ref, mu_main_ref, mu_lora_ref, w0_ref, wup_ref, a0_ref, aup_ref,
                 gup_ref, kk_ref, ka_ref, rk_ref, gng_ref, gnb_ref, y_ref,
                 state_ref, prev_main_ref, prev_lora_ref, *, n_heads, d_decay, d_aaa, chunk):
    nb, Cb, d_r = r_ref.shape
    C = chunk
    CS = Cb // C
    N = RWKV_HEAD
    R = nb * Cb

    @pl.when(pl.program_id(0) == 0)
    def _():
        state_ref[...] = jnp.zeros_like(state_ref)
        prev_main_ref[...] = jnp.zeros_like(prev_main_ref)
        prev_lora_ref[...] = jnp.zeros_like(prev_lora_ref)

    row = lax.broadcasted_iota(jnp.int32, (R, 1), 0)

    def shifted(x_ref, prev_ref, j, mu):
        x = x_ref[...].reshape(R, x_ref.shape[2])
        xp = pltpu.roll(x, 1, 0)
        for b in range(nb):
            xp = jnp.where(row == b * Cb, prev_ref[b, j:j + 1, :], xp)
            prev_ref[b, j:j + 1, :] = x[(b + 1) * Cb - 1:(b + 1) * Cb, :]
        return x + (xp - x) * mu

    r = shifted(r_ref, prev_main_ref, 0, mu_main_ref[0:1, :])
    k = shifted(k_ref, prev_main_ref, 1, mu_main_ref[1:2, :])
    v = shifted(v_ref, prev_main_ref, 2, mu_main_ref[2:3, :])
    lo = shifted(lora_ref, prev_lora_ref, 0, mu_lora_ref[...])

    wd = lo[:, :d_decay]
    ad = lo[:, d_decay:d_decay + d_aaa]
    gd = lo[:, d_decay + d_aaa:]

    z = w0_ref[...] + _dot(jnp.tanh(wd), wup_ref[...])
    w = -(jnp.maximum(-z, 0.0) + jnp.log(1.0 + jnp.exp(-jnp.abs(z)))) - 0.5
    logd = -jnp.exp(w)
    a = jax.nn.sigmoid(a0_ref[...] + _dot(ad, aup_ref[...]))
    g = _dot(jax.nn.sigmoid(gd), gup_ref[...])

    li = lax.broadcasted_iota(jnp.int32, (MXU_DIM, MXU_DIM), 0) // N
    lj = lax.broadcasted_iota(jnp.int32, (MXU_DIM, MXU_DIM), 1) // N
    head_ones = (li == lj).astype(BF16)

    def head_sum(x):
        return jnp.concatenate(
            [_dot(x[:, c * MXU_DIM:(c + 1) * MXU_DIM], head_ones) for c in range(d_r // MXU_DIM)], axis=1)

    kkr = k * kk_ref[...]
    kk = kkr / jnp.maximum(jnp.sqrt(head_sum(kkr * kkr)), 1e-12)
    k2 = k * (1.0 + (a - 1.0) * ka_ref[...])
    a_s = -kk
    b_s = kk * a

    ri = lax.broadcasted_iota(jnp.int32, (R, R), 0)
    rj = lax.broadcasted_iota(jnp.int32, (R, R), 1)
    tri = ((rj <= ri) & ((ri // C) == (rj // C))).astype(BF16)
    hi = logd.astype(BF16)
    rem = logd - hi.astype(F32)
    mid = rem.astype(BF16)
    low = (rem - mid.astype(F32)).astype(BF16)
    cum = (jnp.dot(tri, hi, preferred_element_type=F32) + jnp.dot(tri, mid, preferred_element_type=F32)
           + jnp.dot(tri, low, preferred_element_type=F32))
    p_incl = jnp.exp(cum)
    p_excl = jnp.exp(cum - logd)
    p_inv = jnp.exp(-cum)
    p_last = [p_incl[(q + 1) * C - 1:(q + 1) * C, :] for q in range(nb * CS)]
    p_last_rows = jnp.concatenate([jnp.broadcast_to(p, (C, d_r)) for p in p_last], axis=0)

    at = (a_s * p_excl).astype(BF16)
    rt = (r * p_incl).astype(BF16)
    bt = b_s * p_inv
    kt = k2 * p_inv
    bh = (bt * p_last_rows).astype(BF16)
    kh = (kt * p_last_rows).astype(BF16)
    bt = bt.astype(BF16)
    kt = kt.astype(BF16)
    vb = v.astype(BF16)

    HG = LANES // N
    W = HG * N
    GC = HG * C
    wi = lax.broadcasted_iota(jnp.int32, (GC, W), 0)
    wj = lax.broadcasted_iota(jnp.int32, (GC, W), 1)
    bd_on = ((wi // C) == (wj // N)).astype(BF16) > 0

    def bdiag(x):
        xb = x.astype(BF16)
        return jnp.where(bd_on, jnp.concatenate([xb] * HG, axis=0), jnp.zeros((), BF16))

    ti = lax.broadcasted_iota(jnp.int32, (2 * C, GC), 0)
    tj = lax.broadcasted_iota(jnp.int32, (2 * C, GC), 1) % C
    mask2 = ((ti < C) & (tj < ti)) | ((ti >= C) & (tj <= ti - C))
    li = lax.broadcasted_iota(jnp.int32, (C, GC), 0)
    lj = lax.broadcasted_iota(jnp.int32, (C, GC), 1) % C
    eye = (li == lj).astype(F32)
    level_masks = []
    s = 1
    while s < C:
        level_masks.append(((li // (2 * s)) == (lj // (2 * s))) & (((li // s) % 2) == 1) & (((lj // s) % 2) == 0))
        s *= 2
    colblk = lax.broadcasted_iota(jnp.int32, (N, W), 1) // N

    n_g = n_heads // HG
    units = [(q, g) for q in range(nb * CS) for g in range(n_g)]
    idx = range(len(units))
    rs = [slice(q * C, (q + 1) * C) for q, _ in units]
    ls = [slice(g * W, (g + 1) * W) for _, g in units]
    g_b = [jnp.where(mask2, _dot(jnp.concatenate([at[rs[i], ls[i]], rt[rs[i], ls[i]]], axis=0),
                                 bdiag(bt[rs[i], ls[i]]), _NT), 0.0) for i in idx]
    g_k = [jnp.where(mask2, _dot(jnp.concatenate([at[rs[i], ls[i]], rt[rs[i], ls[i]]], axis=0),
                                 bdiag(kt[rs[i], ls[i]]), _NT), 0.0) for i in idx]
    gv = [_dot(g_k[i], bdiag(vb[rs[i], ls[i]])) for i in idx]
    tinv = [eye + jnp.where(level_masks[0], g_b[i][:C], 0.0) for i in idx]
    for lm in level_masks[1:]:
        xs = [_dot(jnp.where(lm, g_b[i][:C], 0.0), bdiag(tinv[i])) for i in idx]
        tinv = [tinv[i] + _dot(tinv[i], bdiag(xs[i])) for i in idx]
    ta = [_dot(tinv[i], bdiag(at[rs[i], ls[i]])) for i in idx]
    tv = [_dot(tinv[i], bdiag(gv[i][:C])) for i in idx]
    ys = [None] * len(units)
    for b in range(nb):
        state = [state_ref[b * n_g + gi] for gi in range(n_g)]
        for cc in range(CS):
            ids = [(b * CS + cc) * n_g + gi for gi in range(n_g)]
            tr = [_dot(jnp.concatenate([ta[i].astype(BF16), rt[rs[i], ls[i]]], axis=0), bdiag(state[gi]), _NT)
                  for gi, i in enumerate(ids)]
            u = [tr[gi][:C] + tv[i] for gi, i in enumerate(ids)]
            for gi, i in enumerate(ids):
                ys[i] = tr[gi][C:] + gv[i][C:] + _dot(g_b[i][C:], bdiag(u[gi]))
                uv = jnp.concatenate([u[gi].astype(BF16), vb[rs[i], ls[i]]], axis=0)
                bk = jnp.concatenate([bh[rs[i], ls[i]], kh[rs[i], ls[i]]], axis=0)
                full = _dot(uv, bk, _TN)
                upd = state[gi] * p_last[units[i][0]][:, ls[i]]
                for h in range(HG):
                    upd = upd + jnp.where(colblk == h, full[h * N:(h + 1) * N, :], 0.0)
                state[gi] = upd
        for gi in range(n_g):
            state_ref[b * n_g + gi] = state[gi]
    y = jnp.concatenate([jnp.concatenate(ys[q * n_g:(q + 1) * n_g], axis=1) for q in range(nb * CS)],
                        axis=0)

    inv_n = 1.0 / N
    mu_y = head_sum(y) * inv_n
    yc = y - mu_y
    var_y = head_sum(yc * yc) * inv_n
    yn = yc * lax.rsqrt(var_y + GN_EPS) * gng_ref[...] + gnb_ref[...]
    bonus = head_sum(r * k2 * rk_ref[...]) * v
    y_ref[...] = ((yn + bonus) * g).reshape(nb, Cb, d_r).astype(y_ref.dtype)


def rwkv_mixer(proj_main, proj_lora, col0, shift_mu, w0, w_up, a0, a_up, g_up, k_k, k_a, r_k, gn_g, gn_b,
               *, batch, chunk=64, chunks_per_step=2):
    nt = proj_main.shape[0]
    seq = nt // batch
    d_r = w0.shape[-1]
    n_heads = d_r // RWKV_HEAD
    d_lora = proj_lora.shape[1]
    d_decay, d_aaa = w_up.shape[0], a_up.shape[0]
    rows = chunk * chunks_per_step
    n_steps = seq // rows
    assert seq % rows == 0 and col0 % d_r == 0
    cb = col0 // d_r
    row2 = lambda t: t.reshape(1, -1).astype(F32)
    mu_main = shift_mu[:3 * d_r].reshape(3, d_r)
    mu_lora = shift_mu[3 * d_r:].reshape(1, d_lora)
    pm3 = proj_main.reshape(batch, seq, proj_main.shape[1])
    pl3 = proj_lora.reshape(batch, seq, d_lora)

    def main_spec(j):
        return pl.BlockSpec((batch, rows, d_r), lambda c, j=j: (0, c, cb + j))

    full = lambda shape: pl.BlockSpec(shape, lambda c: (0,) * len(shape))
    kern = functools.partial(_rwkv_kernel, n_heads=n_heads, d_decay=d_decay, d_aaa=d_aaa, chunk=chunk)
    y = pl.pallas_call(
        kern,
        grid=(n_steps,),
        in_specs=[main_spec(0), main_spec(1), main_spec(2),
                  pl.BlockSpec((batch, rows, d_lora), lambda c: (0, c, 0)),
                  full((3, d_r)), full((1, d_lora)), full((1, d_r)), full((d_decay, d_r)), full((1, d_r)),
                  full((d_aaa, d_r)), full((g_up.shape[0], d_r)), full((1, d_r)), full((1, d_r)), full((1, d_r)),
                  full((1, d_r)), full((1, d_r))],
        out_specs=pl.BlockSpec((batch, rows, d_r), lambda c: (0, c, 0)),
        out_shape=jax.ShapeDtypeStruct((batch, seq, d_r), BF16),
        scratch_shapes=[pltpu.VMEM((batch * n_heads * RWKV_HEAD // LANES, RWKV_HEAD, LANES), F32),
                        pltpu.VMEM((batch, 3, d_r), F32),
                        pltpu.VMEM((batch, 1, d_lora), F32)],
        compiler_params=pltpu.CompilerParams(dimension_semantics=("arbitrary",), vmem_limit_bytes=VMEM_LIMIT),
        name="rwkv_mixer",
    )(pm3, pm3, pm3, pl3, mu_main, mu_lora, row2(w0), w_up.astype(BF16), row2(a0),
      a_up.astype(BF16), g_up.astype(BF16), row2(k_k), row2(k_a), row2(r_k), row2(gn_g), row2(gn_b))
    return y.reshape(nt, d_r)


def _rms(x, g):
    return x * lax.rsqrt(jnp.mean(x * x, axis=-1, keepdims=True) + RMS_EPS) * g


def _inproj_kernel(x_hbm, g_ref, wt_hbm, wlt_ref, o_ref, ol_ref, x_f, u_ref, w_f, xsem, wsem):
    i, j = pl.program_id(0), pl.program_id(1)
    n_i, n_j = pl.num_programs(0), pl.num_programs(1)
    tm = x_f.shape[0]
    tn = w_f.shape[1]
    step = i * n_j + j
    slot = lax.rem(step, 2)

    def fetch_w(jt, s):
        return pltpu.make_async_copy(wt_hbm.at[pl.ds(pl.multiple_of(jt * tn, tn), tn), :], w_f.at[s], wsem.at[s])

    def fetch_x(it):
        return pltpu.make_async_copy(x_hbm.at[pl.ds(pl.multiple_of(it * tm, tm), tm), :], x_f, xsem)

    @pl.when(step == 0)
    def _():
        fetch_x(0).start()
        fetch_w(0, 0).start(priority=1)

    @pl.when(step + 1 < n_i * n_j)
    def _():
        fetch_w(lax.rem(j + 1, n_j), 1 - slot).start(priority=1)

    @pl.when(j == 0)
    def _():
        fetch_x(i).wait()
        u = _rms(x_f[...], g_ref[...]).astype(BF16)
        u_ref[...] = u

        @pl.when(i + 1 < n_i)
        def _():
            fetch_x(i + 1).start()

        ol_ref[...] = lax.dot_general(u, wlt_ref[...], _NT, preferred_element_type=F32)

    fetch_w(j, slot).wait()
    o_ref[...] = lax.dot_general(u_ref[...], w_f[slot].astype(BF16), _NT, preferred_element_type=F32)


def in_proj(x2, norm_g, w_in, d_main, *, tm=2048, tn=512):
    nt, d = x2.shape
    d_lora = w_in.shape[1] - d_main
    assert nt % tm == 0 and d_main % tn == 0
    w_t = w_in.T
    w_lora_t = w_t[d_main:].astype(BF16)
    hbm = pl.BlockSpec(memory_space=pl.ANY)
    return pl.pallas_call(
        _inproj_kernel,
        grid=(nt // tm, d_main // tn),
        in_specs=[hbm, pl.BlockSpec((1, d), lambda i, j: (0, 0)), hbm,
                  pl.BlockSpec((d_lora, d), lambda i, j: (0, 0))],
        out_specs=[pl.BlockSpec((tm, tn), lambda i, j: (i, j)),
                   pl.BlockSpec((tm, d_lora), lambda i, j: (i, 0))],
        out_shape=[jax.ShapeDtypeStruct((nt, d_main), F32), jax.ShapeDtypeStruct((nt, d_lora), F32)],
        scratch_shapes=[pltpu.VMEM((tm, d), F32), pltpu.VMEM((tm, d), BF16), pltpu.VMEM((2, tn, d), F32),
                        pltpu.SemaphoreType.DMA(()), pltpu.SemaphoreType.DMA((2,))],
        compiler_params=pltpu.CompilerParams(dimension_semantics=("arbitrary", "arbitrary"),
                                             vmem_limit_bytes=VMEM_LIMIT_IN_PROJ),
        name="in_proj",
    )(x2, norm_g.reshape(1, d).astype(F32), w_t, w_lora_t)


CONV_HALO = 32
CONV_ROWS = 32


def _conv_kernel(val_ref, gate_ref, dw_ref, b_ref, g_ref, beta_ref, o_ref, ubuf_ref):
    tt = val_ref.shape[0]
    d_c = val_ref.shape[1]
    t_idx = pl.program_id(1)

    @pl.when(t_idx == 0)
    def _():
        ubuf_ref[0, 0:CONV_HALO, :] = jnp.zeros((CONV_HALO, d_c), F32)

    @pl.when(t_idx > 0)
    def _():
        ubuf_ref[0, 0:CONV_HALO, :] = ubuf_ref[0, tt:tt + CONV_HALO, :]

    ubuf_ref[0, CONV_HALO:CONV_HALO + tt, :] = val_ref[...] * jax.nn.sigmoid(gate_ref[...])
    n_sh = tt + CONV_HALO - SUBLANES
    for s in range(1, SUBLANES):
        for r in range(0, n_sh, CONV_ROWS):
            n = min(CONV_ROWS, n_sh - r)
            ubuf_ref[s, r:r + n, :] = ubuf_ref[0, r + s:r + s + n, :]
    dw = dw_ref[...]
    lead = CONV_HALO - (CONV_WIDTH - 1)
    for c in range(tt // CONV_ROWS):
        r0 = c * CONV_ROWS
        acc = jnp.zeros((CONV_ROWS, d_c), F32)
        for j in range(CONV_WIDTH):
            q, s = divmod(lead + j, SUBLANES)
            r = r0 + q * SUBLANES
            acc = acc + dw[j:j + 1, :] * ubuf_ref[s, r:r + CONV_ROWS, :]
        acc = acc + b_ref[...]
        mu = jnp.mean(acc, axis=-1, keepdims=True)
        cen = acc - mu
        var = jnp.mean(cen * cen, axis=-1, keepdims=True)
        yv = cen * lax.rsqrt(var + LN_EPS) * g_ref[...] + beta_ref[...]
        o_ref[r0:r0 + CONV_ROWS, :] = (yv * jax.nn.sigmoid(yv)).astype(o_ref.dtype)


def conv_mixer(proj_main, dw, bias, ln_g, ln_b, *, batch, tt=512):
    nt = proj_main.shape[0]
    seq = nt // batch
    d_c = dw.shape[1]
    n_t = seq // tt
    assert seq % tt == 0 and tt % CONV_ROWS == 0
    row = lambda t: t.reshape(1, d_c).astype(F32)
    full = lambda shape: pl.BlockSpec(shape, lambda b, t: (0, 0))
    return pl.pallas_call(
        _conv_kernel,
        grid=(batch, n_t),
        in_specs=[pl.BlockSpec((tt, d_c), lambda b, t: (b * n_t + t, 0)),
                  pl.BlockSpec((tt, d_c), lambda b, t: (b * n_t + t, 1)),
                  full((CONV_WIDTH, d_c)), full((1, d_c)), full((1, d_c)), full((1, d_c))],
        out_specs=pl.BlockSpec((tt, d_c), lambda b, t: (b * n_t + t, 0)),
        out_shape=jax.ShapeDtypeStruct((nt, d_c), BF16),
        scratch_shapes=[pltpu.VMEM((SUBLANES, tt + CONV_HALO, d_c), F32)],
        compiler_params=pltpu.CompilerParams(dimension_semantics=("arbitrary", "arbitrary"),
                                             vmem_limit_bytes=VMEM_LIMIT),
        name="conv_mixer",
    )(proj_main, proj_main, dw.astype(F32), row(bias), row(ln_g), row(ln_b))


def _out_router_kernel(x_ref, yc_ref, yr_ref, wc_ref, wr_ref, nf_ref, rw_hi_ref, rw_lo_ref, rb_ref,
                       h_ref, v_ref, eid_ref, gate_ref, count_ref, cnt_ref):
    tm = x_ref.shape[0]
    h = (x_ref[...] + jnp.dot(yc_ref[...], wc_ref[...], preferred_element_type=F32)
         + jnp.dot(yr_ref[...], wr_ref[...], preferred_element_type=F32))
    h_ref[...] = h

    @pl.when(pl.program_id(0) == 0)
    def _():
        cnt_ref[...] = jnp.zeros_like(cnt_ref)

    v = _rms(h, nf_ref[...])
    v_ref[...] = _pack_bf16_pairs(v)
    v_hi = v.astype(BF16)
    v_lo = (v - v_hi.astype(F32)).astype(BF16)
    logits = (jnp.dot(v_hi, rw_hi_ref[...], preferred_element_type=F32)
              + jnp.dot(v_lo, rw_hi_ref[...], preferred_element_type=F32)
              + jnp.dot(v_hi, rw_lo_ref[...], preferred_element_type=F32)) + rb_ref[...]
    lane = lax.broadcasted_iota(jnp.int32, (tm, LANES), 1)
    neg = jnp.float32(-jnp.inf)
    big = jnp.int32(LANES)

    def first_max(vals):
        m = jnp.max(vals, axis=-1, keepdims=True)
        return m, jnp.min(jnp.where(vals == m, lane, big), axis=-1, keepdims=True)

    gl = jnp.where(lane < N_GROUPS, logits, neg)
    gmax, grp = first_max(gl)
    p_grp = 1.0 / jnp.sum(jnp.exp(gl - gmax), axis=-1, keepdims=True)
    lo = N_GROUPS + grp * EXPERTS_PER_GROUP
    el = jnp.where((lane >= lo) & (lane < lo + EXPERTS_PER_GROUP), logits, neg)
    m1, i1 = first_max(el)
    m2, i2 = first_max(jnp.where(lane == i1, neg, el))
    e2 = jnp.exp(m2 - m1)
    g1 = p_grp / (1.0 + e2)
    g2 = p_grp * e2 / (1.0 + e2)
    gate_ref[...] = jnp.where(lane == 0, g1, jnp.where(lane == 1, g2, 0.0))

    x1 = i1 - N_GROUPS
    x2 = i2 - N_GROUPS
    oh1 = (lane == x1).astype(F32)
    oh2 = (lane == x2).astype(F32)
    earlier = (lax.broadcasted_iota(jnp.int32, (tm, tm), 1)
               < lax.broadcasted_iota(jnp.int32, (tm, tm), 0)).astype(BF16)
    before1 = jnp.dot(earlier, oh1.astype(BF16), preferred_element_type=F32)
    before2 = jnp.dot(earlier, oh2.astype(BF16), preferred_element_type=F32)
    carry = cnt_ref[...]
    n1 = jnp.sum(oh1, axis=0, keepdims=True)
    n2 = jnp.sum(oh2, axis=0, keepdims=True)
    rank1 = jnp.sum(oh1 * (before1 + carry), axis=-1, keepdims=True).astype(jnp.int32)
    rank2 = jnp.sum(oh2 * (before2 + carry + n1), axis=-1, keepdims=True).astype(jnp.int32)
    cnt_ref[...] = carry + n1 + n2
    count_ref[...] = (carry + n1 + n2).astype(jnp.int32)
    routing = jnp.where(lane == 0, x1, jnp.where(lane == 1, x2, jnp.where(
        lane == 2, rank1, jnp.where(lane == 3, rank2, 0))))
    eid_ref[...] = routing.T[:SUBLANES, :]


def out_router(x2, y_conv, y_rwkv, w_out, norm_ffn, rg_w, rg_b, re_w, re_b, *, tm=512):
    nt, d = x2.shape
    d_c, d_r = y_conv.shape[1], y_rwkv.shape[1]
    assert d_c == d_r and nt % tm == 0
    n_r = N_GROUPS + N_EXPERTS
    rw = jnp.zeros((d, LANES), F32).at[:, :n_r].set(jnp.concatenate([rg_w, re_w], axis=1).astype(F32))
    rw_hi = rw.astype(BF16)
    rw_lo = (rw - rw_hi.astype(F32)).astype(BF16)
    rb = jnp.zeros((1, LANES), F32).at[0, :n_r].set(jnp.concatenate([rg_b, re_b]).astype(F32))
    full = lambda shape: pl.BlockSpec(shape, lambda i: (0, 0))
    once = lambda shape, r: pl.BlockSpec(shape, lambda i, r=r: (r, 0), pipeline_mode=pl.Buffered(1))
    rows = lambda w: pl.BlockSpec((tm, w), lambda i: (i, 0))
    return pl.pallas_call(
        _out_router_kernel,
        grid=(nt // tm,),
        in_specs=[rows(d), rows(d_c), rows(d_r), once((d_c, d), 0), once((d_r, d), 1),
                  full((1, d)), full((d, LANES)), full((d, LANES)), full((1, LANES))],
        out_specs=[rows(d), rows(d // 2), pl.BlockSpec((SUBLANES, tm), lambda i: (0, i)), rows(LANES),
                   full((1, LANES))],
        out_shape=[jax.ShapeDtypeStruct((nt, d), F32), jax.ShapeDtypeStruct((nt, d // 2), jnp.uint32),
                   jax.ShapeDtypeStruct((SUBLANES, nt), jnp.int32), jax.ShapeDtypeStruct((nt, LANES), F32),
                   jax.ShapeDtypeStruct((1, LANES), jnp.int32)],
        scratch_shapes=[pltpu.VMEM((1, LANES), F32)],
        compiler_params=pltpu.CompilerParams(dimension_semantics=("arbitrary",), vmem_limit_bytes=VMEM_LIMIT),
        name="out_router",
    )(x2, y_conv, y_rwkv, w_out, w_out, norm_ffn.reshape(1, d).astype(F32), rw_hi, rw_lo, rb)


MOE_ROWS = 256
W_SLOTS = 3


def slot_plan(routing, counts, n_rows_pad):
    expert_id = routing[:TOP_K]
    rank = routing[TOP_K:2 * TOP_K]
    padded = (counts + MOE_ROWS - 1) // MOE_ROWS * MOE_ROWS
    pend = jnp.cumsum(padded)
    pstarts = pend - padded
    ids = jnp.arange(N_EXPERTS, dtype=jnp.int32)
    start_of = jnp.sum(jnp.where(expert_id[..., None] == ids, pstarts, 0), axis=-1)
    dest = (start_of + rank).astype(jnp.int32).T
    n_blocks = n_rows_pad // MOE_ROWS
    block_row0 = jnp.arange(n_blocks, dtype=jnp.int32) * MOE_ROWS
    block_e = jnp.minimum(jnp.sum((pend[None, :] <= block_row0[:, None]).astype(jnp.int32), axis=1),
                          N_EXPERTS - 1).astype(jnp.int32)
    n_used = (pend[-1:] // MOE_ROWS).astype(jnp.int32)
    owner = jnp.where(counts > 0, ids, N_EXPERTS)
    nxt = lax.cummin(jnp.concatenate([owner[1:], jnp.full((1,), N_EXPERTS, jnp.int32)]), reverse=True)
    next_used = jnp.where(nxt < N_EXPERTS, nxt, -1).astype(jnp.int32)
    hops = [next_used]
    for _ in range(W_SLOTS - 2):
        hops.append(jnp.where(hops[-1] >= 0, next_used[jnp.maximum(hops[-1], 0)], -1))
    next_used = jnp.stack(hops)
    used_ord = (jnp.cumsum((counts > 0).astype(jnp.int32)) - 1).astype(jnp.int32)
    pad0 = pstarts + counts
    n_single = jnp.minimum((-pad0) % SUBLANES, pend - pad0)
    tile0 = pad0 + n_single
    n_tile = (pend - tile0) // SUBLANES
    tail = jnp.stack([pend[-1] // MOE_ROWS, n_blocks - pend[-1] // MOE_ROWS, jnp.sum(n_single), jnp.sum(n_tile)])
    fill_plan = jnp.concatenate([pad0, n_single, tile0, n_tile, tail]).astype(jnp.int32)
    return dest, block_e, n_used, next_used, used_ord, fill_plan


def _dispatch_kernel(fill_ref, dest_ref, v_ref, xs_ref, zero_ref, sem, fill_sem):
    tm = v_ref.shape[0]

    def fills():
        e0, e1, e2, e3, e4 = (k * N_EXPERTS for k in range(5))
        single = lambda e, i: pltpu.make_async_copy(
            zero_ref.at[pl.ds(0, 1), :], xs_ref.at[pl.ds(fill_ref[e0 + e] + i, 1), :], fill_sem)
        tile = lambda e, i: pltpu.make_async_copy(
            zero_ref.at[pl.ds(0, SUBLANES), :],
            xs_ref.at[pl.ds(pl.multiple_of(fill_ref[e2 + e] + i * SUBLANES, SUBLANES), SUBLANES), :], fill_sem)
        block = lambda i: pltpu.make_async_copy(
            zero_ref, xs_ref.at[pl.ds(pl.multiple_of((fill_ref[e4] + i) * MOE_ROWS, MOE_ROWS), MOE_ROWS), :],
            fill_sem)
        return single, tile, block, e1, e3, e4

    @pl.when(pl.program_id(0) == 0)
    def _():
        zero_ref[...] = jnp.zeros_like(zero_ref)
        single, tile, block, e1, e3, e4 = fills()
        for e in range(N_EXPERTS):
            lax.fori_loop(0, fill_ref[e1 + e], lambda i, c, e=e: (single(e, i).start(), c)[1], 0)
            lax.fori_loop(0, fill_ref[e3 + e], lambda i, c, e=e: (tile(e, i).start(), c)[1], 0)
        lax.fori_loop(0, fill_ref[e4 + 1], lambda i, c: (block(i).start(), c)[1], 0)

    def copy(i, k):
        d = dest_ref[0, 0, i * TOP_K + k]
        return pltpu.make_async_copy(v_ref.at[pl.ds(i, 1), :], xs_ref.at[pl.ds(d, 1), :], sem)

    def start(i, carry):
        for k in range(TOP_K):
            copy(i, k).start(priority=k)
        return carry

    lax.fori_loop(0, tm, start, 0, unroll=8)
    for k in range(TOP_K):
        pltpu.make_async_copy(v_ref, xs_ref.at[pl.ds(0, tm), :], sem).wait()

    @pl.when(pl.program_id(0) == 0)
    def _():
        single, tile, block, e1, e3, e4 = fills()
        lax.fori_loop(0, fill_ref[e4 + 2], lambda i, c: (single(0, 0).wait(), c)[1], 0)
        lax.fori_loop(0, fill_ref[e4 + 3], lambda i, c: (tile(0, 0).wait(), c)[1], 0)
        lax.fori_loop(0, fill_ref[e4 + 1], lambda i, c: (block(0).wait(), c)[1], 0)


def dispatch(v, dest, fill_plan, n_rows_pad, *, tm=2048):
    nt, d = v.shape
    assert nt % tm == 0
    dest3 = dest.reshape(nt // tm, 1, tm * TOP_K)
    grid_spec = pltpu.PrefetchScalarGridSpec(
        num_scalar_prefetch=1,
        grid=(nt // tm,),
        in_specs=[pl.BlockSpec((1, 1, tm * TOP_K), lambda i, fp: (i, 0, 0), memory_space=pltpu.SMEM),
                  pl.BlockSpec((tm, d), lambda i, fp: (i, 0))],
        out_specs=pl.BlockSpec(memory_space=pl.ANY),
        scratch_shapes=[pltpu.VMEM((MOE_ROWS, d), v.dtype), pltpu.SemaphoreType.DMA(()),
                        pltpu.SemaphoreType.DMA(())],
    )
    return pl.pallas_call(
        _dispatch_kernel,
        grid_spec=grid_spec,
        out_shape=jax.ShapeDtypeStruct((n_rows_pad, d), v.dtype),
        compiler_params=pltpu.CompilerParams(dimension_semantics=("arbitrary",), vmem_limit_bytes=VMEM_LIMIT),
        name="moe_dispatch",
    )(fill_plan, dest3, v)


def _experts_kernel(be_ref, nu_ref, nxt_ref, ord_ref, xs_ref, wg_hbm, wu_hbm, wd_hbm, ys_ref,
                    wg_f, wu_f, wd_f, wg_s, wu_s, wd_s, wsem):
    j = pl.program_id(0)
    e = be_ref[j]
    ahead = W_SLOTS - 1

    def fetch(ex, slot):
        return [pltpu.make_async_copy(wg_hbm.at[ex], wg_f.at[slot], wsem.at[slot]),
                pltpu.make_async_copy(wu_hbm.at[ex], wu_f.at[slot], wsem.at[slot]),
                pltpu.make_async_copy(wd_hbm.at[ex], wd_f.at[slot], wsem.at[slot])]

    @pl.when(j == 0)
    def _():
        ex = e
        for k in range(ahead):
            @pl.when(ex >= 0)
            def _(ex=ex):
                for c in fetch(ex, lax.rem(ord_ref[ex], W_SLOTS)):
                    c.start(priority=1)
            ex = jnp.where(ex >= 0, nxt_ref[0, jnp.maximum(ex, 0)], -1)

    first = (j == 0) | (e != be_ref[jnp.maximum(j - 1, 0)])

    @pl.when(first & (j < nu_ref[0]))
    def _():
        slot = lax.rem(ord_ref[e], W_SLOTS)
        for c in fetch(e, slot):
            c.wait()
        nxt = nxt_ref[ahead - 1, e]

        @pl.when(nxt >= 0)
        def _():
            for c in fetch(nxt, lax.rem(ord_ref[nxt], W_SLOTS)):
                c.start(priority=1)

        wg_s[...] = wg_f[slot].astype(BF16)
        wu_s[...] = wu_f[slot].astype(BF16)
        wd_s[...] = wd_f[slot].astype(BF16)

    @pl.when(j < nu_ref[0])
    def _():
        xb = _unpack_bf16_pairs(xs_ref[...]).astype(BF16)
        gt = jnp.dot(xb, wg_s[...], preferred_element_type=F32)
        up = jnp.dot(xb, wu_s[...], preferred_element_type=F32)
        hid = (gt * jax.nn.sigmoid(gt) * up).astype(BF16)
        ys_ref[...] = _pack_bf16_pairs(jnp.dot(hid, wd_s[...], preferred_element_type=F32))

    @pl.when(j >= nu_ref[0])
    def _():
        ys_ref[...] = jnp.zeros_like(ys_ref)


def experts(xs, block_e, n_used, next_used, used_ord, w_gate, w_up, w_down, n_rows_pad):
    dh = xs.shape[1]
    d, d_e = w_gate.shape[1], w_gate.shape[2]
    assert d == 2 * dh
    n_blocks = n_rows_pad // MOE_ROWS
    hbm = pl.BlockSpec(memory_space=pl.ANY)
    grid_spec = pltpu.PrefetchScalarGridSpec(
        num_scalar_prefetch=4,
        grid=(n_blocks,),
        in_specs=[pl.BlockSpec((MOE_ROWS, dh), lambda j, be, nu, nx, od: (jnp.minimum(j, nu[0] - 1), 0)),
                  hbm, hbm, hbm],
        out_specs=pl.BlockSpec((MOE_ROWS, dh), lambda j, be, nu, nx, od: (j, 0)),
        scratch_shapes=[pltpu.VMEM((W_SLOTS, d, d_e), F32), pltpu.VMEM((W_SLOTS, d, d_e), F32),
                        pltpu.VMEM((W_SLOTS, d_e, d), F32),
                        pltpu.VMEM((d, d_e), BF16), pltpu.VMEM((d, d_e), BF16), pltpu.VMEM((d_e, d), BF16),
                        pltpu.SemaphoreType.DMA((W_SLOTS,))],
    )
    return pl.pallas_call(
        _experts_kernel,
        grid_spec=grid_spec,
        out_shape=jax.ShapeDtypeStruct((n_rows_pad, dh), jnp.uint32),
        compiler_params=pltpu.CompilerParams(dimension_semantics=("arbitrary",), vmem_limit_bytes=VMEM_LIMIT),
        name="moe_experts",
    )(block_e, n_used, next_used, used_ord, xs, w_gate, w_up, w_down)


def _combine_kernel(dest_ref, dest_next_ref, h_ref, gate_ref, nfin_ref, ys_ref, o_ref, ybuf, sem):
    step = pl.program_id(0)
    tm = h_ref.shape[0]
    slot = lax.rem(step, 2)

    def copy(d_ref, s, i, k):
        d = d_ref[0, 0, i * TOP_K + k]
        return pltpu.make_async_copy(ys_ref.at[pl.ds(d, 1), :], ybuf.at[s, k, pl.ds(i, 1), :], sem.at[s])

    def start_all(d_ref, s):
        def body(i, carry):
            for k in range(TOP_K):
                copy(d_ref, s, i, k).start(priority=1)
            return carry
        lax.fori_loop(0, tm, body, 0, unroll=8)

    @pl.when(step == 0)
    def _():
        start_all(dest_ref, 0)

    @pl.when(step + 1 < pl.num_programs(0))
    def _():
        start_all(dest_next_ref, 1 - slot)

    for k in range(TOP_K):
        pltpu.make_async_copy(ys_ref.at[pl.ds(0, tm), :], ybuf.at[slot, k], sem.at[slot]).wait()
    gates = gate_ref[...]
    moe = (gates[:, 0:1] * _unpack_bf16_pairs(ybuf[slot, 0])
           + gates[:, 1:2] * _unpack_bf16_pairs(ybuf[slot, 1]))
    o_ref[...] = _rms(h_ref[...] + moe, nfin_ref[...])


def combine(h, gates, dest, ys, norm_final, *, tm=512):
    nt, d = h.shape
    assert nt % tm == 0 and ys.shape[1] * 2 == d
    n_tiles = nt // tm
    dest3 = dest.reshape(n_tiles, 1, tm * TOP_K)
    return pl.pallas_call(
        _combine_kernel,
        grid=(n_tiles,),
        in_specs=[pl.BlockSpec((1, 1, tm * TOP_K), lambda i: (i, 0, 0), memory_space=pltpu.SMEM),
                  pl.BlockSpec((1, 1, tm * TOP_K), lambda i: (jnp.minimum(i + 1, n_tiles - 1), 0, 0),
                               memory_space=pltpu.SMEM),
                  pl.BlockSpec((tm, d), lambda i: (i, 0)),
                  pl.BlockSpec((tm, LANES), lambda i: (i, 0)),
                  pl.BlockSpec((1, d), lambda i: (0, 0)),
                  pl.BlockSpec(memory_space=pl.ANY)],
        out_specs=pl.BlockSpec((tm, d), lambda i: (i, 0)),
        out_shape=jax.ShapeDtypeStruct((nt, d), F32),
        scratch_shapes=[pltpu.VMEM((2, TOP_K, tm, d // 2), jnp.uint32), pltpu.SemaphoreType.DMA((2,))],
        compiler_params=pltpu.CompilerParams(dimension_semantics=("arbitrary",), vmem_limit_bytes=VMEM_LIMIT),
        name="moe_combine",
    )(dest3, dest3, h, gates, norm_final.reshape(1, d).astype(F32), ys)


def kernel(x, norm_mix, w_in, conv_dw, conv_b, conv_ln_g, conv_ln_b, shift_mu, w0, w_lora_up, a0, a_lora_up, g_lora_up, k_k, k_a, r_k, gn_g, gn_b, w_out, norm_ffn, router_group_w, router_group_b, router_expert_w, router_expert_b, expert_w_gate, expert_w_up, expert_w_down, norm_final):
    B, T, D = x.shape
    depth = w_in.shape[0]
    d_c = conv_dw.shape[2]
    d_r = w0.shape[1]
    d_main = 2 * d_c + 3 * d_r
    nt = B * T
    n_rows_pad = -(-(nt * TOP_K + N_EXPERTS * (MOE_ROWS - 1)) // MOE_ROWS) * MOE_ROWS
    h = x.reshape(nt, D)
    for l in range(depth):
        proj_main, proj_lora = in_proj(h, norm_mix[l], w_in[l], d_main)
        y_conv = conv_mixer(proj_main, conv_dw[l], conv_b[l], conv_ln_g[l], conv_ln_b[l], batch=B)
        y_rwkv = rwkv_mixer(proj_main, proj_lora, 2 * d_c, shift_mu[l], w0[l], w_lora_up[l], a0[l], a_lora_up[l],
                            g_lora_up[l], k_k[l], k_a[l], r_k[l].reshape(-1), gn_g[l], gn_b[l], batch=B)
        h, v, routing, gates, counts = out_router(h, y_conv, y_rwkv, w_out[l].astype(BF16), norm_ffn[l],
                                                  router_group_w[l], router_group_b[l], router_expert_w[l],
                                                  router_expert_b[l])
        dest, block_e, n_used, next_used, used_ord, fill_plan = slot_plan(routing, counts[0, :N_EXPERTS],
                                                                          n_rows_pad)
        xs = dispatch(v, dest, fill_plan, n_rows_pad)
        ys = experts(xs, block_e, n_used, next_used, used_ord, expert_w_gate[l], expert_w_up[l], expert_w_down[l],
                     n_rows_pad)
        assert depth == 1
        h = combine(h, gates, dest, ys, norm_final)
    return h.reshape(B, T, D)
```

```python
import functools

import jax
import jax.numpy as jnp
from jax import lax
from jax.experimental import pallas as pl
from jax.experimental.pallas import tpu as pltpu

F32 = jnp.float32
BF16 = jnp.bfloat16

RWKV_HEAD = 64
CONV_WIDTH = 31
N_GROUPS = 4
EXPERTS_PER_GROUP = 8
N_EXPERTS = N_GROUPS * EXPERTS_PER_GROUP
TOP_K = 2
RMS_EPS = 1e-6
LN_EPS = 1e-5
GN_EPS = 64e-5
LANES = 128
SUBLANES = 8
MXU_DIM = 256
VMEM_LIMIT = 56 * 1024 * 1024
VMEM_LIMIT_IN_PROJ = 60 * 1024 * 1024

_NT = (((1,), (1,)), ((), ()))
_TN = (((0,), (0,)), ((), ()))


def _dot(a, b, dims=None):
    a = a.astype(BF16)
    b = b.astype(BF16)
    if dims is None:
        return jnp.dot(a, b, preferred_element_type=F32)
    return lax.dot_general(a, b, dims, preferred_element_type=F32)


def _pack_bf16_pairs(x):
    w = x.shape[1] // 2
    lo = lax.bitcast_convert_type(x[:, :w].astype(BF16).astype(F32), jnp.uint32)
    hi = lax.bitcast_convert_type(x[:, w:].astype(BF16).astype(F32), jnp.uint32)
    return (lo >> 16) | (hi & jnp.uint32(0xFFFF0000))


def _unpack_bf16_pairs(p):
    lo = lax.bitcast_convert_type(p << 16, F32)
    hi = lax.bitcast_convert_type(p & jnp.uint32(0xFFFF0000), F32)
    return jnp.concatenate([lo, hi], axis=1)


def _rwkv_kernel(r_ref, k_ref, v_ref, lora_ref, mu_main_ref, mu_lora_ref, w0_ref, wup_ref, a0_ref, aup_ref,
                 gup_ref, kk_ref, ka_ref, rk_ref, gng_ref, gnb_ref, y_ref,
                 state_ref, prev_main_ref, prev_lora_ref, *, n_heads, d_decay, d_aaa, chunk):
    nb, Cb, d_r = r_ref.shape
    C = chunk
    CS = Cb // C
    N = RWKV_HEAD
    R = nb * Cb

    @pl.when(pl.program_id(0) == 0)
    def _():
        state_ref[...] = jnp.zeros_like(state_ref)
        prev_main_ref[...] = jnp.zeros_like(prev_main_ref)
        prev_lora_ref[...] = jnp.zeros_like(prev_lora_ref)

    row = lax.broadcasted_iota(jnp.int32, (R, 1), 0)

    def shifted(x_ref, prev_ref, j, mu):
        x = x_ref[...].reshape(R, x_ref.shape[2])
        xp = pltpu.roll(x, 1, 0)
        for b in range(nb):
            xp = jnp.where(row == b * Cb, prev_ref[b, j:j + 1, :], xp)
            prev_ref[b, j:j + 1, :] = x[(b + 1) * Cb - 1:(b + 1) * Cb, :]
        return x + (xp - x) * mu

    r = shifted(r_ref, prev_main_ref, 0, mu_main_ref[0:1, :])
    k = shifted(k_ref, prev_main_ref, 1, mu_main_ref[1:2, :])
    v = shifted(v_ref, prev_main_ref, 2, mu_main_ref[2:3, :])
    lo = shifted(lora_ref, prev_lora_ref, 0, mu_lora_ref[...])

    wd = lo[:, :d_decay]
    ad = lo[:, d_decay:d_decay + d_aaa]
    gd = lo[:, d_decay + d_aaa:]

    z = w0_ref[...] + _dot(jnp.tanh(wd), wup_ref[...])
    w = -(jnp.maximum(-z, 0.0) + jnp.log(1.0 + jnp.exp(-jnp.abs(z)))) - 0.5
    logd = -jnp.exp(w)
    a = jax.nn.sigmoid(a0_ref[...] + _dot(ad, aup_ref[...]))
    g = _dot(jax.nn.sigmoid(gd), gup_ref[...])

    li = lax.broadcasted_iota(jnp.int32, (MXU_DIM, MXU_DIM), 0) // N
    lj = lax.broadcasted_iota(jnp.int32, (MXU_DIM, MXU_DIM), 1) // N
    head_ones = (li == lj).astype(BF16)

    def head_sum(x):
        return jnp.concatenate(
            [_dot(x[:, c * MXU_DIM:(c + 1) * MXU_DIM], head_ones) for c in range(d_r // MXU_DIM)], axis=1)

    kkr = k * kk_ref[...]
    kk = kkr / jnp.maximum(jnp.sqrt(head_sum(kkr * kkr)), 1e-12)
    k2 = k * (1.0 + (a - 1.0) * ka_ref[...])
    a_s = -kk
    b_s = kk * a

    ri = lax.broadcasted_iota(jnp.int32, (R, R), 0)
    rj = lax.broadcasted_iota(jnp.int32, (R, R), 1)
    tri = ((rj <= ri) & ((ri // C) == (rj // C))).astype(BF16)
    hi = logd.astype(BF16)
    mid = (logd - hi.astype(F32)).astype(BF16)
    cum = jnp.dot(tri, hi, preferred_element_type=F32) + jnp.dot(tri, mid, preferred_element_type=F32)
    p_incl = jnp.exp(cum)
    p_excl = jnp.exp(cum - logd)
    p_inv = jnp.exp(-cum)
    p_last = [p_incl[(q + 1) * C - 1:(q + 1) * C, :] for q in range(nb * CS)]
    p_last_rows = jnp.concatenate([jnp.broadcast_to(p, (C, d_r)) for p in p_last], axis=0)

    at = (a_s * p_excl).astype(BF16)
    rt = (r * p_incl).astype(BF16)
    bt = b_s * p_inv
    kt = k2 * p_inv
    bh = (bt * p_last_rows).astype(BF16)
    kh = (kt * p_last_rows).astype(BF16)
    bt = bt.astype(BF16)
    kt = kt.astype(BF16)
    vb = v.astype(BF16)

    HG = LANES // N
    W = HG * N
    GC = HG * C
    wi = lax.broadcasted_iota(jnp.int32, (GC, W), 0)
    wj = lax.broadcasted_iota(jnp.int32, (GC, W), 1)
    bd_on = ((wi // C) == (wj // N)).astype(BF16) > 0

    def bdiag(x):
        xb = x.astype(BF16)
        return jnp.where(bd_on, jnp.concatenate([xb] * HG, axis=0), jnp.zeros((), BF16))

    ti = lax.broadcasted_iota(jnp.int32, (2 * C, GC), 0)
    tj = lax.broadcasted_iota(jnp.int32, (2 * C, GC), 1) % C
    mask2 = ((ti < C) & (tj < ti)) | ((ti >= C) & (tj <= ti - C))
    li = lax.broadcasted_iota(jnp.int32, (C, GC), 0)
    lj = lax.broadcasted_iota(jnp.int32, (C, GC), 1) % C
    eye = (li == lj).astype(F32)
    level_masks = []
    s = 1
    while s < C:
        level_masks.append(((li // (2 * s)) == (lj // (2 * s))) & (((li // s) % 2) == 1) & (((lj // s) % 2) == 0))
        s *= 2
    colblk = lax.broadcasted_iota(jnp.int32, (N, W), 1) // N

    n_g = n_heads // HG
    units = [(q, g) for q in range(nb * CS) for g in range(n_g)]
    idx = range(len(units))
    rs = [slice(q * C, (q + 1) * C) for q, _ in units]
    ls = [slice(g * W, (g + 1) * W) for _, g in units]
    g_b = [jnp.where(mask2, _dot(jnp.concatenate([at[rs[i], ls[i]], rt[rs[i], ls[i]]], axis=0),
                                 bdiag(bt[rs[i], ls[i]]), _NT), 0.0) for i in idx]
    g_k = [jnp.where(mask2, _dot(jnp.concatenate([at[rs[i], ls[i]], rt[rs[i], ls[i]]], axis=0),
                                 bdiag(kt[rs[i], ls[i]]), _NT), 0.0) for i in idx]
    gv = [_dot(g_k[i], bdiag(vb[rs[i], ls[i]])) for i in idx]
    tinv = [eye + jnp.where(level_masks[0], g_b[i][:C], 0.0) for i in idx]
    for lm in level_masks[1:]:
        xs = [_dot(jnp.where(lm, g_b[i][:C], 0.0), bdiag(tinv[i])) for i in idx]
        tinv = [tinv[i] + _dot(tinv[i], bdiag(xs[i])) for i in idx]
    ta = [_dot(tinv[i], bdiag(at[rs[i], ls[i]])) for i in idx]
    tv = [_dot(tinv[i], bdiag(gv[i][:C])) for i in idx]
    ys = [None] * len(units)
    for b in range(nb):
        state = [state_ref[b * n_g + gi] for gi in range(n_g)]
        for cc in range(CS):
            ids = [(b * CS + cc) * n_g + gi for gi in range(n_g)]
            tr = [_dot(jnp.concatenate([ta[i].astype(BF16), rt[rs[i], ls[i]]], axis=0), bdiag(state[gi]), _NT)
                  for gi, i in enumerate(ids)]
            u = [tr[gi][:C] + tv[i] for gi, i in enumerate(ids)]
            for gi, i in enumerate(ids):
                ys[i] = tr[gi][C:] + gv[i][C:] + _dot(g_b[i][C:], bdiag(u[gi]))
                uv = jnp.concatenate([u[gi].astype(BF16), vb[rs[i], ls[i]]], axis=0)
                bk = jnp.concatenate([bh[rs[i], ls[i]], kh[rs[i], ls[i]]], axis=0)
                full = _dot(uv, bk, _TN)
                upd = state[gi] * p_last[units[i][0]][:, ls[i]]
                for h in range(HG):
                    upd = upd + jnp.where(colblk == h, full[h * N:(h + 1) * N, :], 0.0)
                state[gi] = upd
        for gi in range(n_g):
            state_ref[b * n_g + gi] = state[gi]
    y = jnp.concatenate([jnp.concatenate(ys[q * n_g:(q + 1) * n_g], axis=1) for q in range(nb * CS)],
                        axis=0)

    inv_n = 1.0 / N
    mu_y = head_sum(y) * inv_n
    yc = y - mu_y
    var_y = head_sum(yc * yc) * inv_n
    yn = yc * lax.rsqrt(var_y + GN_EPS) * gng_ref[...] + gnb_ref[...]
    bonus = head_sum(r * k2 * rk_ref[...]) * v
    y_ref[...] = ((yn + bonus) * g).reshape(nb, Cb, d_r).astype(y_ref.dtype)


def rwkv_mixer(proj_main, proj_lora, col0, shift_mu, w0, w_up, a0, a_up, g_up, k_k, k_a, r_k, gn_g, gn_b,
               *, batch, chunk=64, chunks_per_step=2):
    nt = proj_main.shape[0]
    seq = nt // batch
    d_r = w0.shape[-1]
    n_heads = d_r // RWKV_HEAD
    d_lora = proj_lora.shape[1]
    d_decay, d_aaa = w_up.shape[0], a_up.shape[0]
    rows = chunk * chunks_per_step
    n_steps = seq // rows
    assert seq % rows == 0 and col0 % d_r == 0
    cb = col0 // d_r
    row2 = lambda t: t.reshape(1, -1).astype(F32)
    mu_main = shift_mu[:3 * d_r].reshape(3, d_r)
    mu_lora = shift_mu[3 * d_r:].reshape(1, d_lora)
    pm3 = proj_main.reshape(batch, seq, proj_main.shape[1])
    pl3 = proj_lora.reshape(batch, seq, d_lora)

    def main_spec(j):
        return pl.BlockSpec((batch, rows, d_r), lambda c, j=j: (0, c, cb + j))

    full = lambda shape: pl.BlockSpec(shape, lambda c: (0,) * len(shape))
    kern = functools.partial(_rwkv_kernel, n_heads=n_heads, d_decay=d_decay, d_aaa=d_aaa, chunk=chunk)
    y = pl.pallas_call(
        kern,
        grid=(n_steps,),
        in_specs=[main_spec(0), main_spec(1), main_spec(2),
                  pl.BlockSpec((batch, rows, d_lora), lambda c: (0, c, 0)),
                  full((3, d_r)), full((1, d_lora)), full((1, d_r)), full((d_decay, d_r)), full((1, d_r)),
                  full((d_aaa, d_r)), full((g_up.shape[0], d_r)), full((1, d_r)), full((1, d_r)), full((1, d_r)),
                  full((1, d_r)), full((1, d_r))],
        out_specs=pl.BlockSpec((batch, rows, d_r), lambda c: (0, c, 0)),
        out_shape=jax.ShapeDtypeStruct((batch, seq, d_r), BF16),
        scratch_shapes=[pltpu.VMEM((batch * n_heads * RWKV_HEAD // LANES, RWKV_HEAD, LANES), F32),
                        pltpu.VMEM((batch, 3, d_r), F32),
                        pltpu.VMEM((batch, 1, d_lora), F32)],
        compiler_params=pltpu.CompilerParams(dimension_semantics=("arbitrary",), vmem_limit_bytes=VMEM_LIMIT),
        name="rwkv_mixer",
    )(pm3, pm3, pm3, pl3, mu_main, mu_lora, row2(w0), w_up.astype(BF16), row2(a0),
      a_up.astype(BF16), g_up.astype(BF16), row2(k_k), row2(k_a), row2(r_k), row2(gn_g), row2(gn_b))
    return y.reshape(nt, d_r)


def _rms(x, g):
    return x * lax.rsqrt(jnp.mean(x * x, axis=-1, keepdims=True) + RMS_EPS) * g


def _inproj_kernel(x_hbm, g_ref, wt_hbm, wlt_ref, o_ref, ol_ref, x_f, u_ref, w_f, xsem, wsem):
    i, j = pl.program_id(0), pl.program_id(1)
    n_i, n_j = pl.num_programs(0), pl.num_programs(1)
    tm = x_f.shape[0]
    tn = w_f.shape[1]
    step = i * n_j + j
    slot = lax.rem(step, 2)

    def fetch_w(jt, s):
        return pltpu.make_async_copy(wt_hbm.at[pl.ds(pl.multiple_of(jt * tn, tn), tn), :], w_f.at[s], wsem.at[s])

    def fetch_x(it):
        return pltpu.make_async_copy(x_hbm.at[pl.ds(pl.multiple_of(it * tm, tm), tm), :], x_f, xsem)

    @pl.when(step == 0)
    def _():
        fetch_x(0).start()
        fetch_w(0, 0).start(priority=1)

    @pl.when(step + 1 < n_i * n_j)
    def _():
        fetch_w(lax.rem(j + 1, n_j), 1 - slot).start(priority=1)

    @pl.when(j == 0)
    def _():
        fetch_x(i).wait()
        u = _rms(x_f[...], g_ref[...]).astype(BF16)
        u_ref[...] = u

        @pl.when(i + 1 < n_i)
        def _():
            fetch_x(i + 1).start()

        ol_ref[...] = lax.dot_general(u, wlt_ref[...], _NT, preferred_element_type=F32)

    fetch_w(j, slot).wait()
    o_ref[...] = lax.dot_general(u_ref[...], w_f[slot].astype(BF16), _NT, preferred_element_type=F32)


def in_proj(x2, norm_g, w_in, d_main, *, tm=2048, tn=512):
    nt, d = x2.shape
    d_lora = w_in.shape[1] - d_main
    assert nt % tm == 0 and d_main % tn == 0
    w_t = w_in.T
    w_lora_t = w_t[d_main:].astype(BF16)
    hbm = pl.BlockSpec(memory_space=pl.ANY)
    return pl.pallas_call(
        _inproj_kernel,
        grid=(nt // tm, d_main // tn),
        in_specs=[hbm, pl.BlockSpec((1, d), lambda i, j: (0, 0)), hbm,
                  pl.BlockSpec((d_lora, d), lambda i, j: (0, 0))],
        out_specs=[pl.BlockSpec((tm, tn), lambda i, j: (i, j)),
                   pl.BlockSpec((tm, d_lora), lambda i, j: (i, 0))],
        out_shape=[jax.ShapeDtypeStruct((nt, d_main), F32), jax.ShapeDtypeStruct((nt, d_lora), F32)],
        scratch_shapes=[pltpu.VMEM((tm, d), F32), pltpu.VMEM((tm, d), BF16), pltpu.VMEM((2, tn, d), F32),
                        pltpu.SemaphoreType.DMA(()), pltpu.SemaphoreType.DMA((2,))],
        compiler_params=pltpu.CompilerParams(dimension_semantics=("arbitrary", "arbitrary"),
                                             vmem_limit_bytes=VMEM_LIMIT_IN_PROJ),
        name="in_proj",
    )(x2, norm_g.reshape(1, d).astype(F32), w_t, w_lora_t)


CONV_HALO = 32
CONV_ROWS = 32


def _conv_kernel(val_ref, gate_ref, dw_ref, b_ref, g_ref, beta_ref, o_ref, ubuf_ref):
    tt = val_ref.shape[0]
    d_c = val_ref.shape[1]
    t_idx = pl.program_id(1)

    @pl.when(t_idx == 0)
    def _():
        ubuf_ref[0, 0:CONV_HALO, :] = jnp.zeros((CONV_HALO, d_c), F32)

    @pl.when(t_idx > 0)
    def _():
        ubuf_ref[0, 0:CONV_HALO, :] = ubuf_ref[0, tt:tt + CONV_HALO, :]

    ubuf_ref[0, CONV_HALO:CONV_HALO + tt, :] = val_ref[...] * jax.nn.sigmoid(gate_ref[...])
    n_sh = tt + CONV_HALO - SUBLANES
    for s in range(1, SUBLANES):
        for r in range(0, n_sh, CONV_ROWS):
            n = min(CONV_ROWS, n_sh - r)
            ubuf_ref[s, r:r + n, :] = ubuf_ref[0, r + s:r + s + n, :]
    dw = dw_ref[...]
    lead = CONV_HALO - (CONV_WIDTH - 1)
    for c in range(tt // CONV_ROWS):
        r0 = c * CONV_ROWS
        acc = jnp.zeros((CONV_ROWS, d_c), F32)
        for j in range(CONV_WIDTH):
            q, s = divmod(lead + j, SUBLANES)
            r = r0 + q * SUBLANES
            acc = acc + dw[j:j + 1, :] * ubuf_ref[s, r:r + CONV_ROWS, :]
        acc = acc + b_ref[...]
        mu = jnp.mean(acc, axis=-1, keepdims=True)
        cen = acc - mu
        var = jnp.mean(cen * cen, axis=-1, keepdims=True)
        yv = cen * lax.rsqrt(var + LN_EPS) * g_ref[...] + beta_ref[...]
        o_ref[r0:r0 + CONV_ROWS, :] = (yv * jax.nn.sigmoid(yv)).astype(o_ref.dtype)


def conv_mixer(proj_main, dw, bias, ln_g, ln_b, *, batch, tt=512):
    nt = proj_main.shape[0]
    seq = nt // batch
    d_c = dw.shape[1]
    n_t = seq // tt
    assert seq % tt == 0 and tt % CONV_ROWS == 0
    row = lambda t: t.reshape(1, d_c).astype(F32)
    full = lambda shape: pl.BlockSpec(shape, lambda b, t: (0, 0))
    return pl.pallas_call(
        _conv_kernel,
        grid=(batch, n_t),
        in_specs=[pl.BlockSpec((tt, d_c), lambda b, t: (b * n_t + t, 0)),
                  pl.BlockSpec((tt, d_c), lambda b, t: (b * n_t + t, 1)),
                  full((CONV_WIDTH, d_c)), full((1, d_c)), full((1, d_c)), full((1, d_c))],
        out_specs=pl.BlockSpec((tt, d_c), lambda b, t: (b * n_t + t, 0)),
        out_shape=jax.ShapeDtypeStruct((nt, d_c), BF16),
        scratch_shapes=[pltpu.VMEM((SUBLANES, tt + CONV_HALO, d_c), F32)],
        compiler_params=pltpu.CompilerParams(dimension_semantics=("arbitrary", "arbitrary"),
                                             vmem_limit_bytes=VMEM_LIMIT),
        name="conv_mixer",
    )(proj_main, proj_main, dw.astype(F32), row(bias), row(ln_g), row(ln_b))


def _out_router_kernel(x_ref, yc_ref, yr_ref, wc_ref, wr_ref, nf_ref, rw_hi_ref, rw_lo_ref, rb_ref,
                       h_ref, v_ref, eid_ref, gate_ref, count_ref, cnt_ref):
    tm = x_ref.shape[0]
    h = (x_ref[...] + jnp.dot(yc_ref[...], wc_ref[...], preferred_element_type=F32)
         + jnp.dot(yr_ref[...], wr_ref[...], preferred_element_type=F32))
    h_ref[...] = h

    @pl.when(pl.program_id(0) == 0)
    def _():
        cnt_ref[...] = jnp.zeros_like(cnt_ref)

    v = _rms(h, nf_ref[...])
    v_ref[...] = _pack_bf16_pairs(v)
    v_hi = v.astype(BF16)
    v_lo = (v - v_hi.astype(F32)).astype(BF16)
    logits = (jnp.dot(v_hi, rw_hi_ref[...], preferred_element_type=F32)
              + jnp.dot(v_lo, rw_hi_ref[...], preferred_element_type=F32)
              + jnp.dot(v_hi, rw_lo_ref[...], preferred_element_type=F32)) + rb_ref[...]
    lane = lax.broadcasted_iota(jnp.int32, (tm, LANES), 1)
    neg = jnp.float32(-jnp.inf)
    big = jnp.int32(LANES)

    def first_max(vals):
        m = jnp.max(vals, axis=-1, keepdims=True)
        return m, jnp.min(jnp.where(vals == m, lane, big), axis=-1, keepdims=True)

    gl = jnp.where(lane < N_GROUPS, logits, neg)
    gmax, grp = first_max(gl)
    p_grp = 1.0 / jnp.sum(jnp.exp(gl - gmax), axis=-1, keepdims=True)
    lo = N_GROUPS + grp * EXPERTS_PER_GROUP
    el = jnp.where((lane >= lo) & (lane < lo + EXPERTS_PER_GROUP), logits, neg)
    m1, i1 = first_max(el)
    m2, i2 = first_max(jnp.where(lane == i1, neg, el))
    e2 = jnp.exp(m2 - m1)
    g1 = p_grp / (1.0 + e2)
    g2 = p_grp * e2 / (1.0 + e2)
    gate_ref[...] = jnp.where(lane == 0, g1, jnp.where(lane == 1, g2, 0.0))

    x1 = i1 - N_GROUPS
    x2 = i2 - N_GROUPS
    oh1 = (lane == x1).astype(F32)
    oh2 = (lane == x2).astype(F32)
    earlier = (lax.broadcasted_iota(jnp.int32, (tm, tm), 1)
               < lax.broadcasted_iota(jnp.int32, (tm, tm), 0)).astype(BF16)
    before1 = jnp.dot(earlier, oh1.astype(BF16), preferred_element_type=F32)
    before2 = jnp.dot(earlier, oh2.astype(BF16), preferred_element_type=F32)
    carry = cnt_ref[...]
    n1 = jnp.sum(oh1, axis=0, keepdims=True)
    n2 = jnp.sum(oh2, axis=0, keepdims=True)
    rank1 = jnp.sum(oh1 * (before1 + carry), axis=-1, keepdims=True).astype(jnp.int32)
    rank2 = jnp.sum(oh2 * (before2 + carry + n1), axis=-1, keepdims=True).astype(jnp.int32)
    cnt_ref[...] = carry + n1 + n2
    count_ref[...] = (carry + n1 + n2).astype(jnp.int32)
    routing = jnp.where(lane == 0, x1, jnp.where(lane == 1, x2, jnp.where(
        lane == 2, rank1, jnp.where(lane == 3, rank2, 0))))
    eid_ref[...] = routing.T[:SUBLANES, :]


def out_router(x2, y_conv, y_rwkv, w_out, norm_ffn, rg_w, rg_b, re_w, re_b, *, tm=512):
    nt, d = x2.shape
    d_c, d_r = y_conv.shape[1], y_rwkv.shape[1]
    assert d_c == d_r and nt % tm == 0
    n_r = N_GROUPS + N_EXPERTS
    rw = jnp.zeros((d, LANES), F32).at[:, :n_r].set(jnp.concatenate([rg_w, re_w], axis=1).astype(F32))
    rw_hi = rw.astype(BF16)
    rw_lo = (rw - rw_hi.astype(F32)).astype(BF16)
    rb = jnp.zeros((1, LANES), F32).at[0, :n_r].set(jnp.concatenate([rg_b, re_b]).astype(F32))
    full = lambda shape: pl.BlockSpec(shape, lambda i: (0, 0))
    once = lambda shape, r: pl.BlockSpec(shape, lambda i, r=r: (r, 0), pipeline_mode=pl.Buffered(1))
    rows = lambda w: pl.BlockSpec((tm, w), lambda i: (i, 0))
    return pl.pallas_call(
        _out_router_kernel,
        grid=(nt // tm,),
        in_specs=[rows(d), rows(d_c), rows(d_r), once((d_c, d), 0), once((d_r, d), 1),
                  full((1, d)), full((d, LANES)), full((d, LANES)), full((1, LANES))],
        out_specs=[rows(d), rows(d // 2), pl.BlockSpec((SUBLANES, tm), lambda i: (0, i)), rows(LANES),
                   full((1, LANES))],
        out_shape=[jax.ShapeDtypeStruct((nt, d), F32), jax.ShapeDtypeStruct((nt, d // 2), jnp.uint32),
                   jax.ShapeDtypeStruct((SUBLANES, nt), jnp.int32), jax.ShapeDtypeStruct((nt, LANES), F32),
                   jax.ShapeDtypeStruct((1, LANES), jnp.int32)],
        scratch_shapes=[pltpu.VMEM((1, LANES), F32)],
        compiler_params=pltpu.CompilerParams(dimension_semantics=("arbitrary",), vmem_limit_bytes=VMEM_LIMIT),
        name="out_router",
    )(x2, y_conv, y_rwkv, w_out, w_out, norm_ffn.reshape(1, d).astype(F32), rw_hi, rw_lo, rb)


MOE_ROWS = 256
W_SLOTS = 3


def slot_plan(routing, counts, n_rows_pad):
    expert_id = routing[:TOP_K]
    rank = routing[TOP_K:2 * TOP_K]
    padded = (counts + MOE_ROWS - 1) // MOE_ROWS * MOE_ROWS
    pend = jnp.cumsum(padded)
    pstarts = pend - padded
    ids = jnp.arange(N_EXPERTS, dtype=jnp.int32)
    start_of = jnp.sum(jnp.where(expert_id[..., None] == ids, pstarts, 0), axis=-1)
    dest = (start_of + rank).astype(jnp.int32).T
    n_blocks = n_rows_pad // MOE_ROWS
    block_row0 = jnp.arange(n_blocks, dtype=jnp.int32) * MOE_ROWS
    block_e = jnp.minimum(jnp.sum((pend[None, :] <= block_row0[:, None]).astype(jnp.int32), axis=1),
                          N_EXPERTS - 1).astype(jnp.int32)
    n_used = (pend[-1:] // MOE_ROWS).astype(jnp.int32)
    owner = jnp.where(counts > 0, ids, N_EXPERTS)
    nxt = lax.cummin(jnp.concatenate([owner[1:], jnp.full((1,), N_EXPERTS, jnp.int32)]), reverse=True)
    next_used = jnp.where(nxt < N_EXPERTS, nxt, -1).astype(jnp.int32)
    hops = [next_used]
    for _ in range(W_SLOTS - 2):
        hops.append(jnp.where(hops[-1] >= 0, next_used[jnp.maximum(hops[-1], 0)], -1))
    next_used = jnp.stack(hops)
    used_ord = (jnp.cumsum((counts > 0).astype(jnp.int32)) - 1).astype(jnp.int32)
    pad0 = pstarts + counts
    n_single = jnp.minimum((-pad0) % SUBLANES, pend - pad0)
    tile0 = pad0 + n_single
    n_tile = (pend - tile0) // SUBLANES
    tail = jnp.stack([pend[-1] // MOE_ROWS, n_blocks - pend[-1] // MOE_ROWS, jnp.sum(n_single), jnp.sum(n_tile)])
    fill_plan = jnp.concatenate([pad0, n_single, tile0, n_tile, tail]).astype(jnp.int32)
    return dest, block_e, n_used, next_used, used_ord, fill_plan


def _dispatch_kernel(fill_ref, dest_ref, v_ref, xs_ref, zero_ref, sem, fill_sem):
    tm = v_ref.shape[0]

    def fills():
        e0, e1, e2, e3, e4 = (k * N_EXPERTS for k in range(5))
        single = lambda e, i: pltpu.make_async_copy(
            zero_ref.at[pl.ds(0, 1), :], xs_ref.at[pl.ds(fill_ref[e0 + e] + i, 1), :], fill_sem)
        tile = lambda e, i: pltpu.make_async_copy(
            zero_ref.at[pl.ds(0, SUBLANES), :],
            xs_ref.at[pl.ds(pl.multiple_of(fill_ref[e2 + e] + i * SUBLANES, SUBLANES), SUBLANES), :], fill_sem)
        block = lambda i: pltpu.make_async_copy(
            zero_ref, xs_ref.at[pl.ds(pl.multiple_of((fill_ref[e4] + i) * MOE_ROWS, MOE_ROWS), MOE_ROWS), :],
            fill_sem)
        return single, tile, block, e1, e3, e4

    @pl.when(pl.program_id(0) == 0)
    def _():
        zero_ref[...] = jnp.zeros_like(zero_ref)
        single, tile, block, e1, e3, e4 = fills()
        for e in range(N_EXPERTS):
            lax.fori_loop(0, fill_ref[e1 + e], lambda i, c, e=e: (single(e, i).start(), c)[1], 0)
            lax.fori_loop(0, fill_ref[e3 + e], lambda i, c, e=e: (tile(e, i).start(), c)[1], 0)
        lax.fori_loop(0, fill_ref[e4 + 1], lambda i, c: (block(i).start(), c)[1], 0)

    def copy(i, k):
        d = dest_ref[0, 0, i * TOP_K + k]
        return pltpu.make_async_copy(v_ref.at[pl.ds(i, 1), :], xs_ref.at[pl.ds(d, 1), :], sem)

    def start(i, carry):
        for k in range(TOP_K):
            copy(i, k).start(priority=k)
        return carry

    lax.fori_loop(0, tm, start, 0, unroll=8)
    for k in range(TOP_K):
        pltpu.make_async_copy(v_ref, xs_ref.at[pl.ds(0, tm), :], sem).wait()

    @pl.when(pl.program_id(0) == 0)
    def _():
        single, tile, block, e1, e3, e4 = fills()
        lax.fori_loop(0, fill_ref[e4 + 2], lambda i, c: (single(0, 0).wait(), c)[1], 0)
        lax.fori_loop(0, fill_ref[e4 + 3], lambda i, c: (tile(0, 0).wait(), c)[1], 0)
        lax.fori_loop(0, fill_ref[e4 + 1], lambda i, c: (block(0).wait(), c)[1], 0)


def dispatch(v, dest, fill_plan, n_rows_pad, *, tm=2048):
    nt, d = v.shape
    assert nt % tm == 0
    dest3 = dest.reshape(nt // tm, 1, tm * TOP_K)
    grid_spec = pltpu.PrefetchScalarGridSpec(
        num_scalar_prefetch=1,
        grid=(nt // tm,),
        in_specs=[pl.BlockSpec((1, 1, tm * TOP_K), lambda i, fp: (i, 0, 0), memory_space=pltpu.SMEM),
                  pl.BlockSpec((tm, d), lambda i, fp: (i, 0))],
        out_specs=pl.BlockSpec(memory_space=pl.ANY),
        scratch_shapes=[pltpu.VMEM((MOE_ROWS, d), v.dtype), pltpu.SemaphoreType.DMA(()),
                        pltpu.SemaphoreType.DMA(())],
    )
    return pl.pallas_call(
        _dispatch_kernel,
        grid_spec=grid_spec,
        out_shape=jax.ShapeDtypeStruct((n_rows_pad, d), v.dtype),
        compiler_params=pltpu.CompilerParams(dimension_semantics=("arbitrary",), vmem_limit_bytes=VMEM_LIMIT),
        name="moe_dispatch",
    )(fill_plan, dest3, v)


def _experts_kernel(be_ref, nu_ref, nxt_ref, ord_ref, xs_ref, wg_hbm, wu_hbm, wd_hbm, ys_ref,
                    wg_f, wu_f, wd_f, wg_s, wu_s, wd_s, wsem):
    j = pl.program_id(0)
    e = be_ref[j]
    ahead = W_SLOTS - 1

    def fetch(ex, slot):
        return [pltpu.make_async_copy(wg_hbm.at[ex], wg_f.at[slot], wsem.at[slot]),
                pltpu.make_async_copy(wu_hbm.at[ex], wu_f.at[slot], wsem.at[slot]),
                pltpu.make_async_copy(wd_hbm.at[ex], wd_f.at[slot], wsem.at[slot])]

    @pl.when(j == 0)
    def _():
        ex = e
        for k in range(ahead):
            @pl.when(ex >= 0)
            def _(ex=ex):
                for c in fetch(ex, lax.rem(ord_ref[ex], W_SLOTS)):
                    c.start(priority=1)
            ex = jnp.where(ex >= 0, nxt_ref[0, jnp.maximum(ex, 0)], -1)

    first = (j == 0) | (e != be_ref[jnp.maximum(j - 1, 0)])

    @pl.when(first & (j < nu_ref[0]))
    def _():
        slot = lax.rem(ord_ref[e], W_SLOTS)
        for c in fetch(e, slot):
            c.wait()
        nxt = nxt_ref[ahead - 1, e]

        @pl.when(nxt >= 0)
        def _():
            for c in fetch(nxt, lax.rem(ord_ref[nxt], W_SLOTS)):
                c.start(priority=1)

        wg_s[...] = wg_f[slot].astype(BF16)
        wu_s[...] = wu_f[slot].astype(BF16)
        wd_s[...] = wd_f[slot].astype(BF16)

    @pl.when(j < nu_ref[0])
    def _():
        xb = _unpack_bf16_pairs(xs_ref[...]).astype(BF16)
        gt = jnp.dot(xb, wg_s[...], preferred_element_type=F32)
        up = jnp.dot(xb, wu_s[...], preferred_element_type=F32)
        hid = (gt * jax.nn.sigmoid(gt) * up).astype(BF16)
        packed = _pack_bf16_pairs(jnp.dot(hid, wd_s[...], preferred_element_type=F32))
        for c in range(SUBLANES):
            ys_ref[:, c, :] = packed[:, c * LANES:(c + 1) * LANES]

    @pl.when(j >= nu_ref[0])
    def _():
        ys_ref[...] = jnp.zeros_like(ys_ref)


def experts(xs, block_e, n_used, next_used, used_ord, w_gate, w_up, w_down, n_rows_pad):
    dh = xs.shape[1]
    d, d_e = w_gate.shape[1], w_gate.shape[2]
    assert d == 2 * dh and dh == SUBLANES * LANES
    n_blocks = n_rows_pad // MOE_ROWS
    hbm = pl.BlockSpec(memory_space=pl.ANY)
    grid_spec = pltpu.PrefetchScalarGridSpec(
        num_scalar_prefetch=4,
        grid=(n_blocks,),
        in_specs=[pl.BlockSpec((MOE_ROWS, dh), lambda j, be, nu, nx, od: (jnp.minimum(j, nu[0] - 1), 0)),
                  hbm, hbm, hbm],
        out_specs=pl.BlockSpec((MOE_ROWS, SUBLANES, LANES), lambda j, be, nu, nx, od: (j, 0, 0)),
        scratch_shapes=[pltpu.VMEM((W_SLOTS, d, d_e), F32), pltpu.VMEM((W_SLOTS, d, d_e), F32),
                        pltpu.VMEM((W_SLOTS, d_e, d), F32),
                        pltpu.VMEM((d, d_e), BF16), pltpu.VMEM((d, d_e), BF16), pltpu.VMEM((d_e, d), BF16),
                        pltpu.SemaphoreType.DMA((W_SLOTS,))],
    )
    return pl.pallas_call(
        _experts_kernel,
        grid_spec=grid_spec,
        out_shape=jax.ShapeDtypeStruct((n_rows_pad, SUBLANES, LANES), jnp.uint32),
        compiler_params=pltpu.CompilerParams(dimension_semantics=("arbitrary",), vmem_limit_bytes=VMEM_LIMIT),
        name="moe_experts",
    )(block_e, n_used, next_used, used_ord, xs, w_gate, w_up, w_down)


def _combine_kernel(dest_ref, dest_next_ref, h_ref, gate_ref, nfin_ref, ys_ref, o_ref, ybuf, sem):
    step = pl.program_id(0)
    tm = h_ref.shape[0]
    slot = lax.rem(step, 2)

    def copy(d_ref, s, i, k):
        d = d_ref[0, 0, i * TOP_K + k]
        return pltpu.make_async_copy(ys_ref.at[d], ybuf.at[s, k, i], sem.at[s])

    def start_all(d_ref, s):
        def body(i, carry):
            for k in range(TOP_K):
                copy(d_ref, s, i, k).start(priority=1)
            return carry
        lax.fori_loop(0, tm, body, 0, unroll=8)

    @pl.when(step == 0)
    def _():
        start_all(dest_ref, 0)

    @pl.when(step + 1 < pl.num_programs(0))
    def _():
        start_all(dest_next_ref, 1 - slot)

    for k in range(TOP_K):
        pltpu.make_async_copy(ys_ref.at[pl.ds(0, tm)], ybuf.at[slot, k], sem.at[slot]).wait()
    gates = gate_ref[...]

    def rows(k):
        return jnp.concatenate([ybuf[slot, k, :, c, :] for c in range(SUBLANES)], axis=1)

    moe = (gates[:, 0:1] * _unpack_bf16_pairs(rows(0))
           + gates[:, 1:2] * _unpack_bf16_pairs(rows(1)))
    o_ref[...] = _rms(h_ref[...] + moe, nfin_ref[...])


def combine(h, gates, dest, ys, norm_final, *, tm=512):
    nt, d = h.shape
    assert nt % tm == 0 and ys.shape[1:] == (SUBLANES, LANES) and d == 2 * SUBLANES * LANES
    n_tiles = nt // tm
    dest3 = dest.reshape(n_tiles, 1, tm * TOP_K)
    return pl.pallas_call(
        _combine_kernel,
        grid=(n_tiles,),
        in_specs=[pl.BlockSpec((1, 1, tm * TOP_K), lambda i: (i, 0, 0), memory_space=pltpu.SMEM),
                  pl.BlockSpec((1, 1, tm * TOP_K), lambda i: (jnp.minimum(i + 1, n_tiles - 1), 0, 0),
                               memory_space=pltpu.SMEM),
                  pl.BlockSpec((tm, d), lambda i: (i, 0)),
                  pl.BlockSpec((tm, LANES), lambda i: (i, 0)),
                  pl.BlockSpec((1, d), lambda i: (0, 0)),
                  pl.BlockSpec(memory_space=pl.ANY)],
        out_specs=pl.BlockSpec((tm, d), lambda i: (i, 0)),
        out_shape=jax.ShapeDtypeStruct((nt, d), F32),
        scratch_shapes=[pltpu.VMEM((2, TOP_K, tm, SUBLANES, LANES), jnp.uint32), pltpu.SemaphoreType.DMA((2,))],
        compiler_params=pltpu.CompilerParams(dimension_semantics=("arbitrary",), vmem_limit_bytes=VMEM_LIMIT),
        name="moe_combine",
    )(dest3, dest3, h, gates, norm_final.reshape(1, d).astype(F32), ys)


def kernel(x, norm_mix, w_in, conv_dw, conv_b, conv_ln_g, conv_ln_b, shift_mu, w0, w_lora_up, a0, a_lora_up, g_lora_up, k_k, k_a, r_k, gn_g, gn_b, w_out, norm_ffn, router_group_w, router_group_b, router_expert_w, router_expert_b, expert_w_gate, expert_w_up, expert_w_down, norm_final):
    B, T, D = x.shape
    depth = w_in.shape[0]
    d_c = conv_dw.shape[2]
    d_r = w0.shape[1]
    d_main = 2 * d_c + 3 * d_r
    nt = B * T
    n_rows_pad = -(-(nt * TOP_K + N_EXPERTS * (MOE_ROWS - 1)) // MOE_ROWS) * MOE_ROWS
    h = x.reshape(nt, D)
    for l in range(depth):
        proj_main, proj_lora = in_proj(h, norm_mix[l], w_in[l], d_main)
        y_conv = conv_mixer(proj_main, conv_dw[l], conv_b[l], conv_ln_g[l], conv_ln_b[l], batch=B)
        y_rwkv = rwkv_mixer(proj_main, proj_lora, 2 * d_c, shift_mu[l], w0[l], w_lora_up[l], a0[l], a_lora_up[l],
                            g_lora_up[l], k_k[l], k_a[l], r_k[l].reshape(-1), gn_g[l], gn_b[l], batch=B)
        h, v, routing, gates, counts = out_router(h, y_conv, y_rwkv, w_out[l].astype(BF16), norm_ffn[l],
                                                  router_group_w[l], router_group_b[l], router_expert_w[l],
                                                  router_expert_b[l])
        dest, block_e, n_used, next_used, used_ord, fill_plan = slot_plan(routing, counts[0, :N_EXPERTS],
                                                                          n_rows_pad)
        xs = dispatch(v, dest, fill_plan, n_rows_pad)
        ys = experts(xs, block_e, n_used, next_used, used_ord, expert_w_gate[l], expert_w_up[l], expert_w_down[l],
                     n_rows_pad)
        assert depth == 1
        h = combine(h, gates, dest, ys, norm_final)
    return h.reshape(B, T, D)
```
